```python
import jax, jax.numpy as jnp
from jax import lax
import numpy as np

D_MODEL = 2048
BATCH = 16
SEQ = 256
DEPTH = 2
DEC_BATCH = 8
DEC_SEQ = 2048
PAST_LEN = 256

GRID_W = 64
N_EVEN = (DEPTH + 1) // 2
N_ODD = DEPTH // 2
POOL_WIDTH = D_MODEL // 2
POOL_GROUPS = 4
POOL_GROUP_DIM = POOL_WIDTH // POOL_GROUPS
POOL_WINDOWS = (2, 4, 8, 16)
NA_WIDTH = D_MODEL - POOL_WIDTH
NA_HEAD_DIM = 128
NA_HEADS = NA_WIDTH // NA_HEAD_DIM
WIN_H = 8
WIN_W = 16
NA_QBLOCK = 16
NA_KBLOCK = 2 * WIN_W
CTX_QBLOCK = 128
SSD_INNER = 2 * D_MODEL
SSD_HEAD_DIM = 64
SSD_HEADS = SSD_INNER // SSD_HEAD_DIM
SSD_GROUPS = 8
SSD_STATE = 128
SSD_CONV = 4
SSD_CHUNK = 128
SSD_CONV_DIM = SSD_INNER + 2 * SSD_GROUPS * SSD_STATE
SSD_IN_DIM = SSD_INNER + SSD_CONV_DIM + 2 * SSD_HEADS
N_EXPERTS = 16
EXPERT_FF = 1024
EC_CAPACITY_FACTOR = 2
NORM_EPS = 1e-6

kernel_name = "hybrid_diffusion_pool_na_ssd_ec_step"


def rms_norm(x, g):
    xf = x.astype(jnp.float32)
    y = xf * lax.rsqrt(jnp.mean(xf * xf, axis=-1, keepdims=True) + NORM_EPS)
    return (y * g.astype(jnp.float32)).astype(x.dtype)


def modulation(cond, w, b):
    m = jnp.dot(jax.nn.silu(cond), w) + b
    return [t[:, None, :] for t in jnp.split(m, 6, axis=-1)]


def modulate(h, shift, scale):
    return h * (1 + scale) + shift


def split_heads(t, nh):
    b, n, _ = t.shape
    return t.reshape(b, n, nh, -1).transpose(0, 2, 1, 3)


def merge_heads(t):
    b, h, n, d = t.shape
    return t.transpose(0, 2, 1, 3).reshape(b, n, h * d)


def pool_mixer(u, w, scale):
    b, n, _ = u.shape
    ug = u.astype(jnp.float32).reshape(b, n, POOL_GROUPS, POOL_GROUP_DIM)
    cs = jnp.concatenate([jnp.zeros_like(ug[:, :1]), jnp.cumsum(ug, axis=1)], axis=1)
    t = np.arange(n)[:, None]
    win = np.array(POOL_WINDOWS)[None, :]
    start = np.clip(t - win // 2, 0, n)
    end = np.clip(t - win // 2 + win, 0, n)
    cnt = (end - start).astype(np.float32)
    gi = np.arange(POOL_GROUPS)[None, :]
    mean = (cs[:, end, gi] - cs[:, start, gi]) / cnt[None, :, :, None]
    pooled = (mean - ug).astype(u.dtype)
    out = jnp.einsum('bngc,gcd->bngd', pooled, w) * scale.reshape(POOL_GROUPS, POOL_GROUP_DIM)
    return out.reshape(b, n, POOL_WIDTH)


def ctx_attention(q, k, v):
    b, h, L, d = q.shape
    nb = L // CTX_QBLOCK
    qb = jnp.moveaxis(q.reshape(b, h, nb, CTX_QBLOCK, d), 2, 0)

    def block(qi):
        s = jnp.einsum('bhqd,bhkd->bhqk', qi, k).astype(jnp.float32) * (d ** -0.5)
        p = jax.nn.softmax(s, axis=-1).astype(v.dtype)
        return jnp.einsum('bhqk,bhkd->bhqd', p, v)

    o = lax.map(block, qb)
    return jnp.moveaxis(o, 0, 2).reshape(b, h, L, d)


def _na_column_blocks():
    ncb = GRID_W // NA_QBLOCK
    qcols = np.arange(GRID_W).reshape(ncb, NA_QBLOCK)
    kstart = np.clip(qcols[:, 0] - WIN_W // 2, 0, GRID_W - NA_KBLOCK)
    kcols = kstart[:, None] + np.arange(NA_KBLOCK)[None, :]
    wstart = np.clip(qcols - WIN_W // 2, 0, GRID_W - WIN_W)
    inside = (kcols[:, None, :] >= wstart[:, :, None]) & (kcols[:, None, :] < wstart[:, :, None] + WIN_W)
    rel_idx = np.clip(kcols[:, None, :] - qcols[:, :, None] + WIN_W - 1, 0, 2 * WIN_W - 2)
    return kcols, inside, rel_idx


def neighbourhood_attention(q, k, v, k_ctx, v_ctx, rpb):
    b, h, T, d = q.shape
    rows = T // GRID_W
    kh = min(WIN_H, rows)
    ncb = GRID_W // NA_QBLOCK
    kcols, inside, rel_idx = _na_column_blocks()
    qg = q.reshape(b, h, rows, ncb, NA_QBLOCK, d)
    kg = k.reshape(b, h, rows, GRID_W, d)
    vg = v.reshape(b, h, rows, GRID_W, d)
    col_bias = rpb[:, :, rel_idx]
    scale = d ** -0.5
    n_loc = kh * NA_KBLOCK

    def row(r):
        sr = jnp.clip(r - kh // 2, 0, rows - kh)
        qr = lax.dynamic_index_in_dim(qg, r, axis=2, keepdims=False)
        kb = lax.dynamic_slice_in_dim(kg, sr, kh, axis=2)[:, :, :, kcols]
        vb = lax.dynamic_slice_in_dim(vg, sr, kh, axis=2)[:, :, :, kcols]
        s_loc = jnp.einsum('bhjqd,bhrjkd->bhjqrk', qr, kb).astype(jnp.float32) * scale
        ridx = sr + jnp.arange(kh) - r + (WIN_H - 1)
        bias = jnp.transpose(jnp.take(col_bias, ridx, axis=1), (0, 2, 3, 1, 4))
        s_loc = jnp.where(inside[:, :, None, :], s_loc + bias.astype(jnp.float32), -jnp.inf)
        s_loc = s_loc.reshape(b, h, ncb, NA_QBLOCK, n_loc)
        s_ctx = jnp.einsum('bhjqd,bhcd->bhjqc', qr, k_ctx).astype(jnp.float32) * scale
        p = jax.nn.softmax(jnp.concatenate([s_loc, s_ctx], axis=-1), axis=-1).astype(v.dtype)
        p_loc = p[..., :n_loc].reshape(b, h, ncb, NA_QBLOCK, kh, NA_KBLOCK)
        o = (jnp.einsum('bhjqrk,bhrjkd->bhjqd', p_loc, vb)
             + jnp.einsum('bhjqc,bhcd->bhjqd', p[..., n_loc:], v_ctx))
        return o.reshape(b, h, GRID_W, d)

    o = lax.map(row, jnp.arange(rows))
    return jnp.transpose(o, (1, 2, 0, 3, 4)).reshape(b, h, T, d)


def ab_project(hn, w_in, q_norm, k_norm):
    proj = jnp.dot(hn, w_in)
    u, q, k, v = jnp.split(proj, [POOL_WIDTH, POOL_WIDTH + NA_WIDTH, POOL_WIDTH + 2 * NA_WIDTH], axis=-1)
    q = rms_norm(split_heads(q, NA_HEADS), q_norm)
    k = rms_norm(split_heads(k, NA_HEADS), k_norm)
    return u, q, k, split_heads(v, NA_HEADS)


def centred_depthwise_conv(x, w, bias):
    pad_l = SSD_CONV // 2
    pad_r = SSD_CONV - 1 - pad_l
    y = lax.conv_general_dilated(x, w[:, None, :], window_strides=(1,), padding=[(pad_l, pad_r)],
                                 dimension_numbers=('NWC', 'WIO', 'NWC'), feature_group_count=x.shape[-1])
    return y + bias


def ssd_scan(x, dt, a, bmat, cmat, init):
    bsz, l, h, p = x.shape
    g, n = bmat.shape[2], bmat.shape[3]
    e = h // g
    q = SSD_CHUNK
    nc = l // q
    xc = x.reshape(bsz, nc, q, g, e, p)
    dtc = dt.reshape(bsz, nc, q, g, e)
    bc = bmat.reshape(bsz, nc, q, g, n)
    cc = cmat.reshape(bsz, nc, q, g, n)
    a_cum = jnp.cumsum(dtc * a.reshape(g, e), axis=2)
    seg = a_cum[:, :, :, None] - a_cum[:, :, None, :]
    lower = np.tril(np.ones((q, q), dtype=bool))[:, :, None, None]
    decay = jnp.exp(jnp.where(lower, seg, -jnp.inf))
    xdt = xc * dtc[..., None]
    cb = jnp.einsum('bcign,bcjgn->bcijg', cc, bc)
    y_diag = jnp.einsum('bcijge,bcjgep->bcigep', cb[..., None] * decay, xdt)
    decay_to_end = jnp.exp(a_cum[:, :, -1:] - a_cum)
    states = jnp.einsum('bcqgn,bcqgep->bcgepn', bc, xdt * decay_to_end[..., None])
    chunk_decay = jnp.exp(a_cum[:, :, -1])

    def step(s, inp):
        st, dec = inp
        return s * dec[..., None, None] + st, s

    final, entering = lax.scan(step, init.reshape(bsz, g, e, p, n),
                               (jnp.moveaxis(states, 1, 0), jnp.moveaxis(chunk_decay, 1, 0)))
    y_off = jnp.einsum('bcqgn,cbgepn->bcqgep', cc, entering) * jnp.exp(a_cum)[..., None]
    return (y_diag + y_off).reshape(bsz, l, h, p), final.reshape(bsz, h, p, n)


def ssd_mixer(hn, init_f, init_b, w_in, conv_w, conv_b, a_log, dt_bias, d_skip, norm_g, w_out):
    b, l, _ = hn.shape
    f32 = jnp.float32
    proj = jnp.dot(hn, w_in)
    z, xbc, dt = jnp.split(proj, [SSD_INNER, SSD_INNER + SSD_CONV_DIM], axis=-1)
    xbc = jax.nn.silu(centred_depthwise_conv(xbc, conv_w, conv_b)).astype(f32)
    x, bm, cm = jnp.split(xbc, [SSD_INNER, SSD_INNER + SSD_GROUPS * SSD_STATE], axis=-1)
    x = x.reshape(b, l, SSD_HEADS, SSD_HEAD_DIM)
    bm = bm.reshape(b, l, SSD_GROUPS, SSD_STATE)
    cm = cm.reshape(b, l, SSD_GROUPS, SSD_STATE)
    dt = jax.nn.softplus(dt.astype(f32).reshape(b, l, 2, SSD_HEADS) + dt_bias.astype(f32))
    a = -jnp.exp(a_log.astype(f32))
    y_f, s_f = ssd_scan(x, dt[:, :, 0], a[0], bm, cm, init_f.astype(f32))
    y_b, s_b = ssd_scan(jnp.flip(x, 1), jnp.flip(dt[:, :, 1], 1), a[1], jnp.flip(bm, 1), jnp.flip(cm, 1),
                        init_b.astype(f32))
    y = y_f + jnp.flip(y_b, 1) + x * d_skip.astype(f32)[:, None]
    y = y.reshape(b, l, SSD_INNER) * jax.nn.silu(z.astype(f32))
    y = rms_norm(y, norm_g).astype(hn.dtype)
    return jnp.dot(y, w_out), s_f.astype(hn.dtype), s_b.astype(hn.dtype)


def expert_choice_ffn(x, router_w, w_gate, w_up, w_down):
    b, n, d = x.shape
    cap = EC_CAPACITY_FACTOR * n // N_EXPERTS
    aff = jax.nn.softmax(jnp.dot(x, router_w).astype(jnp.float32), axis=-1)
    gate, idx = lax.top_k(jnp.swapaxes(aff, 1, 2), cap)
    xe = jax.vmap(lambda xb, ib: xb[ib])(x, idx)
    hdn = jax.nn.silu(jnp.einsum('becd,edf->becf', xe, w_gate)) * jnp.einsum('becd,edf->becf', xe, w_up)
    ye = jnp.einsum('becf,efd->becd', hdn, w_down) * gate[..., None].astype(x.dtype)
    return jax.vmap(lambda yb, ib: jnp.zeros((n, d), yb.dtype).at[ib.reshape(-1)].add(yb.reshape(-1, d)))(ye, idx)


def setup_inputs(seed: int = 0) -> dict:
    key = jax.random.key(seed)
    ks = jax.random.split(key, 32)
    nrm = lambda k, shape, s: jax.random.normal(k, shape, jnp.float32) * s
    gain = lambda k, shape: 1.0 + 0.02 * jax.random.normal(k, shape, jnp.float32)
    dt0 = jnp.exp(jax.random.uniform(ks[27], (N_ODD, 2, SSD_HEADS), jnp.float32,
                                     float(np.log(1e-3)), float(np.log(1e-1))))
    return {
        "x_prompt": nrm(ks[0], (BATCH, SEQ, D_MODEL), 1.0),
        "x_sample": nrm(ks[1], (DEC_BATCH, DEC_SEQ, D_MODEL), 1.0),
        "cache_na_k": nrm(ks[2], (DEC_BATCH, N_EVEN, NA_HEADS, PAST_LEN, NA_HEAD_DIM), 1.0),
        "cache_na_v": nrm(ks[3], (DEC_BATCH, N_EVEN, NA_HEADS, PAST_LEN, NA_HEAD_DIM), 1.0),
        "state_ssd": nrm(ks[4], (DEC_BATCH, N_ODD, 2, SSD_HEADS, SSD_HEAD_DIM, SSD_STATE), 0.5),
        "c": nrm(ks[5], (DEC_BATCH, D_MODEL), 1.0),
        "c_ctx": nrm(ks[6], (D_MODEL,), 1.0),
        "ada_w": nrm(ks[7], (DEPTH, D_MODEL, 6 * D_MODEL), 0.5 * D_MODEL ** -0.5),
        "ada_b": nrm(ks[8], (DEPTH, 6 * D_MODEL), 0.01),
        "norm1_g": gain(ks[9], (DEPTH, D_MODEL)),
        "norm2_g": gain(ks[10], (DEPTH, D_MODEL)),
        "router_w": nrm(ks[11], (DEPTH, D_MODEL, N_EXPERTS), D_MODEL ** -0.5),
        "exp_w_gate": nrm(ks[12], (DEPTH, N_EXPERTS, D_MODEL, EXPERT_FF), D_MODEL ** -0.5),
        "exp_w_up": nrm(ks[13], (DEPTH, N_EXPERTS, D_MODEL, EXPERT_FF), D_MODEL ** -0.5),
        "exp_w_down": nrm(ks[14], (DEPTH, N_EXPERTS, EXPERT_FF, D_MODEL), EXPERT_FF ** -0.5),
        "ab_w_in": nrm(ks[15], (N_EVEN, D_MODEL, POOL_WIDTH + 3 * NA_WIDTH), D_MODEL ** -0.5),
        "pool_w": nrm(ks[16], (N_EVEN, POOL_GROUPS, POOL_GROUP_DIM, POOL_GROUP_DIM), POOL_GROUP_DIM ** -0.5),
        "pool_scale": gain(ks[17], (N_EVEN, POOL_WIDTH)),
        "na_q_norm": gain(ks[18], (N_EVEN, NA_HEAD_DIM)),
        "na_k_norm": gain(ks[19], (N_EVEN, NA_HEAD_DIM)),
        "na_rpb": nrm(ks[20], (N_EVEN, NA_HEADS, 2 * WIN_H - 1, 2 * WIN_W - 1), 0.2),
        "ab_w_out": nrm(ks[21], (N_EVEN, POOL_WIDTH + NA_WIDTH, D_MODEL), D_MODEL ** -0.5),
        "ssd_w_in": nrm(ks[22], (N_ODD, D_MODEL, SSD_IN_DIM), D_MODEL ** -0.5),
        "ssd_conv_w": nrm(ks[23], (N_ODD, SSD_CONV, SSD_CONV_DIM), SSD_CONV ** -0.5),
        "ssd_conv_b": nrm(ks[24], (N_ODD, SSD_CONV_DIM), 0.01),
        "ssd_a_log": jnp.log(jax.random.uniform(ks[25], (N_ODD, 2, SSD_HEADS), jnp.float32, 1.0, 16.0)),
        "ssd_dt_bias": dt0 + jnp.log(-jnp.expm1(-dt0)),
        "ssd_d": gain(ks[26], (N_ODD, SSD_HEADS)),
        "ssd_norm_g": gain(ks[28], (N_ODD, SSD_INNER)),
        "ssd_w_out": nrm(ks[29], (N_ODD, SSD_INNER, D_MODEL), SSD_INNER ** -0.5),
    }


def reference(x_prompt, x_sample, cache_na_k, cache_na_v, state_ssd, c, c_ctx, ada_w, ada_b, norm1_g, norm2_g,
              router_w, exp_w_gate, exp_w_up, exp_w_down, ab_w_in, pool_w, pool_scale, na_q_norm, na_k_norm,
              na_rpb, ab_w_out, ssd_w_in, ssd_conv_w, ssd_conv_b, ssd_a_log, ssd_dt_bias, ssd_d, ssd_norm_g,
              ssd_w_out):
    xp, xs = x_prompt, x_sample
    new_k, new_v, new_s = [], [], []
    for layer in range(DEPTH):
        i = layer // 2
        mp = modulation(c_ctx[None, :], ada_w[layer], ada_b[layer])
        ms = modulation(c, ada_w[layer], ada_b[layer])
        hp = modulate(rms_norm(xp, norm1_g[layer]), mp[0], mp[1])
        hs = modulate(rms_norm(xs, norm1_g[layer]), ms[0], ms[1])
        if layer % 2 == 0:
            up, qp, kp, vp = ab_project(hp, ab_w_in[i], na_q_norm[i], na_k_norm[i])
            op = jnp.dot(jnp.concatenate([pool_mixer(up, pool_w[i], pool_scale[i]),
                                          merge_heads(ctx_attention(qp, kp, vp))], axis=-1), ab_w_out[i])
            us, qs, ks_, vs = ab_project(hs, ab_w_in[i], na_q_norm[i], na_k_norm[i])
            att = neighbourhood_attention(qs, ks_, vs, cache_na_k[:, i], cache_na_v[:, i], na_rpb[i])
            os_ = jnp.dot(jnp.concatenate([pool_mixer(us, pool_w[i], pool_scale[i]), merge_heads(att)], axis=-1),
                          ab_w_out[i])
            new_k.append(kp)
            new_v.append(vp)
        else:
            zero = jnp.zeros((xp.shape[0], SSD_HEADS, SSD_HEAD_DIM, SSD_STATE), xp.dtype)
            op, sf, sb = ssd_mixer(hp, zero, zero, ssd_w_in[i], ssd_conv_w[i], ssd_conv_b[i], ssd_a_log[i],
                                   ssd_dt_bias[i], ssd_d[i], ssd_norm_g[i], ssd_w_out[i])
            os_, _, _ = ssd_mixer(hs, state_ssd[:, i, 0], state_ssd[:, i, 1], ssd_w_in[i], ssd_conv_w[i],
                                  ssd_conv_b[i], ssd_a_log[i], ssd_dt_bias[i], ssd_d[i], ssd_norm_g[i], ssd_w_out[i])
            new_s.append(jnp.stack([sf, sb], axis=1))
        xp = xp + mp[2] * op
        xs = xs + ms[2] * os_
        hp = modulate(rms_norm(xp, norm2_g[layer]), mp[3], mp[4])
        hs = modulate(rms_norm(xs, norm2_g[layer]), ms[3], ms[4])
        xp = xp + mp[5] * expert_choice_ffn(hp, router_w[layer], exp_w_gate[layer], exp_w_up[layer], exp_w_down[layer])
        xs = xs + ms[5] * expert_choice_ffn(hs, router_w[layer], exp_w_gate[layer], exp_w_up[layer], exp_w_down[layer])
    new_cache_na_k = jnp.stack(new_k, axis=1)
    new_cache_na_v = jnp.stack(new_v, axis=1)
    new_state_ssd = jnp.stack(new_s, axis=1)
    return (xp, xs, new_cache_na_k, new_cache_na_v, new_state_ssd)
```

```python
import functools

import numpy as np
import jax
import jax.numpy as jnp
from jax import lax
from jax.experimental import pallas as pl
from jax.experimental.pallas import tpu as pltpu

F32 = jnp.float32
BF16 = jnp.bfloat16

D_MODEL = 2048
GRID_W = 64
POOL_WIDTH = 1024
POOL_GROUPS = 4
POOL_GROUP_DIM = 256
POOL_WINDOWS = (2, 4, 8, 16)
NA_WIDTH = 1024
NA_HEAD_DIM = 128
NA_HEADS = 8
WIN_H = 8
WIN_W = 16
SSD_INNER = 4096
SSD_HEAD_DIM = 64
SSD_HEADS = 64
SSD_GROUPS = 8
SSD_GROUP_HEADS = SSD_HEADS // SSD_GROUPS
SSD_GROUP_WIDTH = SSD_GROUP_HEADS * SSD_HEAD_DIM
SSD_STATE = 128
SSD_CONV = 4
SSD_CHUNK = 128
SSD_BC_WIDTH = SSD_GROUPS * SSD_STATE
SSD_MAIN_DIM = 2 * SSD_INNER + 2 * SSD_BC_WIDTH
N_EXPERTS = 16
EXPERT_FF = 1024
EC_CAPACITY_FACTOR = 2
NORM_EPS = 1e-6
N_MOD = 6
MOD_ROWS = 16

VMEM_LIMIT_BYTES = 56 * 1024 * 1024
ROW_TILE = 256


def _cparams(*sem):
    return pltpu.CompilerParams(dimension_semantics=sem, vmem_limit_bytes=VMEM_LIMIT_BYTES)


def _silu(x):
    return x * (1.0 / (1.0 + jnp.exp(-x)))


def _split2(x):
    hi = x.astype(BF16)
    lo = (x - hi.astype(F32)).astype(BF16)
    return hi, lo


def _split3(x):
    hi = x.astype(BF16)
    r = x - hi.astype(F32)
    mid = r.astype(BF16)
    lo = (r - mid.astype(F32)).astype(BF16)
    return hi, mid, lo


def _dot(a, b):
    return jnp.dot(a, b, preferred_element_type=F32)


def _dot_nt(a, b):
    return lax.dot_general(a, b, (((1,), (1,)), ((), ())), preferred_element_type=F32)


def _dot_f32_lhs(x, m):
    hi, mid, lo = _split3(x)
    return _dot(hi, m) + _dot(mid, m) + _dot(lo, m)


def _dot_f32_rhs(m, x):
    hi, mid, lo = _split3(x)
    return _dot(m, hi) + _dot(m, mid) + _dot(m, lo)


def _dot_split(a, b):
    a_hi, a_lo = _split2(a)
    b_hi, b_lo = _split2(b)
    return _dot(a_hi, b_hi) + _dot(a_lo, b_hi) + _dot(a_hi, b_lo)


def _mod_kernel(c_ref, w_ref, b_ref, o_ref):
    s = _silu(c_ref[...])
    o_ref[...] = _dot_split(s, w_ref[...]) + b_ref[...]


def _modulation(cond, ada_w, ada_b, layer):
    tn = 768
    n = N_MOD * D_MODEL
    ada_b3 = ada_b.reshape(ada_b.shape[0], 1, n)
    return pl.pallas_call(
        _mod_kernel,
        grid=(n // tn,),
        in_specs=[
            pl.BlockSpec((MOD_ROWS, D_MODEL), lambda j: (0, 0)),
            pl.BlockSpec((None, D_MODEL, tn), lambda j: (layer, 0, j)),
            pl.BlockSpec((None, 1, tn), lambda j: (layer, 0, j)),
        ],
        out_specs=pl.BlockSpec((MOD_ROWS, tn), lambda j: (0, j)),
        out_shape=jax.ShapeDtypeStruct((MOD_ROWS, n), F32),
        compiler_params=_cparams("parallel"),
        name="modulation",
    )(cond, ada_w, ada_b3)


def _mod_spec(which, rows_per_cond, first_row, tm):
    return pl.BlockSpec((None, None, 1, D_MODEL),
                        lambda i, *_: (first_row + (i * tm) // rows_per_cond, which, 0, 0))


def _norm_mod_kernel(x_ref, g_ref, sh_ref, sc_ref, *rest, with_router):
    x = x_ref[...]
    y = x * lax.rsqrt(jnp.mean(x * x, axis=-1, keepdims=True) + NORM_EPS) * g_ref[...]
    h = y * (1.0 + sc_ref[...]) + sh_ref[...]
    if with_router:
        rw_ref, h_ref, lg_ref = rest
        lg_ref[...] = _dot_split(h, rw_ref[...])
    else:
        (h_ref,) = rest
    h_ref[...] = h.astype(BF16)


def _norm_mod(x, g, mod4, layer, shift_idx, scale_idx, rows_per_cond, first_row, router_w=None):
    m = x.shape[0]
    tm = ROW_TILE
    g3 = g.reshape(g.shape[0], 1, D_MODEL)
    in_specs = [
        pl.BlockSpec((tm, D_MODEL), lambda i: (i, 0)),
        pl.BlockSpec((None, 1, D_MODEL), lambda i: (layer, 0, 0)),
        _mod_spec(shift_idx, rows_per_cond, first_row, tm),
        _mod_spec(scale_idx, rows_per_cond, first_row, tm),
    ]
    args = [x, g3, mod4, mod4]
    out_specs = [pl.BlockSpec((tm, D_MODEL), lambda i: (i, 0))]
    out_shape = [jax.ShapeDtypeStruct((m, D_MODEL), BF16)]
    if router_w is not None:
        in_specs.append(pl.BlockSpec((None, D_MODEL, N_EXPERTS), lambda i: (layer, 0, 0)))
        args.append(router_w)
        out_specs.append(pl.BlockSpec((tm, N_EXPERTS), lambda i: (i, 0)))
        out_shape.append(jax.ShapeDtypeStruct((m, N_EXPERTS), F32))
    return pl.pallas_call(
        functools.partial(_norm_mod_kernel, with_router=router_w is not None),
        grid=(m // tm,),
        in_specs=in_specs,
        out_specs=out_specs,
        out_shape=out_shape,
        compiler_params=_cparams("parallel"),
        name="norm_mod",
    )(*args)


def _matmul_kernel(*refs, n_a, k_sizes, with_res):
    a_refs = refs[:n_a]
    w_ref = refs[n_a]
    if with_res:
        res_ref, gate_ref, o_ref = refs[n_a + 1:]
    else:
        (o_ref,) = refs[n_a + 1:]
    w = w_ref[...].astype(BF16)
    acc = None
    k0 = 0
    for a_ref, k in zip(a_refs, k_sizes):
        part = _dot(a_ref[...], w[k0:k0 + k])
        acc = part if acc is None else acc + part
        k0 += k
    if with_res:
        acc = res_ref[...] + gate_ref[...] * acc
    o_ref[...] = acc.astype(o_ref.dtype)


def _matmul(a_list, w, layer, *, tm, tn, n_blocks, col_block0=0, out_dtype=F32,
            res=None, mod4=None, gate_idx=None, rows_per_cond=None, first_row=None):
    m = a_list[0].shape[0]
    k_sizes = tuple(a.shape[1] for a in a_list)
    k_total = sum(k_sizes)
    in_specs = [pl.BlockSpec((tm, k), lambda i, j: (i, 0)) for k in k_sizes]
    in_specs.append(pl.BlockSpec((None, k_total, tn), lambda i, j: (layer, 0, col_block0 + j)))
    args = list(a_list) + [w]
    if res is not None:
        in_specs.append(pl.BlockSpec((tm, tn), lambda i, j: (i, j)))
        in_specs.append(pl.BlockSpec((None, None, 1, tn),
                                     lambda i, j: (first_row + (i * tm) // rows_per_cond, gate_idx, 0, j)))
        args += [res, mod4]
    return pl.pallas_call(
        functools.partial(_matmul_kernel, n_a=len(a_list), k_sizes=k_sizes, with_res=res is not None),
        grid=(m // tm, n_blocks),
        in_specs=in_specs,
        out_specs=pl.BlockSpec((tm, tn), lambda i, j: (i, j)),
        out_shape=jax.ShapeDtypeStruct((m, n_blocks * tn), out_dtype),
        compiler_params=_cparams("parallel", "arbitrary"),
        name="matmul",
    )(*args)


def _pool_kernel(u_ref, w_ref, s_ref, o_ref, *, n):
    g = pl.program_id(1)
    u = u_ref[...]
    t = lax.broadcasted_iota(jnp.int32, (n, 1), 0)
    for gi, win in enumerate(POOL_WINDOWS):
        @pl.when(g == gi)
        def _(win=win):
            half = win // 2
            total = u
            for d in range(-half, win - half):
                if d == 0:
                    continue
                shifted = pltpu.roll(u, (-d) % n, axis=0)
                valid = jnp.logical_and(t + d >= 0, t + d < n)
                total = total + jnp.where(valid, shifted, 0.0)
            cnt = (jnp.minimum(t - half + win, n) - jnp.maximum(t - half, 0)).astype(F32)
            pooled = total / cnt - u
            out = _dot(pooled.astype(BF16), w_ref[...].astype(BF16)) * s_ref[...]
            o_ref[...] = out.astype(o_ref.dtype)


def _pool_mixer(proj3, pool_w, pool_scale, i_even):
    b, n, _ = proj3.shape
    scale4 = pool_scale.reshape(pool_scale.shape[0], POOL_GROUPS, 1, POOL_GROUP_DIM)
    return pl.pallas_call(
        functools.partial(_pool_kernel, n=n),
        grid=(b, POOL_GROUPS),
        in_specs=[
            pl.BlockSpec((None, n, POOL_GROUP_DIM), lambda bi, g: (bi, 0, g)),
            pl.BlockSpec((None, None, POOL_GROUP_DIM, POOL_GROUP_DIM), lambda bi, g: (i_even, g, 0, 0)),
            pl.BlockSpec((None, None, 1, POOL_GROUP_DIM), lambda bi, g: (i_even, g, 0, 0)),
        ],
        out_specs=pl.BlockSpec((None, n, POOL_GROUP_DIM), lambda bi, g: (bi, 0, g)),
        out_shape=jax.ShapeDtypeStruct((b, n, POOL_WIDTH), BF16),
        compiler_params=_cparams("parallel", "parallel"),
        name="pool_mixer",
    )(proj3, pool_w, scale4)


def _head_norm(x, g):
    return x * lax.rsqrt(jnp.mean(x * x, axis=-1, keepdims=True) + NORM_EPS) * g


_Q_COL0 = POOL_WIDTH // NA_HEAD_DIM
_K_COL0 = (POOL_WIDTH + NA_WIDTH) // NA_HEAD_DIM
_V_COL0 = (POOL_WIDTH + 2 * NA_WIDTH) // NA_HEAD_DIM


def _ctx_attn_kernel(q_ref, k_ref, v_ref, qg_ref, kg_ref, o_ref, kn_ref, vn_ref):
    qn = _head_norm(q_ref[...], qg_ref[...])
    kn = _head_norm(k_ref[...], kg_ref[...])
    v = v_ref[...]
    kn_ref[...] = kn
    vn_ref[...] = v
    s = _dot_nt(qn.astype(BF16), kn.astype(BF16)) * (NA_HEAD_DIM ** -0.5)
    m = jnp.max(s, axis=-1, keepdims=True)
    p = jnp.exp(s - m)
    denom = jnp.sum(p, axis=-1, keepdims=True)
    o = _dot(p.astype(BF16), v.astype(BF16)) / denom
    o_ref[...] = o.astype(o_ref.dtype)


def _ctx_attention(proj3, q_norm, k_norm, i_even):
    b, n, _ = proj3.shape
    qg = q_norm.reshape(q_norm.shape[0], 1, NA_HEAD_DIM)
    kg = k_norm.reshape(k_norm.shape[0], 1, NA_HEAD_DIM)
    head_blk = lambda col0: pl.BlockSpec((None, n, NA_HEAD_DIM), lambda bi, h: (bi, 0, col0 + h))
    gain = pl.BlockSpec((None, 1, NA_HEAD_DIM), lambda bi, h: (i_even, 0, 0))
    cache = pl.BlockSpec((None, None, None, n, NA_HEAD_DIM), lambda bi, h: (bi, 0, h, 0, 0))
    return pl.pallas_call(
        _ctx_attn_kernel,
        grid=(b, NA_HEADS),
        in_specs=[head_blk(_Q_COL0), head_blk(_K_COL0), head_blk(_V_COL0), gain, gain],
        out_specs=[pl.BlockSpec((None, n, NA_HEAD_DIM), lambda bi, h: (bi, 0, h)), cache, cache],
        out_shape=[
            jax.ShapeDtypeStruct((b, n, NA_WIDTH), BF16),
            jax.ShapeDtypeStruct((b, 1, NA_HEADS, n, NA_HEAD_DIM), F32),
            jax.ShapeDtypeStruct((b, 1, NA_HEADS, n, NA_HEAD_DIM), F32),
        ],
        compiler_params=_cparams("parallel", "parallel"),
        name="ctx_attention",
    )(proj3, proj3, proj3, qg, kg)


def _na_bias_table(rpb, rows):
    kh = min(WIN_H, rows)
    col = np.arange(GRID_W)[:, None]
    kc = np.arange(GRID_W)[None, :]
    wstart = np.clip(col - WIN_W // 2, 0, GRID_W - WIN_W)
    inside = (kc >= wstart) & (kc < wstart + WIN_W)
    rel = np.clip(kc - col + WIN_W - 1, 0, 2 * WIN_W - 2)
    d0 = np.arange(kh)[:, None]
    off = np.arange(kh)[None, :]
    ridx = np.clip(d0 + off, 0, 2 * WIN_H - 2)
    tab = rpb[:, ridx][:, :, :, rel]
    tab = jnp.where(inside[None, None, None], tab, -jnp.inf)
    tab = jnp.transpose(tab, (0, 1, 3, 2, 4))
    return tab.reshape(rpb.shape[0], kh, GRID_W, kh * GRID_W)


def _na_kernel(q_ref, k_ref, v_ref, kc_ref, vc_ref, qg_ref, kg_ref, bias_ref, o_ref,
               qs_ref, ks_ref, vs_ref, *, rows, kh):
    qs_ref[...] = _head_norm(q_ref[...], qg_ref[...]).astype(BF16)
    ks_ref[...] = _head_norm(k_ref[...], kg_ref[...]).astype(BF16)
    vs_ref[...] = v_ref[...].astype(BF16)
    k_ctx = kc_ref[...].astype(BF16)
    v_ctx = vc_ref[...].astype(BF16)
    scale = NA_HEAD_DIM ** -0.5
    n_loc = kh * GRID_W

    def row(r, carry):
        sr = jnp.clip(r - kh // 2, 0, rows - kh)
        d0 = sr - r + (WIN_H - 1)
        q_r = qs_ref[pl.ds(pl.multiple_of(r * GRID_W, GRID_W), GRID_W), :]
        kstart = pl.multiple_of(sr * GRID_W, GRID_W)
        k_blk = ks_ref[pl.ds(kstart, n_loc), :]
        v_blk = vs_ref[pl.ds(kstart, n_loc), :]
        s_loc = _dot_nt(q_r, k_blk) * scale + bias_ref[d0]
        s_ctx = _dot_nt(q_r, k_ctx) * scale
        m = jnp.maximum(jnp.max(s_loc, axis=-1, keepdims=True), jnp.max(s_ctx, axis=-1, keepdims=True))
        p_loc = jnp.exp(s_loc - m)
        p_ctx = jnp.exp(s_ctx - m)
        denom = jnp.sum(p_loc, axis=-1, keepdims=True) + jnp.sum(p_ctx, axis=-1, keepdims=True)
        o = (_dot(p_loc.astype(BF16), v_blk) + _dot(p_ctx.astype(BF16), v_ctx)) / denom
        o_ref[pl.ds(pl.multiple_of(r * GRID_W, GRID_W), GRID_W), :] = o.astype(o_ref.dtype)
        return carry

    lax.fori_loop(0, rows, row, 0)


def _neighbourhood_attention(proj3, cache_k, cache_v, q_norm, k_norm, rpb, i_even):
    b, t, _ = proj3.shape
    rows = t // GRID_W
    kh = min(WIN_H, rows)
    n_ctx = cache_k.shape[3]
    bias = _na_bias_table(rpb[i_even], rows)
    qg = q_norm.reshape(q_norm.shape[0], 1, NA_HEAD_DIM)
    kg = k_norm.reshape(k_norm.shape[0], 1, NA_HEAD_DIM)
    head_blk = lambda col0: pl.BlockSpec((None, t, NA_HEAD_DIM), lambda bi, h: (bi, 0, col0 + h))
    gain = pl.BlockSpec((None, 1, NA_HEAD_DIM), lambda bi, h: (i_even, 0, 0))
    cache = pl.BlockSpec((None, None, None, n_ctx, NA_HEAD_DIM), lambda bi, h: (bi, i_even, h, 0, 0))
    return pl.pallas_call(
        functools.partial(_na_kernel, rows=rows, kh=kh),
        grid=(b, NA_HEADS),
        in_specs=[head_blk(_Q_COL0), head_blk(_K_COL0), head_blk(_V_COL0), cache, cache, gain, gain,
                  pl.BlockSpec((None, kh, GRID_W, kh * GRID_W), lambda bi, h: (h, 0, 0, 0))],
        out_specs=pl.BlockSpec((None, t, NA_HEAD_DIM), lambda bi, h: (bi, 0, h)),
        out_shape=jax.ShapeDtypeStruct((b, t, NA_WIDTH), BF16),
        scratch_shapes=[pltpu.VMEM((t, NA_HEAD_DIM), BF16)] * 3,
        compiler_params=_cparams("parallel", "parallel"),
        name="neighbourhood_attention",
    )(proj3, proj3, proj3, cache_k, cache_v, qg, kg, bias)


def _softplus(x):
    return jnp.maximum(x, 0.0) + jnp.log1p(jnp.exp(-jnp.abs(x)))


def _ssd_kernel(*refs, l, with_init):
    (x_ref, b_ref, c_ref, dt_ref, cwx_ref, cwb_ref, cwc_ref, cbx_ref, cbb_ref, cbc_ref,
     alog_ref, dtbias_ref, dsk_ref) = refs[:13]
    if with_init:
        init_ref = refs[13]
        y_ref, sfin_ref, xs_ref, bs_ref, cs_ref, st_ref, cumt_ref = refs[14:]
    else:
        y_ref, sfin_ref, xs_ref, bs_ref, cs_ref, st_ref, cumt_ref = refs[13:]
    g = pl.program_id(1)
    q = SSD_CHUNK
    nc = l // q
    gw = SSD_GROUP_WIDTH
    gh = SSD_GROUP_HEADS
    pad_l = SSD_CONV // 2
    halo = 8

    def conv_chunk(c, carry):
        t0 = pl.multiple_of(c * q, q)
        lo_start = pl.multiple_of(jnp.maximum(t0 - halo, 0), halo)
        hi_start = pl.multiple_of(jnp.minimum(t0 + q, l - halo), halo)
        for src, w_ref, bias_ref, dst in ((x_ref, cwx_ref, cbx_ref, xs_ref),
                                          (b_ref, cwb_ref, cbb_ref, bs_ref),
                                          (c_ref, cwc_ref, cbc_ref, cs_ref)):
            lo = jnp.where(c > 0, src[pl.ds(lo_start, halo), :], 0.0)
            hi = jnp.where(c < nc - 1, src[pl.ds(hi_start, halo), :], 0.0)
            win = jnp.concatenate([lo, src[pl.ds(t0, q), :], hi], axis=0)
            w = w_ref[...]
            acc = bias_ref[...]
            for j in range(SSD_CONV):
                s0 = halo + j - pad_l
                acc = acc + win[s0:s0 + q, :] * w[j:j + 1, :]
            dst[pl.ds(t0, q), :] = _silu(acc)
        return carry

    lax.fori_loop(0, nc, conv_chunk, 0)

    ii = lax.broadcasted_iota(jnp.int32, (q, q), 0)
    jj = lax.broadcasted_iota(jnp.int32, (q, q), 1)
    lane = lax.broadcasted_iota(jnp.int32, (1, gw), 1)
    half_mask = [((lane % (2 * SSD_HEAD_DIM)) // SSD_HEAD_DIM) == s for s in range(2)]
    n_dt = 2 * SSD_HEADS
    dt_all_bias = dtbias_ref[...]
    a_all = -jnp.exp(alog_ref[...])

    def direction(reverse, d):
        if reverse:
            keep = jj >= ii
            edge = 0
        else:
            keep = jj <= ii
            edge = q - 1
        tri = keep.astype(BF16)
        ch0 = d * SSD_HEADS + g * gh
        ch = lax.broadcasted_iota(jnp.int32, (n_dt, gw), 0)
        to_lanes = (ch == ch0 + lax.broadcasted_iota(jnp.int32, (n_dt, gw), 1) // SSD_HEAD_DIM).astype(BF16)
        ch = lax.broadcasted_iota(jnp.int32, (n_dt, gh * q), 0)
        to_cols = (ch == ch0 + lax.broadcasted_iota(jnp.int32, (n_dt, gh * q), 1) // q).astype(BF16)

        def chunk(step, carry):
            c = (nc - 1 - step) if reverse else step
            t0 = pl.multiple_of(c * q, q)
            xc = xs_ref[pl.ds(t0, q), :]
            bc = bs_ref[pl.ds(t0, q), :]
            cc = cs_ref[pl.ds(t0, q), :]
            dt = _softplus(dt_ref[pl.ds(t0, q), :] + dt_all_bias)
            cum = _dot_f32_rhs(tri, dt * a_all)
            cumt_ref[...] = jnp.transpose(cum)
            cum_parts = _split3(cum)
            cum_x = sum(_dot(p, to_lanes) for p in cum_parts)
            cum_cols = sum(_dot(p, to_cols) for p in cum_parts)
            dt_x = _dot_f32_lhs(dt, to_lanes)
            cum_edge = cum_x[edge:edge + 1, :]
            xdt = xc * dt_x
            cb = _dot_nt(cc.astype(BF16), bc.astype(BF16))
            xdt_lo = jnp.where(half_mask[0], xdt, 0.0).astype(BF16)
            xdt_hi = jnp.where(half_mask[1], xdt, 0.0).astype(BF16)
            pieces = []
            for pair in range(gh // 2):
                ls = []
                for e in (2 * pair, 2 * pair + 1):
                    seg = cum_cols[:, e * q:(e + 1) * q] - cumt_ref[pl.ds(ch0 + e, 1), :]
                    ls.append((cb * jnp.exp(jnp.where(keep, seg, -jnp.inf))).astype(BF16))
                lcat = jnp.concatenate(ls, axis=1)
                c0 = pair * 2 * SSD_HEAD_DIM
                rhs = jnp.concatenate([xdt_lo[:, c0:c0 + 2 * SSD_HEAD_DIM],
                                       xdt_hi[:, c0:c0 + 2 * SSD_HEAD_DIM]], axis=0)
                pieces.append(_dot(lcat, rhs))
            y_diag = jnp.concatenate(pieces, axis=1)
            st = st_ref[...]
            y_off = _dot(cc.astype(BF16), st.astype(BF16)) * jnp.exp(cum_x)
            y = y_diag + y_off
            if reverse:
                y_ref[pl.ds(t0, q), :] = y_ref[pl.ds(t0, q), :] + y
            else:
                y_ref[pl.ds(t0, q), :] = y + xc * dsk_ref[...]
            contrib = (xdt * jnp.exp(cum_edge - cum_x)).astype(BF16)
            bct = jnp.transpose(bc).astype(BF16)
            st_ref[...] = st * jnp.exp(cum_edge) + _dot(bct, contrib)
            return carry

        if with_init:
            st_ref[...] = jnp.transpose(init_ref[d].reshape(gw, SSD_STATE))
        else:
            st_ref[...] = jnp.zeros_like(st_ref)
        lax.fori_loop(0, nc, chunk, 0)
        sfin_ref[d] = jnp.transpose(st_ref[...]).reshape(gh, SSD_HEAD_DIM, SSD_STATE)

    direction(False, 0)
    direction(True, 1)


def _ssd_scan(proj3, dt_raw3, conv_w, conv_b, a_log, dt_bias, d_skip, i_odd, init=None):
    b, l, _ = proj3.shape
    gh, gw = SSD_GROUP_HEADS, SSD_GROUP_WIDTH
    n_dt = 2 * SSD_HEADS
    a_log3 = a_log.reshape(a_log.shape[0], 1, n_dt)
    dt_bias3 = dt_bias.reshape(dt_bias.shape[0], 1, n_dt)
    dsk = jnp.repeat(d_skip[i_odd].astype(F32), SSD_HEAD_DIM).reshape(1, SSD_INNER)
    conv_b3 = conv_b.reshape(conv_b.shape[0], 1, conv_b.shape[1])

    x_col0 = SSD_INNER // gw
    b_col0 = (2 * SSD_INNER) // SSD_STATE
    c_col0 = (2 * SSD_INNER + SSD_BC_WIDTH) // SSD_STATE
    cwx_col0 = 0
    cwb_col0 = SSD_INNER // SSD_STATE
    cwc_col0 = (SSD_INNER + SSD_BC_WIDTH) // SSD_STATE

    in_specs = [
        pl.BlockSpec((None, l, gw), lambda bi, g: (bi, 0, x_col0 + g)),
        pl.BlockSpec((None, l, SSD_STATE), lambda bi, g: (bi, 0, b_col0 + g)),
        pl.BlockSpec((None, l, SSD_STATE), lambda bi, g: (bi, 0, c_col0 + g)),
        pl.BlockSpec((None, l, n_dt), lambda bi, g: (bi, 0, 0)),
        pl.BlockSpec((None, SSD_CONV, gw), lambda bi, g: (i_odd, 0, cwx_col0 + g)),
        pl.BlockSpec((None, SSD_CONV, SSD_STATE), lambda bi, g: (i_odd, 0, cwb_col0 + g)),
        pl.BlockSpec((None, SSD_CONV, SSD_STATE), lambda bi, g: (i_odd, 0, cwc_col0 + g)),
        pl.BlockSpec((None, 1, gw), lambda bi, g: (i_odd, 0, cwx_col0 + g)),
        pl.BlockSpec((None, 1, SSD_STATE), lambda bi, g: (i_odd, 0, cwb_col0 + g)),
        pl.BlockSpec((None, 1, SSD_STATE), lambda bi, g: (i_odd, 0, cwc_col0 + g)),
        pl.BlockSpec((None, 1, n_dt), lambda bi, g: (i_odd, 0, 0)),
        pl.BlockSpec((None, 1, n_dt), lambda bi, g: (i_odd, 0, 0)),
        pl.BlockSpec((1, gw), lambda bi, g: (0, g)),
    ]
    args = [proj3, proj3, proj3, dt_raw3, conv_w, conv_w, conv_w,
            conv_b3, conv_b3, conv_b3, a_log3, dt_bias3, dsk]
    if init is not None:
        in_specs.append(pl.BlockSpec((None, None, 2, gh, SSD_HEAD_DIM, SSD_STATE),
                                     lambda bi, g: (bi, i_odd, 0, g, 0, 0)))
        args.append(init)
    return pl.pallas_call(
        functools.partial(_ssd_kernel, l=l, with_init=init is not None),
        grid=(b, SSD_GROUPS),
        in_specs=in_specs,
        out_specs=[
            pl.BlockSpec((None, l, gw), lambda bi, g: (bi, 0, g)),
            pl.BlockSpec((None, 2, gh, SSD_HEAD_DIM, SSD_STATE), lambda bi, g: (bi, 0, g, 0, 0)),
        ],
        out_shape=[
            jax.ShapeDtypeStruct((b, l, SSD_INNER), F32),
            jax.ShapeDtypeStruct((b, 2, SSD_HEADS, SSD_HEAD_DIM, SSD_STATE), F32),
        ],
        scratch_shapes=[
            pltpu.VMEM((l, gw), F32),
            pltpu.VMEM((l, SSD_STATE), F32),
            pltpu.VMEM((l, SSD_STATE), F32),
            pltpu.VMEM((SSD_STATE, gw), F32),
            pltpu.VMEM((n_dt, SSD_CHUNK), F32),
        ],
        compiler_params=_cparams("parallel", "parallel"),
        name="ssd_scan",
    )(*args)


def _gate_norm_kernel(y_ref, z_ref, g_ref, o_ref):
    y = y_ref[...] * _silu(z_ref[...])
    o = y * lax.rsqrt(jnp.mean(y * y, axis=-1, keepdims=True) + NORM_EPS) * g_ref[...]
    o_ref[...] = o.astype(o_ref.dtype)


def _gate_norm(y, proj, norm_g, i_odd):
    m = y.shape[0]
    tm = ROW_TILE
    g3 = norm_g.reshape(norm_g.shape[0], 1, SSD_INNER)
    return pl.pallas_call(
        _gate_norm_kernel,
        grid=(m // tm,),
        in_specs=[
            pl.BlockSpec((tm, SSD_INNER), lambda i: (i, 0)),
            pl.BlockSpec((tm, SSD_INNER), lambda i: (i, 0)),
            pl.BlockSpec((None, 1, SSD_INNER), lambda i: (i_odd, 0, 0)),
        ],
        out_specs=pl.BlockSpec((tm, SSD_INNER), lambda i: (i, 0)),
        out_shape=jax.ShapeDtypeStruct((m, SSD_INNER), BF16),
        compiler_params=_cparams("parallel"),
        name="gate_norm",
    )(y, proj, g3)


def _expert_kernel(x_ref, wg_ref, wu_ref, wd_ref, gate_ref, o_ref):
    f = pl.program_id(2)
    x = x_ref[...]
    hg = _dot(x, wg_ref[...].astype(BF16))
    hu = _dot(x, wu_ref[...].astype(BF16))
    hdn = (_silu(hg) * hu).astype(BF16)
    part = _dot(hdn, wd_ref[...].astype(BF16))

    @pl.when(f == 0)
    def _():
        o_ref[...] = part

    @pl.when(f > 0)
    def _():
        o_ref[...] = o_ref[...] + part

    @pl.when(f == pl.num_programs(2) - 1)
    def _():
        o_ref[...] = o_ref[...] * gate_ref[...]


def _expert_ffn(xe, gate, w_gate, w_up, w_down, layer):
    e, r, _ = xe.shape
    tr = min(r, 1024)
    tf = 256
    return pl.pallas_call(
        _expert_kernel,
        grid=(e, r // tr, EXPERT_FF // tf),
        in_specs=[
            pl.BlockSpec((None, tr, D_MODEL), lambda ei, ri, f: (ei, ri, 0)),
            pl.BlockSpec((None, None, D_MODEL, tf), lambda ei, ri, f: (layer, ei, 0, f)),
            pl.BlockSpec((None, None, D_MODEL, tf), lambda ei, ri, f: (layer, ei, 0, f)),
            pl.BlockSpec((None, None, tf, D_MODEL), lambda ei, ri, f: (layer, ei, f, 0)),
            pl.BlockSpec((None, tr, 1), lambda ei, ri, f: (ei, ri, 0)),
        ],
        out_specs=pl.BlockSpec((None, tr, D_MODEL), lambda ei, ri, f: (ei, ri, 0)),
        out_shape=jax.ShapeDtypeStruct((e, r, D_MODEL), F32),
        compiler_params=_cparams("parallel", "parallel", "arbitrary"),
        name="expert_ffn",
    )(xe, w_gate, w_up, w_down, gate)


def _residual_kernel(x_ref, y_ref, gate_ref, o_ref):
    o_ref[...] = x_ref[...] + gate_ref[...] * y_ref[...]


def _gated_residual(x, y, mod4, gate_idx, rows_per_cond, first_row):
    m = x.shape[0]
    tm = ROW_TILE
    row = pl.BlockSpec((tm, D_MODEL), lambda i: (i, 0))
    return pl.pallas_call(
        _residual_kernel,
        grid=(m // tm,),
        in_specs=[row, row, _mod_spec(gate_idx, rows_per_cond, first_row, tm)],
        out_specs=row,
        out_shape=jax.ShapeDtypeStruct((m, D_MODEL), F32),
        compiler_params=_cparams("parallel"),
        name="gated_residual",
    )(x, y, mod4)


def _expert_choice(h, logits, n_req, w_gate, w_up, w_down, layer):
    m = h.shape[0]
    n = m // n_req
    cap = EC_CAPACITY_FACTOR * n // N_EXPERTS
    aff = jax.nn.softmax(logits.reshape(n_req, n, N_EXPERTS), axis=-1)
    gate, idx = lax.top_k(jnp.swapaxes(aff, 1, 2), cap)
    rows = idx + (jnp.arange(n_req, dtype=idx.dtype) * n)[:, None, None]
    rows = jnp.transpose(rows, (1, 0, 2)).reshape(N_EXPERTS, n_req * cap)
    gate = jnp.transpose(gate, (1, 0, 2)).reshape(N_EXPERTS, n_req * cap, 1)
    xe = h[rows]
    ye = _expert_ffn(xe, gate, w_gate, w_up, w_down, layer)
    return jnp.zeros((m, D_MODEL), F32).at[rows.reshape(-1)].add(ye.reshape(-1, D_MODEL))


class _Stream:
    def __init__(self, x3, first_row, shared_cond):
        self.batch, self.n, _ = x3.shape
        self.x = x3.reshape(self.batch * self.n, D_MODEL)
        self.first_row = first_row
        self.rows_per_cond = self.batch * self.n if shared_cond else self.n

    def mod_args(self):
        return dict(rows_per_cond=self.rows_per_cond, first_row=self.first_row)


def kernel(x_prompt, x_sample, cache_na_k, cache_na_v, state_ssd, c, c_ctx, ada_w, ada_b, norm1_g, norm2_g,
           router_w, exp_w_gate, exp_w_up, exp_w_down, ab_w_in, pool_w, pool_scale, na_q_norm, na_k_norm,
           na_rpb, ab_w_out, ssd_w_in, ssd_conv_w, ssd_conv_b, ssd_a_log, ssd_dt_bias, ssd_d, ssd_norm_g,
           ssd_w_out):
    depth = ada_w.shape[0]
    dec_batch = x_sample.shape[0]
    latent = _Stream(x_sample, 0, shared_cond=False)
    prompt = _Stream(x_prompt, dec_batch, shared_cond=True)
    cond = jnp.concatenate(
        [c, c_ctx[None, :], jnp.zeros((MOD_ROWS - dec_batch - 1, D_MODEL), F32)], axis=0)

    new_k, new_v, new_s = [], [], []
    for layer in range(depth):
        i = layer // 2
        mod4 = _modulation(cond, ada_w, ada_b, layer).reshape(MOD_ROWS, N_MOD, 1, D_MODEL)
        for s in (prompt, latent):
            margs = s.mod_args()
            h = _norm_mod(s.x, norm1_g, mod4, layer, 0, 1, **margs)[0]
            tm = min(s.x.shape[0], 2048)
            if layer % 2 == 0:
                proj = _matmul([h], ab_w_in, i, tm=tm, tn=512, n_blocks=ab_w_in.shape[2] // 512)
                proj3 = proj.reshape(s.batch, s.n, -1)
                pooled = _pool_mixer(proj3, pool_w, pool_scale, i)
                if s is prompt:
                    att, k_new, v_new = _ctx_attention(proj3, na_q_norm, na_k_norm, i)
                    new_k.append(k_new)
                    new_v.append(v_new)
                else:
                    att = _neighbourhood_attention(proj3, cache_na_k, cache_na_v, na_q_norm, na_k_norm,
                                                   na_rpb, i)
                s.x = _matmul([pooled.reshape(-1, POOL_WIDTH), att.reshape(-1, NA_WIDTH)], ab_w_out, i,
                              tm=tm, tn=512, n_blocks=D_MODEL // 512, res=s.x, mod4=mod4, gate_idx=2,
                              **margs)
            else:
                proj = _matmul([h], ssd_w_in, i, tm=tm, tn=512, n_blocks=SSD_MAIN_DIM // 512)
                dt_raw = _matmul([h], ssd_w_in, i, tm=tm, tn=2 * SSD_HEADS, n_blocks=1,
                                 col_block0=SSD_MAIN_DIM // (2 * SSD_HEADS))
                proj3 = proj.reshape(s.batch, s.n, SSD_MAIN_DIM)
                dt3 = dt_raw.reshape(s.batch, s.n, 2 * SSD_HEADS)
                init = None if s is prompt else state_ssd
                y, s_fin = _ssd_scan(proj3, dt3, ssd_conv_w, ssd_conv_b, ssd_a_log, ssd_dt_bias, ssd_d, i,
                                     init=init)
                if s is prompt:
                    new_s.append(s_fin[:, None])
                yn = _gate_norm(y.reshape(-1, SSD_INNER), proj, ssd_norm_g, i)
                s.x = _matmul([yn], ssd_w_out, i, tm=min(tm, 1024), tn=512, n_blocks=D_MODEL // 512,
                              res=s.x, mod4=mod4, gate_idx=2, **margs)
            h2, logits = _norm_mod(s.x, norm2_g, mod4, layer, 3, 4, router_w=router_w, **margs)
            moe = _expert_choice(h2, logits, s.batch, exp_w_gate, exp_w_up, exp_w_down, layer)
            s.x = _gated_residual(s.x, moe, mod4, 5, **margs)

    y_prompt = prompt.x.reshape(x_prompt.shape)
    y_sample = latent.x.reshape(x_sample.shape)
    return (y_prompt, y_sample, jnp.concatenate(new_k, axis=1), jnp.concatenate(new_v, axis=1),
            jnp.concatenate(new_s, axis=1))
```

```python
import functools

import numpy as np
import jax
import jax.numpy as jnp
from jax import lax
from jax.experimental import pallas as pl
from jax.experimental.pallas import tpu as pltpu

F32 = jnp.float32
BF16 = jnp.bfloat16

D_MODEL = 2048
GRID_W = 64
POOL_WIDTH = 1024
POOL_GROUPS = 4
POOL_GROUP_DIM = 256
POOL_WINDOWS = (2, 4, 8, 16)
NA_WIDTH = 1024
NA_HEAD_DIM = 128
NA_HEADS = 8
WIN_H = 8
WIN_W = 16
SSD_INNER = 4096
SSD_HEAD_DIM = 64
SSD_HEADS = 64
SSD_GROUPS = 8
SSD_GROUP_HEADS = SSD_HEADS // SSD_GROUPS
SSD_GROUP_WIDTH = SSD_GROUP_HEADS * SSD_HEAD_DIM
SSD_STATE = 128
SSD_CONV = 4
SSD_CHUNK = 128
SSD_BC_WIDTH = SSD_GROUPS * SSD_STATE
SSD_MAIN_DIM = 2 * SSD_INNER + 2 * SSD_BC_WIDTH
N_EXPERTS = 16
EXPERT_FF = 1024
EC_CAPACITY_FACTOR = 2
NORM_EPS = 1e-6
N_MOD = 6
MOD_ROWS = 16

VMEM_LIMIT_BYTES = 56 * 1024 * 1024
ROW_TILE = 256


def _cparams(*sem):
    return pltpu.CompilerParams(dimension_semantics=sem, vmem_limit_bytes=VMEM_LIMIT_BYTES)


def _silu(x):
    return x * (1.0 / (1.0 + jnp.exp(-x)))


def _split2(x):
    hi = x.astype(BF16)
    lo = (x - hi.astype(F32)).astype(BF16)
    return hi, lo


def _split3(x):
    hi = x.astype(BF16)
    r = x - hi.astype(F32)
    mid = r.astype(BF16)
    lo = (r - mid.astype(F32)).astype(BF16)
    return hi, mid, lo


def _dot(a, b):
    return jnp.dot(a, b, preferred_element_type=F32)


def _dot_nt(a, b):
    return lax.dot_general(a, b, (((1,), (1,)), ((), ())), preferred_element_type=F32)


def _dot_f32_lhs(x, m):
    hi, mid, lo = _split3(x)
    return _dot(hi, m) + _dot(mid, m) + _dot(lo, m)


def _dot_f32_rhs(m, x):
    hi, mid, lo = _split3(x)
    return _dot(m, hi) + _dot(m, mid) + _dot(m, lo)


def _dot_split(a, b):
    a_hi, a_lo = _split2(a)
    b_hi, b_lo = _split2(b)
    return _dot(a_hi, b_hi) + _dot(a_lo, b_hi) + _dot(a_hi, b_lo)


def _mod_kernel(c_ref, w_ref, b_ref, o_ref):
    s = _silu(c_ref[...])
    o_ref[...] = _dot_split(s, w_ref[...]) + b_ref[...]


def _modulation(cond, ada_w, ada_b, layer):
    tn = 768
    n = N_MOD * D_MODEL
    ada_b3 = ada_b.reshape(ada_b.shape[0], 1, n)
    return pl.pallas_call(
        _mod_kernel,
        grid=(n // tn,),
        in_specs=[
            pl.BlockSpec((MOD_ROWS, D_MODEL), lambda j: (0, 0)),
            pl.BlockSpec((None, D_MODEL, tn), lambda j: (layer, 0, j)),
            pl.BlockSpec((None, 1, tn), lambda j: (layer, 0, j)),
        ],
        out_specs=pl.BlockSpec((MOD_ROWS, tn), lambda j: (0, j)),
        out_shape=jax.ShapeDtypeStruct((MOD_ROWS, n), F32),
        compiler_params=_cparams("parallel"),
        name="modulation",
    )(cond, ada_w, ada_b3)


def _mod_spec(which, rows_per_cond, first_row, tm):
    return pl.BlockSpec((None, None, 1, D_MODEL),
                        lambda i, *_: (first_row + (i * tm) // rows_per_cond, which, 0, 0))


def _norm_mod_kernel(x_ref, g_ref, sh_ref, sc_ref, *rest, with_router):
    x = x_ref[...]
    y = x * lax.rsqrt(jnp.mean(x * x, axis=-1, keepdims=True) + NORM_EPS) * g_ref[...]
    h = y * (1.0 + sc_ref[...]) + sh_ref[...]
    if with_router:
        rw_ref, h_ref, lg_ref = rest
        lg_ref[...] = _dot_split(h, rw_ref[...])
    else:
        (h_ref,) = rest
    h_ref[...] = h.astype(BF16)


def _norm_mod(x, g, mod4, layer, shift_idx, scale_idx, rows_per_cond, first_row, router_w=None):
    m = x.shape[0]
    tm = ROW_TILE
    g3 = g.reshape(g.shape[0], 1, D_MODEL)
    in_specs = [
        pl.BlockSpec((tm, D_MODEL), lambda i: (i, 0)),
        pl.BlockSpec((None, 1, D_MODEL), lambda i: (layer, 0, 0)),
        _mod_spec(shift_idx, rows_per_cond, first_row, tm),
        _mod_spec(scale_idx, rows_per_cond, first_row, tm),
    ]
    args = [x, g3, mod4, mod4]
    out_specs = [pl.BlockSpec((tm, D_MODEL), lambda i: (i, 0))]
    out_shape = [jax.ShapeDtypeStruct((m, D_MODEL), BF16)]
    if router_w is not None:
        in_specs.append(pl.BlockSpec((None, D_MODEL, N_EXPERTS), lambda i: (layer, 0, 0)))
        args.append(router_w)
        out_specs.append(pl.BlockSpec((tm, N_EXPERTS), lambda i: (i, 0)))
        out_shape.append(jax.ShapeDtypeStruct((m, N_EXPERTS), F32))
    return pl.pallas_call(
        functools.partial(_norm_mod_kernel, with_router=router_w is not None),
        grid=(m // tm,),
        in_specs=in_specs,
        out_specs=out_specs,
        out_shape=out_shape,
        compiler_params=_cparams("parallel"),
        name="norm_mod",
    )(*args)


def _matmul_kernel(*refs, n_a, k_sizes, with_res):
    a_refs = refs[:n_a]
    w_ref = refs[n_a]
    if with_res:
        res_ref, gate_ref, o_ref = refs[n_a + 1:]
    else:
        (o_ref,) = refs[n_a + 1:]
    w = w_ref[...].astype(BF16)
    acc = None
    k0 = 0
    for a_ref, k in zip(a_refs, k_sizes):
        part = _dot(a_ref[...], w[k0:k0 + k])
        acc = part if acc is None else acc + part
        k0 += k
    if with_res:
        acc = res_ref[...] + gate_ref[...] * acc
    o_ref[...] = acc.astype(o_ref.dtype)


def _matmul(a_list, w, layer, *, tm, tn, n_blocks, col_block0=0, out_dtype=F32,
            res=None, mod4=None, gate_idx=None, rows_per_cond=None, first_row=None):
    m = a_list[0].shape[0]
    k_sizes = tuple(a.shape[1] for a in a_list)
    k_total = sum(k_sizes)
    in_specs = [pl.BlockSpec((tm, k), lambda i, j: (i, 0)) for k in k_sizes]
    in_specs.append(pl.BlockSpec((None, k_total, tn), lambda i, j: (layer, 0, col_block0 + j)))
    args = list(a_list) + [w]
    if res is not None:
        in_specs.append(pl.BlockSpec((tm, tn), lambda i, j: (i, j)))
        in_specs.append(pl.BlockSpec((None, None, 1, tn),
                                     lambda i, j: (first_row + (i * tm) // rows_per_cond, gate_idx, 0, j)))
        args += [res, mod4]
    return pl.pallas_call(
        functools.partial(_matmul_kernel, n_a=len(a_list), k_sizes=k_sizes, with_res=res is not None),
        grid=(m // tm, n_blocks),
        in_specs=in_specs,
        out_specs=pl.BlockSpec((tm, tn), lambda i, j: (i, j)),
        out_shape=jax.ShapeDtypeStruct((m, n_blocks * tn), out_dtype),
        compiler_params=_cparams("parallel", "arbitrary"),
        name="matmul",
    )(*args)


def _pool_kernel(u_ref, w_ref, s_ref, o_ref, *, n):
    g = pl.program_id(1)
    u = u_ref[...]
    t = lax.broadcasted_iota(jnp.int32, (n, 1), 0)
    for gi, win in enumerate(POOL_WINDOWS):
        @pl.when(g == gi)
        def _(win=win):
            half = win // 2
            total = u
            for d in range(-half, win - half):
                if d == 0:
                    continue
                shifted = pltpu.roll(u, (-d) % n, axis=0)
                valid = jnp.logical_and(t + d >= 0, t + d < n)
                total = total + jnp.where(valid, shifted, 0.0)
            cnt = (jnp.minimum(t - half + win, n) - jnp.maximum(t - half, 0)).astype(F32)
            pooled = total / cnt - u
            out = _dot(pooled.astype(BF16), w_ref[...].astype(BF16)) * s_ref[...]
            o_ref[...] = out.astype(o_ref.dtype)


def _pool_mixer(proj3, pool_w, pool_scale, i_even):
    b, n, _ = proj3.shape
    scale4 = pool_scale.reshape(pool_scale.shape[0], POOL_GROUPS, 1, POOL_GROUP_DIM)
    return pl.pallas_call(
        functools.partial(_pool_kernel, n=n),
        grid=(b, POOL_GROUPS),
        in_specs=[
            pl.BlockSpec((None, n, POOL_GROUP_DIM), lambda bi, g: (bi, 0, g)),
            pl.BlockSpec((None, None, POOL_GROUP_DIM, POOL_GROUP_DIM), lambda bi, g: (i_even, g, 0, 0)),
            pl.BlockSpec((None, None, 1, POOL_GROUP_DIM), lambda bi, g: (i_even, g, 0, 0)),
        ],
        out_specs=pl.BlockSpec((None, n, POOL_GROUP_DIM), lambda bi, g: (bi, 0, g)),
        out_shape=jax.ShapeDtypeStruct((b, n, POOL_WIDTH), BF16),
        compiler_params=_cparams("parallel", "parallel"),
        name="pool_mixer",
    )(proj3, pool_w, scale4)


def _head_norm(x, g):
    return x * lax.rsqrt(jnp.mean(x * x, axis=-1, keepdims=True) + NORM_EPS) * g


_Q_COL0 = POOL_WIDTH // NA_HEAD_DIM
_K_COL0 = (POOL_WIDTH + NA_WIDTH) // NA_HEAD_DIM
_V_COL0 = (POOL_WIDTH + 2 * NA_WIDTH) // NA_HEAD_DIM


def _ctx_attn_kernel(q_ref, k_ref, v_ref, qg_ref, kg_ref, o_ref, kn_ref, vn_ref):
    qn = _head_norm(q_ref[...], qg_ref[...])
    kn = _head_norm(k_ref[...], kg_ref[...])
    v = v_ref[...]
    kn_ref[...] = kn
    vn_ref[...] = v
    s = _dot_nt(qn.astype(BF16), kn.astype(BF16)) * (NA_HEAD_DIM ** -0.5)
    m = jnp.max(s, axis=-1, keepdims=True)
    p = jnp.exp(s - m)
    denom = jnp.sum(p, axis=-1, keepdims=True)
    o = _dot(p.astype(BF16), v.astype(BF16)) / denom
    o_ref[...] = o.astype(o_ref.dtype)


def _ctx_attention(proj3, q_norm, k_norm, i_even):
    b, n, _ = proj3.shape
    qg = q_norm.reshape(q_norm.shape[0], 1, NA_HEAD_DIM)
    kg = k_norm.reshape(k_norm.shape[0], 1, NA_HEAD_DIM)
    head_blk = lambda col0: pl.BlockSpec((None, n, NA_HEAD_DIM), lambda bi, h: (bi, 0, col0 + h))
    gain = pl.BlockSpec((None, 1, NA_HEAD_DIM), lambda bi, h: (i_even, 0, 0))
    cache = pl.BlockSpec((None, None, None, n, NA_HEAD_DIM), lambda bi, h: (bi, 0, h, 0, 0))
    return pl.pallas_call(
        _ctx_attn_kernel,
        grid=(b, NA_HEADS),
        in_specs=[head_blk(_Q_COL0), head_blk(_K_COL0), head_blk(_V_COL0), gain, gain],
        out_specs=[pl.BlockSpec((None, n, NA_HEAD_DIM), lambda bi, h: (bi, 0, h)), cache, cache],
        out_shape=[
            jax.ShapeDtypeStruct((b, n, NA_WIDTH), BF16),
            jax.ShapeDtypeStruct((b, 1, NA_HEADS, n, NA_HEAD_DIM), F32),
            jax.ShapeDtypeStruct((b, 1, NA_HEADS, n, NA_HEAD_DIM), F32),
        ],
        compiler_params=_cparams("parallel", "parallel"),
        name="ctx_attention",
    )(proj3, proj3, proj3, qg, kg)


def _na_bias_table(rpb, rows):
    kh = min(WIN_H, rows)
    col = np.arange(GRID_W)[:, None]
    kc = np.arange(GRID_W)[None, :]
    wstart = np.clip(col - WIN_W // 2, 0, GRID_W - WIN_W)
    inside = (kc >= wstart) & (kc < wstart + WIN_W)
    rel = np.clip(kc - col + WIN_W - 1, 0, 2 * WIN_W - 2)
    d0 = np.arange(kh)[:, None]
    off = np.arange(kh)[None, :]
    ridx = np.clip(d0 + off, 0, 2 * WIN_H - 2)
    tab = rpb[:, ridx][:, :, :, rel]
    tab = jnp.where(inside[None, None, None], tab, -jnp.inf)
    tab = jnp.transpose(tab, (0, 1, 3, 2, 4))
    return tab.reshape(rpb.shape[0], kh, GRID_W, kh * GRID_W)


def _na_kernel(q_ref, k_ref, v_ref, kc_ref, vc_ref, qg_ref, kg_ref, bias_ref, o_ref,
               qs_ref, ks_ref, vs_ref, *, rows, kh):
    qs_ref[...] = _head_norm(q_ref[...], qg_ref[...]).astype(BF16)
    ks_ref[...] = _head_norm(k_ref[...], kg_ref[...]).astype(BF16)
    vs_ref[...] = v_ref[...].astype(BF16)
    k_ctx = kc_ref[...].astype(BF16)
    v_ctx = vc_ref[...].astype(BF16)
    scale = NA_HEAD_DIM ** -0.5
    n_loc = kh * GRID_W

    def row(r):
        sr = jnp.clip(r - kh // 2, 0, rows - kh)
        d0 = sr - r + (WIN_H - 1)
        q_r = qs_ref[pl.ds(pl.multiple_of(r * GRID_W, GRID_W), GRID_W), :]
        kstart = pl.multiple_of(sr * GRID_W, GRID_W)
        k_blk = ks_ref[pl.ds(kstart, n_loc), :]
        v_blk = vs_ref[pl.ds(kstart, n_loc), :]
        s_loc = _dot_nt(q_r, k_blk) * scale + bias_ref[d0]
        s_ctx = _dot_nt(q_r, k_ctx) * scale
        m = jnp.maximum(jnp.max(s_loc, axis=-1, keepdims=True), jnp.max(s_ctx, axis=-1, keepdims=True))
        p_loc = jnp.exp(s_loc - m)
        p_ctx = jnp.exp(s_ctx - m)
        denom = jnp.sum(p_loc, axis=-1, keepdims=True) + jnp.sum(p_ctx, axis=-1, keepdims=True)
        o = (_dot(p_loc.astype(BF16), v_blk) + _dot(p_ctx.astype(BF16), v_ctx)) / denom
        o_ref[pl.ds(pl.multiple_of(r * GRID_W, GRID_W), GRID_W), :] = o.astype(o_ref.dtype)

    rows_per_trip = 4 if rows % 4 == 0 else 1

    def trip(i, carry):
        for k in range(rows_per_trip):
            row(i * rows_per_trip + k)
        return carry

    lax.fori_loop(0, rows // rows_per_trip, trip, 0)


def _neighbourhood_attention(proj3, cache_k, cache_v, q_norm, k_norm, rpb, i_even):
    b, t, _ = proj3.shape
    rows = t // GRID_W
    kh = min(WIN_H, rows)
    n_ctx = cache_k.shape[3]
    bias = _na_bias_table(rpb[i_even], rows)
    qg = q_norm.reshape(q_norm.shape[0], 1, NA_HEAD_DIM)
    kg = k_norm.reshape(k_norm.shape[0], 1, NA_HEAD_DIM)
    head_blk = lambda col0: pl.BlockSpec((None, t, NA_HEAD_DIM), lambda bi, h: (bi, 0, col0 + h))
    gain = pl.BlockSpec((None, 1, NA_HEAD_DIM), lambda bi, h: (i_even, 0, 0))
    cache = pl.BlockSpec((None, None, None, n_ctx, NA_HEAD_DIM), lambda bi, h: (bi, i_even, h, 0, 0))
    return pl.pallas_call(
        functools.partial(_na_kernel, rows=rows, kh=kh),
        grid=(b, NA_HEADS),
        in_specs=[head_blk(_Q_COL0), head_blk(_K_COL0), head_blk(_V_COL0), cache, cache, gain, gain,
                  pl.BlockSpec((None, kh, GRID_W, kh * GRID_W), lambda bi, h: (h, 0, 0, 0))],
        out_specs=pl.BlockSpec((None, t, NA_HEAD_DIM), lambda bi, h: (bi, 0, h)),
        out_shape=jax.ShapeDtypeStruct((b, t, NA_WIDTH), BF16),
        scratch_shapes=[pltpu.VMEM((t, NA_HEAD_DIM), BF16)] * 3,
        compiler_params=_cparams("parallel", "parallel"),
        name="neighbourhood_attention",
    )(proj3, proj3, proj3, cache_k, cache_v, qg, kg, bias)


def _softplus(x):
    return jnp.maximum(x, 0.0) + jnp.log1p(jnp.exp(-jnp.abs(x)))


def _ssd_kernel(*refs, l, with_init):
    (x_ref, b_ref, c_ref, dt_ref, cwx_ref, cwb_ref, cwc_ref, cbx_ref, cbb_ref, cbc_ref,
     alog_ref, dtbias_ref, dsk_ref) = refs[:13]
    if with_init:
        init_ref = refs[13]
        outs = refs[14:]
    else:
        outs = refs[13:]
    (y_ref, sfin_ref, xs_ref, bs_ref, cs_ref, stf_ref, stb_ref,
     cumf_ref, cumb_ref, cumft_ref, cumbt_ref, dtt_ref) = outs
    g = pl.program_id(1)
    q = SSD_CHUNK
    nc = l // q
    gw = SSD_GROUP_WIDTH
    gh = SSD_GROUP_HEADS
    pad_l = SSD_CONV // 2
    halo = 8

    win_rows = q + 2 * halo
    sel_col = lax.broadcasted_iota(jnp.int32, (q, SSD_CONV * win_rows), 1)
    sel_row = lax.broadcasted_iota(jnp.int32, (q, SSD_CONV * win_rows), 0)
    shift_mat = ((sel_col % win_rows) == sel_row + (halo - pad_l) + sel_col // win_rows).astype(BF16)

    def conv_chunk(c, carry):
        t0 = pl.multiple_of(c * q, q)
        lo_start = pl.multiple_of(jnp.maximum(t0 - halo, 0), halo)
        hi_start = pl.multiple_of(jnp.minimum(t0 + q, l - halo), halo)
        for src, w_ref, bias_ref, dst in ((x_ref, cwx_ref, cbx_ref, xs_ref),
                                          (b_ref, cwb_ref, cbb_ref, bs_ref),
                                          (c_ref, cwc_ref, cbc_ref, cs_ref)):
            lo = jnp.where(c > 0, src[pl.ds(lo_start, halo), :], 0.0)
            hi = jnp.where(c < nc - 1, src[pl.ds(hi_start, halo), :], 0.0)
            win = jnp.concatenate([lo, src[pl.ds(t0, q), :], hi], axis=0)
            w = w_ref[...]
            taps = jnp.concatenate([(win * w[j:j + 1, :]).astype(BF16) for j in range(SSD_CONV)], axis=0)
            out = _silu(_dot(shift_mat, taps) + bias_ref[...])
            dst[pl.ds(t0, q), :] = out
            if dst is xs_ref:
                y_ref[pl.ds(t0, q), :] = out * dsk_ref[...]
        dt = _softplus(pltpu.roll(dt_ref[pl.ds(t0, q), :], to_lane0, axis=1) + bias_g)
        dta = dt * a_g
        cum_f = _dot_f32_rhs(tri_f, dta)
        cum_b = _dot_f32_rhs(tri_b, dta)
        cumf_ref[pl.ds(t0, q), :] = cum_f
        cumb_ref[pl.ds(t0, q), :] = cum_b
        cumft_ref[pl.ds(t0, q), :] = jnp.transpose(cum_f)
        cumbt_ref[pl.ds(t0, q), :] = jnp.transpose(cum_b)
        dtt_ref[pl.ds(t0, q), :] = jnp.transpose(dt)
        return carry

    ii = lax.broadcasted_iota(jnp.int32, (q, q), 0)
    jj = lax.broadcasted_iota(jnp.int32, (q, q), 1)
    keep_f = jj <= ii
    keep_b = jj >= ii
    tri_f = keep_f.astype(BF16)
    tri_b = keep_b.astype(BF16)
    n_dt = 2 * SSD_HEADS
    assert q == n_dt
    to_lane0 = (n_dt - g * gh) % n_dt
    bias_g = pltpu.roll(jnp.broadcast_to(dtbias_ref[...], (8, n_dt)), to_lane0, axis=1)[0:1, :]
    a_g = pltpu.roll(jnp.broadcast_to(-jnp.exp(alog_ref[...]), (8, n_dt)), to_lane0, axis=1)[0:1, :]

    lax.fori_loop(0, nc, conv_chunk, 0)

    pair_w = 2 * SSD_HEAD_DIM
    lane = lax.broadcasted_iota(jnp.int32, (1, gw), 1)
    half_mask = [((lane % pair_w) // SSD_HEAD_DIM) == s for s in range(2)]
    first_head_lanes = lax.broadcasted_iota(jnp.int32, (1, pair_w), 1) < SSD_HEAD_DIM

    def scan_chunk(c, reverse):
        keep, edge, lane0 = (keep_b, 0, SSD_HEADS) if reverse else (keep_f, q - 1, 0)
        cum_ref, cumt_ref, st_ref = (cumb_ref, cumbt_ref, stb_ref) if reverse else (cumf_ref, cumft_ref, stf_ref)
        t0 = pl.multiple_of(c * q, q)
        xc = xs_ref[pl.ds(t0, q), :]
        bc = bs_ref[pl.ds(t0, q), :]
        cc = cs_ref[pl.ds(t0, q), :]
        cb = _dot_nt(cc.astype(BF16), bc.astype(BF16))
        bct = jnp.transpose(bc)
        x_bf = xc.astype(BF16)
        st = st_ref[...]
        st_bf = st.astype(BF16)
        zero = jnp.zeros((), BF16)
        x_half = [jnp.where(half_mask[s], x_bf, zero) for s in range(2)]
        st_half = [jnp.where(half_mask[s], st_bf, zero) for s in range(2)]
        pieces = []
        for pair in range(gh // 2):
            c0 = pair * pair_w
            lhs_y, lhs_s, carry_decay = [], [], []
            for e in (2 * pair, 2 * pair + 1):
                ln = lane0 + e
                col = jnp.broadcast_to(cum_ref[pl.ds(t0, q), ln:ln + 1], (q, q))
                row = cumt_ref[pl.ds(t0 + ln, 1), :]
                dt_row = dtt_ref[pl.ds(t0 + ln, 1), :]
                at_edge = row[:, edge:edge + 1]
                decay = jnp.exp(jnp.where(keep, col - row, -jnp.inf))
                lhs_y.append((cb * decay * dt_row).astype(BF16))
                lhs_y.append((cc * jnp.exp(col)).astype(BF16))
                lhs_s.append((bct * (dt_row * jnp.exp(at_edge - row))).astype(BF16))
                carry_decay.append(jnp.exp(at_edge))
            rhs_y = jnp.concatenate([x_half[0][:, c0:c0 + pair_w], st_half[0][:, c0:c0 + pair_w],
                                     x_half[1][:, c0:c0 + pair_w], st_half[1][:, c0:c0 + pair_w]], axis=0)
            pieces.append(_dot(jnp.concatenate(lhs_y, axis=1), rhs_y))
            rhs_s = jnp.concatenate([x_half[0][:, c0:c0 + pair_w], x_half[1][:, c0:c0 + pair_w]], axis=0)
            keep_frac = jnp.where(first_head_lanes, carry_decay[0], carry_decay[1])
            st_ref[:, c0:c0 + pair_w] = (st[:, c0:c0 + pair_w] * keep_frac
                                         + _dot(jnp.concatenate(lhs_s, axis=1), rhs_s))
        y_ref[pl.ds(t0, q), :] = y_ref[pl.ds(t0, q), :] + jnp.concatenate(pieces, axis=1)

    for d, st_ref in enumerate((stf_ref, stb_ref)):
        if with_init:
            st_ref[...] = jnp.transpose(init_ref[d].reshape(gw, SSD_STATE))
        else:
            st_ref[...] = jnp.zeros_like(st_ref)

    def step(s, carry):
        scan_chunk(s, False)
        scan_chunk(nc - 1 - s, True)
        return carry

    lax.fori_loop(0, nc, step, 0)
    for d, st_ref in enumerate((stf_ref, stb_ref)):
        sfin_ref[d] = jnp.transpose(st_ref[...]).reshape(gh, SSD_HEAD_DIM, SSD_STATE)


def _ssd_scan(proj3, dt_raw3, conv_w, conv_b, a_log, dt_bias, d_skip, i_odd, init=None):
    b, l, _ = proj3.shape
    gh, gw = SSD_GROUP_HEADS, SSD_GROUP_WIDTH
    n_dt = 2 * SSD_HEADS
    a_log3 = a_log.reshape(a_log.shape[0], 1, n_dt)
    dt_bias3 = dt_bias.reshape(dt_bias.shape[0], 1, n_dt)
    dsk = jnp.repeat(d_skip[i_odd].astype(F32), SSD_HEAD_DIM).reshape(1, SSD_INNER)
    conv_b3 = conv_b.reshape(conv_b.shape[0], 1, conv_b.shape[1])

    x_col0 = SSD_INNER // gw
    b_col0 = (2 * SSD_INNER) // SSD_STATE
    c_col0 = (2 * SSD_INNER + SSD_BC_WIDTH) // SSD_STATE
    cwx_col0 = 0
    cwb_col0 = SSD_INNER // SSD_STATE
    cwc_col0 = (SSD_INNER + SSD_BC_WIDTH) // SSD_STATE

    in_specs = [
        pl.BlockSpec((None, l, gw), lambda bi, g: (bi, 0, x_col0 + g)),
        pl.BlockSpec((None, l, SSD_STATE), lambda bi, g: (bi, 0, b_col0 + g)),
        pl.BlockSpec((None, l, SSD_STATE), lambda bi, g: (bi, 0, c_col0 + g)),
        pl.BlockSpec((None, l, n_dt), lambda bi, g: (bi, 0, 0)),
        pl.BlockSpec((None, SSD_CONV, gw), lambda bi, g: (i_odd, 0, cwx_col0 + g)),
        pl.BlockSpec((None, SSD_CONV, SSD_STATE), lambda bi, g: (i_odd, 0, cwb_col0 + g)),
        pl.BlockSpec((None, SSD_CONV, SSD_STATE), lambda bi, g: (i_odd, 0, cwc_col0 + g)),
        pl.BlockSpec((None, 1, gw), lambda bi, g: (i_odd, 0, cwx_col0 + g)),
        pl.BlockSpec((None, 1, SSD_STATE), lambda bi, g: (i_odd, 0, cwb_col0 + g)),
        pl.BlockSpec((None, 1, SSD_STATE), lambda bi, g: (i_odd, 0, cwc_col0 + g)),
        pl.BlockSpec((None, 1, n_dt), lambda bi, g: (i_odd, 0, 0)),
        pl.BlockSpec((None, 1, n_dt), lambda bi, g: (i_odd, 0, 0)),
        pl.BlockSpec((1, gw), lambda bi, g: (0, g)),
    ]
    args = [proj3, proj3, proj3, dt_raw3, conv_w, conv_w, conv_w,
            conv_b3, conv_b3, conv_b3, a_log3, dt_bias3, dsk]
    if init is not None:
        in_specs.append(pl.BlockSpec((None, None, 2, gh, SSD_HEAD_DIM, SSD_STATE),
                                     lambda bi, g: (bi, i_odd, 0, g, 0, 0)))
        args.append(init)
    return pl.pallas_call(
        functools.partial(_ssd_kernel, l=l, with_init=init is not None),
        grid=(b, SSD_GROUPS),
        in_specs=in_specs,
        out_specs=[
            pl.BlockSpec((None, l, gw), lambda bi, g: (bi, 0, g)),
            pl.BlockSpec((None, 2, gh, SSD_HEAD_DIM, SSD_STATE), lambda bi, g: (bi, 0, g, 0, 0)),
        ],
        out_shape=[
            jax.ShapeDtypeStruct((b, l, SSD_INNER), F32),
            jax.ShapeDtypeStruct((b, 2, SSD_HEADS, SSD_HEAD_DIM, SSD_STATE), F32),
        ],
        scratch_shapes=[
            pltpu.VMEM((l, gw), F32),
            pltpu.VMEM((l, SSD_STATE), F32),
            pltpu.VMEM((l, SSD_STATE), F32),
            pltpu.VMEM((SSD_STATE, gw), F32),
            pltpu.VMEM((SSD_STATE, gw), F32),
        ] + [pltpu.VMEM((l, n_dt), F32)] * 5,
        compiler_params=_cparams("parallel", "parallel"),
        name="ssd_scan",
    )(*args)


def _gate_norm_kernel(y_ref, z_ref, g_ref, o_ref):
    y = y_ref[...] * _silu(z_ref[...])
    o = y * lax.rsqrt(jnp.mean(y * y, axis=-1, keepdims=True) + NORM_EPS) * g_ref[...]
    o_ref[...] = o.astype(o_ref.dtype)


def _gate_norm(y, proj, norm_g, i_odd):
    m = y.shape[0]
    tm = ROW_TILE
    g3 = norm_g.reshape(norm_g.shape[0], 1, SSD_INNER)
    return pl.pallas_call(
        _gate_norm_kernel,
        grid=(m // tm,),
        in_specs=[
            pl.BlockSpec((tm, SSD_INNER), lambda i: (i, 0)),
            pl.BlockSpec((tm, SSD_INNER), lambda i: (i, 0)),
            pl.BlockSpec((None, 1, SSD_INNER), lambda i: (i_odd, 0, 0)),
        ],
        out_specs=pl.BlockSpec((tm, SSD_INNER), lambda i: (i, 0)),
        out_shape=jax.ShapeDtypeStruct((m, SSD_INNER), BF16),
        compiler_params=_cparams("parallel"),
        name="gate_norm",
    )(y, proj, g3)


def _expert_kernel(x_ref, wg_ref, wu_ref, wd_ref, gate_ref, o_ref, acc_ref):
    f = pl.program_id(2)
    x = x_ref[...]
    hg = _dot(x, wg_ref[...].astype(BF16))
    hu = _dot(x, wu_ref[...].astype(BF16))
    hdn = (_silu(hg) * hu).astype(BF16)
    part = _dot(hdn, wd_ref[...].astype(BF16))

    @pl.when(f == 0)
    def _():
        acc_ref[...] = part

    @pl.when(f > 0)
    def _():
        acc_ref[...] = acc_ref[...] + part

    @pl.when(f == pl.num_programs(2) - 1)
    def _():
        o_ref[...] = (acc_ref[...] * gate_ref[...]).astype(o_ref.dtype)


def _expert_ffn(xe, gate, w_gate, w_up, w_down, layer):
    e, r, _ = xe.shape
    tr = min(r, 1024)
    tf = 256
    return pl.pallas_call(
        _expert_kernel,
        grid=(e, r // tr, EXPERT_FF // tf),
        in_specs=[
            pl.BlockSpec((None, tr, D_MODEL), lambda ei, ri, f: (ei, ri, 0)),
            pl.BlockSpec((None, None, D_MODEL, tf), lambda ei, ri, f: (layer, ei, 0, f)),
            pl.BlockSpec((None, None, D_MODEL, tf), lambda ei, ri, f: (layer, ei, 0, f)),
            pl.BlockSpec((None, None, tf, D_MODEL), lambda ei, ri, f: (layer, ei, f, 0)),
            pl.BlockSpec((None, tr, 1), lambda ei, ri, f: (ei, ri, 0)),
        ],
        out_specs=pl.BlockSpec((None, tr, D_MODEL), lambda ei, ri, f: (ei, ri, 0)),
        out_shape=jax.ShapeDtypeStruct((e, r, D_MODEL), BF16),
        scratch_shapes=[pltpu.VMEM((tr, D_MODEL), F32)],
        compiler_params=_cparams("parallel", "parallel", "arbitrary"),
        name="expert_ffn",
    )(xe, w_gate, w_up, w_down, gate)


def _combine_kernel(idx_ref, ye_ref, x_ref, gate_ref, o_ref, *, n, ec, cap):
    step = pl.program_id(2)
    token = lax.broadcasted_iota(jnp.int32, (n, cap), 0)
    onehot = jnp.concatenate(
        [jnp.where(token == idx_ref[k], 1.0, 0.0).astype(BF16) for k in range(ec)], axis=1)
    part = _dot(onehot, ye_ref[...].reshape(ec * cap, ye_ref.shape[-1]))

    @pl.when(step == 0)
    def _():
        o_ref[...] = part

    @pl.when(step > 0)
    def _():
        o_ref[...] = o_ref[...] + part

    @pl.when(step == pl.num_programs(2) - 1)
    def _():
        o_ref[...] = x_ref[...] + gate_ref[...] * o_ref[...]


def _expert_combine(ye, idx, x, mod4, gate_idx, first_row, shared_cond):
    n_req, n_exp, cap = idx.shape
    m = x.shape[0]
    n = m // n_req
    ec = max(1, min(n_exp, 512 // cap))
    tn = D_MODEL // 2
    idx4 = idx.reshape(n_req, n_exp, 1, cap)
    cond_row = (lambda b: first_row) if shared_cond else (lambda b: first_row + b)
    return pl.pallas_call(
        functools.partial(_combine_kernel, n=n, ec=ec, cap=cap),
        grid=(n_req, D_MODEL // tn, n_exp // ec),
        in_specs=[
            pl.BlockSpec((None, ec, 1, cap), lambda b, j, s: (b, s, 0, 0)),
            pl.BlockSpec((ec, cap, tn), lambda b, j, s: (s, b, j)),
            pl.BlockSpec((n, tn), lambda b, j, s: (b, j)),
            pl.BlockSpec((None, None, 1, tn), lambda b, j, s: (cond_row(b), gate_idx, 0, j)),
        ],
        out_specs=pl.BlockSpec((n, tn), lambda b, j, s: (b, j)),
        out_shape=jax.ShapeDtypeStruct((m, D_MODEL), F32),
        compiler_params=_cparams("parallel", "parallel", "arbitrary"),
        name="expert_combine",
    )(idx4, ye, x, mod4)


def _expert_choice(h, logits, x, n_req, w_gate, w_up, w_down, layer, mod4, gate_idx, first_row, shared_cond):
    m = h.shape[0]
    n = m // n_req
    cap = EC_CAPACITY_FACTOR * n // N_EXPERTS
    aff = jax.nn.softmax(logits.reshape(n_req, n, N_EXPERTS), axis=-1)
    gate, idx = lax.top_k(jnp.swapaxes(aff, 1, 2), cap)
    rows = idx + (jnp.arange(n_req, dtype=idx.dtype) * n)[:, None, None]
    rows = jnp.transpose(rows, (1, 0, 2)).reshape(N_EXPERTS, n_req * cap)
    gate = jnp.transpose(gate, (1, 0, 2)).reshape(N_EXPERTS, n_req * cap, 1)
    xe = h[rows]
    ye = _expert_ffn(xe, gate, w_gate, w_up, w_down, layer)
    return _expert_combine(ye, idx, x, mod4, gate_idx, first_row, shared_cond)


class _Stream:
    def __init__(self, x3, first_row, shared_cond):
        self.batch, self.n, _ = x3.shape
        self.x = x3.reshape(self.batch * self.n, D_MODEL)
        self.first_row = first_row
        self.shared_cond = shared_cond
        self.rows_per_cond = self.batch * self.n if shared_cond else self.n

    def mod_args(self):
        return dict(rows_per_cond=self.rows_per_cond, first_row=self.first_row)


def kernel(x_prompt, x_sample, cache_na_k, cache_na_v, state_ssd, c, c_ctx, ada_w, ada_b, norm1_g, norm2_g,
           router_w, exp_w_gate, exp_w_up, exp_w_down, ab_w_in, pool_w, pool_scale, na_q_norm, na_k_norm,
           na_rpb, ab_w_out, ssd_w_in, ssd_conv_w, ssd_conv_b, ssd_a_log, ssd_dt_bias, ssd_d, ssd_norm_g,
           ssd_w_out):
    depth = ada_w.shape[0]
    dec_batch = x_sample.shape[0]
    latent = _Stream(x_sample, 0, shared_cond=False)
    prompt = _Stream(x_prompt, dec_batch, shared_cond=True)
    cond = jnp.concatenate(
        [c, c_ctx[None, :], jnp.zeros((MOD_ROWS - dec_batch - 1, D_MODEL), F32)], axis=0)

    new_k, new_v, new_s = [], [], []
    for layer in range(depth):
        i = layer // 2
        mod4 = _modulation(cond, ada_w, ada_b, layer).reshape(MOD_ROWS, N_MOD, 1, D_MODEL)
        for s in (prompt, latent):
            margs = s.mod_args()
            h = _norm_mod(s.x, norm1_g, mod4, layer, 0, 1, **margs)[0]
            tm = min(s.x.shape[0], 2048)
            if layer % 2 == 0:
                proj = _matmul([h], ab_w_in, i, tm=tm, tn=512, n_blocks=ab_w_in.shape[2] // 512)
                proj3 = proj.reshape(s.batch, s.n, -1)
                pooled = _pool_mixer(proj3, pool_w, pool_scale, i)
                if s is prompt:
                    att, k_new, v_new = _ctx_attention(proj3, na_q_norm, na_k_norm, i)
                    new_k.append(k_new)
                    new_v.append(v_new)
                else:
                    att = _neighbourhood_attention(proj3, cache_na_k, cache_na_v, na_q_norm, na_k_norm,
                                                   na_rpb, i)
                s.x = _matmul([pooled.reshape(-1, POOL_WIDTH), att.reshape(-1, NA_WIDTH)], ab_w_out, i,
                              tm=tm, tn=512, n_blocks=D_MODEL // 512, res=s.x, mod4=mod4, gate_idx=2,
                              **margs)
            else:
                proj = _matmul([h], ssd_w_in, i, tm=tm, tn=512, n_blocks=SSD_MAIN_DIM // 512)
                dt_raw = _matmul([h], ssd_w_in, i, tm=tm, tn=2 * SSD_HEADS, n_blocks=1,
                                 col_block0=SSD_MAIN_DIM // (2 * SSD_HEADS))
                proj3 = proj.reshape(s.batch, s.n, SSD_MAIN_DIM)
                dt3 = dt_raw.reshape(s.batch, s.n, 2 * SSD_HEADS)
                init = None if s is prompt else state_ssd
                y, s_fin = _ssd_scan(proj3, dt3, ssd_conv_w, ssd_conv_b, ssd_a_log, ssd_dt_bias, ssd_d, i,
                                     init=init)
                if s is prompt:
                    new_s.append(s_fin[:, None])
                yn = _gate_norm(y.reshape(-1, SSD_INNER), proj, ssd_norm_g, i)
                s.x = _matmul([yn], ssd_w_out, i, tm=min(tm, 1024), tn=512, n_blocks=D_MODEL // 512,
                              res=s.x, mod4=mod4, gate_idx=2, **margs)
            h2, logits = _norm_mod(s.x, norm2_g, mod4, layer, 3, 4, router_w=router_w, **margs)
            s.x = _expert_choice(h2, logits, s.x, s.batch, exp_w_gate, exp_w_up, exp_w_down, layer,
                                 mod4, 5, s.first_row, s.shared_cond)

    y_prompt = prompt.x.reshape(x_prompt.shape)
    y_sample = latent.x.reshape(x_sample.shape)
    return (y_prompt, y_sample, jnp.concatenate(new_k, axis=1), jnp.concatenate(new_v, axis=1),
            jnp.concatenate(new_s, axis=1))
```

```python
import functools

import numpy as np
import jax
import jax.numpy as jnp
from jax import lax
from jax.experimental import pallas as pl
from jax.experimental.pallas import tpu as pltpu

F32 = jnp.float32
BF16 = jnp.bfloat16

D_MODEL = 2048
GRID_W = 64
POOL_WIDTH = 1024
POOL_GROUPS = 4
POOL_GROUP_DIM = 256
POOL_WINDOWS = (2, 4, 8, 16)
NA_WIDTH = 1024
NA_HEAD_DIM = 128
NA_HEADS = 8
WIN_H = 8
WIN_W = 16
SSD_INNER = 4096
SSD_HEAD_DIM = 64
SSD_HEADS = 64
SSD_GROUPS = 8
SSD_GROUP_HEADS = SSD_HEADS // SSD_GROUPS
SSD_GROUP_WIDTH = SSD_GROUP_HEADS * SSD_HEAD_DIM
SSD_STATE = 128
SSD_CONV = 4
SSD_CHUNK = 128
SSD_BC_WIDTH = SSD_GROUPS * SSD_STATE
SSD_MAIN_DIM = 2 * SSD_INNER + 2 * SSD_BC_WIDTH
N_EXPERTS = 16
EXPERT_FF = 1024
EC_CAPACITY_FACTOR = 2
NORM_EPS = 1e-6
N_MOD = 6
MOD_ROWS = 16

VMEM_LIMIT_BYTES = 56 * 1024 * 1024
ROW_TILE = 256


def _cparams(*sem):
    return pltpu.CompilerParams(dimension_semantics=sem, vmem_limit_bytes=VMEM_LIMIT_BYTES)


def _silu(x):
    return x * (1.0 / (1.0 + jnp.exp(-x)))


def _split2(x):
    hi = x.astype(BF16)
    lo = (x - hi.astype(F32)).astype(BF16)
    return hi, lo


def _split3(x):
    hi = x.astype(BF16)
    r = x - hi.astype(F32)
    mid = r.astype(BF16)
    lo = (r - mid.astype(F32)).astype(BF16)
    return hi, mid, lo


def _dot(a, b):
    return jnp.dot(a, b, preferred_element_type=F32)


def _dot_nt(a, b):
    return lax.dot_general(a, b, (((1,), (1,)), ((), ())), preferred_element_type=F32)


def _dot_f32_lhs(x, m):
    hi, mid, lo = _split3(x)
    return _dot(hi, m) + _dot(mid, m) + _dot(lo, m)


def _dot_f32_rhs(m, x):
    hi, mid, lo = _split3(x)
    return _dot(m, hi) + _dot(m, mid) + _dot(m, lo)


def _dot_split(a, b):
    a_hi, a_lo = _split2(a)
    b_hi, b_lo = _split2(b)
    return _dot(a_hi, b_hi) + _dot(a_lo, b_hi) + _dot(a_hi, b_lo)


def _mod_kernel(c_ref, w_ref, b_ref, o_ref):
    s = _silu(c_ref[...])
    o_ref[...] = _dot_split(s, w_ref[...]) + b_ref[...]


def _modulation(cond, ada_w, ada_b, layer):
    tn = 768
    n = N_MOD * D_MODEL
    ada_b3 = ada_b.reshape(ada_b.shape[0], 1, n)
    return pl.pallas_call(
        _mod_kernel,
        grid=(n // tn,),
        in_specs=[
            pl.BlockSpec((MOD_ROWS, D_MODEL), lambda j: (0, 0)),
            pl.BlockSpec((None, D_MODEL, tn), lambda j: (layer, 0, j)),
            pl.BlockSpec((None, 1, tn), lambda j: (layer, 0, j)),
        ],
        out_specs=pl.BlockSpec((MOD_ROWS, tn), lambda j: (0, j)),
        out_shape=jax.ShapeDtypeStruct((MOD_ROWS, n), F32),
        compiler_params=_cparams("parallel"),
        name="modulation",
    )(cond, ada_w, ada_b3)


def _mod_spec(which, rows_per_cond, first_row, tm):
    return pl.BlockSpec((None, None, 1, D_MODEL),
                        lambda i, *_: (first_row + (i * tm) // rows_per_cond, which, 0, 0))


def _norm_mod_kernel(x_ref, g_ref, sh_ref, sc_ref, *rest, with_router):
    x = x_ref[...]
    y = x * lax.rsqrt(jnp.mean(x * x, axis=-1, keepdims=True) + NORM_EPS) * g_ref[...]
    h = y * (1.0 + sc_ref[...]) + sh_ref[...]
    if with_router:
        rw_ref, h_ref, lg_ref = rest
        lg_ref[...] = _dot_split(h, rw_ref[...])
    else:
        (h_ref,) = rest
    h_ref[...] = h.astype(BF16)


def _norm_mod(x, g, mod4, layer, shift_idx, scale_idx, rows_per_cond, first_row, router_w=None):
    m = x.shape[0]
    tm = ROW_TILE
    g3 = g.reshape(g.shape[0], 1, D_MODEL)
    in_specs = [
        pl.BlockSpec((tm, D_MODEL), lambda i: (i, 0)),
        pl.BlockSpec((None, 1, D_MODEL), lambda i: (layer, 0, 0)),
        _mod_spec(shift_idx, rows_per_cond, first_row, tm),
        _mod_spec(scale_idx, rows_per_cond, first_row, tm),
    ]
    args = [x, g3, mod4, mod4]
    out_specs = [pl.BlockSpec((tm, D_MODEL), lambda i: (i, 0))]
    out_shape = [jax.ShapeDtypeStruct((m, D_MODEL), BF16)]
    if router_w is not None:
        in_specs.append(pl.BlockSpec((None, D_MODEL, N_EXPERTS), lambda i: (layer, 0, 0)))
        args.append(router_w)
        out_specs.append(pl.BlockSpec((tm, N_EXPERTS), lambda i: (i, 0)))
        out_shape.append(jax.ShapeDtypeStruct((m, N_EXPERTS), F32))
    return pl.pallas_call(
        functools.partial(_norm_mod_kernel, with_router=router_w is not None),
        grid=(m // tm,),
        in_specs=in_specs,
        out_specs=out_specs,
        out_shape=out_shape,
        compiler_params=_cparams("parallel"),
        name="norm_mod",
    )(*args)


def _matmul_kernel(*refs, n_a, k_sizes, with_res):
    a_refs = refs[:n_a]
    w_ref = refs[n_a]
    if with_res:
        res_ref, gate_ref, o_ref = refs[n_a + 1:]
    else:
        (o_ref,) = refs[n_a + 1:]
    w = w_ref[...].astype(BF16)
    acc = None
    k0 = 0
    for a_ref, k in zip(a_refs, k_sizes):
        part = _dot(a_ref[...], w[k0:k0 + k])
        acc = part if acc is None else acc + part
        k0 += k
    if with_res:
        acc = res_ref[...] + gate_ref[...] * acc
    o_ref[...] = acc.astype(o_ref.dtype)


def _matmul(a_list, w, layer, *, tm, tn, n_blocks, col_block0=0, out_dtype=F32,
            res=None, mod4=None, gate_idx=None, rows_per_cond=None, first_row=None):
    m = a_list[0].shape[0]
    k_sizes = tuple(a.shape[1] for a in a_list)
    k_total = sum(k_sizes)
    in_specs = [pl.BlockSpec((tm, k), lambda i, j: (i, 0)) for k in k_sizes]
    in_specs.append(pl.BlockSpec((None, k_total, tn), lambda i, j: (layer, 0, col_block0 + j)))
    args = list(a_list) + [w]
    if res is not None:
        in_specs.append(pl.BlockSpec((tm, tn), lambda i, j: (i, j)))
        in_specs.append(pl.BlockSpec((None, None, 1, tn),
                                     lambda i, j: (first_row + (i * tm) // rows_per_cond, gate_idx, 0, j)))
        args += [res, mod4]
    return pl.pallas_call(
        functools.partial(_matmul_kernel, n_a=len(a_list), k_sizes=k_sizes, with_res=res is not None),
        grid=(m // tm, n_blocks),
        in_specs=in_specs,
        out_specs=pl.BlockSpec((tm, tn), lambda i, j: (i, j)),
        out_shape=jax.ShapeDtypeStruct((m, n_blocks * tn), out_dtype),
        compiler_params=_cparams("parallel", "arbitrary"),
        name="matmul",
    )(*args)


def _pool_kernel(u_ref, w_ref, s_ref, o_ref, *, n):
    g = pl.program_id(1)
    u = u_ref[...]
    t = lax.broadcasted_iota(jnp.int32, (n, 1), 0)
    for gi, win in enumerate(POOL_WINDOWS):
        @pl.when(g == gi)
        def _(win=win):
            half = win // 2
            total = u
            for d in range(-half, win - half):
                if d == 0:
                    continue
                shifted = pltpu.roll(u, (-d) % n, axis=0)
                valid = jnp.logical_and(t + d >= 0, t + d < n)
                total = total + jnp.where(valid, shifted, 0.0)
            cnt = (jnp.minimum(t - half + win, n) - jnp.maximum(t - half, 0)).astype(F32)
            pooled = total / cnt - u
            out = _dot(pooled.astype(BF16), w_ref[...].astype(BF16)) * s_ref[...]
            o_ref[...] = out.astype(o_ref.dtype)


def _pool_mixer(proj3, pool_w, pool_scale, i_even):
    b, n, _ = proj3.shape
    scale4 = pool_scale.reshape(pool_scale.shape[0], POOL_GROUPS, 1, POOL_GROUP_DIM)
    return pl.pallas_call(
        functools.partial(_pool_kernel, n=n),
        grid=(b, POOL_GROUPS),
        in_specs=[
            pl.BlockSpec((None, n, POOL_GROUP_DIM), lambda bi, g: (bi, 0, g)),
            pl.BlockSpec((None, None, POOL_GROUP_DIM, POOL_GROUP_DIM), lambda bi, g: (i_even, g, 0, 0)),
            pl.BlockSpec((None, None, 1, POOL_GROUP_DIM), lambda bi, g: (i_even, g, 0, 0)),
        ],
        out_specs=pl.BlockSpec((None, n, POOL_GROUP_DIM), lambda bi, g: (bi, 0, g)),
        out_shape=jax.ShapeDtypeStruct((b, n, POOL_WIDTH), BF16),
        compiler_params=_cparams("parallel", "parallel"),
        name="pool_mixer",
    )(proj3, pool_w, scale4)


def _head_norm(x, g):
    return x * lax.rsqrt(jnp.mean(x * x, axis=-1, keepdims=True) + NORM_EPS) * g


_Q_COL0 = POOL_WIDTH // NA_HEAD_DIM
_K_COL0 = (POOL_WIDTH + NA_WIDTH) // NA_HEAD_DIM
_V_COL0 = (POOL_WIDTH + 2 * NA_WIDTH) // NA_HEAD_DIM


def _ctx_attn_kernel(q_ref, k_ref, v_ref, qg_ref, kg_ref, o_ref, kn_ref, vn_ref):
    qn = _head_norm(q_ref[...], qg_ref[...])
    kn = _head_norm(k_ref[...], kg_ref[...])
    v = v_ref[...]
    kn_ref[...] = kn
    vn_ref[...] = v
    s = _dot_nt(qn.astype(BF16), kn.astype(BF16)) * (NA_HEAD_DIM ** -0.5)
    m = jnp.max(s, axis=-1, keepdims=True)
    p = jnp.exp(s - m)
    denom = jnp.sum(p, axis=-1, keepdims=True)
    o = _dot(p.astype(BF16), v.astype(BF16)) / denom
    o_ref[...] = o.astype(o_ref.dtype)


def _ctx_attention(proj3, q_norm, k_norm, i_even):
    b, n, _ = proj3.shape
    qg = q_norm.reshape(q_norm.shape[0], 1, NA_HEAD_DIM)
    kg = k_norm.reshape(k_norm.shape[0], 1, NA_HEAD_DIM)
    head_blk = lambda col0: pl.BlockSpec((None, n, NA_HEAD_DIM), lambda bi, h: (bi, 0, col0 + h))
    gain = pl.BlockSpec((None, 1, NA_HEAD_DIM), lambda bi, h: (i_even, 0, 0))
    cache = pl.BlockSpec((None, None, None, n, NA_HEAD_DIM), lambda bi, h: (bi, 0, h, 0, 0))
    return pl.pallas_call(
        _ctx_attn_kernel,
        grid=(b, NA_HEADS),
        in_specs=[head_blk(_Q_COL0), head_blk(_K_COL0), head_blk(_V_COL0), gain, gain],
        out_specs=[pl.BlockSpec((None, n, NA_HEAD_DIM), lambda bi, h: (bi, 0, h)), cache, cache],
        out_shape=[
            jax.ShapeDtypeStruct((b, n, NA_WIDTH), BF16),
            jax.ShapeDtypeStruct((b, 1, NA_HEADS, n, NA_HEAD_DIM), F32),
            jax.ShapeDtypeStruct((b, 1, NA_HEADS, n, NA_HEAD_DIM), F32),
        ],
        compiler_params=_cparams("parallel", "parallel"),
        name="ctx_attention",
    )(proj3, proj3, proj3, qg, kg)


NA_BAND = 4
NA_BAND_KEY_ROWS = WIN_H + NA_BAND - 1


def _na_band_key_start(r0, rows):
    return np.clip(r0 - WIN_H // 2, 0, rows - NA_BAND_KEY_ROWS)


def _na_bias_table(rpb, rows):
    col = np.arange(GRID_W)[:, None]
    kc = np.arange(GRID_W)[None, :]
    wstart = np.clip(col - WIN_W // 2, 0, GRID_W - WIN_W)
    inside = (kc >= wstart) & (kc < wstart + WIN_W)
    rel = np.clip(kc - col + WIN_W - 1, 0, 2 * WIN_W - 2)
    a = np.arange(NA_BAND)[:, None]
    kr = np.arange(NA_BAND_KEY_ROWS)[None, :]
    ridx, row_ok = [], []
    for r0 in (0, NA_BAND, rows - NA_BAND):
        r = r0 + a
        sr = np.clip(r - WIN_H // 2, 0, rows - WIN_H)
        krow = _na_band_key_start(r0, rows) + kr
        row_ok.append((krow >= sr) & (krow < sr + WIN_H))
        ridx.append(np.clip(krow - r + WIN_H - 1, 0, 2 * WIN_H - 2))
    ridx = np.stack(ridx)
    ok = np.stack(row_ok)[:, :, :, None, None] & inside[None, None, None]
    tab = rpb[:, ridx][:, :, :, :, rel]
    tab = jnp.where(ok[None], tab, -jnp.inf)
    tab = jnp.transpose(tab, (0, 1, 2, 4, 3, 5))
    return tab.reshape(rpb.shape[0], 3, NA_BAND * GRID_W, NA_BAND_KEY_ROWS * GRID_W)


def _na_kernel(q_ref, k_ref, v_ref, kc_ref, vc_ref, qg_ref, kg_ref, bias_ref, o_ref,
               qs_ref, ks_ref, vs_ref, *, rows):
    qs_ref[...] = _head_norm(q_ref[...], qg_ref[...]).astype(BF16)
    ks_ref[...] = _head_norm(k_ref[...], kg_ref[...]).astype(BF16)
    vs_ref[...] = v_ref[...].astype(BF16)
    k_ctx = kc_ref[...].astype(BF16)
    v_ctx = vc_ref[...].astype(BF16)
    scale = NA_HEAD_DIM ** -0.5
    n_bands = rows // NA_BAND
    n_q = NA_BAND * GRID_W
    n_loc = NA_BAND_KEY_ROWS * GRID_W

    def band(i):
        r0 = i * NA_BAND
        key_row0 = jnp.clip(r0 - WIN_H // 2, 0, rows - NA_BAND_KEY_ROWS)
        variant = jnp.where(i == 0, 0, jnp.where(i == n_bands - 1, 2, 1))
        q0 = pl.multiple_of(r0 * GRID_W, n_q)
        q_b = qs_ref[pl.ds(q0, n_q), :]
        kstart = pl.multiple_of(key_row0 * GRID_W, GRID_W)
        k_blk = ks_ref[pl.ds(kstart, n_loc), :]
        v_blk = vs_ref[pl.ds(kstart, n_loc), :]
        s_loc = _dot_nt(q_b, k_blk) * scale + bias_ref[variant]
        s_ctx = _dot_nt(q_b, k_ctx) * scale
        m = jnp.maximum(jnp.max(s_loc, axis=-1, keepdims=True), jnp.max(s_ctx, axis=-1, keepdims=True))
        p_loc = jnp.exp(s_loc - m)
        p_ctx = jnp.exp(s_ctx - m)
        denom = jnp.sum(p_loc, axis=-1, keepdims=True) + jnp.sum(p_ctx, axis=-1, keepdims=True)
        o = (_dot(p_loc.astype(BF16), v_blk) + _dot(p_ctx.astype(BF16), v_ctx)) / denom
        o_ref[pl.ds(q0, n_q), :] = o.astype(o_ref.dtype)

    per_trip = 2 if n_bands % 2 == 0 else 1

    def trip(i, carry):
        for k in range(per_trip):
            band(i * per_trip + k)
        return carry

    lax.fori_loop(0, n_bands // per_trip, trip, 0)


def _neighbourhood_attention(proj3, cache_k, cache_v, q_norm, k_norm, rpb, i_even):
    b, t, _ = proj3.shape
    rows = t // GRID_W
    assert rows % NA_BAND == 0 and rows >= NA_BAND_KEY_ROWS + 1, rows
    n_ctx = cache_k.shape[3]
    bias = _na_bias_table(rpb[i_even], rows)
    qg = q_norm.reshape(q_norm.shape[0], 1, NA_HEAD_DIM)
    kg = k_norm.reshape(k_norm.shape[0], 1, NA_HEAD_DIM)
    head_blk = lambda col0: pl.BlockSpec((None, t, NA_HEAD_DIM), lambda bi, h: (bi, 0, col0 + h))
    gain = pl.BlockSpec((None, 1, NA_HEAD_DIM), lambda bi, h: (i_even, 0, 0))
    cache = pl.BlockSpec((None, None, None, n_ctx, NA_HEAD_DIM), lambda bi, h: (bi, i_even, h, 0, 0))
    return pl.pallas_call(
        functools.partial(_na_kernel, rows=rows),
        grid=(b, NA_HEADS),
        in_specs=[head_blk(_Q_COL0), head_blk(_K_COL0), head_blk(_V_COL0), cache, cache, gain, gain,
                  pl.BlockSpec((None,) + bias.shape[1:], lambda bi, h: (h, 0, 0, 0))],
        out_specs=pl.BlockSpec((None, t, NA_HEAD_DIM), lambda bi, h: (bi, 0, h)),
        out_shape=jax.ShapeDtypeStruct((b, t, NA_WIDTH), BF16),
        scratch_shapes=[pltpu.VMEM((t, NA_HEAD_DIM), BF16)] * 3,
        compiler_params=_cparams("parallel", "parallel"),
        name="neighbourhood_attention",
    )(proj3, proj3, proj3, cache_k, cache_v, qg, kg, bias)


def _softplus(x):
    return jnp.maximum(x, 0.0) + jnp.log1p(jnp.exp(-jnp.abs(x)))


def _ssd_kernel(*refs, l, with_init):
    (x_ref, b_ref, c_ref, dt_ref, cwx_ref, cwb_ref, cwc_ref, cbx_ref, cbb_ref, cbc_ref,
     alog_ref, dtbias_ref, dsk_ref) = refs[:13]
    if with_init:
        init_ref = refs[13]
        outs = refs[14:]
    else:
        outs = refs[13:]
    (y_ref, sfin_ref, xs_ref, bs_ref, cs_ref, stf_ref, stb_ref,
     cumf_ref, cumb_ref, srcf_ref, srcb_ref) = outs
    g = pl.program_id(1)
    q = SSD_CHUNK
    nc = l // q
    gw = SSD_GROUP_WIDTH
    gh = SSD_GROUP_HEADS
    pad_l = SSD_CONV // 2
    halo = 8

    win_rows = q + 2 * halo
    sel_col = lax.broadcasted_iota(jnp.int32, (q, SSD_CONV * win_rows), 1)
    sel_row = lax.broadcasted_iota(jnp.int32, (q, SSD_CONV * win_rows), 0)
    shift_mat = ((sel_col % win_rows) == sel_row + (halo - pad_l) + sel_col // win_rows).astype(BF16)

    def conv_chunk(c, carry):
        t0 = pl.multiple_of(c * q, q)
        lo_start = pl.multiple_of(jnp.maximum(t0 - halo, 0), halo)
        hi_start = pl.multiple_of(jnp.minimum(t0 + q, l - halo), halo)
        for src, w_ref, bias_ref, dst in ((x_ref, cwx_ref, cbx_ref, xs_ref),
                                          (b_ref, cwb_ref, cbb_ref, bs_ref),
                                          (c_ref, cwc_ref, cbc_ref, cs_ref)):
            lo = jnp.where(c > 0, src[pl.ds(lo_start, halo), :], 0.0)
            hi = jnp.where(c < nc - 1, src[pl.ds(hi_start, halo), :], 0.0)
            win = jnp.concatenate([lo, src[pl.ds(t0, q), :], hi], axis=0)
            w = w_ref[...]
            taps = jnp.concatenate([(win * w[j:j + 1, :]).astype(BF16) for j in range(SSD_CONV)], axis=0)
            out = _silu(_dot(shift_mat, taps) + bias_ref[...])
            dst[pl.ds(t0, q), :] = out
            if dst is xs_ref:
                y_ref[pl.ds(t0, q), :] = out * dsk_ref[...]
        dt = _softplus(pltpu.roll(dt_ref[pl.ds(t0, q), :], to_lane0, axis=1) + bias_g)
        dta = dt * a_g
        cum_f = _dot_f32_rhs(tri_f, dta)
        cum_b = _dot_f32_rhs(tri_b, dta)
        cumf_ref[pl.ds(t0, q), :] = cum_f
        cumb_ref[pl.ds(t0, q), :] = cum_b
        log_dt = jnp.log(dt)
        srcf_ref[pl.ds(t0, q), :] = jnp.transpose(cum_f - log_dt)
        srcb_ref[pl.ds(t0, q), :] = jnp.transpose(cum_b - log_dt)
        return carry

    ii = lax.broadcasted_iota(jnp.int32, (q, q), 0)
    jj = lax.broadcasted_iota(jnp.int32, (q, q), 1)
    keep_f = jj <= ii
    keep_b = jj >= ii
    tri_f = keep_f.astype(BF16)
    tri_b = keep_b.astype(BF16)
    n_dt = 2 * SSD_HEADS
    assert q == n_dt
    to_lane0 = (n_dt - g * gh) % n_dt
    bias_g = pltpu.roll(jnp.broadcast_to(dtbias_ref[...], (8, n_dt)), to_lane0, axis=1)[0:1, :]
    a_g = pltpu.roll(jnp.broadcast_to(-jnp.exp(alog_ref[...]), (8, n_dt)), to_lane0, axis=1)[0:1, :]

    lax.fori_loop(0, nc, conv_chunk, 0)

    pair_w = 2 * SSD_HEAD_DIM
    lane = lax.broadcasted_iota(jnp.int32, (1, gw), 1)
    half_mask = [((lane % pair_w) // SSD_HEAD_DIM) == s for s in range(2)]
    first_head_lanes = lax.broadcasted_iota(jnp.int32, (1, pair_w), 1) < SSD_HEAD_DIM

    def scan_chunk(c, reverse):
        keep, edge, lane0 = (keep_b, 0, SSD_HEADS) if reverse else (keep_f, q - 1, 0)
        cum_ref, src_ref, st_ref = ((cumb_ref, srcb_ref, stb_ref) if reverse
                                    else (cumf_ref, srcf_ref, stf_ref))
        t0 = pl.multiple_of(c * q, q)
        xc = xs_ref[pl.ds(t0, q), :]
        bc = bs_ref[pl.ds(t0, q), :]
        cc = cs_ref[pl.ds(t0, q), :]
        cb = _dot_nt(cc.astype(BF16), bc.astype(BF16))
        bct = jnp.transpose(bc)
        x_bf = xc.astype(BF16)
        st = st_ref[...]
        st_bf = st.astype(BF16)
        zero = jnp.zeros((), BF16)
        x_half = [jnp.where(half_mask[s], x_bf, zero) for s in range(2)]
        st_half = [jnp.where(half_mask[s], st_bf, zero) for s in range(2)]
        pieces = []
        for pair in range(gh // 2):
            c0 = pair * pair_w
            lhs_y, lhs_s, carry_decay = [], [], []
            for e in (2 * pair, 2 * pair + 1):
                ln = lane0 + e
                col = jnp.broadcast_to(cum_ref[pl.ds(t0, q), ln:ln + 1], (q, q))
                src = src_ref[pl.ds(t0 + ln, 1), :]
                at_edge = cum_ref[pl.ds(t0 + edge, 1), ln:ln + 1]
                mix = jnp.exp(jnp.where(keep, col - src, -jnp.inf))
                lhs_y.append((cb * mix).astype(BF16))
                lhs_y.append((cc * jnp.exp(col)).astype(BF16))
                lhs_s.append((bct * jnp.exp(at_edge - src)).astype(BF16))
                carry_decay.append(jnp.exp(at_edge))
            rhs_y = jnp.concatenate([x_half[0][:, c0:c0 + pair_w], st_half[0][:, c0:c0 + pair_w],
                                     x_half[1][:, c0:c0 + pair_w], st_half[1][:, c0:c0 + pair_w]], axis=0)
            pieces.append(_dot(jnp.concatenate(lhs_y, axis=1), rhs_y))
            rhs_s = jnp.concatenate([x_half[0][:, c0:c0 + pair_w], x_half[1][:, c0:c0 + pair_w]], axis=0)
            keep_frac = jnp.where(first_head_lanes, carry_decay[0], carry_decay[1])
            st_ref[:, c0:c0 + pair_w] = (st[:, c0:c0 + pair_w] * keep_frac
                                         + _dot(jnp.concatenate(lhs_s, axis=1), rhs_s))
        y_ref[pl.ds(t0, q), :] = y_ref[pl.ds(t0, q), :] + jnp.concatenate(pieces, axis=1)

    for d, st_ref in enumerate((stf_ref, stb_ref)):
        if with_init:
            st_ref[...] = jnp.transpose(init_ref[d].reshape(gw, SSD_STATE))
        else:
            st_ref[...] = jnp.zeros_like(st_ref)

    def step(s, carry):
        scan_chunk(s, False)
        scan_chunk(nc - 1 - s, True)
        return carry

    lax.fori_loop(0, nc, step, 0)
    for d, st_ref in enumerate((stf_ref, stb_ref)):
        sfin_ref[d] = jnp.transpose(st_ref[...]).reshape(gh, SSD_HEAD_DIM, SSD_STATE)


def _ssd_scan(proj3, dt_raw3, conv_w, conv_b, a_log, dt_bias, d_skip, i_odd, init=None):
    b, l, _ = proj3.shape
    gh, gw = SSD_GROUP_HEADS, SSD_GROUP_WIDTH
    n_dt = 2 * SSD_HEADS
    a_log3 = a_log.reshape(a_log.shape[0], 1, n_dt)
    dt_bias3 = dt_bias.reshape(dt_bias.shape[0], 1, n_dt)
    dsk = jnp.repeat(d_skip[i_odd].astype(F32), SSD_HEAD_DIM).reshape(1, SSD_INNER)
    conv_b3 = conv_b.reshape(conv_b.shape[0], 1, conv_b.shape[1])

    x_col0 = SSD_INNER // gw
    b_col0 = (2 * SSD_INNER) // SSD_STATE
    c_col0 = (2 * SSD_INNER + SSD_BC_WIDTH) // SSD_STATE
    cwx_col0 = 0
    cwb_col0 = SSD_INNER // SSD_STATE
    cwc_col0 = (SSD_INNER + SSD_BC_WIDTH) // SSD_STATE

    in_specs = [
        pl.BlockSpec((None, l, gw), lambda bi, g: (bi, 0, x_col0 + g)),
        pl.BlockSpec((None, l, SSD_STATE), lambda bi, g: (bi, 0, b_col0 + g)),
        pl.BlockSpec((None, l, SSD_STATE), lambda bi, g: (bi, 0, c_col0 + g)),
        pl.BlockSpec((None, l, n_dt), lambda bi, g: (bi, 0, 0)),
        pl.BlockSpec((None, SSD_CONV, gw), lambda bi, g: (i_odd, 0, cwx_col0 + g)),
        pl.BlockSpec((None, SSD_CONV, SSD_STATE), lambda bi, g: (i_odd, 0, cwb_col0 + g)),
        pl.BlockSpec((None, SSD_CONV, SSD_STATE), lambda bi, g: (i_odd, 0, cwc_col0 + g)),
        pl.BlockSpec((None, 1, gw), lambda bi, g: (i_odd, 0, cwx_col0 + g)),
        pl.BlockSpec((None, 1, SSD_STATE), lambda bi, g: (i_odd, 0, cwb_col0 + g)),
        pl.BlockSpec((None, 1, SSD_STATE), lambda bi, g: (i_odd, 0, cwc_col0 + g)),
        pl.BlockSpec((None, 1, n_dt), lambda bi, g: (i_odd, 0, 0)),
        pl.BlockSpec((None, 1, n_dt), lambda bi, g: (i_odd, 0, 0)),
        pl.BlockSpec((1, gw), lambda bi, g: (0, g)),
    ]
    args = [proj3, proj3, proj3, dt_raw3, conv_w, conv_w, conv_w,
            conv_b3, conv_b3, conv_b3, a_log3, dt_bias3, dsk]
    if init is not None:
        in_specs.append(pl.BlockSpec((None, None, 2, gh, SSD_HEAD_DIM, SSD_STATE),
                                     lambda bi, g: (bi, i_odd, 0, g, 0, 0)))
        args.append(init)
    return pl.pallas_call(
        functools.partial(_ssd_kernel, l=l, with_init=init is not None),
        grid=(b, SSD_GROUPS),
        in_specs=in_specs,
        out_specs=[
            pl.BlockSpec((None, l, gw), lambda bi, g: (bi, 0, g)),
            pl.BlockSpec((None, 2, gh, SSD_HEAD_DIM, SSD_STATE), lambda bi, g: (bi, 0, g, 0, 0)),
        ],
        out_shape=[
            jax.ShapeDtypeStruct((b, l, SSD_INNER), F32),
            jax.ShapeDtypeStruct((b, 2, SSD_HEADS, SSD_HEAD_DIM, SSD_STATE), F32),
        ],
        scratch_shapes=[
            pltpu.VMEM((l, gw), F32),
            pltpu.VMEM((l, SSD_STATE), F32),
            pltpu.VMEM((l, SSD_STATE), F32),
            pltpu.VMEM((SSD_STATE, gw), F32),
            pltpu.VMEM((SSD_STATE, gw), F32),
        ] + [pltpu.VMEM((l, n_dt), F32)] * 4,
        compiler_params=_cparams("parallel", "parallel"),
        name="ssd_scan",
    )(*args)


def _gate_norm_kernel(y_ref, z_ref, g_ref, o_ref):
    y = y_ref[...] * _silu(z_ref[...])
    o = y * lax.rsqrt(jnp.mean(y * y, axis=-1, keepdims=True) + NORM_EPS) * g_ref[...]
    o_ref[...] = o.astype(o_ref.dtype)


def _gate_norm(y, proj, norm_g, i_odd):
    m = y.shape[0]
    tm = ROW_TILE
    g3 = norm_g.reshape(norm_g.shape[0], 1, SSD_INNER)
    return pl.pallas_call(
        _gate_norm_kernel,
        grid=(m // tm,),
        in_specs=[
            pl.BlockSpec((tm, SSD_INNER), lambda i: (i, 0)),
            pl.BlockSpec((tm, SSD_INNER), lambda i: (i, 0)),
            pl.BlockSpec((None, 1, SSD_INNER), lambda i: (i_odd, 0, 0)),
        ],
        out_specs=pl.BlockSpec((tm, SSD_INNER), lambda i: (i, 0)),
        out_shape=jax.ShapeDtypeStruct((m, SSD_INNER), BF16),
        compiler_params=_cparams("parallel"),
        name="gate_norm",
    )(y, proj, g3)


def _expert_kernel(x_ref, wg_ref, wu_ref, wd_ref, gate_ref, o_ref, acc_ref):
    f = pl.program_id(2)
    x = x_ref[...]
    hg = _dot(x, wg_ref[...].astype(BF16))
    hu = _dot(x, wu_ref[...].astype(BF16))
    hdn = (_silu(hg) * hu).astype(BF16)

    @pl.when(f == 0)
    def _():
        acc_ref[...] = jnp.zeros_like(acc_ref)

    acc_ref[...] += _dot(hdn, wd_ref[...].astype(BF16))

    @pl.when(f == pl.num_programs(2) - 1)
    def _():
        o_ref[...] = (acc_ref[...] * gate_ref[...]).astype(o_ref.dtype)


def _expert_ffn(xe, gate, w_gate, w_up, w_down, layer):
    e, r, _ = xe.shape
    tr = min(r, 1024)
    tf = 256
    return pl.pallas_call(
        _expert_kernel,
        grid=(e, r // tr, EXPERT_FF // tf),
        in_specs=[
            pl.BlockSpec((None, tr, D_MODEL), lambda ei, ri, f: (ei, ri, 0)),
            pl.BlockSpec((None, None, D_MODEL, tf), lambda ei, ri, f: (layer, ei, 0, f)),
            pl.BlockSpec((None, None, D_MODEL, tf), lambda ei, ri, f: (layer, ei, 0, f)),
            pl.BlockSpec((None, None, tf, D_MODEL), lambda ei, ri, f: (layer, ei, f, 0)),
            pl.BlockSpec((None, tr, 1), lambda ei, ri, f: (ei, ri, 0)),
        ],
        out_specs=pl.BlockSpec((None, tr, D_MODEL), lambda ei, ri, f: (ei, ri, 0)),
        out_shape=jax.ShapeDtypeStruct((e, r, D_MODEL), BF16),
        scratch_shapes=[pltpu.VMEM((tr, D_MODEL), F32)],
        compiler_params=_cparams("parallel", "parallel", "arbitrary"),
        name="expert_ffn",
    )(xe, w_gate, w_up, w_down, gate)


def _combine_kernel(idx_ref, ye_ref, x_ref, gate_ref, o_ref, *, n, ec, cap):
    step = pl.program_id(2)
    token = lax.broadcasted_iota(jnp.int32, (n, cap), 0)
    onehot = jnp.concatenate(
        [jnp.where(token == idx_ref[k], 1.0, 0.0).astype(BF16) for k in range(ec)], axis=1)

    @pl.when(step == 0)
    def _():
        o_ref[...] = jnp.zeros_like(o_ref)

    o_ref[...] += _dot(onehot, ye_ref[...].reshape(ec * cap, ye_ref.shape[-1]))

    @pl.when(step == pl.num_programs(2) - 1)
    def _():
        o_ref[...] = x_ref[...] + gate_ref[...] * o_ref[...]


def _expert_combine(ye, idx, x, mod4, gate_idx, first_row, shared_cond):
    n_req, n_exp, cap = idx.shape
    m = x.shape[0]
    n = m // n_req
    ec = max(1, min(n_exp, 512 // cap))
    tn = D_MODEL // 2
    idx4 = idx.reshape(n_req, n_exp, 1, cap)
    cond_row = (lambda b: first_row) if shared_cond else (lambda b: first_row + b)
    return pl.pallas_call(
        functools.partial(_combine_kernel, n=n, ec=ec, cap=cap),
        grid=(n_req, D_MODEL // tn, n_exp // ec),
        in_specs=[
            pl.BlockSpec((None, ec, 1, cap), lambda b, j, s: (b, s, 0, 0)),
            pl.BlockSpec((ec, cap, tn), lambda b, j, s: (s, b, j)),
            pl.BlockSpec((n, tn), lambda b, j, s: (b, j)),
            pl.BlockSpec((None, None, 1, tn), lambda b, j, s: (cond_row(b), gate_idx, 0, j)),
        ],
        out_specs=pl.BlockSpec((n, tn), lambda b, j, s: (b, j)),
        out_shape=jax.ShapeDtypeStruct((m, D_MODEL), F32),
        compiler_params=_cparams("parallel", "parallel", "arbitrary"),
        name="expert_combine",
    )(idx4, ye, x, mod4)


def _expert_choice(h, logits, x, n_req, w_gate, w_up, w_down, layer, mod4, gate_idx, first_row, shared_cond):
    m = h.shape[0]
    n = m // n_req
    cap = EC_CAPACITY_FACTOR * n // N_EXPERTS
    aff = jax.nn.softmax(logits.reshape(n_req, n, N_EXPERTS), axis=-1)
    gate, idx = lax.top_k(jnp.swapaxes(aff, 1, 2), cap)
    rows = idx + (jnp.arange(n_req, dtype=idx.dtype) * n)[:, None, None]
    rows = jnp.transpose(rows, (1, 0, 2)).reshape(N_EXPERTS, n_req * cap)
    gate = jnp.transpose(gate, (1, 0, 2)).reshape(N_EXPERTS, n_req * cap, 1)
    xe = h[rows]
    ye = _expert_ffn(xe, gate, w_gate, w_up, w_down, layer)
    return _expert_combine(ye, idx, x, mod4, gate_idx, first_row, shared_cond)


class _Stream:
    def __init__(self, x3, first_row, shared_cond):
        self.batch, self.n, _ = x3.shape
        self.x = x3.reshape(self.batch * self.n, D_MODEL)
        self.first_row = first_row
        self.shared_cond = shared_cond
        self.rows_per_cond = self.batch * self.n if shared_cond else self.n

    def mod_args(self):
        return dict(rows_per_cond=self.rows_per_cond, first_row=self.first_row)


def kernel(x_prompt, x_sample, cache_na_k, cache_na_v, state_ssd, c, c_ctx, ada_w, ada_b, norm1_g, norm2_g,
           router_w, exp_w_gate, exp_w_up, exp_w_down, ab_w_in, pool_w, pool_scale, na_q_norm, na_k_norm,
           na_rpb, ab_w_out, ssd_w_in, ssd_conv_w, ssd_conv_b, ssd_a_log, ssd_dt_bias, ssd_d, ssd_norm_g,
           ssd_w_out):
    depth = ada_w.shape[0]
    dec_batch = x_sample.shape[0]
    latent = _Stream(x_sample, 0, shared_cond=False)
    prompt = _Stream(x_prompt, dec_batch, shared_cond=True)
    cond = jnp.concatenate(
        [c, c_ctx[None, :], jnp.zeros((MOD_ROWS - dec_batch - 1, D_MODEL), F32)], axis=0)

    new_k, new_v, new_s = [], [], []
    for layer in range(depth):
        i = layer // 2
        mod4 = _modulation(cond, ada_w, ada_b, layer).reshape(MOD_ROWS, N_MOD, 1, D_MODEL)
        for s in (prompt, latent):
            margs = s.mod_args()
            h = _norm_mod(s.x, norm1_g, mod4, layer, 0, 1, **margs)[0]
            tm = min(s.x.shape[0], 2048)
            if layer % 2 == 0:
                proj = _matmul([h], ab_w_in, i, tm=tm, tn=512, n_blocks=ab_w_in.shape[2] // 512)
                proj3 = proj.reshape(s.batch, s.n, -1)
                pooled = _pool_mixer(proj3, pool_w, pool_scale, i)
                if s is prompt:
                    att, k_new, v_new = _ctx_attention(proj3, na_q_norm, na_k_norm, i)
                    new_k.append(k_new)
                    new_v.append(v_new)
                else:
                    att = _neighbourhood_attention(proj3, cache_na_k, cache_na_v, na_q_norm, na_k_norm,
                                                   na_rpb, i)
                s.x = _matmul([pooled.reshape(-1, POOL_WIDTH), att.reshape(-1, NA_WIDTH)], ab_w_out, i,
                              tm=tm, tn=512, n_blocks=D_MODEL // 512, res=s.x, mod4=mod4, gate_idx=2,
                              **margs)
            else:
                proj = _matmul([h], ssd_w_in, i, tm=tm, tn=512, n_blocks=SSD_MAIN_DIM // 512)
                dt_raw = _matmul([h], ssd_w_in, i, tm=tm, tn=2 * SSD_HEADS, n_blocks=1,
                                 col_block0=SSD_MAIN_DIM // (2 * SSD_HEADS))
                proj3 = proj.reshape(s.batch, s.n, SSD_MAIN_DIM)
                dt3 = dt_raw.reshape(s.batch, s.n, 2 * SSD_HEADS)
                init = None if s is prompt else state_ssd
                y, s_fin = _ssd_scan(proj3, dt3, ssd_conv_w, ssd_conv_b, ssd_a_log, ssd_dt_bias, ssd_d, i,
                                     init=init)
                if s is prompt:
                    new_s.append(s_fin[:, None])
                yn = _gate_norm(y.reshape(-1, SSD_INNER), proj, ssd_norm_g, i)
                s.x = _matmul([yn], ssd_w_out, i, tm=min(tm, 1024), tn=512, n_blocks=D_MODEL // 512,
                              res=s.x, mod4=mod4, gate_idx=2, **margs)
            h2, logits = _norm_mod(s.x, norm2_g, mod4, layer, 3, 4, router_w=router_w, **margs)
            s.x = _expert_choice(h2, logits, s.x, s.batch, exp_w_gate, exp_w_up, exp_w_down, layer,
                                 mod4, 5, s.first_row, s.shared_cond)

    y_prompt = prompt.x.reshape(x_prompt.shape)
    y_sample = latent.x.reshape(x_sample.shape)
    return (y_prompt, y_sample, jnp.concatenate(new_k, axis=1), jnp.concatenate(new_v, axis=1),
            jnp.concatenate(new_s, axis=1))
```

```python
import functools

import numpy as np
import jax
import jax.numpy as jnp
from jax import lax
from jax.experimental import pallas as pl
from jax.experimental.pallas import tpu as pltpu

F32 = jnp.float32
BF16 = jnp.bfloat16

D_MODEL = 2048
GRID_W = 64
POOL_WIDTH = 1024
POOL_GROUPS = 4
POOL_GROUP_DIM = 256
POOL_WINDOWS = (2, 4, 8, 16)
NA_WIDTH = 1024
NA_HEAD_DIM = 128
NA_HEADS = 8
WIN_H = 8
WIN_W = 16
SSD_INNER = 4096
SSD_HEAD_DIM = 64
SSD_HEADS = 64
SSD_GROUPS = 8
SSD_GROUP_HEADS = SSD_HEADS // SSD_GROUPS
SSD_GROUP_WIDTH = SSD_GROUP_HEADS * SSD_HEAD_DIM
SSD_STATE = 128
SSD_CONV = 4
SSD_CHUNK = 128
SSD_BC_WIDTH = SSD_GROUPS * SSD_STATE
SSD_MAIN_DIM = 2 * SSD_INNER + 2 * SSD_BC_WIDTH
N_EXPERTS = 16
EXPERT_FF = 1024
EC_CAPACITY_FACTOR = 2
NORM_EPS = 1e-6
N_MOD = 6
MOD_ROWS = 16
ROUTE_LANES = 128
ROUTE_TOKEN_LANE = 3 * N_EXPERTS
ROUTE_TOKEN_RADIX = 64
ROUTE_ROWS_PER_STEP = 8
PREFIX_BLOCK = 256

VMEM_LIMIT_BYTES = 56 * 1024 * 1024
ROW_TILE = 256


def _cparams(*sem):
    return pltpu.CompilerParams(dimension_semantics=sem, vmem_limit_bytes=VMEM_LIMIT_BYTES)


def _silu(x):
    return x * (1.0 / (1.0 + jnp.exp(-x)))


def _split2(x):
    hi = x.astype(BF16)
    lo = (x - hi.astype(F32)).astype(BF16)
    return hi, lo


def _split3(x):
    hi = x.astype(BF16)
    r = x - hi.astype(F32)
    mid = r.astype(BF16)
    lo = (r - mid.astype(F32)).astype(BF16)
    return hi, mid, lo


def _dot(a, b):
    return jnp.dot(a, b, preferred_element_type=F32)


def _dot_nt(a, b):
    return lax.dot_general(a, b, (((1,), (1,)), ((), ())), preferred_element_type=F32)


def _dot_f32_lhs(x, m):
    hi, mid, lo = _split3(x)
    return _dot(hi, m) + _dot(mid, m) + _dot(lo, m)


def _dot_f32_rhs(m, x):
    hi, mid, lo = _split3(x)
    return _dot(m, hi) + _dot(m, mid) + _dot(m, lo)


def _dot_split(a, b):
    a_hi, a_lo = _split2(a)
    b_hi, b_lo = _split2(b)
    return _dot(a_hi, b_hi) + _dot(a_lo, b_hi) + _dot(a_hi, b_lo)


def _mod_kernel(c_ref, w_ref, b_ref, o_ref):
    s = _silu(c_ref[...])
    o_ref[...] = _dot_split(s, w_ref[...]) + b_ref[...]


def _modulation(cond, ada_w, ada_b, layer):
    tn = 768
    n = N_MOD * D_MODEL
    ada_b3 = ada_b.reshape(ada_b.shape[0], 1, n)
    return pl.pallas_call(
        _mod_kernel,
        grid=(n // tn,),
        in_specs=[
            pl.BlockSpec((MOD_ROWS, D_MODEL), lambda j: (0, 0)),
            pl.BlockSpec((None, D_MODEL, tn), lambda j: (layer, 0, j)),
            pl.BlockSpec((None, 1, tn), lambda j: (layer, 0, j)),
        ],
        out_specs=pl.BlockSpec((MOD_ROWS, tn), lambda j: (0, j)),
        out_shape=jax.ShapeDtypeStruct((MOD_ROWS, n), F32),
        compiler_params=_cparams("parallel"),
        name="modulation",
    )(cond, ada_w, ada_b3)


def _mod_spec(which, rows_per_cond, first_row, tm):
    return pl.BlockSpec((None, None, 1, D_MODEL),
                        lambda i, *_: (first_row + (i * tm) // rows_per_cond, which, 0, 0))


def _norm_mod_kernel(x_ref, g_ref, sh_ref, sc_ref, *rest, with_router):
    x = x_ref[...]
    y = x * lax.rsqrt(jnp.mean(x * x, axis=-1, keepdims=True) + NORM_EPS) * g_ref[...]
    h = y * (1.0 + sc_ref[...]) + sh_ref[...]
    if with_router:
        rw_ref, h_ref, lg_ref = rest
        lg_ref[...] = _dot_split(h, rw_ref[...])
    else:
        (h_ref,) = rest
    h_ref[...] = h.astype(BF16)


def _norm_mod(x, g, mod4, layer, shift_idx, scale_idx, rows_per_cond, first_row, router_w=None):
    m = x.shape[0]
    tm = 2 * ROW_TILE
    g3 = g.reshape(g.shape[0], 1, D_MODEL)
    in_specs = [
        pl.BlockSpec((tm, D_MODEL), lambda i: (i, 0)),
        pl.BlockSpec((None, 1, D_MODEL), lambda i: (layer, 0, 0)),
        _mod_spec(shift_idx, rows_per_cond, first_row, tm),
        _mod_spec(scale_idx, rows_per_cond, first_row, tm),
    ]
    args = [x, g3, mod4, mod4]
    out_specs = [pl.BlockSpec((tm, D_MODEL), lambda i: (i, 0))]
    out_shape = [jax.ShapeDtypeStruct((m, D_MODEL), BF16)]
    if router_w is not None:
        in_specs.append(pl.BlockSpec((None, D_MODEL, ROUTE_LANES), lambda i: (layer, 0, 0)))
        args.append(router_w)
        out_specs.append(pl.BlockSpec((tm, ROUTE_LANES), lambda i: (i, 0)))
        out_shape.append(jax.ShapeDtypeStruct((m, ROUTE_LANES), F32))
    return pl.pallas_call(
        functools.partial(_norm_mod_kernel, with_router=router_w is not None),
        grid=(m // tm,),
        in_specs=in_specs,
        out_specs=out_specs,
        out_shape=out_shape,
        compiler_params=_cparams("parallel"),
        name="norm_mod",
    )(*args)


def _matmul_kernel(*refs, n_a, k_sizes, with_res):
    a_refs = refs[:n_a]
    w_ref = refs[n_a]
    if with_res:
        res_ref, gate_ref, o_ref = refs[n_a + 1:]
    else:
        (o_ref,) = refs[n_a + 1:]
    w = w_ref[...].astype(BF16)
    acc = None
    k0 = 0
    for a_ref, k in zip(a_refs, k_sizes):
        part = _dot(a_ref[...], w[k0:k0 + k])
        acc = part if acc is None else acc + part
        k0 += k
    if with_res:
        acc = res_ref[...] + gate_ref[...] * acc
    o_ref[...] = acc.astype(o_ref.dtype)


def _matmul(a_list, w, layer, *, tm, tn, n_blocks, col_block0=0, out_dtype=F32,
            res=None, mod4=None, gate_idx=None, rows_per_cond=None, first_row=None):
    m = a_list[0].shape[0]
    k_sizes = tuple(a.shape[1] for a in a_list)
    k_total = sum(k_sizes)
    in_specs = [pl.BlockSpec((tm, k), lambda i, j: (i, 0)) for k in k_sizes]
    in_specs.append(pl.BlockSpec((None, k_total, tn), lambda i, j: (layer, 0, col_block0 + j)))
    args = list(a_list) + [w]
    if res is not None:
        in_specs.append(pl.BlockSpec((tm, tn), lambda i, j: (i, j)))
        in_specs.append(pl.BlockSpec((None, None, 1, tn),
                                     lambda i, j: (first_row + (i * tm) // rows_per_cond, gate_idx, 0, j)))
        args += [res, mod4]
    return pl.pallas_call(
        functools.partial(_matmul_kernel, n_a=len(a_list), k_sizes=k_sizes, with_res=res is not None),
        grid=(m // tm, n_blocks),
        in_specs=in_specs,
        out_specs=pl.BlockSpec((tm, tn), lambda i, j: (i, j)),
        out_shape=jax.ShapeDtypeStruct((m, n_blocks * tn), out_dtype),
        compiler_params=_cparams("parallel", "arbitrary"),
        name="matmul",
    )(*args)


def _pool_kernel(u_ref, w_ref, s_ref, o_ref, *, n):
    g = pl.program_id(1)
    u = u_ref[...]
    t = lax.broadcasted_iota(jnp.int32, (n, 1), 0)
    for gi, win in enumerate(POOL_WINDOWS):
        @pl.when(g == gi)
        def _(win=win):
            half = win // 2
            total = u
            for d in range(-half, win - half):
                if d == 0:
                    continue
                shifted = pltpu.roll(u, (-d) % n, axis=0)
                valid = jnp.logical_and(t + d >= 0, t + d < n)
                total = total + jnp.where(valid, shifted, 0.0)
            cnt = (jnp.minimum(t - half + win, n) - jnp.maximum(t - half, 0)).astype(F32)
            pooled = total / cnt - u
            out = _dot(pooled.astype(BF16), w_ref[...].astype(BF16)) * s_ref[...]
            o_ref[...] = out.astype(o_ref.dtype)


def _pool_mixer(proj3, pool_w, pool_scale, i_even):
    b, n, _ = proj3.shape
    scale4 = pool_scale.reshape(pool_scale.shape[0], POOL_GROUPS, 1, POOL_GROUP_DIM)
    return pl.pallas_call(
        functools.partial(_pool_kernel, n=n),
        grid=(b, POOL_GROUPS),
        in_specs=[
            pl.BlockSpec((None, n, POOL_GROUP_DIM), lambda bi, g: (bi, 0, g)),
            pl.BlockSpec((None, None, POOL_GROUP_DIM, POOL_GROUP_DIM), lambda bi, g: (i_even, g, 0, 0)),
            pl.BlockSpec((None, None, 1, POOL_GROUP_DIM), lambda bi, g: (i_even, g, 0, 0)),
        ],
        out_specs=pl.BlockSpec((None, n, POOL_GROUP_DIM), lambda bi, g: (bi, 0, g)),
        out_shape=jax.ShapeDtypeStruct((b, n, POOL_WIDTH), BF16),
        compiler_params=_cparams("parallel", "parallel"),
        name="pool_mixer",
    )(proj3, pool_w, scale4)


def _head_norm(x, g):
    return x * lax.rsqrt(jnp.mean(x * x, axis=-1, keepdims=True) + NORM_EPS) * g


_Q_COL0 = POOL_WIDTH // NA_HEAD_DIM
_K_COL0 = (POOL_WIDTH + NA_WIDTH) // NA_HEAD_DIM
_V_COL0 = (POOL_WIDTH + 2 * NA_WIDTH) // NA_HEAD_DIM


def _ctx_attn_kernel(q_ref, k_ref, v_ref, qg_ref, kg_ref, o_ref, kn_ref, vn_ref):
    qn = _head_norm(q_ref[...], qg_ref[...])
    kn = _head_norm(k_ref[...], kg_ref[...])
    v = v_ref[...]
    kn_ref[...] = kn
    vn_ref[...] = v
    s = _dot_nt(qn.astype(BF16), kn.astype(BF16)) * (NA_HEAD_DIM ** -0.5)
    m = jnp.max(s, axis=-1, keepdims=True)
    p = jnp.exp(s - m)
    denom = jnp.sum(p, axis=-1, keepdims=True)
    o = _dot(p.astype(BF16), v.astype(BF16)) / denom
    o_ref[...] = o.astype(o_ref.dtype)


def _ctx_attention(proj3, q_norm, k_norm, i_even):
    b, n, _ = proj3.shape
    qg = q_norm.reshape(q_norm.shape[0], 1, NA_HEAD_DIM)
    kg = k_norm.reshape(k_norm.shape[0], 1, NA_HEAD_DIM)
    head_blk = lambda col0: pl.BlockSpec((None, n, NA_HEAD_DIM), lambda bi, h: (bi, 0, col0 + h))
    gain = pl.BlockSpec((None, 1, NA_HEAD_DIM), lambda bi, h: (i_even, 0, 0))
    cache = pl.BlockSpec((None, None, None, n, NA_HEAD_DIM), lambda bi, h: (bi, 0, h, 0, 0))
    return pl.pallas_call(
        _ctx_attn_kernel,
        grid=(b, NA_HEADS),
        in_specs=[head_blk(_Q_COL0), head_blk(_K_COL0), head_blk(_V_COL0), gain, gain],
        out_specs=[pl.BlockSpec((None, n, NA_HEAD_DIM), lambda bi, h: (bi, 0, h)), cache, cache],
        out_shape=[
            jax.ShapeDtypeStruct((b, n, NA_WIDTH), BF16),
            jax.ShapeDtypeStruct((b, 1, NA_HEADS, n, NA_HEAD_DIM), F32),
            jax.ShapeDtypeStruct((b, 1, NA_HEADS, n, NA_HEAD_DIM), F32),
        ],
        compiler_params=_cparams("parallel", "parallel"),
        name="ctx_attention",
    )(proj3, proj3, proj3, qg, kg)


NA_BAND = 4
NA_BAND_KEY_ROWS = WIN_H + NA_BAND - 1


def _na_band_key_start(r0, rows):
    return np.clip(r0 - WIN_H // 2, 0, rows - NA_BAND_KEY_ROWS)


def _na_bias_table(rpb, rows):
    col = np.arange(GRID_W)[:, None]
    kc = np.arange(GRID_W)[None, :]
    wstart = np.clip(col - WIN_W // 2, 0, GRID_W - WIN_W)
    inside = (kc >= wstart) & (kc < wstart + WIN_W)
    rel = np.clip(kc - col + WIN_W - 1, 0, 2 * WIN_W - 2)
    a = np.arange(NA_BAND)[:, None]
    kr = np.arange(NA_BAND_KEY_ROWS)[None, :]
    ridx, row_ok = [], []
    for r0 in (0, NA_BAND, rows - NA_BAND):
        r = r0 + a
        sr = np.clip(r - WIN_H // 2, 0, rows - WIN_H)
        krow = _na_band_key_start(r0, rows) + kr
        row_ok.append((krow >= sr) & (krow < sr + WIN_H))
        ridx.append(np.clip(krow - r + WIN_H - 1, 0, 2 * WIN_H - 2))
    ridx = np.stack(ridx)
    ok = np.stack(row_ok)[:, :, :, None, None] & inside[None, None, None]
    tab = rpb[:, ridx][:, :, :, :, rel]
    tab = jnp.where(ok[None], tab, -jnp.inf)
    tab = jnp.transpose(tab, (0, 1, 2, 4, 3, 5))
    return tab.reshape(rpb.shape[0], 3, NA_BAND * GRID_W, NA_BAND_KEY_ROWS * GRID_W)


def _na_kernel(q_ref, k_ref, v_ref, kc_ref, vc_ref, qg_ref, kg_ref, bias_ref, o_ref,
               qs_ref, ks_ref, vs_ref, *, rows):
    qs_ref[...] = _head_norm(q_ref[...], qg_ref[...]).astype(BF16)
    ks_ref[...] = _head_norm(k_ref[...], kg_ref[...]).astype(BF16)
    vs_ref[...] = v_ref[...].astype(BF16)
    k_ctx = kc_ref[...].astype(BF16)
    v_ctx = vc_ref[...].astype(BF16)
    scale = NA_HEAD_DIM ** -0.5
    n_bands = rows // NA_BAND
    n_q = NA_BAND * GRID_W
    n_loc = NA_BAND_KEY_ROWS * GRID_W

    def band(i):
        r0 = i * NA_BAND
        key_row0 = jnp.clip(r0 - WIN_H // 2, 0, rows - NA_BAND_KEY_ROWS)
        variant = jnp.where(i == 0, 0, jnp.where(i == n_bands - 1, 2, 1))
        q0 = pl.multiple_of(r0 * GRID_W, n_q)
        q_b = qs_ref[pl.ds(q0, n_q), :]
        kstart = pl.multiple_of(key_row0 * GRID_W, GRID_W)
        k_blk = ks_ref[pl.ds(kstart, n_loc), :]
        v_blk = vs_ref[pl.ds(kstart, n_loc), :]
        s_loc = _dot_nt(q_b, k_blk) * scale + bias_ref[variant]
        s_ctx = _dot_nt(q_b, k_ctx) * scale
        m = jnp.maximum(jnp.max(s_loc, axis=-1, keepdims=True), jnp.max(s_ctx, axis=-1, keepdims=True))
        p_loc = jnp.exp(s_loc - m)
        p_ctx = jnp.exp(s_ctx - m)
        denom = jnp.sum(p_loc, axis=-1, keepdims=True) + jnp.sum(p_ctx, axis=-1, keepdims=True)
        o = (_dot(p_loc.astype(BF16), v_blk) + _dot(p_ctx.astype(BF16), v_ctx)) / denom
        o_ref[pl.ds(q0, n_q), :] = o.astype(o_ref.dtype)

    per_trip = 2 if n_bands % 2 == 0 else 1

    def trip(i, carry):
        for k in range(per_trip):
            band(i * per_trip + k)
        return carry

    lax.fori_loop(0, n_bands // per_trip, trip, 0)


def _neighbourhood_attention(proj3, cache_k, cache_v, q_norm, k_norm, rpb, i_even):
    b, t, _ = proj3.shape
    rows = t // GRID_W
    assert rows % NA_BAND == 0 and rows >= NA_BAND_KEY_ROWS + 1, rows
    n_ctx = cache_k.shape[3]
    bias = _na_bias_table(rpb[i_even], rows)
    qg = q_norm.reshape(q_norm.shape[0], 1, NA_HEAD_DIM)
    kg = k_norm.reshape(k_norm.shape[0], 1, NA_HEAD_DIM)
    head_blk = lambda col0: pl.BlockSpec((None, t, NA_HEAD_DIM), lambda bi, h: (bi, 0, col0 + h))
    gain = pl.BlockSpec((None, 1, NA_HEAD_DIM), lambda bi, h: (i_even, 0, 0))
    cache = pl.BlockSpec((None, None, None, n_ctx, NA_HEAD_DIM), lambda bi, h: (bi, i_even, h, 0, 0))
    return pl.pallas_call(
        functools.partial(_na_kernel, rows=rows),
        grid=(b, NA_HEADS),
        in_specs=[head_blk(_Q_COL0), head_blk(_K_COL0), head_blk(_V_COL0), cache, cache, gain, gain,
                  pl.BlockSpec((None,) + bias.shape[1:], lambda bi, h: (h, 0, 0, 0))],
        out_specs=pl.BlockSpec((None, t, NA_HEAD_DIM), lambda bi, h: (bi, 0, h)),
        out_shape=jax.ShapeDtypeStruct((b, t, NA_WIDTH), BF16),
        scratch_shapes=[pltpu.VMEM((t, NA_HEAD_DIM), BF16)] * 3,
        compiler_params=_cparams("parallel", "parallel"),
        name="neighbourhood_attention",
    )(proj3, proj3, proj3, cache_k, cache_v, qg, kg, bias)


def _softplus(x):
    return jnp.maximum(x, 0.0) + jnp.log1p(jnp.exp(-jnp.abs(x)))


def _ssd_kernel(*refs, l, with_init):
    (x_ref, b_ref, c_ref, dt_ref, cwx_ref, cwb_ref, cwc_ref, cbx_ref, cbb_ref, cbc_ref,
     alog_ref, dtbias_ref, dsk_ref) = refs[:13]
    if with_init:
        init_ref = refs[13]
        outs = refs[14:]
    else:
        outs = refs[13:]
    (y_ref, sfin_ref, xs_ref, bs_ref, cs_ref, stf_ref, stb_ref,
     cumf_ref, cumb_ref, srcf_ref, srcb_ref) = outs
    g = pl.program_id(1)
    q = SSD_CHUNK
    nc = l // q
    gw = SSD_GROUP_WIDTH
    gh = SSD_GROUP_HEADS
    pad_l = SSD_CONV // 2
    halo = 8

    win_rows = q + 2 * halo
    sel_col = lax.broadcasted_iota(jnp.int32, (q, SSD_CONV * win_rows), 1)
    sel_row = lax.broadcasted_iota(jnp.int32, (q, SSD_CONV * win_rows), 0)
    shift_mat = ((sel_col % win_rows) == sel_row + (halo - pad_l) + sel_col // win_rows).astype(BF16)

    tap_w = {id(w_ref): [jnp.broadcast_to(w_ref[j:j + 1, :], (win_rows, w_ref.shape[1])).astype(BF16)
                         for j in range(SSD_CONV)]
             for w_ref in (cwx_ref, cwb_ref, cwc_ref)}

    def conv_chunk(c, carry):
        t0 = pl.multiple_of(c * q, q)
        lo_start = pl.multiple_of(jnp.maximum(t0 - halo, 0), halo)
        hi_start = pl.multiple_of(jnp.minimum(t0 + q, l - halo), halo)
        for src, w_ref, bias_ref, dst in ((x_ref, cwx_ref, cbx_ref, xs_ref),
                                          (b_ref, cwb_ref, cbb_ref, bs_ref),
                                          (c_ref, cwc_ref, cbc_ref, cs_ref)):
            lo = jnp.where(c > 0, src[pl.ds(lo_start, halo), :], 0.0)
            hi = jnp.where(c < nc - 1, src[pl.ds(hi_start, halo), :], 0.0)
            win = jnp.concatenate([lo, src[pl.ds(t0, q), :], hi], axis=0).astype(BF16)
            taps = jnp.concatenate([win * wj for wj in tap_w[id(w_ref)]], axis=0)
            out = _silu(_dot(shift_mat, taps) + bias_ref[...])
            dst[pl.ds(t0, q), :] = out
            if dst is xs_ref:
                y_ref[pl.ds(t0, q), :] = out * dsk_ref[...]
        dt = _softplus(pltpu.roll(dt_ref[pl.ds(t0, q), :], to_lane0, axis=1) + bias_g)
        dta = dt * a_g
        cum_f = _dot_f32_rhs(tri_f, dta)
        cum_b = _dot_f32_rhs(tri_b, dta)
        cumf_ref[pl.ds(t0, q), :] = cum_f
        cumb_ref[pl.ds(t0, q), :] = cum_b
        log_dt = jnp.log(dt)
        srcf_ref[pl.ds(t0, q), :] = jnp.transpose(cum_f - log_dt)
        srcb_ref[pl.ds(t0, q), :] = jnp.transpose(cum_b - log_dt)
        return carry

    ii = lax.broadcasted_iota(jnp.int32, (q, q), 0)
    jj = lax.broadcasted_iota(jnp.int32, (q, q), 1)
    keep_f = jj <= ii
    keep_b = jj >= ii
    tri_f = keep_f.astype(BF16)
    tri_b = keep_b.astype(BF16)
    n_dt = 2 * SSD_HEADS
    assert q == n_dt
    to_lane0 = (n_dt - g * gh) % n_dt
    bias_g = pltpu.roll(jnp.broadcast_to(dtbias_ref[...], (8, n_dt)), to_lane0, axis=1)[0:1, :]
    a_g = pltpu.roll(jnp.broadcast_to(-jnp.exp(alog_ref[...]), (8, n_dt)), to_lane0, axis=1)[0:1, :]

    lax.fori_loop(0, nc, conv_chunk, 0)

    pair_w = 2 * SSD_HEAD_DIM
    lane = lax.broadcasted_iota(jnp.int32, (1, gw), 1)
    half_mask = [((lane % pair_w) // SSD_HEAD_DIM) == s for s in range(2)]
    first_head_lanes = lax.broadcasted_iota(jnp.int32, (1, pair_w), 1) < SSD_HEAD_DIM

    def scan_chunk(c, reverse):
        keep, edge, lane0 = (keep_b, 0, SSD_HEADS) if reverse else (keep_f, q - 1, 0)
        cum_ref, src_ref, st_ref = ((cumb_ref, srcb_ref, stb_ref) if reverse
                                    else (cumf_ref, srcf_ref, stf_ref))
        t0 = pl.multiple_of(c * q, q)
        xc = xs_ref[pl.ds(t0, q), :]
        bc = bs_ref[pl.ds(t0, q), :]
        cc = cs_ref[pl.ds(t0, q), :]
        cb = _dot_nt(cc.astype(BF16), bc.astype(BF16))
        bct = jnp.transpose(bc)
        x_bf = xc.astype(BF16)
        st = st_ref[...]
        st_bf = st.astype(BF16)
        zero = jnp.zeros((), BF16)
        x_half = [jnp.where(half_mask[s], x_bf, zero) for s in range(2)]
        st_half = [jnp.where(half_mask[s], st_bf, zero) for s in range(2)]
        pieces = []
        for pair in range(gh // 2):
            c0 = pair * pair_w
            lhs_y, lhs_s, carry_decay = [], [], []
            for e in (2 * pair, 2 * pair + 1):
                ln = lane0 + e
                col = jnp.broadcast_to(cum_ref[pl.ds(t0, q), ln:ln + 1], (q, q))
                src = src_ref[pl.ds(t0 + ln, 1), :]
                at_edge = cum_ref[pl.ds(t0 + edge, 1), ln:ln + 1]
                mix = jnp.exp(jnp.where(keep, col - src, -jnp.inf))
                lhs_y.append((cb * mix).astype(BF16))
                lhs_y.append((cc * jnp.exp(col)).astype(BF16))
                lhs_s.append((bct * jnp.exp(at_edge - src)).astype(BF16))
                carry_decay.append(jnp.exp(at_edge))
            rhs_y = jnp.concatenate([x_half[0][:, c0:c0 + pair_w], st_half[0][:, c0:c0 + pair_w],
                                     x_half[1][:, c0:c0 + pair_w], st_half[1][:, c0:c0 + pair_w]], axis=0)
            pieces.append(_dot(jnp.concatenate(lhs_y, axis=1), rhs_y))
            rhs_s = jnp.concatenate([x_half[0][:, c0:c0 + pair_w], x_half[1][:, c0:c0 + pair_w]], axis=0)
            keep_frac = jnp.where(first_head_lanes, carry_decay[0], carry_decay[1])
            st_ref[:, c0:c0 + pair_w] = (st[:, c0:c0 + pair_w] * keep_frac
                                         + _dot(jnp.concatenate(lhs_s, axis=1), rhs_s))
        y_ref[pl.ds(t0, q), :] = y_ref[pl.ds(t0, q), :] + jnp.concatenate(pieces, axis=1)

    for d, st_ref in enumerate((stf_ref, stb_ref)):
        if with_init:
            st_ref[...] = jnp.transpose(init_ref[d].reshape(gw, SSD_STATE))
        else:
            st_ref[...] = jnp.zeros_like(st_ref)

    def step(s, carry):
        scan_chunk(s, False)
        scan_chunk(nc - 1 - s, True)
        return carry

    lax.fori_loop(0, nc, step, 0)
    for d, st_ref in enumerate((stf_ref, stb_ref)):
        sfin_ref[d] = jnp.transpose(st_ref[...]).reshape(gh, SSD_HEAD_DIM, SSD_STATE)


def _ssd_scan(proj3, dt_raw3, conv_w, conv_b, a_log, dt_bias, d_skip, i_odd, init=None):
    b, l, _ = proj3.shape
    gh, gw = SSD_GROUP_HEADS, SSD_GROUP_WIDTH
    n_dt = 2 * SSD_HEADS
    a_log3 = a_log.reshape(a_log.shape[0], 1, n_dt)
    dt_bias3 = dt_bias.reshape(dt_bias.shape[0], 1, n_dt)
    dsk = jnp.repeat(d_skip[i_odd].astype(F32), SSD_HEAD_DIM).reshape(1, SSD_INNER)
    conv_b3 = conv_b.reshape(conv_b.shape[0], 1, conv_b.shape[1])

    x_col0 = SSD_INNER // gw
    b_col0 = (2 * SSD_INNER) // SSD_STATE
    c_col0 = (2 * SSD_INNER + SSD_BC_WIDTH) // SSD_STATE
    cwx_col0 = 0
    cwb_col0 = SSD_INNER // SSD_STATE
    cwc_col0 = (SSD_INNER + SSD_BC_WIDTH) // SSD_STATE

    in_specs = [
        pl.BlockSpec((None, l, gw), lambda bi, g: (bi, 0, x_col0 + g)),
        pl.BlockSpec((None, l, SSD_STATE), lambda bi, g: (bi, 0, b_col0 + g)),
        pl.BlockSpec((None, l, SSD_STATE), lambda bi, g: (bi, 0, c_col0 + g)),
        pl.BlockSpec((None, l, n_dt), lambda bi, g: (bi, 0, 0)),
        pl.BlockSpec((None, SSD_CONV, gw), lambda bi, g: (i_odd, 0, cwx_col0 + g)),
        pl.BlockSpec((None, SSD_CONV, SSD_STATE), lambda bi, g: (i_odd, 0, cwb_col0 + g)),
        pl.BlockSpec((None, SSD_CONV, SSD_STATE), lambda bi, g: (i_odd, 0, cwc_col0 + g)),
        pl.BlockSpec((None, 1, gw), lambda bi, g: (i_odd, 0, cwx_col0 + g)),
        pl.BlockSpec((None, 1, SSD_STATE), lambda bi, g: (i_odd, 0, cwb_col0 + g)),
        pl.BlockSpec((None, 1, SSD_STATE), lambda bi, g: (i_odd, 0, cwc_col0 + g)),
        pl.BlockSpec((None, 1, n_dt), lambda bi, g: (i_odd, 0, 0)),
        pl.BlockSpec((None, 1, n_dt), lambda bi, g: (i_odd, 0, 0)),
        pl.BlockSpec((1, gw), lambda bi, g: (0, g)),
    ]
    args = [proj3, proj3, proj3, dt_raw3, conv_w, conv_w, conv_w,
            conv_b3, conv_b3, conv_b3, a_log3, dt_bias3, dsk]
    if init is not None:
        in_specs.append(pl.BlockSpec((None, None, 2, gh, SSD_HEAD_DIM, SSD_STATE),
                                     lambda bi, g: (bi, i_odd, 0, g, 0, 0)))
        args.append(init)
    return pl.pallas_call(
        functools.partial(_ssd_kernel, l=l, with_init=init is not None),
        grid=(b, SSD_GROUPS),
        in_specs=in_specs,
        out_specs=[
            pl.BlockSpec((None, l, gw), lambda bi, g: (bi, 0, g)),
            pl.BlockSpec((None, 2, gh, SSD_HEAD_DIM, SSD_STATE), lambda bi, g: (bi, 0, g, 0, 0)),
        ],
        out_shape=[
            jax.ShapeDtypeStruct((b, l, SSD_INNER), F32),
            jax.ShapeDtypeStruct((b, 2, SSD_HEADS, SSD_HEAD_DIM, SSD_STATE), F32),
        ],
        scratch_shapes=[
            pltpu.VMEM((l, gw), F32),
            pltpu.VMEM((l, SSD_STATE), F32),
            pltpu.VMEM((l, SSD_STATE), F32),
            pltpu.VMEM((SSD_STATE, gw), F32),
            pltpu.VMEM((SSD_STATE, gw), F32),
        ] + [pltpu.VMEM((l, n_dt), F32)] * 4,
        compiler_params=_cparams("parallel", "parallel"),
        name="ssd_scan",
    )(*args)


def _gate_norm_kernel(y_ref, z_ref, g_ref, o_ref):
    y = y_ref[...] * _silu(z_ref[...])
    o = y * lax.rsqrt(jnp.mean(y * y, axis=-1, keepdims=True) + NORM_EPS) * g_ref[...]
    o_ref[...] = o.astype(o_ref.dtype)


def _gate_norm(y, proj, norm_g, i_odd):
    m = y.shape[0]
    tm = ROW_TILE
    g3 = norm_g.reshape(norm_g.shape[0], 1, SSD_INNER)
    return pl.pallas_call(
        _gate_norm_kernel,
        grid=(m // tm,),
        in_specs=[
            pl.BlockSpec((tm, SSD_INNER), lambda i: (i, 0)),
            pl.BlockSpec((tm, SSD_INNER), lambda i: (i, 0)),
            pl.BlockSpec((None, 1, SSD_INNER), lambda i: (i_odd, 0, 0)),
        ],
        out_specs=pl.BlockSpec((tm, SSD_INNER), lambda i: (i, 0)),
        out_shape=jax.ShapeDtypeStruct((m, SSD_INNER), BF16),
        compiler_params=_cparams("parallel"),
        name="gate_norm",
    )(y, proj, g3)


def _expert_kernel(x_ref, wg_ref, wu_ref, wd_ref, gate_ref, o_ref, acc_ref):
    f = pl.program_id(2)
    x = x_ref[...]
    hg = _dot(x, wg_ref[...].astype(BF16))
    hu = _dot(x, wu_ref[...].astype(BF16))
    hdn = (_silu(hg) * hu).astype(BF16)

    @pl.when(f == 0)
    def _():
        acc_ref[...] = jnp.zeros_like(acc_ref)

    acc_ref[...] += _dot(hdn, wd_ref[...].astype(BF16))

    @pl.when(f == pl.num_programs(2) - 1)
    def _():
        o_ref[...] = (acc_ref[...] * gate_ref[...]).astype(o_ref.dtype)


def _expert_ffn(xe, gate, w_gate, w_up, w_down, layer):
    e, r, _ = xe.shape
    tr = min(r, 1024)
    tf = 256
    return pl.pallas_call(
        _expert_kernel,
        grid=(e, r // tr, EXPERT_FF // tf),
        in_specs=[
            pl.BlockSpec((None, tr, D_MODEL), lambda ei, ri, f: (ei, ri, 0)),
            pl.BlockSpec((None, None, D_MODEL, tf), lambda ei, ri, f: (layer, ei, 0, f)),
            pl.BlockSpec((None, None, D_MODEL, tf), lambda ei, ri, f: (layer, ei, 0, f)),
            pl.BlockSpec((None, None, tf, D_MODEL), lambda ei, ri, f: (layer, ei, f, 0)),
            pl.BlockSpec((None, tr, 1), lambda ei, ri, f: (ei, ri, 0)),
        ],
        out_specs=pl.BlockSpec((None, tr, D_MODEL), lambda ei, ri, f: (ei, ri, 0)),
        out_shape=jax.ShapeDtypeStruct((e, r, D_MODEL), BF16),
        scratch_shapes=[pltpu.VMEM((tr, D_MODEL), F32)],
        compiler_params=_cparams("parallel", "parallel", "arbitrary"),
        name="expert_ffn",
    )(xe, w_gate, w_up, w_down, gate)


def _combine_kernel(idx_ref, ye_ref, x_ref, gate_ref, o_ref, *, n, ec, cap):
    step = pl.program_id(2)
    token = lax.broadcasted_iota(jnp.int32, (n, cap), 0)
    onehot = jnp.concatenate(
        [jnp.where(token == idx_ref[k], 1.0, 0.0).astype(BF16) for k in range(ec)], axis=1)

    @pl.when(step == 0)
    def _():
        o_ref[...] = jnp.zeros_like(o_ref)

    o_ref[...] += _dot(onehot, ye_ref[...].reshape(ec * cap, ye_ref.shape[-1]))

    @pl.when(step == pl.num_programs(2) - 1)
    def _():
        o_ref[...] = x_ref[...] + gate_ref[...] * o_ref[...]


def _expert_combine(ye, idx, x, mod4, gate_idx, first_row, shared_cond):
    n_req, n_exp, cap = idx.shape
    m = x.shape[0]
    n = m // n_req
    ec = max(1, min(n_exp, 512 // cap))
    tn = D_MODEL // 2
    idx4 = idx.reshape(n_req, n_exp, 1, cap)
    cond_row = (lambda b: first_row) if shared_cond else (lambda b: first_row + b)
    return pl.pallas_call(
        functools.partial(_combine_kernel, n=n, ec=ec, cap=cap),
        grid=(n_req, D_MODEL // tn, n_exp // ec),
        in_specs=[
            pl.BlockSpec((None, ec, 1, cap), lambda b, j, s: (b, s, 0, 0)),
            pl.BlockSpec((ec, cap, tn), lambda b, j, s: (s, b, j)),
            pl.BlockSpec((n, tn), lambda b, j, s: (b, j)),
            pl.BlockSpec((None, None, 1, tn), lambda b, j, s: (cond_row(b), gate_idx, 0, j)),
        ],
        out_specs=pl.BlockSpec((n, tn), lambda b, j, s: (b, j)),
        out_shape=jax.ShapeDtypeStruct((m, D_MODEL), F32),
        compiler_params=_cparams("parallel", "parallel", "arbitrary"),
        name="expert_combine",
    )(idx4, ye, x, mod4)


def _prefix_count(mask_bf, strict_upper):
    n = mask_bf.shape[1]
    carry = jnp.zeros((mask_bf.shape[0], 1), F32)
    out = []
    for k in range(n // PREFIX_BLOCK):
        blk = mask_bf[:, k * PREFIX_BLOCK:(k + 1) * PREFIX_BLOCK]
        out.append(_dot(blk, strict_upper) + carry)
        carry = carry + jnp.sum(blk.astype(F32), axis=-1, keepdims=True)
    return jnp.concatenate(out, axis=1)


def _router_kernel(lg_ref, o_ref, slot_ref, feat_ref, *, n_req, n, cap):
    step = pl.program_id(0)
    n_exp = N_EXPERTS

    @pl.when(step == 0)
    def _():
        tok = lax.broadcasted_iota(jnp.int32, (n, ROUTE_LANES), 0)
        tlane = lax.broadcasted_iota(jnp.int32, (n, ROUTE_LANES), 1)
        radix_bits = ROUTE_TOKEN_RADIX.bit_length() - 1
        tok_feat = (jnp.where(tlane == ROUTE_TOKEN_LANE, tok >> radix_bits, 0)
                    + jnp.where(tlane == ROUTE_TOKEN_LANE + 1, tok & (ROUTE_TOKEN_RADIX - 1), 0)).astype(F32)
        is_expert = tlane < n_exp
        aff_rows = []
        for r in range(n_req):
            x = jnp.where(is_expert, lg_ref[pl.ds(r * n, n), :], -jnp.inf)
            e = jnp.exp(x - jnp.max(x, axis=-1, keepdims=True))
            aff = e / jnp.sum(e, axis=-1, keepdims=True)
            feat = tok_feat
            for p, part in enumerate(_split3(aff)):
                piece = part.astype(F32)
                feat = feat + (pltpu.roll(piece, p * n_exp, axis=1) if p else piece)
            feat_ref[r] = feat.astype(BF16)
            aff_rows.append(jnp.transpose(aff)[:n_exp])
        aff_t = jnp.concatenate(aff_rows, axis=0)
        bits = pltpu.bitcast(aff_t, jnp.int32)

        def bisect(_, carry):
            lo, hi = carry
            mid = lax.shift_right_arithmetic(lo + hi, 1)
            count = jnp.sum(jnp.where(bits >= mid, 1.0, 0.0), axis=-1, keepdims=True)
            enough = count >= cap
            return jnp.where(enough, mid, lo), jnp.where(enough, hi, mid)

        rows = n_req * n_exp
        lo0 = jnp.zeros((rows, 1), jnp.int32)
        hi0 = jnp.full((rows, 1), 0x3F800001, jnp.int32)
        kth, _ = lax.fori_loop(0, 31, bisect, (lo0, hi0))
        above = bits > kth
        tied = bits == kth
        blk_i = lax.broadcasted_iota(jnp.int32, (PREFIX_BLOCK, PREFIX_BLOCK), 0)
        blk_j = lax.broadcasted_iota(jnp.int32, (PREFIX_BLOCK, PREFIX_BLOCK), 1)
        strict_upper = (blk_i < blk_j).astype(BF16)
        need = cap - jnp.sum(jnp.where(above, 1.0, 0.0), axis=-1, keepdims=True)
        tie_rank = _prefix_count(jnp.where(tied, 1.0, 0.0).astype(BF16), strict_upper)
        chosen = jnp.logical_or(above, jnp.logical_and(tied, tie_rank < need))
        slot = _prefix_count(jnp.where(chosen, 1.0, 0.0).astype(BF16), strict_upper)
        slot_ref[...] = jnp.where(chosen, slot, -1.0)

    slot_id = lax.broadcasted_iota(jnp.int32, (cap, n), 0).astype(F32)
    for k in range(ROUTE_ROWS_PER_STEP):
        row = step * ROUTE_ROWS_PER_STEP + k
        onehot_t = jnp.where(slot_id == slot_ref[pl.ds(row, 1), :], 1.0, 0.0).astype(BF16)
        o_ref[k] = _dot(onehot_t, feat_ref[row // n_exp])


def _route(logits, n_req):
    m = logits.shape[0]
    n = m // n_req
    cap = EC_CAPACITY_FACTOR * n // N_EXPERTS
    rows = n_req * N_EXPERTS
    assert n % PREFIX_BLOCK == 0 and rows % ROUTE_ROWS_PER_STEP == 0
    rec = pl.pallas_call(
        functools.partial(_router_kernel, n_req=n_req, n=n, cap=cap),
        grid=(rows // ROUTE_ROWS_PER_STEP,),
        in_specs=[pl.BlockSpec((m, ROUTE_LANES), lambda i: (0, 0))],
        out_specs=pl.BlockSpec((ROUTE_ROWS_PER_STEP, cap, ROUTE_LANES), lambda i: (i, 0, 0)),
        out_shape=jax.ShapeDtypeStruct((rows, cap, ROUTE_LANES), F32),
        scratch_shapes=[pltpu.VMEM((rows, n), F32), pltpu.VMEM((n_req, n, ROUTE_LANES), BF16)],
        compiler_params=_cparams("arbitrary"),
        name="router",
    )(logits)
    rec = rec.reshape(n_req, N_EXPERTS, cap, ROUTE_LANES)
    idx = (ROUTE_TOKEN_RADIX * rec[..., ROUTE_TOKEN_LANE] + rec[..., ROUTE_TOKEN_LANE + 1]).astype(jnp.int32)
    pieces = rec[..., :3 * N_EXPERTS].reshape(n_req, N_EXPERTS, cap, 3, N_EXPERTS).sum(axis=3)
    own = jnp.eye(N_EXPERTS, dtype=F32)[None, :, None, :]
    gate = jnp.sum(pieces * own, axis=3)
    return idx, gate


def _gather_kernel(rows_ref, h_hbm, o_hbm, sem, *, n_rows):
    e = pl.program_id(0)

    def issue(i, carry):
        pltpu.make_async_copy(h_hbm.at[rows_ref[0, i]], o_hbm.at[e, i], sem).start()
        return carry

    lax.fori_loop(0, n_rows, issue, 0, unroll=8)
    pltpu.make_async_copy(o_hbm.at[e], o_hbm.at[e], sem).wait()


def _gather_rows(h, rows):
    m, d = h.shape
    n_exp, n_rows = rows.shape
    sub = d // 128
    out = pl.pallas_call(
        functools.partial(_gather_kernel, n_rows=n_rows),
        grid=(n_exp,),
        in_specs=[pl.BlockSpec((None, 1, n_rows), lambda e: (e, 0, 0), memory_space=pltpu.SMEM),
                  pl.BlockSpec(memory_space=pl.ANY)],
        out_specs=pl.BlockSpec(memory_space=pl.ANY),
        out_shape=jax.ShapeDtypeStruct((n_exp, n_rows, sub, 128), h.dtype),
        scratch_shapes=[pltpu.SemaphoreType.DMA],
        compiler_params=_cparams("arbitrary"),
        name="gather_rows",
    )(rows.reshape(n_exp, 1, n_rows), h.reshape(m, sub, 128))
    return out.reshape(n_exp, n_rows, d)


def _expert_choice(h, logits, x, n_req, w_gate, w_up, w_down, layer, mod4, gate_idx, first_row, shared_cond):
    m = h.shape[0]
    n = m // n_req
    idx, gate = _route(logits, n_req)
    cap = idx.shape[2]
    rows = idx + (jnp.arange(n_req, dtype=idx.dtype) * n)[:, None, None]
    rows = jnp.transpose(rows, (1, 0, 2)).reshape(N_EXPERTS, n_req * cap)
    gate = jnp.transpose(gate, (1, 0, 2)).reshape(N_EXPERTS, n_req * cap, 1)
    xe = _gather_rows(h, rows)
    ye = _expert_ffn(xe, gate, w_gate, w_up, w_down, layer)
    return _expert_combine(ye, idx, x, mod4, gate_idx, first_row, shared_cond)


class _Stream:
    def __init__(self, x3, first_row, shared_cond):
        self.batch, self.n, _ = x3.shape
        self.x = x3.reshape(self.batch * self.n, D_MODEL)
        self.first_row = first_row
        self.shared_cond = shared_cond
        self.rows_per_cond = self.batch * self.n if shared_cond else self.n

    def mod_args(self):
        return dict(rows_per_cond=self.rows_per_cond, first_row=self.first_row)


def kernel(x_prompt, x_sample, cache_na_k, cache_na_v, state_ssd, c, c_ctx, ada_w, ada_b, norm1_g, norm2_g,
           router_w, exp_w_gate, exp_w_up, exp_w_down, ab_w_in, pool_w, pool_scale, na_q_norm, na_k_norm,
           na_rpb, ab_w_out, ssd_w_in, ssd_conv_w, ssd_conv_b, ssd_a_log, ssd_dt_bias, ssd_d, ssd_norm_g,
           ssd_w_out):
    depth = ada_w.shape[0]
    dec_batch = x_sample.shape[0]
    latent = _Stream(x_sample, 0, shared_cond=False)
    prompt = _Stream(x_prompt, dec_batch, shared_cond=True)
    cond = jnp.concatenate(
        [c, c_ctx[None, :], jnp.zeros((MOD_ROWS - dec_batch - 1, D_MODEL), F32)], axis=0)

    router_pad = jnp.pad(router_w, ((0, 0), (0, 0), (0, ROUTE_LANES - N_EXPERTS)))

    new_k, new_v, new_s = [], [], []
    for layer in range(depth):
        i = layer // 2
        mod4 = _modulation(cond, ada_w, ada_b, layer).reshape(MOD_ROWS, N_MOD, 1, D_MODEL)
        for s in (prompt, latent):
            margs = s.mod_args()
            h = _norm_mod(s.x, norm1_g, mod4, layer, 0, 1, **margs)[0]
            tm = min(s.x.shape[0], 2048)
            if layer % 2 == 0:
                proj = _matmul([h], ab_w_in, i, tm=tm, tn=512, n_blocks=ab_w_in.shape[2] // 512)
                proj3 = proj.reshape(s.batch, s.n, -1)
                pooled = _pool_mixer(proj3, pool_w, pool_scale, i)
                if s is prompt:
                    att, k_new, v_new = _ctx_attention(proj3, na_q_norm, na_k_norm, i)
                    new_k.append(k_new)
                    new_v.append(v_new)
                else:
                    att = _neighbourhood_attention(proj3, cache_na_k, cache_na_v, na_q_norm, na_k_norm,
                                                   na_rpb, i)
                s.x = _matmul([pooled.reshape(-1, POOL_WIDTH), att.reshape(-1, NA_WIDTH)], ab_w_out, i,
                              tm=tm, tn=512, n_blocks=D_MODEL // 512, res=s.x, mod4=mod4, gate_idx=2,
                              **margs)
            else:
                proj = _matmul([h], ssd_w_in, i, tm=tm, tn=512, n_blocks=SSD_MAIN_DIM // 512)
                dt_raw = _matmul([h], ssd_w_in, i, tm=tm, tn=2 * SSD_HEADS, n_blocks=1,
                                 col_block0=SSD_MAIN_DIM // (2 * SSD_HEADS))
                proj3 = proj.reshape(s.batch, s.n, SSD_MAIN_DIM)
                dt3 = dt_raw.reshape(s.batch, s.n, 2 * SSD_HEADS)
                init = None if s is prompt else state_ssd
                y, s_fin = _ssd_scan(proj3, dt3, ssd_conv_w, ssd_conv_b, ssd_a_log, ssd_dt_bias, ssd_d, i,
                                     init=init)
                if s is prompt:
                    new_s.append(s_fin[:, None])
                yn = _gate_norm(y.reshape(-1, SSD_INNER), proj, ssd_norm_g, i)
                s.x = _matmul([yn], ssd_w_out, i, tm=min(tm, 1024), tn=512, n_blocks=D_MODEL // 512,
                              res=s.x, mod4=mod4, gate_idx=2, **margs)
            h2, logits = _norm_mod(s.x, norm2_g, mod4, layer, 3, 4, router_w=router_pad, **margs)
            s.x = _expert_choice(h2, logits, s.x, s.batch, exp_w_gate, exp_w_up, exp_w_down, layer,
                                 mod4, 5, s.first_row, s.shared_cond)

    y_prompt = prompt.x.reshape(x_prompt.shape)
    y_sample = latent.x.reshape(x_sample.shape)
    return (y_prompt, y_sample, jnp.concatenate(new_k, axis=1), jnp.concatenate(new_v, axis=1),
            jnp.concatenate(new_s, axis=1))
```

```python
import functools

import numpy as np
import jax
import jax.numpy as jnp
from jax import lax
from jax.experimental import pallas as pl
from jax.experimental.pallas import tpu as pltpu

F32 = jnp.float32
BF16 = jnp.bfloat16

D_MODEL = 2048
GRID_W = 64
POOL_WIDTH = 1024
POOL_GROUPS = 4
POOL_GROUP_DIM = 256
POOL_WINDOWS = (2, 4, 8, 16)
NA_WIDTH = 1024
NA_HEAD_DIM = 128
NA_HEADS = 8
WIN_H = 8
WIN_W = 16
SSD_INNER = 4096
SSD_HEAD_DIM = 64
SSD_HEADS = 64
SSD_GROUPS = 8
SSD_GROUP_HEADS = SSD_HEADS // SSD_GROUPS
SSD_GROUP_WIDTH = SSD_GROUP_HEADS * SSD_HEAD_DIM
SSD_STATE = 128
SSD_CONV = 4
SSD_CHUNK = 128
SSD_BC_WIDTH = SSD_GROUPS * SSD_STATE
SSD_MAIN_DIM = 2 * SSD_INNER + 2 * SSD_BC_WIDTH
N_EXPERTS = 16
EXPERT_FF = 1024
EC_CAPACITY_FACTOR = 2
NORM_EPS = 1e-6
N_MOD = 6
MOD_ROWS = 16
ROUTE_LANES = 128
ROUTE_TOKEN_LANE = 3 * N_EXPERTS
ROUTE_TOKEN_RADIX = 64
ROUTE_ROWS_PER_STEP = 8
PREFIX_BLOCK = 256

VMEM_LIMIT_BYTES = 56 * 1024 * 1024
ROW_TILE = 256


def _cparams(*sem):
    return pltpu.CompilerParams(dimension_semantics=sem, vmem_limit_bytes=VMEM_LIMIT_BYTES)


def _silu(x):
    return x * (1.0 / (1.0 + jnp.exp(-x)))


def _split2(x):
    hi = x.astype(BF16)
    lo = (x - hi.astype(F32)).astype(BF16)
    return hi, lo


def _split3(x):
    hi = x.astype(BF16)
    r = x - hi.astype(F32)
    mid = r.astype(BF16)
    lo = (r - mid.astype(F32)).astype(BF16)
    return hi, mid, lo


def _dot(a, b):
    return jnp.dot(a, b, preferred_element_type=F32)


def _dot_nt(a, b):
    return lax.dot_general(a, b, (((1,), (1,)), ((), ())), preferred_element_type=F32)


def _dot_f32_lhs(x, m):
    hi, mid, lo = _split3(x)
    return _dot(hi, m) + _dot(mid, m) + _dot(lo, m)


def _dot_f32_rhs(m, x):
    hi, mid, lo = _split3(x)
    return _dot(m, hi) + _dot(m, mid) + _dot(m, lo)


def _dot_split(a, b):
    a_hi, a_lo = _split2(a)
    b_hi, b_lo = _split2(b)
    return _dot(a_hi, b_hi) + _dot(a_lo, b_hi) + _dot(a_hi, b_lo)


def _mod_kernel(c_ref, w_ref, b_ref, o_ref):
    s = _silu(c_ref[...])
    o_ref[...] = _dot_split(s, w_ref[...]) + b_ref[...]


def _modulation(cond, ada_w, ada_b, layer):
    tn = 768
    n = N_MOD * D_MODEL
    ada_b3 = ada_b.reshape(ada_b.shape[0], 1, n)
    return pl.pallas_call(
        _mod_kernel,
        grid=(n // tn,),
        in_specs=[
            pl.BlockSpec((MOD_ROWS, D_MODEL), lambda j: (0, 0)),
            pl.BlockSpec((None, D_MODEL, tn), lambda j: (layer, 0, j)),
            pl.BlockSpec((None, 1, tn), lambda j: (layer, 0, j)),
        ],
        out_specs=pl.BlockSpec((MOD_ROWS, tn), lambda j: (0, j)),
        out_shape=jax.ShapeDtypeStruct((MOD_ROWS, n), F32),
        compiler_params=_cparams("parallel"),
        name="modulation",
    )(cond, ada_w, ada_b3)


def _mod_spec(which, rows_per_cond, first_row, tm):
    return pl.BlockSpec((None, None, 1, D_MODEL),
                        lambda i, *_: (first_row + (i * tm) // rows_per_cond, which, 0, 0))


def _norm_mod_kernel(x_ref, g_ref, sh_ref, sc_ref, *rest, with_router):
    x = x_ref[...]
    y = x * lax.rsqrt(jnp.mean(x * x, axis=-1, keepdims=True) + NORM_EPS) * g_ref[...]
    h = y * (1.0 + sc_ref[...]) + sh_ref[...]
    if with_router:
        rw_ref, h_ref, lg_ref = rest
        lg_ref[...] = _dot_split(h, rw_ref[...])
    else:
        (h_ref,) = rest
    h_ref[...] = h.astype(BF16)


def _norm_mod(x, g, mod4, layer, shift_idx, scale_idx, rows_per_cond, first_row, router_w=None):
    m = x.shape[0]
    tm = 2 * ROW_TILE
    g3 = g.reshape(g.shape[0], 1, D_MODEL)
    in_specs = [
        pl.BlockSpec((tm, D_MODEL), lambda i: (i, 0)),
        pl.BlockSpec((None, 1, D_MODEL), lambda i: (layer, 0, 0)),
        _mod_spec(shift_idx, rows_per_cond, first_row, tm),
        _mod_spec(scale_idx, rows_per_cond, first_row, tm),
    ]
    args = [x, g3, mod4, mod4]
    out_specs = [pl.BlockSpec((tm, D_MODEL), lambda i: (i, 0))]
    out_shape = [jax.ShapeDtypeStruct((m, D_MODEL), BF16)]
    if router_w is not None:
        in_specs.append(pl.BlockSpec((None, D_MODEL, ROUTE_LANES), lambda i: (layer, 0, 0)))
        args.append(router_w)
        out_specs.append(pl.BlockSpec((tm, ROUTE_LANES), lambda i: (i, 0)))
        out_shape.append(jax.ShapeDtypeStruct((m, ROUTE_LANES), F32))
    return pl.pallas_call(
        functools.partial(_norm_mod_kernel, with_router=router_w is not None),
        grid=(m // tm,),
        in_specs=in_specs,
        out_specs=out_specs,
        out_shape=out_shape,
        compiler_params=_cparams("parallel"),
        name="norm_mod",
    )(*args)


def _matmul_kernel(*refs, n_a, k_sizes, with_res):
    a_refs = refs[:n_a]
    w_ref = refs[n_a]
    if with_res:
        res_ref, gate_ref, o_ref = refs[n_a + 1:]
    else:
        (o_ref,) = refs[n_a + 1:]
    w = w_ref[...].astype(BF16)
    acc = None
    k0 = 0
    for a_ref, k in zip(a_refs, k_sizes):
        part = _dot(a_ref[...], w[k0:k0 + k])
        acc = part if acc is None else acc + part
        k0 += k
    if with_res:
        acc = res_ref[...] + gate_ref[...] * acc
    o_ref[...] = acc.astype(o_ref.dtype)


def _matmul(a_list, w, layer, *, tm, tn, n_blocks, col_block0=0, out_dtype=F32,
            res=None, mod4=None, gate_idx=None, rows_per_cond=None, first_row=None):
    m = a_list[0].shape[0]
    k_sizes = tuple(a.shape[1] for a in a_list)
    k_total = sum(k_sizes)
    in_specs = [pl.BlockSpec((tm, k), lambda i, j: (i, 0)) for k in k_sizes]
    in_specs.append(pl.BlockSpec((None, k_total, tn), lambda i, j: (layer, 0, col_block0 + j)))
    args = list(a_list) + [w]
    if res is not None:
        in_specs.append(pl.BlockSpec((tm, tn), lambda i, j: (i, j)))
        in_specs.append(pl.BlockSpec((None, None, 1, tn),
                                     lambda i, j: (first_row + (i * tm) // rows_per_cond, gate_idx, 0, j)))
        args += [res, mod4]
    return pl.pallas_call(
        functools.partial(_matmul_kernel, n_a=len(a_list), k_sizes=k_sizes, with_res=res is not None),
        grid=(m // tm, n_blocks),
        in_specs=in_specs,
        out_specs=pl.BlockSpec((tm, tn), lambda i, j: (i, j)),
        out_shape=jax.ShapeDtypeStruct((m, n_blocks * tn), out_dtype),
        compiler_params=_cparams("parallel", "arbitrary"),
        name="matmul",
    )(*args)


def _pool_kernel(u_ref, w_ref, s_ref, o_ref, *, n):
    g = pl.program_id(1)
    u = u_ref[...]
    t = lax.broadcasted_iota(jnp.int32, (n, 1), 0)
    for gi, win in enumerate(POOL_WINDOWS):
        @pl.when(g == gi)
        def _(win=win):
            half = win // 2
            total = u
            for d in range(-half, win - half):
                if d == 0:
                    continue
                shifted = pltpu.roll(u, (-d) % n, axis=0)
                valid = jnp.logical_and(t + d >= 0, t + d < n)
                total = total + jnp.where(valid, shifted, 0.0)
            cnt = (jnp.minimum(t - half + win, n) - jnp.maximum(t - half, 0)).astype(F32)
            pooled = total / cnt - u
            out = _dot(pooled.astype(BF16), w_ref[...].astype(BF16)) * s_ref[...]
            o_ref[...] = out.astype(o_ref.dtype)


def _pool_mixer(proj3, pool_w, pool_scale, i_even):
    b, n, _ = proj3.shape
    scale4 = pool_scale.reshape(pool_scale.shape[0], POOL_GROUPS, 1, POOL_GROUP_DIM)
    return pl.pallas_call(
        functools.partial(_pool_kernel, n=n),
        grid=(b, POOL_GROUPS),
        in_specs=[
            pl.BlockSpec((None, n, POOL_GROUP_DIM), lambda bi, g: (bi, 0, g)),
            pl.BlockSpec((None, None, POOL_GROUP_DIM, POOL_GROUP_DIM), lambda bi, g: (i_even, g, 0, 0)),
            pl.BlockSpec((None, None, 1, POOL_GROUP_DIM), lambda bi, g: (i_even, g, 0, 0)),
        ],
        out_specs=pl.BlockSpec((None, n, POOL_GROUP_DIM), lambda bi, g: (bi, 0, g)),
        out_shape=jax.ShapeDtypeStruct((b, n, POOL_WIDTH), BF16),
        compiler_params=_cparams("parallel", "parallel"),
        name="pool_mixer",
    )(proj3, pool_w, scale4)


def _head_norm(x, g):
    return x * lax.rsqrt(jnp.mean(x * x, axis=-1, keepdims=True) + NORM_EPS) * g


_Q_COL0 = POOL_WIDTH // NA_HEAD_DIM
_K_COL0 = (POOL_WIDTH + NA_WIDTH) // NA_HEAD_DIM
_V_COL0 = (POOL_WIDTH + 2 * NA_WIDTH) // NA_HEAD_DIM


def _ctx_attn_kernel(q_ref, k_ref, v_ref, qg_ref, kg_ref, o_ref, kn_ref, vn_ref):
    qn = _head_norm(q_ref[...], qg_ref[...])
    kn = _head_norm(k_ref[...], kg_ref[...])
    v = v_ref[...]
    kn_ref[...] = kn
    vn_ref[...] = v
    s = _dot_nt(qn.astype(BF16), kn.astype(BF16)) * (NA_HEAD_DIM ** -0.5)
    m = jnp.max(s, axis=-1, keepdims=True)
    p = jnp.exp(s - m)
    denom = jnp.sum(p, axis=-1, keepdims=True)
    o = _dot(p.astype(BF16), v.astype(BF16)) / denom
    o_ref[...] = o.astype(o_ref.dtype)


def _ctx_attention(proj3, q_norm, k_norm, i_even):
    b, n, _ = proj3.shape
    qg = q_norm.reshape(q_norm.shape[0], 1, NA_HEAD_DIM)
    kg = k_norm.reshape(k_norm.shape[0], 1, NA_HEAD_DIM)
    head_blk = lambda col0: pl.BlockSpec((None, n, NA_HEAD_DIM), lambda bi, h: (bi, 0, col0 + h))
    gain = pl.BlockSpec((None, 1, NA_HEAD_DIM), lambda bi, h: (i_even, 0, 0))
    cache = pl.BlockSpec((None, None, None, n, NA_HEAD_DIM), lambda bi, h: (bi, 0, h, 0, 0))
    return pl.pallas_call(
        _ctx_attn_kernel,
        grid=(b, NA_HEADS),
        in_specs=[head_blk(_Q_COL0), head_blk(_K_COL0), head_blk(_V_COL0), gain, gain],
        out_specs=[pl.BlockSpec((None, n, NA_HEAD_DIM), lambda bi, h: (bi, 0, h)), cache, cache],
        out_shape=[
            jax.ShapeDtypeStruct((b, n, NA_WIDTH), BF16),
            jax.ShapeDtypeStruct((b, 1, NA_HEADS, n, NA_HEAD_DIM), F32),
            jax.ShapeDtypeStruct((b, 1, NA_HEADS, n, NA_HEAD_DIM), F32),
        ],
        compiler_params=_cparams("parallel", "parallel"),
        name="ctx_attention",
    )(proj3, proj3, proj3, qg, kg)


NA_BAND = 4
NA_BAND_KEY_ROWS = WIN_H + NA_BAND - 1


def _na_band_key_start(r0, rows):
    return np.clip(r0 - WIN_H // 2, 0, rows - NA_BAND_KEY_ROWS)


def _na_bias_table(rpb, rows):
    col = np.arange(GRID_W)[:, None]
    kc = np.arange(GRID_W)[None, :]
    wstart = np.clip(col - WIN_W // 2, 0, GRID_W - WIN_W)
    inside = (kc >= wstart) & (kc < wstart + WIN_W)
    rel = np.clip(kc - col + WIN_W - 1, 0, 2 * WIN_W - 2)
    a = np.arange(NA_BAND)[:, None]
    kr = np.arange(NA_BAND_KEY_ROWS)[None, :]
    ridx, row_ok = [], []
    for r0 in (0, NA_BAND, rows - NA_BAND):
        r = r0 + a
        sr = np.clip(r - WIN_H // 2, 0, rows - WIN_H)
        krow = _na_band_key_start(r0, rows) + kr
        row_ok.append((krow >= sr) & (krow < sr + WIN_H))
        ridx.append(np.clip(krow - r + WIN_H - 1, 0, 2 * WIN_H - 2))
    ridx = np.stack(ridx)
    ok = np.stack(row_ok)[:, :, :, None, None] & inside[None, None, None]
    tab = rpb[:, ridx][:, :, :, :, rel]
    tab = jnp.where(ok[None], tab, -jnp.inf)
    tab = jnp.transpose(tab, (0, 1, 2, 4, 3, 5))
    return tab.reshape(rpb.shape[0], 3, NA_BAND * GRID_W, NA_BAND_KEY_ROWS * GRID_W)


def _na_kernel(q_ref, k_ref, v_ref, kc_ref, vc_ref, qg_ref, kg_ref, bias_ref, o_ref,
               qs_ref, ks_ref, vs_ref, *, rows):
    qs_ref[...] = _head_norm(q_ref[...], qg_ref[...]).astype(BF16)
    ks_ref[...] = _head_norm(k_ref[...], kg_ref[...]).astype(BF16)
    vs_ref[...] = v_ref[...].astype(BF16)
    k_ctx = kc_ref[...].astype(BF16)
    v_ctx = vc_ref[...].astype(BF16)
    scale = NA_HEAD_DIM ** -0.5
    n_bands = rows // NA_BAND
    n_q = NA_BAND * GRID_W
    n_loc = NA_BAND_KEY_ROWS * GRID_W

    def band(i):
        r0 = i * NA_BAND
        key_row0 = jnp.clip(r0 - WIN_H // 2, 0, rows - NA_BAND_KEY_ROWS)
        variant = jnp.where(i == 0, 0, jnp.where(i == n_bands - 1, 2, 1))
        q0 = pl.multiple_of(r0 * GRID_W, n_q)
        q_b = qs_ref[pl.ds(q0, n_q), :]
        kstart = pl.multiple_of(key_row0 * GRID_W, GRID_W)
        k_blk = ks_ref[pl.ds(kstart, n_loc), :]
        v_blk = vs_ref[pl.ds(kstart, n_loc), :]
        s_loc = _dot_nt(q_b, k_blk) * scale + bias_ref[variant]
        s_ctx = _dot_nt(q_b, k_ctx) * scale
        m = jnp.maximum(jnp.max(s_loc, axis=-1, keepdims=True), jnp.max(s_ctx, axis=-1, keepdims=True))
        p_loc = jnp.exp(s_loc - m)
        p_ctx = jnp.exp(s_ctx - m)
        denom = jnp.sum(p_loc, axis=-1, keepdims=True) + jnp.sum(p_ctx, axis=-1, keepdims=True)
        o = (_dot(p_loc.astype(BF16), v_blk) + _dot(p_ctx.astype(BF16), v_ctx)) / denom
        o_ref[pl.ds(q0, n_q), :] = o.astype(o_ref.dtype)

    per_trip = 2 if n_bands % 2 == 0 else 1

    def trip(i, carry):
        for k in range(per_trip):
            band(i * per_trip + k)
        return carry

    lax.fori_loop(0, n_bands // per_trip, trip, 0)


def _neighbourhood_attention(proj3, cache_k, cache_v, q_norm, k_norm, rpb, i_even):
    b, t, _ = proj3.shape
    rows = t // GRID_W
    assert rows % NA_BAND == 0 and rows >= NA_BAND_KEY_ROWS + 1, rows
    n_ctx = cache_k.shape[3]
    bias = _na_bias_table(rpb[i_even], rows)
    qg = q_norm.reshape(q_norm.shape[0], 1, NA_HEAD_DIM)
    kg = k_norm.reshape(k_norm.shape[0], 1, NA_HEAD_DIM)
    head_blk = lambda col0: pl.BlockSpec((None, t, NA_HEAD_DIM), lambda bi, h: (bi, 0, col0 + h))
    gain = pl.BlockSpec((None, 1, NA_HEAD_DIM), lambda bi, h: (i_even, 0, 0))
    cache = pl.BlockSpec((None, None, None, n_ctx, NA_HEAD_DIM), lambda bi, h: (bi, i_even, h, 0, 0))
    return pl.pallas_call(
        functools.partial(_na_kernel, rows=rows),
        grid=(b, NA_HEADS),
        in_specs=[head_blk(_Q_COL0), head_blk(_K_COL0), head_blk(_V_COL0), cache, cache, gain, gain,
                  pl.BlockSpec((None,) + bias.shape[1:], lambda bi, h: (h, 0, 0, 0))],
        out_specs=pl.BlockSpec((None, t, NA_HEAD_DIM), lambda bi, h: (bi, 0, h)),
        out_shape=jax.ShapeDtypeStruct((b, t, NA_WIDTH), BF16),
        scratch_shapes=[pltpu.VMEM((t, NA_HEAD_DIM), BF16)] * 3,
        compiler_params=_cparams("parallel", "parallel"),
        name="neighbourhood_attention",
    )(proj3, proj3, proj3, cache_k, cache_v, qg, kg, bias)


def _softplus(x):
    return jnp.maximum(x, 0.0) + jnp.log1p(jnp.exp(-jnp.abs(x)))


def _ssd_kernel(*refs, l, with_init):
    (x_ref, b_ref, c_ref, dt_ref, cwx_ref, cwb_ref, cwc_ref, cbx_ref, cbb_ref, cbc_ref,
     alog_ref, dtbias_ref, dsk_ref) = refs[:13]
    if with_init:
        init_ref = refs[13]
        outs = refs[14:]
    else:
        outs = refs[13:]
    (y_ref, sfin_ref, xs_ref, bs_ref, cs_ref, stf_ref, stb_ref,
     cumf_ref, cumb_ref, srcf_ref, srcb_ref) = outs
    g = pl.program_id(1)
    q = SSD_CHUNK
    nc = l // q
    gw = SSD_GROUP_WIDTH
    gh = SSD_GROUP_HEADS
    pad_l = SSD_CONV // 2
    halo = 8

    win_rows = q + 2 * halo
    sel_col = lax.broadcasted_iota(jnp.int32, (q, SSD_CONV * win_rows), 1)
    sel_row = lax.broadcasted_iota(jnp.int32, (q, SSD_CONV * win_rows), 0)
    shift_mat = ((sel_col % win_rows) == sel_row + (halo - pad_l) + sel_col // win_rows).astype(BF16)

    tap_w = {id(w_ref): [jnp.broadcast_to(w_ref[j:j + 1, :], (win_rows, w_ref.shape[1])).astype(BF16)
                         for j in range(SSD_CONV)]
             for w_ref in (cwx_ref, cwb_ref, cwc_ref)}

    def conv_chunk(c, carry):
        t0 = pl.multiple_of(c * q, q)
        lo_start = pl.multiple_of(jnp.maximum(t0 - halo, 0), halo)
        hi_start = pl.multiple_of(jnp.minimum(t0 + q, l - halo), halo)
        for src, w_ref, bias_ref, dst in ((x_ref, cwx_ref, cbx_ref, xs_ref),
                                          (b_ref, cwb_ref, cbb_ref, bs_ref),
                                          (c_ref, cwc_ref, cbc_ref, cs_ref)):
            lo = jnp.where(c > 0, src[pl.ds(lo_start, halo), :], 0.0)
            hi = jnp.where(c < nc - 1, src[pl.ds(hi_start, halo), :], 0.0)
            win = jnp.concatenate([lo, src[pl.ds(t0, q), :], hi], axis=0).astype(BF16)
            taps = jnp.concatenate([win * wj for wj in tap_w[id(w_ref)]], axis=0)
            out = _silu(_dot(shift_mat, taps) + bias_ref[...])
            dst[pl.ds(t0, q), :] = out
            if dst is xs_ref:
                y_ref[pl.ds(t0, q), :] = out * dsk_ref[...]
        dt = _softplus(pltpu.roll(dt_ref[pl.ds(t0, q), :], to_lane0, axis=1) + bias_g)
        dta = dt * a_g
        cum_f = _dot_f32_rhs(tri_f, dta)
        cum_b = _dot_f32_rhs(tri_b, dta)
        cumf_ref[pl.ds(t0, q), :] = cum_f
        cumb_ref[pl.ds(t0, q), :] = cum_b
        log_dt = jnp.log(dt)
        srcf_ref[pl.ds(t0, q), :] = jnp.transpose(cum_f - log_dt)
        srcb_ref[pl.ds(t0, q), :] = jnp.transpose(cum_b - log_dt)
        return carry

    ii = lax.broadcasted_iota(jnp.int32, (q, q), 0)
    jj = lax.broadcasted_iota(jnp.int32, (q, q), 1)
    keep_f = jj <= ii
    keep_b = jj >= ii
    tri_f = keep_f.astype(BF16)
    tri_b = keep_b.astype(BF16)
    n_dt = 2 * SSD_HEADS
    assert q == n_dt
    to_lane0 = (n_dt - g * gh) % n_dt
    bias_g = pltpu.roll(jnp.broadcast_to(dtbias_ref[...], (8, n_dt)), to_lane0, axis=1)[0:1, :]
    a_g = pltpu.roll(jnp.broadcast_to(-jnp.exp(alog_ref[...]), (8, n_dt)), to_lane0, axis=1)[0:1, :]

    lax.fori_loop(0, nc, conv_chunk, 0)

    pair_w = 2 * SSD_HEAD_DIM
    lane = lax.broadcasted_iota(jnp.int32, (1, gw), 1)
    half_mask = [((lane % pair_w) // SSD_HEAD_DIM) == s for s in range(2)]
    first_head_lanes = lax.broadcasted_iota(jnp.int32, (1, pair_w), 1) < SSD_HEAD_DIM

    def scan_chunk(c, reverse):
        keep, edge, lane0 = (keep_b, 0, SSD_HEADS) if reverse else (keep_f, q - 1, 0)
        cum_ref, src_ref, st_ref = ((cumb_ref, srcb_ref, stb_ref) if reverse
                                    else (cumf_ref, srcf_ref, stf_ref))
        t0 = pl.multiple_of(c * q, q)
        xc = xs_ref[pl.ds(t0, q), :]
        bc = bs_ref[pl.ds(t0, q), :]
        cc = cs_ref[pl.ds(t0, q), :]
        cb = _dot_nt(cc.astype(BF16), bc.astype(BF16))
        bct = jnp.transpose(bc)
        x_bf = xc.astype(BF16)
        st = st_ref[...]
        st_bf = st.astype(BF16)
        zero = jnp.zeros((), BF16)
        x_half = [jnp.where(half_mask[s], x_bf, zero) for s in range(2)]
        st_half = [jnp.where(half_mask[s], st_bf, zero) for s in range(2)]
        pieces = []
        for pair in range(gh // 2):
            c0 = pair * pair_w
            lhs_y, lhs_s, carry_decay = [], [], []
            for e in (2 * pair, 2 * pair + 1):
                ln = lane0 + e
                col = jnp.broadcast_to(cum_ref[pl.ds(t0, q), ln:ln + 1], (q, q))
                src = src_ref[pl.ds(t0 + ln, 1), :]
                at_edge = cum_ref[pl.ds(t0 + edge, 1), ln:ln + 1]
                mix = jnp.exp(jnp.where(keep, col - src, -jnp.inf))
                lhs_y.append((cb * mix).astype(BF16))
                lhs_y.append((cc * jnp.exp(col)).astype(BF16))
                lhs_s.append((bct * jnp.exp(at_edge - src)).astype(BF16))
                carry_decay.append(jnp.exp(at_edge))
            rhs_y = jnp.concatenate([x_half[0][:, c0:c0 + pair_w], st_half[0][:, c0:c0 + pair_w],
                                     x_half[1][:, c0:c0 + pair_w], st_half[1][:, c0:c0 + pair_w]], axis=0)
            pieces.append(_dot(jnp.concatenate(lhs_y, axis=1), rhs_y))
            rhs_s = jnp.concatenate([x_half[0][:, c0:c0 + pair_w], x_half[1][:, c0:c0 + pair_w]], axis=0)
            keep_frac = jnp.where(first_head_lanes, carry_decay[0], carry_decay[1])
            st_ref[:, c0:c0 + pair_w] = (st[:, c0:c0 + pair_w] * keep_frac
                                         + _dot(jnp.concatenate(lhs_s, axis=1), rhs_s))
        y_ref[pl.ds(t0, q), :] = y_ref[pl.ds(t0, q), :] + jnp.concatenate(pieces, axis=1)

    for d, st_ref in enumerate((stf_ref, stb_ref)):
        if with_init:
            st_ref[...] = jnp.transpose(init_ref[d].reshape(gw, SSD_STATE))
        else:
            st_ref[...] = jnp.zeros_like(st_ref)

    def step(s, carry):
        scan_chunk(s, False)
        scan_chunk(nc - 1 - s, True)
        return carry

    lax.fori_loop(0, nc, step, 0)
    for d, st_ref in enumerate((stf_ref, stb_ref)):
        sfin_ref[d] = jnp.transpose(st_ref[...]).reshape(gh, SSD_HEAD_DIM, SSD_STATE)


def _ssd_scan(proj3, dt_raw3, conv_w, conv_b, a_log, dt_bias, d_skip, i_odd, init=None):
    b, l, _ = proj3.shape
    gh, gw = SSD_GROUP_HEADS, SSD_GROUP_WIDTH
    n_dt = 2 * SSD_HEADS
    a_log3 = a_log.reshape(a_log.shape[0], 1, n_dt)
    dt_bias3 = dt_bias.reshape(dt_bias.shape[0], 1, n_dt)
    dsk = jnp.repeat(d_skip[i_odd].astype(F32), SSD_HEAD_DIM).reshape(1, SSD_INNER)
    conv_b3 = conv_b.reshape(conv_b.shape[0], 1, conv_b.shape[1])

    x_col0 = SSD_INNER // gw
    b_col0 = (2 * SSD_INNER) // SSD_STATE
    c_col0 = (2 * SSD_INNER + SSD_BC_WIDTH) // SSD_STATE
    cwx_col0 = 0
    cwb_col0 = SSD_INNER // SSD_STATE
    cwc_col0 = (SSD_INNER + SSD_BC_WIDTH) // SSD_STATE

    in_specs = [
        pl.BlockSpec((None, l, gw), lambda bi, g: (bi, 0, x_col0 + g)),
        pl.BlockSpec((None, l, SSD_STATE), lambda bi, g: (bi, 0, b_col0 + g)),
        pl.BlockSpec((None, l, SSD_STATE), lambda bi, g: (bi, 0, c_col0 + g)),
        pl.BlockSpec((None, l, n_dt), lambda bi, g: (bi, 0, 0)),
        pl.BlockSpec((None, SSD_CONV, gw), lambda bi, g: (i_odd, 0, cwx_col0 + g)),
        pl.BlockSpec((None, SSD_CONV, SSD_STATE), lambda bi, g: (i_odd, 0, cwb_col0 + g)),
        pl.BlockSpec((None, SSD_CONV, SSD_STATE), lambda bi, g: (i_odd, 0, cwc_col0 + g)),
        pl.BlockSpec((None, 1, gw), lambda bi, g: (i_odd, 0, cwx_col0 + g)),
        pl.BlockSpec((None, 1, SSD_STATE), lambda bi, g: (i_odd, 0, cwb_col0 + g)),
        pl.BlockSpec((None, 1, SSD_STATE), lambda bi, g: (i_odd, 0, cwc_col0 + g)),
        pl.BlockSpec((None, 1, n_dt), lambda bi, g: (i_odd, 0, 0)),
        pl.BlockSpec((None, 1, n_dt), lambda bi, g: (i_odd, 0, 0)),
        pl.BlockSpec((1, gw), lambda bi, g: (0, g)),
    ]
    args = [proj3, proj3, proj3, dt_raw3, conv_w, conv_w, conv_w,
            conv_b3, conv_b3, conv_b3, a_log3, dt_bias3, dsk]
    if init is not None:
        in_specs.append(pl.BlockSpec((None, None, 2, gh, SSD_HEAD_DIM, SSD_STATE),
                                     lambda bi, g: (bi, i_odd, 0, g, 0, 0)))
        args.append(init)
    return pl.pallas_call(
        functools.partial(_ssd_kernel, l=l, with_init=init is not None),
        grid=(b, SSD_GROUPS),
        in_specs=in_specs,
        out_specs=[
            pl.BlockSpec((None, l, gw), lambda bi, g: (bi, 0, g)),
            pl.BlockSpec((None, 2, gh, SSD_HEAD_DIM, SSD_STATE), lambda bi, g: (bi, 0, g, 0, 0)),
        ],
        out_shape=[
            jax.ShapeDtypeStruct((b, l, SSD_INNER), F32),
            jax.ShapeDtypeStruct((b, 2, SSD_HEADS, SSD_HEAD_DIM, SSD_STATE), F32),
        ],
        scratch_shapes=[
            pltpu.VMEM((l, gw), F32),
            pltpu.VMEM((l, SSD_STATE), F32),
            pltpu.VMEM((l, SSD_STATE), F32),
            pltpu.VMEM((SSD_STATE, gw), F32),
            pltpu.VMEM((SSD_STATE, gw), F32),
        ] + [pltpu.VMEM((l, n_dt), F32)] * 4,
        compiler_params=_cparams("parallel", "parallel"),
        name="ssd_scan",
    )(*args)


def _gate_norm_kernel(y_ref, z_ref, g_ref, o_ref):
    y = y_ref[...] * _silu(z_ref[...])
    o = y * lax.rsqrt(jnp.mean(y * y, axis=-1, keepdims=True) + NORM_EPS) * g_ref[...]
    o_ref[...] = o.astype(o_ref.dtype)


def _gate_norm(y, proj, norm_g, i_odd):
    m = y.shape[0]
    tm = ROW_TILE
    g3 = norm_g.reshape(norm_g.shape[0], 1, SSD_INNER)
    return pl.pallas_call(
        _gate_norm_kernel,
        grid=(m // tm,),
        in_specs=[
            pl.BlockSpec((tm, SSD_INNER), lambda i: (i, 0)),
            pl.BlockSpec((tm, SSD_INNER), lambda i: (i, 0)),
            pl.BlockSpec((None, 1, SSD_INNER), lambda i: (i_odd, 0, 0)),
        ],
        out_specs=pl.BlockSpec((tm, SSD_INNER), lambda i: (i, 0)),
        out_shape=jax.ShapeDtypeStruct((m, SSD_INNER), BF16),
        compiler_params=_cparams("parallel"),
        name="gate_norm",
    )(y, proj, g3)


def _expert_kernel(x_ref, wg_ref, wu_ref, wd_ref, gate_ref, o_ref, acc_ref):
    f = pl.program_id(2)
    x = x_ref[...]
    hg = _dot(x, wg_ref[...].astype(BF16))
    hu = _dot(x, wu_ref[...].astype(BF16))
    hdn = (_silu(hg) * hu).astype(BF16)

    @pl.when(f == 0)
    def _():
        acc_ref[...] = jnp.zeros_like(acc_ref)

    acc_ref[...] += _dot(hdn, wd_ref[...].astype(BF16))

    @pl.when(f == pl.num_programs(2) - 1)
    def _():
        o_ref[...] = (acc_ref[...] * gate_ref[...]).astype(o_ref.dtype)


def _expert_ffn(xe, gate, w_gate, w_up, w_down, layer):
    e, r, _ = xe.shape
    tr = min(r, 1024)
    tf = 256
    return pl.pallas_call(
        _expert_kernel,
        grid=(e, r // tr, EXPERT_FF // tf),
        in_specs=[
            pl.BlockSpec((None, tr, D_MODEL), lambda ei, ri, f: (ei, ri, 0)),
            pl.BlockSpec((None, None, D_MODEL, tf), lambda ei, ri, f: (layer, ei, 0, f)),
            pl.BlockSpec((None, None, D_MODEL, tf), lambda ei, ri, f: (layer, ei, 0, f)),
            pl.BlockSpec((None, None, tf, D_MODEL), lambda ei, ri, f: (layer, ei, f, 0)),
            pl.BlockSpec((None, tr, 1), lambda ei, ri, f: (ei, ri, 0)),
        ],
        out_specs=pl.BlockSpec((None, tr, D_MODEL), lambda ei, ri, f: (ei, ri, 0)),
        out_shape=jax.ShapeDtypeStruct((e, r, D_MODEL), BF16),
        scratch_shapes=[pltpu.VMEM((tr, D_MODEL), F32)],
        compiler_params=_cparams("parallel", "parallel", "arbitrary"),
        name="expert_ffn",
    )(xe, w_gate, w_up, w_down, gate)


def _combine_kernel(idx_ref, ye_ref, x_ref, gate_ref, o_ref, *, n, ec, cap):
    step = pl.program_id(2)
    token = lax.broadcasted_iota(jnp.int32, (n, cap), 0)
    onehot = jnp.concatenate(
        [jnp.where(token == idx_ref[k], 1.0, 0.0).astype(BF16) for k in range(ec)], axis=1)

    @pl.when(step == 0)
    def _():
        o_ref[...] = jnp.zeros_like(o_ref)

    o_ref[...] += _dot(onehot, ye_ref[...].reshape(ec * cap, ye_ref.shape[-1]))

    @pl.when(step == pl.num_programs(2) - 1)
    def _():
        o_ref[...] = x_ref[...] + gate_ref[...] * o_ref[...]


def _expert_combine(ye, idx, x, mod4, gate_idx, first_row, shared_cond):
    n_req, n_exp, cap = idx.shape
    m = x.shape[0]
    n = m // n_req
    ec = max(1, min(n_exp, 512 // cap))
    tn = D_MODEL // 2
    idx4 = idx.reshape(n_req, n_exp, 1, cap)
    cond_row = (lambda b: first_row) if shared_cond else (lambda b: first_row + b)
    return pl.pallas_call(
        functools.partial(_combine_kernel, n=n, ec=ec, cap=cap),
        grid=(n_req, D_MODEL // tn, n_exp // ec),
        in_specs=[
            pl.BlockSpec((None, ec, 1, cap), lambda b, j, s: (b, s, 0, 0)),
            pl.BlockSpec((ec, cap, tn), lambda b, j, s: (s, b, j)),
            pl.BlockSpec((n, tn), lambda b, j, s: (b, j)),
            pl.BlockSpec((None, None, 1, tn), lambda b, j, s: (cond_row(b), gate_idx, 0, j)),
        ],
        out_specs=pl.BlockSpec((n, tn), lambda b, j, s: (b, j)),
        out_shape=jax.ShapeDtypeStruct((m, D_MODEL), F32),
        compiler_params=_cparams("parallel", "parallel", "arbitrary"),
        name="expert_combine",
    )(idx4, ye, x, mod4)


def _prefix_count(mask_bf, strict_upper):
    n = mask_bf.shape[1]
    carry = jnp.zeros((mask_bf.shape[0], 1), F32)
    out = []
    for k in range(n // PREFIX_BLOCK):
        blk = mask_bf[:, k * PREFIX_BLOCK:(k + 1) * PREFIX_BLOCK]
        out.append(_dot(blk, strict_upper) + carry)
        carry = carry + jnp.sum(blk.astype(F32), axis=-1, keepdims=True)
    return jnp.concatenate(out, axis=1)


def _router_kernel(lg_ref, o_ref, slot_ref, feat_ref, *, n_req, n, cap):
    step = pl.program_id(0)
    n_exp = N_EXPERTS

    @pl.when(step == 0)
    def _():
        tok = lax.broadcasted_iota(jnp.int32, (n, ROUTE_LANES), 0)
        tlane = lax.broadcasted_iota(jnp.int32, (n, ROUTE_LANES), 1)
        radix_bits = ROUTE_TOKEN_RADIX.bit_length() - 1
        tok_feat = (jnp.where(tlane == ROUTE_TOKEN_LANE, tok >> radix_bits, 0)
                    + jnp.where(tlane == ROUTE_TOKEN_LANE + 1, tok & (ROUTE_TOKEN_RADIX - 1), 0)).astype(F32)
        is_expert = tlane < n_exp
        aff_rows = []
        for r in range(n_req):
            x = jnp.where(is_expert, lg_ref[pl.ds(r * n, n), :], -jnp.inf)
            e = jnp.exp(x - jnp.max(x, axis=-1, keepdims=True))
            aff = e / jnp.sum(e, axis=-1, keepdims=True)
            feat = tok_feat
            for p, part in enumerate(_split3(aff)):
                piece = part.astype(F32)
                feat = feat + (pltpu.roll(piece, p * n_exp, axis=1) if p else piece)
            feat_ref[r] = feat.astype(BF16)
            aff_rows.append(jnp.transpose(aff)[:n_exp])
        aff_t = jnp.concatenate(aff_rows, axis=0)
        bits = pltpu.bitcast(aff_t, jnp.int32)

        def bisect(_, carry):
            lo, hi = carry
            mid = lax.shift_right_arithmetic(lo + hi, 1)
            count = jnp.sum(jnp.where(bits >= mid, 1.0, 0.0), axis=-1, keepdims=True)
            enough = count >= cap
            return jnp.where(enough, mid, lo), jnp.where(enough, hi, mid)

        rows = n_req * n_exp
        lo0 = jnp.zeros((rows, 1), jnp.int32)
        hi0 = jnp.full((rows, 1), 0x3F800001, jnp.int32)
        kth, _ = lax.fori_loop(0, 31, bisect, (lo0, hi0))
        above = bits > kth
        tied = bits == kth
        blk_i = lax.broadcasted_iota(jnp.int32, (PREFIX_BLOCK, PREFIX_BLOCK), 0)
        blk_j = lax.broadcasted_iota(jnp.int32, (PREFIX_BLOCK, PREFIX_BLOCK), 1)
        strict_upper = (blk_i < blk_j).astype(BF16)
        need = cap - jnp.sum(jnp.where(above, 1.0, 0.0), axis=-1, keepdims=True)
        tie_rank = _prefix_count(jnp.where(tied, 1.0, 0.0).astype(BF16), strict_upper)
        chosen = jnp.logical_or(above, jnp.logical_and(tied, tie_rank < need))
        slot = _prefix_count(jnp.where(chosen, 1.0, 0.0).astype(BF16), strict_upper)
        slot_ref[...] = jnp.where(chosen, slot, -1.0)

    slot_id = lax.broadcasted_iota(jnp.int32, (cap, n), 0).astype(F32)
    for k in range(ROUTE_ROWS_PER_STEP):
        row = step * ROUTE_ROWS_PER_STEP + k
        onehot_t = jnp.where(slot_id == slot_ref[pl.ds(row, 1), :], 1.0, 0.0).astype(BF16)
        o_ref[k] = _dot(onehot_t, feat_ref[row // n_exp])


def _route(logits, n_req):
    m = logits.shape[0]
    n = m // n_req
    cap = EC_CAPACITY_FACTOR * n // N_EXPERTS
    rows = n_req * N_EXPERTS
    assert n % PREFIX_BLOCK == 0 and rows % ROUTE_ROWS_PER_STEP == 0
    rec, slot = pl.pallas_call(
        functools.partial(_router_kernel, n_req=n_req, n=n, cap=cap),
        grid=(rows // ROUTE_ROWS_PER_STEP,),
        in_specs=[pl.BlockSpec((m, ROUTE_LANES), lambda i: (0, 0))],
        out_specs=[pl.BlockSpec((ROUTE_ROWS_PER_STEP, cap, ROUTE_LANES), lambda i: (i, 0, 0)),
                   pl.BlockSpec((rows, n), lambda i: (0, 0))],
        out_shape=[jax.ShapeDtypeStruct((rows, cap, ROUTE_LANES), F32),
                   jax.ShapeDtypeStruct((rows, n), F32)],
        scratch_shapes=[pltpu.VMEM((n_req, n, ROUTE_LANES), BF16)],
        compiler_params=_cparams("arbitrary"),
        name="router",
    )(logits)
    rec = rec.reshape(n_req, N_EXPERTS, cap, ROUTE_LANES)
    idx = (ROUTE_TOKEN_RADIX * rec[..., ROUTE_TOKEN_LANE] + rec[..., ROUTE_TOKEN_LANE + 1]).astype(jnp.int32)
    pieces = rec[..., :3 * N_EXPERTS].reshape(n_req, N_EXPERTS, cap, 3, N_EXPERTS).sum(axis=3)
    own = jnp.eye(N_EXPERTS, dtype=F32)[None, :, None, :]
    gate = jnp.sum(pieces * own, axis=3)
    return idx, gate, slot


def _gather_kernel(slot_ref, h_ref, o_ref, *, n, ec, cap):
    slot_id = lax.broadcasted_iota(jnp.int32, (cap, n), 0).astype(F32)
    onehot = jnp.concatenate(
        [jnp.where(slot_id == slot_ref[k], 1.0, 0.0).astype(BF16) for k in range(ec)], axis=0)
    o_ref[...] = _dot(onehot, h_ref[...]).reshape(o_ref.shape).astype(o_ref.dtype)


def _gather_rows(h, slot, n_req):
    m, d = h.shape
    n = m // n_req
    cap = EC_CAPACITY_FACTOR * n // N_EXPERTS
    ec = max(1, min(N_EXPERTS, 2048 // cap))
    tn = d // 2
    slot4 = slot.reshape(n_req, N_EXPERTS, 1, n)
    return pl.pallas_call(
        functools.partial(_gather_kernel, n=n, ec=ec, cap=cap),
        grid=(n_req, d // tn, N_EXPERTS // ec),
        in_specs=[pl.BlockSpec((None, ec, 1, n), lambda b, j, s: (b, s, 0, 0)),
                  pl.BlockSpec((n, tn), lambda b, j, s: (b, j))],
        out_specs=pl.BlockSpec((ec, cap, tn), lambda b, j, s: (s, b, j)),
        out_shape=jax.ShapeDtypeStruct((N_EXPERTS, n_req * cap, d), h.dtype),
        compiler_params=_cparams("parallel", "parallel", "arbitrary"),
        name="gather_rows",
    )(slot4, h)


def _expert_choice(h, logits, x, n_req, w_gate, w_up, w_down, layer, mod4, gate_idx, first_row, shared_cond):
    idx, gate, slot = _route(logits, n_req)
    cap = idx.shape[2]
    gate = jnp.transpose(gate, (1, 0, 2)).reshape(N_EXPERTS, n_req * cap, 1)
    xe = _gather_rows(h, slot, n_req)
    ye = _expert_ffn(xe, gate, w_gate, w_up, w_down, layer)
    return _expert_combine(ye, idx, x, mod4, gate_idx, first_row, shared_cond)


class _Stream:
    def __init__(self, x3, first_row, shared_cond):
        self.batch, self.n, _ = x3.shape
        self.x = x3.reshape(self.batch * self.n, D_MODEL)
        self.first_row = first_row
        self.shared_cond = shared_cond
        self.rows_per_cond = self.batch * self.n if shared_cond else self.n

    def mod_args(self):
        return dict(rows_per_cond=self.rows_per_cond, first_row=self.first_row)


def kernel(x_prompt, x_sample, cache_na_k, cache_na_v, state_ssd, c, c_ctx, ada_w, ada_b, norm1_g, norm2_g,
           router_w, exp_w_gate, exp_w_up, exp_w_down, ab_w_in, pool_w, pool_scale, na_q_norm, na_k_norm,
           na_rpb, ab_w_out, ssd_w_in, ssd_conv_w, ssd_conv_b, ssd_a_log, ssd_dt_bias, ssd_d, ssd_norm_g,
           ssd_w_out):
    depth = ada_w.shape[0]
    dec_batch = x_sample.shape[0]
    latent = _Stream(x_sample, 0, shared_cond=False)
    prompt = _Stream(x_prompt, dec_batch, shared_cond=True)
    cond = jnp.concatenate(
        [c, c_ctx[None, :], jnp.zeros((MOD_ROWS - dec_batch - 1, D_MODEL), F32)], axis=0)

    router_pad = jnp.pad(router_w, ((0, 0), (0, 0), (0, ROUTE_LANES - N_EXPERTS)))

    new_k, new_v, new_s = [], [], []
    for layer in range(depth):
        i = layer // 2
        mod4 = _modulation(cond, ada_w, ada_b, layer).reshape(MOD_ROWS, N_MOD, 1, D_MODEL)
        for s in (prompt, latent):
            margs = s.mod_args()
            h = _norm_mod(s.x, norm1_g, mod4, layer, 0, 1, **margs)[0]
            tm = min(s.x.shape[0], 2048)
            if layer % 2 == 0:
                proj = _matmul([h], ab_w_in, i, tm=tm, tn=512, n_blocks=ab_w_in.shape[2] // 512)
                proj3 = proj.reshape(s.batch, s.n, -1)
                pooled = _pool_mixer(proj3, pool_w, pool_scale, i)
                if s is prompt:
                    att, k_new, v_new = _ctx_attention(proj3, na_q_norm, na_k_norm, i)
                    new_k.append(k_new)
                    new_v.append(v_new)
                else:
                    att = _neighbourhood_attention(proj3, cache_na_k, cache_na_v, na_q_norm, na_k_norm,
                                                   na_rpb, i)
                s.x = _matmul([pooled.reshape(-1, POOL_WIDTH), att.reshape(-1, NA_WIDTH)], ab_w_out, i,
                              tm=tm, tn=512, n_blocks=D_MODEL // 512, res=s.x, mod4=mod4, gate_idx=2,
                              **margs)
            else:
                proj = _matmul([h], ssd_w_in, i, tm=tm, tn=512, n_blocks=SSD_MAIN_DIM // 512)
                dt_raw = _matmul([h], ssd_w_in, i, tm=tm, tn=2 * SSD_HEADS, n_blocks=1,
                                 col_block0=SSD_MAIN_DIM // (2 * SSD_HEADS))
                proj3 = proj.reshape(s.batch, s.n, SSD_MAIN_DIM)
                dt3 = dt_raw.reshape(s.batch, s.n, 2 * SSD_HEADS)
                init = None if s is prompt else state_ssd
                y, s_fin = _ssd_scan(proj3, dt3, ssd_conv_w, ssd_conv_b, ssd_a_log, ssd_dt_bias, ssd_d, i,
                                     init=init)
                if s is prompt:
                    new_s.append(s_fin[:, None])
                yn = _gate_norm(y.reshape(-1, SSD_INNER), proj, ssd_norm_g, i)
                s.x = _matmul([yn], ssd_w_out, i, tm=min(tm, 1024), tn=512, n_blocks=D_MODEL // 512,
                              res=s.x, mod4=mod4, gate_idx=2, **margs)
            h2, logits = _norm_mod(s.x, norm2_g, mod4, layer, 3, 4, router_w=router_pad, **margs)
            s.x = _expert_choice(h2, logits, s.x, s.batch, exp_w_gate, exp_w_up, exp_w_down, layer,
                                 mod4, 5, s.first_row, s.shared_cond)

    y_prompt = prompt.x.reshape(x_prompt.shape)
    y_sample = latent.x.reshape(x_sample.shape)
    return (y_prompt, y_sample, jnp.concatenate(new_k, axis=1), jnp.concatenate(new_v, axis=1),
            jnp.concatenate(new_s, axis=1))
```

```python
import functools

import numpy as np
import jax
import jax.numpy as jnp
from jax import lax
from jax.experimental import pallas as pl
from jax.experimental.pallas import tpu as pltpu

F32 = jnp.float32
BF16 = jnp.bfloat16

D_MODEL = 2048
GRID_W = 64
POOL_WIDTH = 1024
POOL_GROUPS = 4
POOL_GROUP_DIM = 256
POOL_WINDOWS = (2, 4, 8, 16)
NA_WIDTH = 1024
NA_HEAD_DIM = 128
NA_HEADS = 8
WIN_H = 8
WIN_W = 16
SSD_INNER = 4096
SSD_HEAD_DIM = 64
SSD_HEADS = 64
SSD_GROUPS = 8
SSD_GROUP_HEADS = SSD_HEADS // SSD_GROUPS
SSD_GROUP_WIDTH = SSD_GROUP_HEADS * SSD_HEAD_DIM
SSD_STATE = 128
SSD_CONV = 4
SSD_CHUNK = 128
SSD_BC_WIDTH = SSD_GROUPS * SSD_STATE
SSD_MAIN_DIM = 2 * SSD_INNER + 2 * SSD_BC_WIDTH
N_EXPERTS = 16
EXPERT_FF = 1024
EC_CAPACITY_FACTOR = 2
NORM_EPS = 1e-6
N_MOD = 6
MOD_ROWS = 16
ROUTE_LANES = 128
ROUTE_TOKEN_LANE = 3 * N_EXPERTS
ROUTE_TOKEN_RADIX = 64
ROUTE_ROWS_PER_STEP = 8
PREFIX_BLOCK = 256

VMEM_LIMIT_BYTES = 56 * 1024 * 1024
ROW_TILE = 256


def _cparams(*sem):
    return pltpu.CompilerParams(dimension_semantics=sem, vmem_limit_bytes=VMEM_LIMIT_BYTES)


def _silu(x):
    return x * (1.0 / (1.0 + jnp.exp(-x)))


def _split2(x):
    hi = x.astype(BF16)
    lo = (x - hi.astype(F32)).astype(BF16)
    return hi, lo


def _split3(x):
    hi = x.astype(BF16)
    r = x - hi.astype(F32)
    mid = r.astype(BF16)
    lo = (r - mid.astype(F32)).astype(BF16)
    return hi, mid, lo


def _dot(a, b):
    return jnp.dot(a, b, preferred_element_type=F32)


def _dot_nt(a, b):
    return lax.dot_general(a, b, (((1,), (1,)), ((), ())), preferred_element_type=F32)


def _dot_f32_rhs(m, x):
    hi, mid, lo = _split3(x)
    return _dot(m, hi) + _dot(m, mid) + _dot(m, lo)


def _dot_split(a, b):
    a_hi, a_lo = _split2(a)
    b_hi, b_lo = _split2(b)
    return _dot(a_hi, b_hi) + _dot(a_lo, b_hi) + _dot(a_hi, b_lo)


def _mod_kernel(c_ref, w_ref, b_ref, o_ref):
    s = _silu(c_ref[...])
    o_ref[...] = _dot_split(s, w_ref[...]) + b_ref[...]


def _modulation(cond, ada_w, ada_b, layer):
    tn = 768
    n = N_MOD * D_MODEL
    ada_b3 = ada_b.reshape(ada_b.shape[0], 1, n)
    return pl.pallas_call(
        _mod_kernel,
        grid=(n // tn,),
        in_specs=[
            pl.BlockSpec((MOD_ROWS, D_MODEL), lambda j: (0, 0)),
            pl.BlockSpec((None, D_MODEL, tn), lambda j: (layer, 0, j)),
            pl.BlockSpec((None, 1, tn), lambda j: (layer, 0, j)),
        ],
        out_specs=pl.BlockSpec((MOD_ROWS, tn), lambda j: (0, j)),
        out_shape=jax.ShapeDtypeStruct((MOD_ROWS, n), F32),
        compiler_params=_cparams("parallel"),
        name="modulation",
    )(cond, ada_w, ada_b3)


def _mod_spec(which, rows_per_cond, first_row, tm):
    return pl.BlockSpec((None, None, 1, D_MODEL),
                        lambda i, *_: (first_row + (i * tm) // rows_per_cond, which, 0, 0))


def _norm_mod_kernel(x_ref, g_ref, sh_ref, sc_ref, *rest, with_router):
    x = x_ref[...]
    y = x * lax.rsqrt(jnp.mean(x * x, axis=-1, keepdims=True) + NORM_EPS) * g_ref[...]
    h = y * (1.0 + sc_ref[...]) + sh_ref[...]
    if with_router:
        rw_ref, h_ref, lg_ref = rest
        lg_ref[...] = _dot_split(h, rw_ref[...])
    else:
        (h_ref,) = rest
    h_ref[...] = h.astype(BF16)


def _norm_mod(x, g, mod4, layer, shift_idx, scale_idx, rows_per_cond, first_row, router_w=None):
    m = x.shape[0]
    tm = 2 * ROW_TILE
    g3 = g.reshape(g.shape[0], 1, D_MODEL)
    in_specs = [
        pl.BlockSpec((tm, D_MODEL), lambda i: (i, 0)),
        pl.BlockSpec((None, 1, D_MODEL), lambda i: (layer, 0, 0)),
        _mod_spec(shift_idx, rows_per_cond, first_row, tm),
        _mod_spec(scale_idx, rows_per_cond, first_row, tm),
    ]
    args = [x, g3, mod4, mod4]
    out_specs = [pl.BlockSpec((tm, D_MODEL), lambda i: (i, 0))]
    out_shape = [jax.ShapeDtypeStruct((m, D_MODEL), BF16)]
    if router_w is not None:
        in_specs.append(pl.BlockSpec((None, D_MODEL, ROUTE_LANES), lambda i: (layer, 0, 0)))
        args.append(router_w)
        out_specs.append(pl.BlockSpec((tm, ROUTE_LANES), lambda i: (i, 0)))
        out_shape.append(jax.ShapeDtypeStruct((m, ROUTE_LANES), F32))
    return pl.pallas_call(
        functools.partial(_norm_mod_kernel, with_router=router_w is not None),
        grid=(m // tm,),
        in_specs=in_specs,
        out_specs=out_specs,
        out_shape=out_shape,
        compiler_params=_cparams("parallel"),
        name="norm_mod",
    )(*args)


def _matmul_kernel(*refs, n_a, k_sizes, with_res):
    a_refs = refs[:n_a]
    w_ref = refs[n_a]
    if with_res:
        res_ref, gate_ref, o_ref = refs[n_a + 1:]
    else:
        (o_ref,) = refs[n_a + 1:]
    w = w_ref[...].astype(BF16)
    acc = None
    k0 = 0
    for a_ref, k in zip(a_refs, k_sizes):
        part = _dot(a_ref[...], w[k0:k0 + k])
        acc = part if acc is None else acc + part
        k0 += k
    if with_res:
        acc = res_ref[...] + gate_ref[...] * acc
    o_ref[...] = acc.astype(o_ref.dtype)


def _matmul(a_list, w, layer, *, tm, tn, n_blocks, col_block0=0, out_dtype=F32,
            res=None, mod4=None, gate_idx=None, rows_per_cond=None, first_row=None):
    m = a_list[0].shape[0]
    k_sizes = tuple(a.shape[1] for a in a_list)
    k_total = sum(k_sizes)
    in_specs = [pl.BlockSpec((tm, k), lambda i, j: (i, 0)) for k in k_sizes]
    in_specs.append(pl.BlockSpec((None, k_total, tn), lambda i, j: (layer, 0, col_block0 + j)))
    args = list(a_list) + [w]
    if res is not None:
        in_specs.append(pl.BlockSpec((tm, tn), lambda i, j: (i, j)))
        in_specs.append(pl.BlockSpec((None, None, 1, tn),
                                     lambda i, j: (first_row + (i * tm) // rows_per_cond, gate_idx, 0, j)))
        args += [res, mod4]
    return pl.pallas_call(
        functools.partial(_matmul_kernel, n_a=len(a_list), k_sizes=k_sizes, with_res=res is not None),
        grid=(m // tm, n_blocks),
        in_specs=in_specs,
        out_specs=pl.BlockSpec((tm, tn), lambda i, j: (i, j)),
        out_shape=jax.ShapeDtypeStruct((m, n_blocks * tn), out_dtype),
        compiler_params=_cparams("parallel", "arbitrary"),
        name="matmul",
    )(*args)


def _pool_kernel(u_ref, w_ref, s_ref, o_ref, *, n):
    g = pl.program_id(1)
    u = u_ref[...]
    t = lax.broadcasted_iota(jnp.int32, (n, 1), 0)
    for gi, win in enumerate(POOL_WINDOWS):
        @pl.when(g == gi)
        def _(win=win):
            half = win // 2
            total = u
            for d in range(-half, win - half):
                if d == 0:
                    continue
                shifted = pltpu.roll(u, (-d) % n, axis=0)
                valid = jnp.logical_and(t + d >= 0, t + d < n)
                total = total + jnp.where(valid, shifted, 0.0)
            cnt = (jnp.minimum(t - half + win, n) - jnp.maximum(t - half, 0)).astype(F32)
            pooled = total / cnt - u
            out = _dot(pooled.astype(BF16), w_ref[...].astype(BF16)) * s_ref[...]
            o_ref[...] = out.astype(o_ref.dtype)


def _pool_mixer(proj3, pool_w, pool_scale, i_even):
    b, n, _ = proj3.shape
    scale4 = pool_scale.reshape(pool_scale.shape[0], POOL_GROUPS, 1, POOL_GROUP_DIM)
    return pl.pallas_call(
        functools.partial(_pool_kernel, n=n),
        grid=(b, POOL_GROUPS),
        in_specs=[
            pl.BlockSpec((None, n, POOL_GROUP_DIM), lambda bi, g: (bi, 0, g)),
            pl.BlockSpec((None, None, POOL_GROUP_DIM, POOL_GROUP_DIM), lambda bi, g: (i_even, g, 0, 0)),
            pl.BlockSpec((None, None, 1, POOL_GROUP_DIM), lambda bi, g: (i_even, g, 0, 0)),
        ],
        out_specs=pl.BlockSpec((None, n, POOL_GROUP_DIM), lambda bi, g: (bi, 0, g)),
        out_shape=jax.ShapeDtypeStruct((b, n, POOL_WIDTH), BF16),
        compiler_params=_cparams("parallel", "parallel"),
        name="pool_mixer",
    )(proj3, pool_w, scale4)


def _head_norm(x, g):
    return x * lax.rsqrt(jnp.mean(x * x, axis=-1, keepdims=True) + NORM_EPS) * g


_Q_COL0 = POOL_WIDTH // NA_HEAD_DIM
_K_COL0 = (POOL_WIDTH + NA_WIDTH) // NA_HEAD_DIM
_V_COL0 = (POOL_WIDTH + 2 * NA_WIDTH) // NA_HEAD_DIM


def _ctx_attn_kernel(q_ref, k_ref, v_ref, qg_ref, kg_ref, o_ref, kn_ref, vn_ref):
    qn = _head_norm(q_ref[...], qg_ref[...])
    kn = _head_norm(k_ref[...], kg_ref[...])
    v = v_ref[...]
    kn_ref[...] = kn
    vn_ref[...] = v
    s = _dot_nt(qn.astype(BF16), kn.astype(BF16)) * (NA_HEAD_DIM ** -0.5)
    m = jnp.max(s, axis=-1, keepdims=True)
    p = jnp.exp(s - m)
    denom = jnp.sum(p, axis=-1, keepdims=True)
    o = _dot(p.astype(BF16), v.astype(BF16)) / denom
    o_ref[...] = o.astype(o_ref.dtype)


def _ctx_attention(proj3, q_norm, k_norm, i_even):
    b, n, _ = proj3.shape
    qg = q_norm.reshape(q_norm.shape[0], 1, NA_HEAD_DIM)
    kg = k_norm.reshape(k_norm.shape[0], 1, NA_HEAD_DIM)
    head_blk = lambda col0: pl.BlockSpec((None, n, NA_HEAD_DIM), lambda bi, h: (bi, 0, col0 + h))
    gain = pl.BlockSpec((None, 1, NA_HEAD_DIM), lambda bi, h: (i_even, 0, 0))
    cache = pl.BlockSpec((None, None, None, n, NA_HEAD_DIM), lambda bi, h: (bi, 0, h, 0, 0))
    return pl.pallas_call(
        _ctx_attn_kernel,
        grid=(b, NA_HEADS),
        in_specs=[head_blk(_Q_COL0), head_blk(_K_COL0), head_blk(_V_COL0), gain, gain],
        out_specs=[pl.BlockSpec((None, n, NA_HEAD_DIM), lambda bi, h: (bi, 0, h)), cache, cache],
        out_shape=[
            jax.ShapeDtypeStruct((b, n, NA_WIDTH), BF16),
            jax.ShapeDtypeStruct((b, 1, NA_HEADS, n, NA_HEAD_DIM), F32),
            jax.ShapeDtypeStruct((b, 1, NA_HEADS, n, NA_HEAD_DIM), F32),
        ],
        compiler_params=_cparams("parallel", "parallel"),
        name="ctx_attention",
    )(proj3, proj3, proj3, qg, kg)


NA_BAND = 4
NA_BAND_KEY_ROWS = WIN_H + NA_BAND - 1


def _na_band_key_start(r0, rows):
    return np.clip(r0 - WIN_H // 2, 0, rows - NA_BAND_KEY_ROWS)


def _na_bias_table(rpb, rows):
    col = np.arange(GRID_W)[:, None]
    kc = np.arange(GRID_W)[None, :]
    wstart = np.clip(col - WIN_W // 2, 0, GRID_W - WIN_W)
    inside = (kc >= wstart) & (kc < wstart + WIN_W)
    rel = np.clip(kc - col + WIN_W - 1, 0, 2 * WIN_W - 2)
    a = np.arange(NA_BAND)[:, None]
    kr = np.arange(NA_BAND_KEY_ROWS)[None, :]
    ridx, row_ok = [], []
    for r0 in (0, NA_BAND, rows - NA_BAND):
        r = r0 + a
        sr = np.clip(r - WIN_H // 2, 0, rows - WIN_H)
        krow = _na_band_key_start(r0, rows) + kr
        row_ok.append((krow >= sr) & (krow < sr + WIN_H))
        ridx.append(np.clip(krow - r + WIN_H - 1, 0, 2 * WIN_H - 2))
    ridx = np.stack(ridx)
    ok = np.stack(row_ok)[:, :, None, :, None] & inside[None, None, :, None, :]
    row_sel = np.eye(2 * WIN_H - 1, dtype=np.float32)[ridx]
    col_sel = np.eye(2 * WIN_W - 1, dtype=np.float32)[rel]
    tab = jnp.einsum('vaki,hij,cdj->hvackd', row_sel, rpb, col_sel, precision=lax.Precision.HIGHEST)
    tab = jnp.where(ok[None], tab, -jnp.inf)
    return tab.reshape(rpb.shape[0], 3, NA_BAND * GRID_W, NA_BAND_KEY_ROWS * GRID_W)


def _na_kernel(q_ref, k_ref, v_ref, kc_ref, vc_ref, qg_ref, kg_ref, bias_ref, o_ref,
               qs_ref, ks_ref, vs_ref, *, rows):
    qs_ref[...] = _head_norm(q_ref[...], qg_ref[...]).astype(BF16)
    ks_ref[...] = _head_norm(k_ref[...], kg_ref[...]).astype(BF16)
    vs_ref[...] = v_ref[...].astype(BF16)
    k_ctx = kc_ref[...].astype(BF16)
    v_ctx = vc_ref[...].astype(BF16)
    scale = NA_HEAD_DIM ** -0.5
    n_bands = rows // NA_BAND
    n_q = NA_BAND * GRID_W
    n_loc = NA_BAND_KEY_ROWS * GRID_W

    def band(i):
        r0 = i * NA_BAND
        key_row0 = jnp.clip(r0 - WIN_H // 2, 0, rows - NA_BAND_KEY_ROWS)
        variant = jnp.where(i == 0, 0, jnp.where(i == n_bands - 1, 2, 1))
        q0 = pl.multiple_of(r0 * GRID_W, n_q)
        q_b = qs_ref[pl.ds(q0, n_q), :]
        kstart = pl.multiple_of(key_row0 * GRID_W, GRID_W)
        k_blk = ks_ref[pl.ds(kstart, n_loc), :]
        v_blk = vs_ref[pl.ds(kstart, n_loc), :]
        s_loc = _dot_nt(q_b, k_blk) * scale + bias_ref[variant]
        s_ctx = _dot_nt(q_b, k_ctx) * scale
        m = jnp.maximum(jnp.max(s_loc, axis=-1, keepdims=True), jnp.max(s_ctx, axis=-1, keepdims=True))
        p_loc = jnp.exp(s_loc - m)
        p_ctx = jnp.exp(s_ctx - m)
        denom = jnp.sum(p_loc, axis=-1, keepdims=True) + jnp.sum(p_ctx, axis=-1, keepdims=True)
        o = (_dot(p_loc.astype(BF16), v_blk) + _dot(p_ctx.astype(BF16), v_ctx)) / denom
        o_ref[pl.ds(q0, n_q), :] = o.astype(o_ref.dtype)

    per_trip = 2 if n_bands % 2 == 0 else 1

    def trip(i, carry):
        for k in range(per_trip):
            band(i * per_trip + k)
        return carry

    lax.fori_loop(0, n_bands // per_trip, trip, 0)


def _neighbourhood_attention(proj3, cache_k, cache_v, q_norm, k_norm, rpb, i_even):
    b, t, _ = proj3.shape
    rows = t // GRID_W
    assert rows % NA_BAND == 0 and rows >= NA_BAND_KEY_ROWS + 1, rows
    n_ctx = cache_k.shape[3]
    bias = _na_bias_table(rpb[i_even], rows)
    qg = q_norm.reshape(q_norm.shape[0], 1, NA_HEAD_DIM)
    kg = k_norm.reshape(k_norm.shape[0], 1, NA_HEAD_DIM)
    head_blk = lambda col0: pl.BlockSpec((None, t, NA_HEAD_DIM), lambda bi, h: (bi, 0, col0 + h))
    gain = pl.BlockSpec((None, 1, NA_HEAD_DIM), lambda bi, h: (i_even, 0, 0))
    cache = pl.BlockSpec((None, None, None, n_ctx, NA_HEAD_DIM), lambda bi, h: (bi, i_even, h, 0, 0))
    return pl.pallas_call(
        functools.partial(_na_kernel, rows=rows),
        grid=(b, NA_HEADS),
        in_specs=[head_blk(_Q_COL0), head_blk(_K_COL0), head_blk(_V_COL0), cache, cache, gain, gain,
                  pl.BlockSpec((None,) + bias.shape[1:], lambda bi, h: (h, 0, 0, 0))],
        out_specs=pl.BlockSpec((None, t, NA_HEAD_DIM), lambda bi, h: (bi, 0, h)),
        out_shape=jax.ShapeDtypeStruct((b, t, NA_WIDTH), BF16),
        scratch_shapes=[pltpu.VMEM((t, NA_HEAD_DIM), BF16)] * 3,
        compiler_params=_cparams("parallel", "parallel"),
        name="neighbourhood_attention",
    )(proj3, proj3, proj3, cache_k, cache_v, qg, kg, bias)


def _softplus(x):
    return jnp.maximum(x, 0.0) + jnp.log1p(jnp.exp(-jnp.abs(x)))


def _ssd_kernel(*refs, l, with_init):
    (x_ref, b_ref, c_ref, dt_ref, cwx_ref, cwb_ref, cwc_ref, cbx_ref, cbb_ref, cbc_ref,
     alog_ref, dtbias_ref, dsk_ref) = refs[:13]
    if with_init:
        init_ref = refs[13]
        outs = refs[14:]
    else:
        outs = refs[13:]
    (y_ref, sfin_ref, xs_ref, bs_ref, cs_ref, stf_ref, stb_ref,
     cumf_ref, cumb_ref, srcf_ref, srcb_ref) = outs
    g = pl.program_id(1)
    q = SSD_CHUNK
    nc = l // q
    gw = SSD_GROUP_WIDTH
    gh = SSD_GROUP_HEADS
    pad_l = SSD_CONV // 2
    halo = 8

    win_rows = q + 2 * halo
    sel_col = lax.broadcasted_iota(jnp.int32, (q, SSD_CONV * win_rows), 1)
    sel_row = lax.broadcasted_iota(jnp.int32, (q, SSD_CONV * win_rows), 0)
    shift_mat = ((sel_col % win_rows) == sel_row + (halo - pad_l) + sel_col // win_rows).astype(BF16)

    tap_w = {id(w_ref): [jnp.broadcast_to(w_ref[j:j + 1, :], (win_rows, w_ref.shape[1])).astype(BF16)
                         for j in range(SSD_CONV)]
             for w_ref in (cwx_ref, cwb_ref, cwc_ref)}

    def conv_chunk(c, carry):
        t0 = pl.multiple_of(c * q, q)
        lo_start = pl.multiple_of(jnp.maximum(t0 - halo, 0), halo)
        hi_start = pl.multiple_of(jnp.minimum(t0 + q, l - halo), halo)
        for src, w_ref, bias_ref, dst in ((x_ref, cwx_ref, cbx_ref, xs_ref),
                                          (b_ref, cwb_ref, cbb_ref, bs_ref),
                                          (c_ref, cwc_ref, cbc_ref, cs_ref)):
            lo = jnp.where(c > 0, src[pl.ds(lo_start, halo), :], 0.0)
            hi = jnp.where(c < nc - 1, src[pl.ds(hi_start, halo), :], 0.0)
            win = jnp.concatenate([lo, src[pl.ds(t0, q), :], hi], axis=0).astype(BF16)
            taps = jnp.concatenate([win * wj for wj in tap_w[id(w_ref)]], axis=0)
            out = _silu(_dot(shift_mat, taps) + bias_ref[...])
            dst[pl.ds(t0, q), :] = out
            if dst is xs_ref:
                y_ref[pl.ds(t0, q), :] = out * dsk_ref[...]
        dt = _softplus(pltpu.roll(dt_ref[pl.ds(t0, q), :], to_lane0, axis=1) + bias_g)
        dta = dt * a_g
        cum_f = _dot_f32_rhs(tri_f, dta)
        cum_b = _dot_f32_rhs(tri_b, dta)
        cumf_ref[pl.ds(t0, q), :] = cum_f
        cumb_ref[pl.ds(t0, q), :] = cum_b
        log_dt = jnp.log(dt)
        srcf_ref[pl.ds(t0, q), :] = jnp.transpose(cum_f - log_dt)
        srcb_ref[pl.ds(t0, q), :] = jnp.transpose(cum_b - log_dt)
        return carry

    ii = lax.broadcasted_iota(jnp.int32, (q, q), 0)
    jj = lax.broadcasted_iota(jnp.int32, (q, q), 1)
    keep_f = jj <= ii
    keep_b = jj >= ii
    tri_f = keep_f.astype(BF16)
    tri_b = keep_b.astype(BF16)
    n_dt = 2 * SSD_HEADS
    assert q == n_dt
    to_lane0 = (n_dt - g * gh) % n_dt
    bias_g = pltpu.roll(jnp.broadcast_to(dtbias_ref[...], (8, n_dt)), to_lane0, axis=1)[0:1, :]
    a_g = pltpu.roll(jnp.broadcast_to(-jnp.exp(alog_ref[...]), (8, n_dt)), to_lane0, axis=1)[0:1, :]

    lax.fori_loop(0, nc, conv_chunk, 0)

    pair_w = 2 * SSD_HEAD_DIM
    lane = lax.broadcasted_iota(jnp.int32, (1, gw), 1)
    half_mask = [((lane % pair_w) // SSD_HEAD_DIM) == s for s in range(2)]
    first_head_lanes = lax.broadcasted_iota(jnp.int32, (1, pair_w), 1) < SSD_HEAD_DIM

    def scan_chunk(c, reverse):
        keep, edge, lane0 = (keep_b, 0, SSD_HEADS) if reverse else (keep_f, q - 1, 0)
        cum_ref, src_ref, st_ref = ((cumb_ref, srcb_ref, stb_ref) if reverse
                                    else (cumf_ref, srcf_ref, stf_ref))
        t0 = pl.multiple_of(c * q, q)
        xc = xs_ref[pl.ds(t0, q), :]
        bc = bs_ref[pl.ds(t0, q), :]
        cc = cs_ref[pl.ds(t0, q), :]
        cb = _dot_nt(cc.astype(BF16), bc.astype(BF16))
        bct = jnp.transpose(bc)
        x_bf = xc.astype(BF16)
        st = st_ref[...]
        st_bf = st.astype(BF16)
        zero = jnp.zeros((), BF16)
        x_half = [jnp.where(half_mask[s], x_bf, zero) for s in range(2)]
        st_half = [jnp.where(half_mask[s], st_bf, zero) for s in range(2)]
        pieces = []
        for pair in range(gh // 2):
            c0 = pair * pair_w
            lhs_y, lhs_s, carry_decay = [], [], []
            for e in (2 * pair, 2 * pair + 1):
                ln = lane0 + e
                col = jnp.broadcast_to(cum_ref[pl.ds(t0, q), ln:ln + 1], (q, q))
                src = src_ref[pl.ds(t0 + ln, 1), :]
                at_edge = cum_ref[pl.ds(t0 + edge, 1), ln:ln + 1]
                mix = jnp.exp(jnp.where(keep, col - src, -jnp.inf))
                lhs_y.append((cb * mix).astype(BF16))
                lhs_y.append((cc * jnp.exp(col)).astype(BF16))
                lhs_s.append((bct * jnp.exp(at_edge - src)).astype(BF16))
                carry_decay.append(jnp.exp(at_edge))
            rhs_y = jnp.concatenate([x_half[0][:, c0:c0 + pair_w], st_half[0][:, c0:c0 + pair_w],
                                     x_half[1][:, c0:c0 + pair_w], st_half[1][:, c0:c0 + pair_w]], axis=0)
            pieces.append(_dot(jnp.concatenate(lhs_y, axis=1), rhs_y))
            rhs_s = jnp.concatenate([x_half[0][:, c0:c0 + pair_w], x_half[1][:, c0:c0 + pair_w]], axis=0)
            keep_frac = jnp.where(first_head_lanes, carry_decay[0], carry_decay[1])
            st_ref[:, c0:c0 + pair_w] = (st[:, c0:c0 + pair_w] * keep_frac
                                         + _dot(jnp.concatenate(lhs_s, axis=1), rhs_s))
        y_ref[pl.ds(t0, q), :] = y_ref[pl.ds(t0, q), :] + jnp.concatenate(pieces, axis=1)

    for d, st_ref in enumerate((stf_ref, stb_ref)):
        if with_init:
            st_ref[...] = jnp.transpose(init_ref[d].reshape(gw, SSD_STATE))
        else:
            st_ref[...] = jnp.zeros_like(st_ref)

    def step(s, carry):
        scan_chunk(s, False)
        scan_chunk(nc - 1 - s, True)
        return carry

    lax.fori_loop(0, nc, step, 0)
    for d, st_ref in enumerate((stf_ref, stb_ref)):
        sfin_ref[d] = jnp.transpose(st_ref[...]).reshape(gh, SSD_HEAD_DIM, SSD_STATE)


def _ssd_scan(proj3, dt_raw3, conv_w, conv_b, a_log, dt_bias, d_skip, i_odd, init=None):
    b, l, _ = proj3.shape
    gh, gw = SSD_GROUP_HEADS, SSD_GROUP_WIDTH
    n_dt = 2 * SSD_HEADS
    a_log3 = a_log.reshape(a_log.shape[0], 1, n_dt)
    dt_bias3 = dt_bias.reshape(dt_bias.shape[0], 1, n_dt)
    dsk = jnp.repeat(d_skip[i_odd].astype(F32), SSD_HEAD_DIM).reshape(1, SSD_INNER)
    conv_b3 = conv_b.reshape(conv_b.shape[0], 1, conv_b.shape[1])

    x_col0 = SSD_INNER // gw
    b_col0 = (2 * SSD_INNER) // SSD_STATE
    c_col0 = (2 * SSD_INNER + SSD_BC_WIDTH) // SSD_STATE
    cwx_col0 = 0
    cwb_col0 = SSD_INNER // SSD_STATE
    cwc_col0 = (SSD_INNER + SSD_BC_WIDTH) // SSD_STATE

    in_specs = [
        pl.BlockSpec((None, l, gw), lambda bi, g: (bi, 0, x_col0 + g)),
        pl.BlockSpec((None, l, SSD_STATE), lambda bi, g: (bi, 0, b_col0 + g)),
        pl.BlockSpec((None, l, SSD_STATE), lambda bi, g: (bi, 0, c_col0 + g)),
        pl.BlockSpec((None, l, n_dt), lambda bi, g: (bi, 0, 0)),
        pl.BlockSpec((None, SSD_CONV, gw), lambda bi, g: (i_odd, 0, cwx_col0 + g)),
        pl.BlockSpec((None, SSD_CONV, SSD_STATE), lambda bi, g: (i_odd, 0, cwb_col0 + g)),
        pl.BlockSpec((None, SSD_CONV, SSD_STATE), lambda bi, g: (i_odd, 0, cwc_col0 + g)),
        pl.BlockSpec((None, 1, gw), lambda bi, g: (i_odd, 0, cwx_col0 + g)),
        pl.BlockSpec((None, 1, SSD_STATE), lambda bi, g: (i_odd, 0, cwb_col0 + g)),
        pl.BlockSpec((None, 1, SSD_STATE), lambda bi, g: (i_odd, 0, cwc_col0 + g)),
        pl.BlockSpec((None, 1, n_dt), lambda bi, g: (i_odd, 0, 0)),
        pl.BlockSpec((None, 1, n_dt), lambda bi, g: (i_odd, 0, 0)),
        pl.BlockSpec((1, gw), lambda bi, g: (0, g)),
    ]
    args = [proj3, proj3, proj3, dt_raw3, conv_w, conv_w, conv_w,
            conv_b3, conv_b3, conv_b3, a_log3, dt_bias3, dsk]
    if init is not None:
        in_specs.append(pl.BlockSpec((None, None, 2, gh, SSD_HEAD_DIM, SSD_STATE),
                                     lambda bi, g: (bi, i_odd, 0, g, 0, 0)))
        args.append(init)
    return pl.pallas_call(
        functools.partial(_ssd_kernel, l=l, with_init=init is not None),
        grid=(b, SSD_GROUPS),
        in_specs=in_specs,
        out_specs=[
            pl.BlockSpec((None, l, gw), lambda bi, g: (bi, 0, g)),
            pl.BlockSpec((None, 2, gh, SSD_HEAD_DIM, SSD_STATE), lambda bi, g: (bi, 0, g, 0, 0)),
        ],
        out_shape=[
            jax.ShapeDtypeStruct((b, l, SSD_INNER), F32),
            jax.ShapeDtypeStruct((b, 2, SSD_HEADS, SSD_HEAD_DIM, SSD_STATE), F32),
        ],
        scratch_shapes=[
            pltpu.VMEM((l, gw), F32),
            pltpu.VMEM((l, SSD_STATE), F32),
            pltpu.VMEM((l, SSD_STATE), F32),
            pltpu.VMEM((SSD_STATE, gw), F32),
            pltpu.VMEM((SSD_STATE, gw), F32),
        ] + [pltpu.VMEM((l, n_dt), F32)] * 4,
        compiler_params=_cparams("parallel", "parallel"),
        name="ssd_scan",
    )(*args)


def _gate_norm_kernel(y_ref, z_ref, g_ref, o_ref):
    y = y_ref[...] * _silu(z_ref[...])
    o = y * lax.rsqrt(jnp.mean(y * y, axis=-1, keepdims=True) + NORM_EPS) * g_ref[...]
    o_ref[...] = o.astype(o_ref.dtype)


def _gate_norm(y, proj, norm_g, i_odd):
    m = y.shape[0]
    tm = ROW_TILE
    g3 = norm_g.reshape(norm_g.shape[0], 1, SSD_INNER)
    return pl.pallas_call(
        _gate_norm_kernel,
        grid=(m // tm,),
        in_specs=[
            pl.BlockSpec((tm, SSD_INNER), lambda i: (i, 0)),
            pl.BlockSpec((tm, SSD_INNER), lambda i: (i, 0)),
            pl.BlockSpec((None, 1, SSD_INNER), lambda i: (i_odd, 0, 0)),
        ],
        out_specs=pl.BlockSpec((tm, SSD_INNER), lambda i: (i, 0)),
        out_shape=jax.ShapeDtypeStruct((m, SSD_INNER), BF16),
        compiler_params=_cparams("parallel"),
        name="gate_norm",
    )(y, proj, g3)


def _expert_kernel(x_ref, wg_ref, wu_ref, wd_ref, gate_ref, o_ref, acc_ref):
    f = pl.program_id(2)
    x = x_ref[...]
    hg = _dot(x, wg_ref[...].astype(BF16))
    hu = _dot(x, wu_ref[...].astype(BF16))
    hdn = (_silu(hg) * hu).astype(BF16)

    @pl.when(f == 0)
    def _():
        acc_ref[...] = jnp.zeros_like(acc_ref)

    acc_ref[...] += _dot(hdn, wd_ref[...].astype(BF16))

    @pl.when(f == pl.num_programs(2) - 1)
    def _():
        o_ref[...] = (acc_ref[...] * gate_ref[...]).astype(o_ref.dtype)


def _expert_ffn(xe, gate, w_gate, w_up, w_down, layer):
    e, r, _ = xe.shape
    tr = min(r, 1024)
    tf = 256
    return pl.pallas_call(
        _expert_kernel,
        grid=(e, r // tr, EXPERT_FF // tf),
        in_specs=[
            pl.BlockSpec((None, tr, D_MODEL), lambda ei, ri, f: (ei, ri, 0)),
            pl.BlockSpec((None, None, D_MODEL, tf), lambda ei, ri, f: (layer, ei, 0, f)),
            pl.BlockSpec((None, None, D_MODEL, tf), lambda ei, ri, f: (layer, ei, 0, f)),
            pl.BlockSpec((None, None, tf, D_MODEL), lambda ei, ri, f: (layer, ei, f, 0)),
            pl.BlockSpec((None, tr, 1), lambda ei, ri, f: (ei, ri, 0)),
        ],
        out_specs=pl.BlockSpec((None, tr, D_MODEL), lambda ei, ri, f: (ei, ri, 0)),
        out_shape=jax.ShapeDtypeStruct((e, r, D_MODEL), BF16),
        scratch_shapes=[pltpu.VMEM((tr, D_MODEL), F32)],
        compiler_params=_cparams("parallel", "parallel", "arbitrary"),
        name="expert_ffn",
    )(xe, w_gate, w_up, w_down, gate)


def _combine_kernel(idx_ref, ye_ref, x_ref, gate_ref, o_ref, *, n, ec, cap):
    step = pl.program_id(2)
    token = lax.broadcasted_iota(jnp.int32, (n, cap), 0)
    onehot = jnp.concatenate(
        [jnp.where(token == idx_ref[k], 1.0, 0.0).astype(BF16) for k in range(ec)], axis=1)

    @pl.when(step == 0)
    def _():
        o_ref[...] = jnp.zeros_like(o_ref)

    o_ref[...] += _dot(onehot, ye_ref[...].reshape(ec * cap, ye_ref.shape[-1]))

    @pl.when(step == pl.num_programs(2) - 1)
    def _():
        o_ref[...] = x_ref[...] + gate_ref[...] * o_ref[...]


def _expert_combine(ye, idx, x, mod4, gate_idx, first_row, shared_cond):
    n_req, n_exp, cap = idx.shape
    m = x.shape[0]
    n = m // n_req
    ec = max(1, min(n_exp, 512 // cap))
    tn = D_MODEL // 2
    idx4 = idx.reshape(n_req, n_exp, 1, cap)
    cond_row = (lambda b: first_row) if shared_cond else (lambda b: first_row + b)
    return pl.pallas_call(
        functools.partial(_combine_kernel, n=n, ec=ec, cap=cap),
        grid=(n_req, D_MODEL // tn, n_exp // ec),
        in_specs=[
            pl.BlockSpec((None, ec, 1, cap), lambda b, j, s: (b, s, 0, 0)),
            pl.BlockSpec((ec, cap, tn), lambda b, j, s: (s, b, j)),
            pl.BlockSpec((n, tn), lambda b, j, s: (b, j)),
            pl.BlockSpec((None, None, 1, tn), lambda b, j, s: (cond_row(b), gate_idx, 0, j)),
        ],
        out_specs=pl.BlockSpec((n, tn), lambda b, j, s: (b, j)),
        out_shape=jax.ShapeDtypeStruct((m, D_MODEL), F32),
        compiler_params=_cparams("parallel", "parallel", "arbitrary"),
        name="expert_combine",
    )(idx4, ye, x, mod4)


def _prefix_count(mask_bf, strict_upper):
    n = mask_bf.shape[1]
    carry = jnp.zeros((mask_bf.shape[0], 1), F32)
    out = []
    for k in range(n // PREFIX_BLOCK):
        blk = mask_bf[:, k * PREFIX_BLOCK:(k + 1) * PREFIX_BLOCK]
        out.append(_dot(blk, strict_upper) + carry)
        carry = carry + jnp.sum(blk.astype(F32), axis=-1, keepdims=True)
    return jnp.concatenate(out, axis=1)


def _router_kernel(lg_ref, o_ref, slot_ref, feat_ref, *, n_req, n, cap):
    step = pl.program_id(0)
    n_exp = N_EXPERTS

    @pl.when(step == 0)
    def _():
        tok = lax.broadcasted_iota(jnp.int32, (n, ROUTE_LANES), 0)
        tlane = lax.broadcasted_iota(jnp.int32, (n, ROUTE_LANES), 1)
        radix_bits = ROUTE_TOKEN_RADIX.bit_length() - 1
        tok_feat = (jnp.where(tlane == ROUTE_TOKEN_LANE, tok >> radix_bits, 0)
                    + jnp.where(tlane == ROUTE_TOKEN_LANE + 1, tok & (ROUTE_TOKEN_RADIX - 1), 0)).astype(F32)
        is_expert = tlane < n_exp
        aff_rows = []
        for r in range(n_req):
            x = jnp.where(is_expert, lg_ref[pl.ds(r * n, n), :], -jnp.inf)
            e = jnp.exp(x - jnp.max(x, axis=-1, keepdims=True))
            aff = e / jnp.sum(e, axis=-1, keepdims=True)
            feat = tok_feat
            for p, part in enumerate(_split3(aff)):
                piece = part.astype(F32)
                feat = feat + (pltpu.roll(piece, p * n_exp, axis=1) if p else piece)
            feat_ref[r] = feat.astype(BF16)
            aff_rows.append(jnp.transpose(aff)[:n_exp])
        aff_t = jnp.concatenate(aff_rows, axis=0)
        bits = pltpu.bitcast(aff_t, jnp.int32)

        def bisect(_, carry):
            lo, hi = carry
            mid = lax.shift_right_arithmetic(lo + hi, 1)
            count = jnp.sum(jnp.where(bits >= mid, 1.0, 0.0), axis=-1, keepdims=True)
            enough = count >= cap
            return jnp.where(enough, mid, lo), jnp.where(enough, hi, mid)

        rows = n_req * n_exp
        lo0 = jnp.zeros((rows, 1), jnp.int32)
        hi0 = jnp.full((rows, 1), 0x3F800001, jnp.int32)
        kth, _ = lax.fori_loop(0, 31, bisect, (lo0, hi0))
        above = bits > kth
        tied = bits == kth
        blk_i = lax.broadcasted_iota(jnp.int32, (PREFIX_BLOCK, PREFIX_BLOCK), 0)
        blk_j = lax.broadcasted_iota(jnp.int32, (PREFIX_BLOCK, PREFIX_BLOCK), 1)
        strict_upper = (blk_i < blk_j).astype(BF16)
        need = cap - jnp.sum(jnp.where(above, 1.0, 0.0), axis=-1, keepdims=True)
        tie_rank = _prefix_count(jnp.where(tied, 1.0, 0.0).astype(BF16), strict_upper)
        chosen = jnp.logical_or(above, jnp.logical_and(tied, tie_rank < need))
        slot = _prefix_count(jnp.where(chosen, 1.0, 0.0).astype(BF16), strict_upper)
        slot_ref[...] = jnp.where(chosen, slot, -1.0)

    slot_id = lax.broadcasted_iota(jnp.int32, (cap, n), 0).astype(F32)
    for k in range(ROUTE_ROWS_PER_STEP):
        row = step * ROUTE_ROWS_PER_STEP + k
        onehot_t = jnp.where(slot_id == slot_ref[pl.ds(row, 1), :], 1.0, 0.0).astype(BF16)
        o_ref[k] = _dot(onehot_t, feat_ref[row // n_exp])


def _route(logits, n_req):
    m = logits.shape[0]
    n = m // n_req
    cap = EC_CAPACITY_FACTOR * n // N_EXPERTS
    rows = n_req * N_EXPERTS
    assert n % PREFIX_BLOCK == 0 and rows % ROUTE_ROWS_PER_STEP == 0
    rec, slot = pl.pallas_call(
        functools.partial(_router_kernel, n_req=n_req, n=n, cap=cap),
        grid=(rows // ROUTE_ROWS_PER_STEP,),
        in_specs=[pl.BlockSpec((m, ROUTE_LANES), lambda i: (0, 0))],
        out_specs=[pl.BlockSpec((ROUTE_ROWS_PER_STEP, cap, ROUTE_LANES), lambda i: (i, 0, 0)),
                   pl.BlockSpec((rows, n), lambda i: (0, 0))],
        out_shape=[jax.ShapeDtypeStruct((rows, cap, ROUTE_LANES), F32),
                   jax.ShapeDtypeStruct((rows, n), F32)],
        scratch_shapes=[pltpu.VMEM((n_req, n, ROUTE_LANES), BF16)],
        compiler_params=_cparams("arbitrary"),
        name="router",
    )(logits)
    rec = rec.reshape(n_req, N_EXPERTS, cap, ROUTE_LANES)
    idx = (ROUTE_TOKEN_RADIX * rec[..., ROUTE_TOKEN_LANE] + rec[..., ROUTE_TOKEN_LANE + 1]).astype(jnp.int32)
    pieces = rec[..., :3 * N_EXPERTS].reshape(n_req, N_EXPERTS, cap, 3, N_EXPERTS).sum(axis=3)
    own = jnp.eye(N_EXPERTS, dtype=F32)[None, :, None, :]
    gate = jnp.sum(pieces * own, axis=3)
    return idx, gate, slot


def _gather_kernel(slot_ref, h_ref, o_ref, *, n, ec, cap):
    slot_id = lax.broadcasted_iota(jnp.int32, (cap, n), 0).astype(F32)
    onehot = jnp.concatenate(
        [jnp.where(slot_id == slot_ref[k], 1.0, 0.0).astype(BF16) for k in range(ec)], axis=0)
    o_ref[...] = _dot(onehot, h_ref[...]).reshape(o_ref.shape).astype(o_ref.dtype)


def _gather_rows(h, slot, n_req):
    m, d = h.shape
    n = m // n_req
    cap = EC_CAPACITY_FACTOR * n // N_EXPERTS
    ec = max(1, min(N_EXPERTS, 2048 // cap))
    tn = d // 2
    slot4 = slot.reshape(n_req, N_EXPERTS, 1, n)
    return pl.pallas_call(
        functools.partial(_gather_kernel, n=n, ec=ec, cap=cap),
        grid=(n_req, d // tn, N_EXPERTS // ec),
        in_specs=[pl.BlockSpec((None, ec, 1, n), lambda b, j, s: (b, s, 0, 0)),
                  pl.BlockSpec((n, tn), lambda b, j, s: (b, j))],
        out_specs=pl.BlockSpec((ec, cap, tn), lambda b, j, s: (s, b, j)),
        out_shape=jax.ShapeDtypeStruct((N_EXPERTS, n_req * cap, d), h.dtype),
        compiler_params=_cparams("parallel", "parallel", "arbitrary"),
        name="gather_rows",
    )(slot4, h)


def _expert_choice(h, logits, x, n_req, w_gate, w_up, w_down, layer, mod4, gate_idx, first_row, shared_cond):
    idx, gate, slot = _route(logits, n_req)
    cap = idx.shape[2]
    gate = jnp.transpose(gate, (1, 0, 2)).reshape(N_EXPERTS, n_req * cap, 1)
    xe = _gather_rows(h, slot, n_req)
    ye = _expert_ffn(xe, gate, w_gate, w_up, w_down, layer)
    return _expert_combine(ye, idx, x, mod4, gate_idx, first_row, shared_cond)


class _Stream:
    def __init__(self, x3, first_row, shared_cond):
        self.batch, self.n, _ = x3.shape
        self.x = x3.reshape(self.batch * self.n, D_MODEL)
        self.first_row = first_row
        self.shared_cond = shared_cond
        self.rows_per_cond = self.batch * self.n if shared_cond else self.n

    def mod_args(self):
        return dict(rows_per_cond=self.rows_per_cond, first_row=self.first_row)


def kernel(x_prompt, x_sample, cache_na_k, cache_na_v, state_ssd, c, c_ctx, ada_w, ada_b, norm1_g, norm2_g,
           router_w, exp_w_gate, exp_w_up, exp_w_down, ab_w_in, pool_w, pool_scale, na_q_norm, na_k_norm,
           na_rpb, ab_w_out, ssd_w_in, ssd_conv_w, ssd_conv_b, ssd_a_log, ssd_dt_bias, ssd_d, ssd_norm_g,
           ssd_w_out):
    depth = ada_w.shape[0]
    dec_batch = x_sample.shape[0]
    latent = _Stream(x_sample, 0, shared_cond=False)
    prompt = _Stream(x_prompt, dec_batch, shared_cond=True)
    cond = jnp.concatenate(
        [c, c_ctx[None, :], jnp.zeros((MOD_ROWS - dec_batch - 1, D_MODEL), F32)], axis=0)

    router_pad = jnp.pad(router_w, ((0, 0), (0, 0), (0, ROUTE_LANES - N_EXPERTS)))

    new_k, new_v, new_s = [], [], []
    for layer in range(depth):
        i = layer // 2
        mod4 = _modulation(cond, ada_w, ada_b, layer).reshape(MOD_ROWS, N_MOD, 1, D_MODEL)
        for s in (prompt, latent):
            margs = s.mod_args()
            h = _norm_mod(s.x, norm1_g, mod4, layer, 0, 1, **margs)[0]
            tm = min(s.x.shape[0], 2048)
            if layer % 2 == 0:
                proj = _matmul([h], ab_w_in, i, tm=tm, tn=512, n_blocks=ab_w_in.shape[2] // 512)
                proj3 = proj.reshape(s.batch, s.n, -1)
                pooled = _pool_mixer(proj3, pool_w, pool_scale, i)
                if s is prompt:
                    att, k_new, v_new = _ctx_attention(proj3, na_q_norm, na_k_norm, i)
                    new_k.append(k_new)
                    new_v.append(v_new)
                else:
                    att = _neighbourhood_attention(proj3, cache_na_k, cache_na_v, na_q_norm, na_k_norm,
                                                   na_rpb, i)
                s.x = _matmul([pooled.reshape(-1, POOL_WIDTH), att.reshape(-1, NA_WIDTH)], ab_w_out, i,
                              tm=tm, tn=512, n_blocks=D_MODEL // 512, res=s.x, mod4=mod4, gate_idx=2,
                              **margs)
            else:
                proj = _matmul([h], ssd_w_in, i, tm=tm, tn=512, n_blocks=SSD_MAIN_DIM // 512)
                dt_raw = _matmul([h], ssd_w_in, i, tm=tm, tn=2 * SSD_HEADS, n_blocks=1,
                                 col_block0=SSD_MAIN_DIM // (2 * SSD_HEADS))
                proj3 = proj.reshape(s.batch, s.n, SSD_MAIN_DIM)
                dt3 = dt_raw.reshape(s.batch, s.n, 2 * SSD_HEADS)
                init = None if s is prompt else state_ssd
                y, s_fin = _ssd_scan(proj3, dt3, ssd_conv_w, ssd_conv_b, ssd_a_log, ssd_dt_bias, ssd_d, i,
                                     init=init)
                if s is prompt:
                    new_s.append(s_fin[:, None])
                yn = _gate_norm(y.reshape(-1, SSD_INNER), proj, ssd_norm_g, i)
                s.x = _matmul([yn], ssd_w_out, i, tm=min(tm, 1024), tn=512, n_blocks=D_MODEL // 512,
                              res=s.x, mod4=mod4, gate_idx=2, **margs)
            h2, logits = _norm_mod(s.x, norm2_g, mod4, layer, 3, 4, router_w=router_pad, **margs)
            s.x = _expert_choice(h2, logits, s.x, s.batch, exp_w_gate, exp_w_up, exp_w_down, layer,
                                 mod4, 5, s.first_row, s.shared_cond)

    y_prompt = prompt.x.reshape(x_prompt.shape)
    y_sample = latent.x.reshape(x_sample.shape)
    return (y_prompt, y_sample, jnp.concatenate(new_k, axis=1), jnp.concatenate(new_v, axis=1),
            jnp.concatenate(new_s, axis=1))
```

```python
import functools

import numpy as np
import jax
import jax.numpy as jnp
from jax import lax
from jax.experimental import pallas as pl
from jax.experimental.pallas import tpu as pltpu

F32 = jnp.float32
BF16 = jnp.bfloat16

D_MODEL = 2048
GRID_W = 64
POOL_WIDTH = 1024
POOL_GROUPS = 4
POOL_GROUP_DIM = 256
POOL_WINDOWS = (2, 4, 8, 16)
NA_WIDTH = 1024
NA_HEAD_DIM = 128
NA_HEADS = 8
WIN_H = 8
WIN_W = 16
SSD_INNER = 4096
SSD_HEAD_DIM = 64
SSD_HEADS = 64
SSD_GROUPS = 8
SSD_GROUP_HEADS = SSD_HEADS // SSD_GROUPS
SSD_GROUP_WIDTH = SSD_GROUP_HEADS * SSD_HEAD_DIM
SSD_STATE = 128
SSD_CONV = 4
SSD_CHUNK = 128
SSD_BC_WIDTH = SSD_GROUPS * SSD_STATE
SSD_MAIN_DIM = 2 * SSD_INNER + 2 * SSD_BC_WIDTH
N_EXPERTS = 16
EXPERT_FF = 1024
EC_CAPACITY_FACTOR = 2
NORM_EPS = 1e-6
N_MOD = 6
MOD_ROWS = 16
ROUTE_LANES = 128
ROUTE_TOKEN_LANE = 3 * N_EXPERTS
ROUTE_TOKEN_RADIX = 64
ROUTE_ROWS_PER_STEP = 8
PREFIX_BLOCK = 256

VMEM_LIMIT_BYTES = 56 * 1024 * 1024
ROW_TILE = 256


def _cparams(*sem):
    return pltpu.CompilerParams(dimension_semantics=sem, vmem_limit_bytes=VMEM_LIMIT_BYTES)


def _silu(x):
    return x * (1.0 / (1.0 + jnp.exp(-x)))


def _split2(x):
    hi = x.astype(BF16)
    lo = (x - hi.astype(F32)).astype(BF16)
    return hi, lo


def _split3(x):
    hi = x.astype(BF16)
    r = x - hi.astype(F32)
    mid = r.astype(BF16)
    lo = (r - mid.astype(F32)).astype(BF16)
    return hi, mid, lo


def _dot(a, b):
    return jnp.dot(a, b, preferred_element_type=F32)


def _dot_nt(a, b):
    return lax.dot_general(a, b, (((1,), (1,)), ((), ())), preferred_element_type=F32)


def _dot_f32_rhs(m, x):
    hi, mid, lo = _split3(x)
    return _dot(m, hi) + _dot(m, mid) + _dot(m, lo)


def _dot_split(a, b):
    a_hi, a_lo = _split2(a)
    b_hi, b_lo = _split2(b)
    return _dot(a_hi, b_hi) + _dot(a_lo, b_hi) + _dot(a_hi, b_lo)


def _mod_kernel(c_ref, w_ref, b_ref, o_ref):
    s = _silu(c_ref[...])
    o_ref[...] = _dot_split(s, w_ref[...]) + b_ref[...]


def _modulation(cond, ada_w, ada_b, layer):
    tn = 768
    n = N_MOD * D_MODEL
    ada_b3 = ada_b.reshape(ada_b.shape[0], 1, n)
    return pl.pallas_call(
        _mod_kernel,
        grid=(n // tn,),
        in_specs=[
            pl.BlockSpec((MOD_ROWS, D_MODEL), lambda j: (0, 0)),
            pl.BlockSpec((None, D_MODEL, tn), lambda j: (layer, 0, j)),
            pl.BlockSpec((None, 1, tn), lambda j: (layer, 0, j)),
        ],
        out_specs=pl.BlockSpec((MOD_ROWS, tn), lambda j: (0, j)),
        out_shape=jax.ShapeDtypeStruct((MOD_ROWS, n), F32),
        compiler_params=_cparams("parallel"),
        name="modulation",
    )(cond, ada_w, ada_b3)


def _mod_spec(which, rows_per_cond, first_row, tm):
    return pl.BlockSpec((None, None, 1, D_MODEL),
                        lambda i, *_: (first_row + (i * tm) // rows_per_cond, which, 0, 0))


def _norm_mod_kernel(x_ref, g_ref, sh_ref, sc_ref, *rest, with_router):
    x = x_ref[...]
    y = x * lax.rsqrt(jnp.mean(x * x, axis=-1, keepdims=True) + NORM_EPS) * g_ref[...]
    h = y * (1.0 + sc_ref[...]) + sh_ref[...]
    if with_router:
        rw_ref, h_ref, lg_ref = rest
        lg_ref[...] = _dot_split(h, rw_ref[...])
    else:
        (h_ref,) = rest
    h_ref[...] = h.astype(BF16)


def _norm_mod(x, g, mod4, layer, shift_idx, scale_idx, rows_per_cond, first_row, router_w=None):
    m = x.shape[0]
    tm = 2 * ROW_TILE
    g3 = g.reshape(g.shape[0], 1, D_MODEL)
    in_specs = [
        pl.BlockSpec((tm, D_MODEL), lambda i: (i, 0)),
        pl.BlockSpec((None, 1, D_MODEL), lambda i: (layer, 0, 0)),
        _mod_spec(shift_idx, rows_per_cond, first_row, tm),
        _mod_spec(scale_idx, rows_per_cond, first_row, tm),
    ]
    args = [x, g3, mod4, mod4]
    out_specs = [pl.BlockSpec((tm, D_MODEL), lambda i: (i, 0))]
    out_shape = [jax.ShapeDtypeStruct((m, D_MODEL), BF16)]
    if router_w is not None:
        in_specs.append(pl.BlockSpec((None, D_MODEL, ROUTE_LANES), lambda i: (layer, 0, 0)))
        args.append(router_w)
        out_specs.append(pl.BlockSpec((tm, ROUTE_LANES), lambda i: (i, 0)))
        out_shape.append(jax.ShapeDtypeStruct((m, ROUTE_LANES), F32))
    return pl.pallas_call(
        functools.partial(_norm_mod_kernel, with_router=router_w is not None),
        grid=(m // tm,),
        in_specs=in_specs,
        out_specs=out_specs,
        out_shape=out_shape,
        compiler_params=_cparams("parallel"),
        name="norm_mod",
    )(*args)


def _matmul_kernel(*refs, n_a, k_sizes, with_res):
    a_refs = refs[:n_a]
    w_ref = refs[n_a]
    if with_res:
        res_ref, gate_ref, o_ref = refs[n_a + 1:]
    else:
        (o_ref,) = refs[n_a + 1:]
    w = w_ref[...].astype(BF16)
    acc = None
    k0 = 0
    for a_ref, k in zip(a_refs, k_sizes):
        part = _dot(a_ref[...], w[k0:k0 + k])
        acc = part if acc is None else acc + part
        k0 += k
    if with_res:
        acc = res_ref[...] + gate_ref[...] * acc
    o_ref[...] = acc.astype(o_ref.dtype)


def _matmul(a_list, w, layer, *, tm, tn, n_blocks, col_block0=0, out_dtype=F32,
            res=None, mod4=None, gate_idx=None, rows_per_cond=None, first_row=None):
    m = a_list[0].shape[0]
    k_sizes = tuple(a.shape[1] for a in a_list)
    k_total = sum(k_sizes)
    in_specs = [pl.BlockSpec((tm, k), lambda i, j: (i, 0)) for k in k_sizes]
    in_specs.append(pl.BlockSpec((None, k_total, tn), lambda i, j: (layer, 0, col_block0 + j)))
    args = list(a_list) + [w]
    if res is not None:
        in_specs.append(pl.BlockSpec((tm, tn), lambda i, j: (i, j)))
        in_specs.append(pl.BlockSpec((None, None, 1, tn),
                                     lambda i, j: (first_row + (i * tm) // rows_per_cond, gate_idx, 0, j)))
        args += [res, mod4]
    return pl.pallas_call(
        functools.partial(_matmul_kernel, n_a=len(a_list), k_sizes=k_sizes, with_res=res is not None),
        grid=(m // tm, n_blocks),
        in_specs=in_specs,
        out_specs=pl.BlockSpec((tm, tn), lambda i, j: (i, j)),
        out_shape=jax.ShapeDtypeStruct((m, n_blocks * tn), out_dtype),
        compiler_params=_cparams("parallel", "arbitrary"),
        name="matmul",
    )(*args)


def _pool_kernel(u_ref, w_ref, s_ref, o_ref, *, n):
    g = pl.program_id(1)
    u = u_ref[...]
    t = lax.broadcasted_iota(jnp.int32, (n, 1), 0)
    for gi, win in enumerate(POOL_WINDOWS):
        @pl.when(g == gi)
        def _(win=win):
            half = win // 2
            total = u
            for d in range(-half, win - half):
                if d == 0:
                    continue
                shifted = pltpu.roll(u, (-d) % n, axis=0)
                valid = jnp.logical_and(t + d >= 0, t + d < n)
                total = total + jnp.where(valid, shifted, 0.0)
            cnt = (jnp.minimum(t - half + win, n) - jnp.maximum(t - half, 0)).astype(F32)
            pooled = total / cnt - u
            out = _dot(pooled.astype(BF16), w_ref[...].astype(BF16)) * s_ref[...]
            o_ref[...] = out.astype(o_ref.dtype)


def _pool_mixer(proj3, pool_w, pool_scale, i_even):
    b, n, _ = proj3.shape
    scale4 = pool_scale.reshape(pool_scale.shape[0], POOL_GROUPS, 1, POOL_GROUP_DIM)
    return pl.pallas_call(
        functools.partial(_pool_kernel, n=n),
        grid=(b, POOL_GROUPS),
        in_specs=[
            pl.BlockSpec((None, n, POOL_GROUP_DIM), lambda bi, g: (bi, 0, g)),
            pl.BlockSpec((None, None, POOL_GROUP_DIM, POOL_GROUP_DIM), lambda bi, g: (i_even, g, 0, 0)),
            pl.BlockSpec((None, None, 1, POOL_GROUP_DIM), lambda bi, g: (i_even, g, 0, 0)),
        ],
        out_specs=pl.BlockSpec((None, n, POOL_GROUP_DIM), lambda bi, g: (bi, 0, g)),
        out_shape=jax.ShapeDtypeStruct((b, n, POOL_WIDTH), BF16),
        compiler_params=_cparams("parallel", "parallel"),
        name="pool_mixer",
    )(proj3, pool_w, scale4)


def _head_norm(x, g):
    return x * lax.rsqrt(jnp.mean(x * x, axis=-1, keepdims=True) + NORM_EPS) * g


_Q_COL0 = POOL_WIDTH // NA_HEAD_DIM
_K_COL0 = (POOL_WIDTH + NA_WIDTH) // NA_HEAD_DIM
_V_COL0 = (POOL_WIDTH + 2 * NA_WIDTH) // NA_HEAD_DIM


def _ctx_attn_kernel(q_ref, k_ref, v_ref, qg_ref, kg_ref, o_ref, kn_ref, vn_ref):
    qn = _head_norm(q_ref[...], qg_ref[...])
    kn = _head_norm(k_ref[...], kg_ref[...])
    v = v_ref[...]
    kn_ref[...] = kn
    vn_ref[...] = v
    s = _dot_nt(qn.astype(BF16), kn.astype(BF16)) * (NA_HEAD_DIM ** -0.5)
    m = jnp.max(s, axis=-1, keepdims=True)
    p = jnp.exp(s - m)
    denom = jnp.sum(p, axis=-1, keepdims=True)
    o = _dot(p.astype(BF16), v.astype(BF16)) / denom
    o_ref[...] = o.astype(o_ref.dtype)


def _ctx_attention(proj3, q_norm, k_norm, i_even):
    b, n, _ = proj3.shape
    qg = q_norm.reshape(q_norm.shape[0], 1, NA_HEAD_DIM)
    kg = k_norm.reshape(k_norm.shape[0], 1, NA_HEAD_DIM)
    head_blk = lambda col0: pl.BlockSpec((None, n, NA_HEAD_DIM), lambda bi, h: (bi, 0, col0 + h))
    gain = pl.BlockSpec((None, 1, NA_HEAD_DIM), lambda bi, h: (i_even, 0, 0))
    cache = pl.BlockSpec((None, None, None, n, NA_HEAD_DIM), lambda bi, h: (bi, 0, h, 0, 0))
    return pl.pallas_call(
        _ctx_attn_kernel,
        grid=(b, NA_HEADS),
        in_specs=[head_blk(_Q_COL0), head_blk(_K_COL0), head_blk(_V_COL0), gain, gain],
        out_specs=[pl.BlockSpec((None, n, NA_HEAD_DIM), lambda bi, h: (bi, 0, h)), cache, cache],
        out_shape=[
            jax.ShapeDtypeStruct((b, n, NA_WIDTH), BF16),
            jax.ShapeDtypeStruct((b, 1, NA_HEADS, n, NA_HEAD_DIM), F32),
            jax.ShapeDtypeStruct((b, 1, NA_HEADS, n, NA_HEAD_DIM), F32),
        ],
        compiler_params=_cparams("parallel", "parallel"),
        name="ctx_attention",
    )(proj3, proj3, proj3, qg, kg)


NA_BAND = 4
NA_BAND_KEY_ROWS = WIN_H + NA_BAND - 1


def _na_band_key_start(r0, rows):
    return np.clip(r0 - WIN_H // 2, 0, rows - NA_BAND_KEY_ROWS)


def _na_bias_table(rpb, rows):
    col = np.arange(GRID_W)[:, None]
    kc = np.arange(GRID_W)[None, :]
    wstart = np.clip(col - WIN_W // 2, 0, GRID_W - WIN_W)
    inside = (kc >= wstart) & (kc < wstart + WIN_W)
    rel = np.clip(kc - col + WIN_W - 1, 0, 2 * WIN_W - 2)
    a = np.arange(NA_BAND)[:, None]
    kr = np.arange(NA_BAND_KEY_ROWS)[None, :]
    ridx, row_ok = [], []
    for r0 in (0, NA_BAND, rows - NA_BAND):
        r = r0 + a
        sr = np.clip(r - WIN_H // 2, 0, rows - WIN_H)
        krow = _na_band_key_start(r0, rows) + kr
        row_ok.append((krow >= sr) & (krow < sr + WIN_H))
        ridx.append(np.clip(krow - r + WIN_H - 1, 0, 2 * WIN_H - 2))
    ridx = np.stack(ridx)
    ok = np.stack(row_ok)[:, :, None, :, None] & inside[None, None, :, None, :]
    row_sel = np.eye(2 * WIN_H - 1, dtype=np.float32)[ridx]
    col_sel = np.eye(2 * WIN_W - 1, dtype=np.float32)[rel]
    tab = jnp.einsum('vaki,hij,cdj->hvackd', row_sel, rpb, col_sel, precision=lax.Precision.HIGHEST)
    tab = jnp.where(ok[None], tab, -jnp.inf)
    return tab.reshape(rpb.shape[0], 3, NA_BAND * GRID_W, NA_BAND_KEY_ROWS * GRID_W)


def _na_kernel(q_ref, k_ref, v_ref, kc_ref, vc_ref, qg_ref, kg_ref, bias_ref, o_ref,
               qs_ref, ks_ref, vs_ref, *, rows):
    qs_ref[...] = _head_norm(q_ref[...], qg_ref[...]).astype(BF16)
    ks_ref[...] = _head_norm(k_ref[...], kg_ref[...]).astype(BF16)
    vs_ref[...] = v_ref[...].astype(BF16)
    k_ctx = kc_ref[...].astype(BF16)
    v_ctx = vc_ref[...].astype(BF16)
    scale = NA_HEAD_DIM ** -0.5
    n_bands = rows // NA_BAND
    n_q = NA_BAND * GRID_W
    n_loc = NA_BAND_KEY_ROWS * GRID_W

    def band(i):
        r0 = i * NA_BAND
        key_row0 = jnp.clip(r0 - WIN_H // 2, 0, rows - NA_BAND_KEY_ROWS)
        variant = jnp.where(i == 0, 0, jnp.where(i == n_bands - 1, 2, 1))
        q0 = pl.multiple_of(r0 * GRID_W, n_q)
        q_b = qs_ref[pl.ds(q0, n_q), :]
        kstart = pl.multiple_of(key_row0 * GRID_W, GRID_W)
        k_blk = ks_ref[pl.ds(kstart, n_loc), :]
        v_blk = vs_ref[pl.ds(kstart, n_loc), :]
        s_loc = _dot_nt(q_b, k_blk) * scale + bias_ref[variant]
        s_ctx = _dot_nt(q_b, k_ctx) * scale
        m = jnp.maximum(jnp.max(s_loc, axis=-1, keepdims=True), jnp.max(s_ctx, axis=-1, keepdims=True))
        p_loc = jnp.exp(s_loc - m)
        p_ctx = jnp.exp(s_ctx - m)
        denom = jnp.sum(p_loc, axis=-1, keepdims=True) + jnp.sum(p_ctx, axis=-1, keepdims=True)
        o = (_dot(p_loc.astype(BF16), v_blk) + _dot(p_ctx.astype(BF16), v_ctx)) / denom
        o_ref[pl.ds(q0, n_q), :] = o.astype(o_ref.dtype)

    per_trip = 2 if n_bands % 2 == 0 else 1

    def trip(i, carry):
        for k in range(per_trip):
            band(i * per_trip + k)
        return carry

    lax.fori_loop(0, n_bands // per_trip, trip, 0)


def _neighbourhood_attention(proj3, cache_k, cache_v, q_norm, k_norm, rpb, i_even):
    b, t, _ = proj3.shape
    rows = t // GRID_W
    assert rows % NA_BAND == 0 and rows >= NA_BAND_KEY_ROWS + 1, rows
    n_ctx = cache_k.shape[3]
    bias = _na_bias_table(rpb[i_even], rows)
    qg = q_norm.reshape(q_norm.shape[0], 1, NA_HEAD_DIM)
    kg = k_norm.reshape(k_norm.shape[0], 1, NA_HEAD_DIM)
    head_blk = lambda col0: pl.BlockSpec((None, t, NA_HEAD_DIM), lambda bi, h: (bi, 0, col0 + h))
    gain = pl.BlockSpec((None, 1, NA_HEAD_DIM), lambda bi, h: (i_even, 0, 0))
    cache = pl.BlockSpec((None, None, None, n_ctx, NA_HEAD_DIM), lambda bi, h: (bi, i_even, h, 0, 0))
    return pl.pallas_call(
        functools.partial(_na_kernel, rows=rows),
        grid=(b, NA_HEADS),
        in_specs=[head_blk(_Q_COL0), head_blk(_K_COL0), head_blk(_V_COL0), cache, cache, gain, gain,
                  pl.BlockSpec((None,) + bias.shape[1:], lambda bi, h: (h, 0, 0, 0))],
        out_specs=pl.BlockSpec((None, t, NA_HEAD_DIM), lambda bi, h: (bi, 0, h)),
        out_shape=jax.ShapeDtypeStruct((b, t, NA_WIDTH), BF16),
        scratch_shapes=[pltpu.VMEM((t, NA_HEAD_DIM), BF16)] * 3,
        compiler_params=_cparams("parallel", "parallel"),
        name="neighbourhood_attention",
    )(proj3, proj3, proj3, cache_k, cache_v, qg, kg, bias)


def _softplus(x):
    return jnp.maximum(x, 0.0) + jnp.log1p(jnp.exp(-jnp.abs(x)))


def _ssd_kernel(*refs, l, with_init):
    (x_ref, b_ref, c_ref, dt_ref, cwx_ref, cwb_ref, cwc_ref, cbx_ref, cbb_ref, cbc_ref,
     alog_ref, dtbias_ref, dsk_ref) = refs[:13]
    if with_init:
        init_ref = refs[13]
        outs = refs[14:]
    else:
        outs = refs[13:]
    (y_ref, sfin_ref, xs_ref, bs_ref, cs_ref, stf_ref, stb_ref,
     cumf_ref, cumb_ref, srcf_ref, srcb_ref) = outs
    g = pl.program_id(1)
    q = SSD_CHUNK
    nc = l // q
    gw = SSD_GROUP_WIDTH
    gh = SSD_GROUP_HEADS
    pad_l = SSD_CONV // 2
    halo = 8

    win_rows = q + 2 * halo
    sel_col = lax.broadcasted_iota(jnp.int32, (q, SSD_CONV * win_rows), 1)
    sel_row = lax.broadcasted_iota(jnp.int32, (q, SSD_CONV * win_rows), 0)
    shift_mat = ((sel_col % win_rows) == sel_row + (halo - pad_l) + sel_col // win_rows).astype(BF16)

    tap_w = {id(w_ref): [jnp.broadcast_to(w_ref[j:j + 1, :], (win_rows, w_ref.shape[1])).astype(BF16)
                         for j in range(SSD_CONV)]
             for w_ref in (cwx_ref, cwb_ref, cwc_ref)}

    def conv_chunk(c, carry):
        t0 = pl.multiple_of(c * q, q)
        lo_start = pl.multiple_of(jnp.maximum(t0 - halo, 0), halo)
        hi_start = pl.multiple_of(jnp.minimum(t0 + q, l - halo), halo)
        for src, w_ref, bias_ref, dst in ((x_ref, cwx_ref, cbx_ref, xs_ref),
                                          (b_ref, cwb_ref, cbb_ref, bs_ref),
                                          (c_ref, cwc_ref, cbc_ref, cs_ref)):
            lo = jnp.where(c > 0, src[pl.ds(lo_start, halo), :], 0.0)
            hi = jnp.where(c < nc - 1, src[pl.ds(hi_start, halo), :], 0.0)
            win = jnp.concatenate([lo, src[pl.ds(t0, q), :], hi], axis=0).astype(BF16)
            taps = jnp.concatenate([win * wj for wj in tap_w[id(w_ref)]], axis=0)
            out = _silu(_dot(shift_mat, taps) + bias_ref[...])
            dst[pl.ds(t0, q), :] = out
            if dst is xs_ref:
                y_ref[pl.ds(t0, q), :] = out * dsk_ref[...]
        dt = _softplus(pltpu.roll(dt_ref[pl.ds(t0, q), :], to_lane0, axis=1) + bias_g)
        dta = dt * a_g
        cum_f = _dot_f32_rhs(tri_f, dta)
        cum_b = _dot_f32_rhs(tri_b, dta)
        cumf_ref[pl.ds(t0, q), :] = cum_f
        cumb_ref[pl.ds(t0, q), :] = cum_b
        log_dt = jnp.log(dt)
        srcf_ref[pl.ds(t0, q), :] = jnp.transpose(cum_f - log_dt)
        srcb_ref[pl.ds(t0, q), :] = jnp.transpose(cum_b - log_dt)
        return carry

    ii = lax.broadcasted_iota(jnp.int32, (q, q), 0)
    jj = lax.broadcasted_iota(jnp.int32, (q, q), 1)
    keep_f = jj <= ii
    keep_b = jj >= ii
    tri_f = keep_f.astype(BF16)
    tri_b = keep_b.astype(BF16)
    n_dt = 2 * SSD_HEADS
    assert q == n_dt
    to_lane0 = (n_dt - g * gh) % n_dt
    bias_g = pltpu.roll(jnp.broadcast_to(dtbias_ref[...], (8, n_dt)), to_lane0, axis=1)[0:1, :]
    a_g = pltpu.roll(jnp.broadcast_to(-jnp.exp(alog_ref[...]), (8, n_dt)), to_lane0, axis=1)[0:1, :]

    per_trip = 2 if nc % 2 == 0 else 1

    def conv_trip(i, carry):
        for k in range(per_trip):
            conv_chunk(i * per_trip + k, carry)
        return carry

    lax.fori_loop(0, nc // per_trip, conv_trip, 0)

    pair_w = 2 * SSD_HEAD_DIM
    lane = lax.broadcasted_iota(jnp.int32, (1, gw), 1)
    half_mask = [((lane % pair_w) // SSD_HEAD_DIM) == s for s in range(2)]
    first_head_lanes = lax.broadcasted_iota(jnp.int32, (1, pair_w), 1) < SSD_HEAD_DIM

    def scan_chunk(c, reverse):
        keep, edge, lane0 = (keep_b, 0, SSD_HEADS) if reverse else (keep_f, q - 1, 0)
        cum_ref, src_ref, st_ref = ((cumb_ref, srcb_ref, stb_ref) if reverse
                                    else (cumf_ref, srcf_ref, stf_ref))
        t0 = pl.multiple_of(c * q, q)
        xc = xs_ref[pl.ds(t0, q), :]
        bc = bs_ref[pl.ds(t0, q), :]
        cc = cs_ref[pl.ds(t0, q), :]
        cb = _dot_nt(cc.astype(BF16), bc.astype(BF16))
        bct = jnp.transpose(bc)
        x_bf = xc.astype(BF16)
        st = st_ref[...]
        st_bf = st.astype(BF16)
        zero = jnp.zeros((), BF16)
        x_half = [jnp.where(half_mask[s], x_bf, zero) for s in range(2)]
        st_half = [jnp.where(half_mask[s], st_bf, zero) for s in range(2)]
        pieces = []
        for pair in range(gh // 2):
            c0 = pair * pair_w
            lhs_y, lhs_s, carry_decay = [], [], []
            for e in (2 * pair, 2 * pair + 1):
                ln = lane0 + e
                col = jnp.broadcast_to(cum_ref[pl.ds(t0, q), ln:ln + 1], (q, q))
                src = src_ref[pl.ds(t0 + ln, 1), :]
                at_edge = cum_ref[pl.ds(t0 + edge, 1), ln:ln + 1]
                mix = jnp.exp(jnp.where(keep, col - src, -jnp.inf))
                lhs_y.append((cb * mix).astype(BF16))
                lhs_y.append((cc * jnp.exp(col)).astype(BF16))
                lhs_s.append((bct * jnp.exp(at_edge - src)).astype(BF16))
                carry_decay.append(jnp.exp(at_edge))
            rhs_y = jnp.concatenate([x_half[0][:, c0:c0 + pair_w], st_half[0][:, c0:c0 + pair_w],
                                     x_half[1][:, c0:c0 + pair_w], st_half[1][:, c0:c0 + pair_w]], axis=0)
            pieces.append(_dot(jnp.concatenate(lhs_y, axis=1), rhs_y))
            rhs_s = jnp.concatenate([x_half[0][:, c0:c0 + pair_w], x_half[1][:, c0:c0 + pair_w]], axis=0)
            keep_frac = jnp.where(first_head_lanes, carry_decay[0], carry_decay[1])
            st_ref[:, c0:c0 + pair_w] = (st[:, c0:c0 + pair_w] * keep_frac
                                         + _dot(jnp.concatenate(lhs_s, axis=1), rhs_s))
        y_ref[pl.ds(t0, q), :] = y_ref[pl.ds(t0, q), :] + jnp.concatenate(pieces, axis=1)

    for d, st_ref in enumerate((stf_ref, stb_ref)):
        if with_init:
            st_ref[...] = jnp.transpose(init_ref[d].reshape(gw, SSD_STATE))
        else:
            st_ref[...] = jnp.zeros_like(st_ref)

    def step(s, carry):
        scan_chunk(s, False)
        scan_chunk(nc - 1 - s, True)
        return carry

    lax.fori_loop(0, nc, step, 0)
    for d, st_ref in enumerate((stf_ref, stb_ref)):
        sfin_ref[d] = jnp.transpose(st_ref[...]).reshape(gh, SSD_HEAD_DIM, SSD_STATE)


def _ssd_scan(proj3, dt_raw3, conv_w, conv_b, a_log, dt_bias, d_skip, i_odd, init=None):
    b, l, _ = proj3.shape
    gh, gw = SSD_GROUP_HEADS, SSD_GROUP_WIDTH
    n_dt = 2 * SSD_HEADS
    a_log3 = a_log.reshape(a_log.shape[0], 1, n_dt)
    dt_bias3 = dt_bias.reshape(dt_bias.shape[0], 1, n_dt)
    dsk = jnp.repeat(d_skip[i_odd].astype(F32), SSD_HEAD_DIM).reshape(1, SSD_INNER)
    conv_b3 = conv_b.reshape(conv_b.shape[0], 1, conv_b.shape[1])

    x_col0 = SSD_INNER // gw
    b_col0 = (2 * SSD_INNER) // SSD_STATE
    c_col0 = (2 * SSD_INNER + SSD_BC_WIDTH) // SSD_STATE
    cwx_col0 = 0
    cwb_col0 = SSD_INNER // SSD_STATE
    cwc_col0 = (SSD_INNER + SSD_BC_WIDTH) // SSD_STATE

    in_specs = [
        pl.BlockSpec((None, l, gw), lambda bi, g: (bi, 0, x_col0 + g)),
        pl.BlockSpec((None, l, SSD_STATE), lambda bi, g: (bi, 0, b_col0 + g)),
        pl.BlockSpec((None, l, SSD_STATE), lambda bi, g: (bi, 0, c_col0 + g)),
        pl.BlockSpec((None, l, n_dt), lambda bi, g: (bi, 0, 0)),
        pl.BlockSpec((None, SSD_CONV, gw), lambda bi, g: (i_odd, 0, cwx_col0 + g)),
        pl.BlockSpec((None, SSD_CONV, SSD_STATE), lambda bi, g: (i_odd, 0, cwb_col0 + g)),
        pl.BlockSpec((None, SSD_CONV, SSD_STATE), lambda bi, g: (i_odd, 0, cwc_col0 + g)),
        pl.BlockSpec((None, 1, gw), lambda bi, g: (i_odd, 0, cwx_col0 + g)),
        pl.BlockSpec((None, 1, SSD_STATE), lambda bi, g: (i_odd, 0, cwb_col0 + g)),
        pl.BlockSpec((None, 1, SSD_STATE), lambda bi, g: (i_odd, 0, cwc_col0 + g)),
        pl.BlockSpec((None, 1, n_dt), lambda bi, g: (i_odd, 0, 0)),
        pl.BlockSpec((None, 1, n_dt), lambda bi, g: (i_odd, 0, 0)),
        pl.BlockSpec((1, gw), lambda bi, g: (0, g)),
    ]
    args = [proj3, proj3, proj3, dt_raw3, conv_w, conv_w, conv_w,
            conv_b3, conv_b3, conv_b3, a_log3, dt_bias3, dsk]
    if init is not None:
        in_specs.append(pl.BlockSpec((None, None, 2, gh, SSD_HEAD_DIM, SSD_STATE),
                                     lambda bi, g: (bi, i_odd, 0, g, 0, 0)))
        args.append(init)
    return pl.pallas_call(
        functools.partial(_ssd_kernel, l=l, with_init=init is not None),
        grid=(b, SSD_GROUPS),
        in_specs=in_specs,
        out_specs=[
            pl.BlockSpec((None, l, gw), lambda bi, g: (bi, 0, g)),
            pl.BlockSpec((None, 2, gh, SSD_HEAD_DIM, SSD_STATE), lambda bi, g: (bi, 0, g, 0, 0)),
        ],
        out_shape=[
            jax.ShapeDtypeStruct((b, l, SSD_INNER), F32),
            jax.ShapeDtypeStruct((b, 2, SSD_HEADS, SSD_HEAD_DIM, SSD_STATE), F32),
        ],
        scratch_shapes=[
            pltpu.VMEM((l, gw), F32),
            pltpu.VMEM((l, SSD_STATE), F32),
            pltpu.VMEM((l, SSD_STATE), F32),
            pltpu.VMEM((SSD_STATE, gw), F32),
            pltpu.VMEM((SSD_STATE, gw), F32),
        ] + [pltpu.VMEM((l, n_dt), F32)] * 4,
        compiler_params=_cparams("parallel", "parallel"),
        name="ssd_scan",
    )(*args)


def _gate_norm_kernel(y_ref, z_ref, g_ref, o_ref):
    y = y_ref[...] * _silu(z_ref[...])
    o = y * lax.rsqrt(jnp.mean(y * y, axis=-1, keepdims=True) + NORM_EPS) * g_ref[...]
    o_ref[...] = o.astype(o_ref.dtype)


def _gate_norm(y, proj, norm_g, i_odd):
    m = y.shape[0]
    tm = ROW_TILE
    g3 = norm_g.reshape(norm_g.shape[0], 1, SSD_INNER)
    return pl.pallas_call(
        _gate_norm_kernel,
        grid=(m // tm,),
        in_specs=[
            pl.BlockSpec((tm, SSD_INNER), lambda i: (i, 0)),
            pl.BlockSpec((tm, SSD_INNER), lambda i: (i, 0)),
            pl.BlockSpec((None, 1, SSD_INNER), lambda i: (i_odd, 0, 0)),
        ],
        out_specs=pl.BlockSpec((tm, SSD_INNER), lambda i: (i, 0)),
        out_shape=jax.ShapeDtypeStruct((m, SSD_INNER), BF16),
        compiler_params=_cparams("parallel"),
        name="gate_norm",
    )(y, proj, g3)


def _expert_kernel(x_ref, wg_ref, wu_ref, wd_ref, gate_ref, o_ref, acc_ref):
    f = pl.program_id(2)
    x = x_ref[...]
    hg = _dot(x, wg_ref[...].astype(BF16))
    hu = _dot(x, wu_ref[...].astype(BF16))
    hdn = (_silu(hg) * hu).astype(BF16)

    @pl.when(f == 0)
    def _():
        acc_ref[...] = jnp.zeros_like(acc_ref)

    acc_ref[...] += _dot(hdn, wd_ref[...].astype(BF16))

    @pl.when(f == pl.num_programs(2) - 1)
    def _():
        o_ref[...] = (acc_ref[...] * gate_ref[...]).astype(o_ref.dtype)


def _expert_ffn(xe, gate, w_gate, w_up, w_down, layer):
    e, r, _ = xe.shape
    tr = min(r, 1024)
    tf = 256
    return pl.pallas_call(
        _expert_kernel,
        grid=(e, r // tr, EXPERT_FF // tf),
        in_specs=[
            pl.BlockSpec((None, tr, D_MODEL), lambda ei, ri, f: (ei, ri, 0)),
            pl.BlockSpec((None, None, D_MODEL, tf), lambda ei, ri, f: (layer, ei, 0, f)),
            pl.BlockSpec((None, None, D_MODEL, tf), lambda ei, ri, f: (layer, ei, 0, f)),
            pl.BlockSpec((None, None, tf, D_MODEL), lambda ei, ri, f: (layer, ei, f, 0)),
            pl.BlockSpec((None, tr, 1), lambda ei, ri, f: (ei, ri, 0)),
        ],
        out_specs=pl.BlockSpec((None, tr, D_MODEL), lambda ei, ri, f: (ei, ri, 0)),
        out_shape=jax.ShapeDtypeStruct((e, r, D_MODEL), BF16),
        scratch_shapes=[pltpu.VMEM((tr, D_MODEL), F32)],
        compiler_params=_cparams("parallel", "parallel", "arbitrary"),
        name="expert_ffn",
    )(xe, w_gate, w_up, w_down, gate)


def _combine_kernel(idx_ref, ye_ref, x_ref, gate_ref, o_ref, *, n, ec, cap):
    step = pl.program_id(2)
    token = lax.broadcasted_iota(jnp.int32, (n, cap), 0)
    onehot = jnp.concatenate(
        [jnp.where(token == idx_ref[k], 1.0, 0.0).astype(BF16) for k in range(ec)], axis=1)

    @pl.when(step == 0)
    def _():
        o_ref[...] = jnp.zeros_like(o_ref)

    o_ref[...] += _dot(onehot, ye_ref[...].reshape(ec * cap, ye_ref.shape[-1]))

    @pl.when(step == pl.num_programs(2) - 1)
    def _():
        o_ref[...] = x_ref[...] + gate_ref[...] * o_ref[...]


def _expert_combine(ye, idx, x, mod4, gate_idx, first_row, shared_cond):
    n_req, n_exp, cap = idx.shape
    m = x.shape[0]
    n = m // n_req
    ec = max(1, min(n_exp, 512 // cap))
    tn = D_MODEL // 2
    idx4 = idx.reshape(n_req, n_exp, 1, cap)
    cond_row = (lambda b: first_row) if shared_cond else (lambda b: first_row + b)
    return pl.pallas_call(
        functools.partial(_combine_kernel, n=n, ec=ec, cap=cap),
        grid=(n_req, D_MODEL // tn, n_exp // ec),
        in_specs=[
            pl.BlockSpec((None, ec, 1, cap), lambda b, j, s: (b, s, 0, 0)),
            pl.BlockSpec((ec, cap, tn), lambda b, j, s: (s, b, j)),
            pl.BlockSpec((n, tn), lambda b, j, s: (b, j)),
            pl.BlockSpec((None, None, 1, tn), lambda b, j, s: (cond_row(b), gate_idx, 0, j)),
        ],
        out_specs=pl.BlockSpec((n, tn), lambda b, j, s: (b, j)),
        out_shape=jax.ShapeDtypeStruct((m, D_MODEL), F32),
        compiler_params=_cparams("parallel", "parallel", "arbitrary"),
        name="expert_combine",
    )(idx4, ye, x, mod4)


def _prefix_count(mask_bf, strict_upper):
    n = mask_bf.shape[1]
    carry = jnp.zeros((mask_bf.shape[0], 1), F32)
    out = []
    for k in range(n // PREFIX_BLOCK):
        blk = mask_bf[:, k * PREFIX_BLOCK:(k + 1) * PREFIX_BLOCK]
        out.append(_dot(blk, strict_upper) + carry)
        carry = carry + jnp.sum(blk.astype(F32), axis=-1, keepdims=True)
    return jnp.concatenate(out, axis=1)


def _router_kernel(lg_ref, o_ref, slot_ref, feat_ref, *, n_req, n, cap):
    step = pl.program_id(0)
    n_exp = N_EXPERTS

    @pl.when(step == 0)
    def _():
        tok = lax.broadcasted_iota(jnp.int32, (n, ROUTE_LANES), 0)
        tlane = lax.broadcasted_iota(jnp.int32, (n, ROUTE_LANES), 1)
        radix_bits = ROUTE_TOKEN_RADIX.bit_length() - 1
        tok_feat = (jnp.where(tlane == ROUTE_TOKEN_LANE, tok >> radix_bits, 0)
                    + jnp.where(tlane == ROUTE_TOKEN_LANE + 1, tok & (ROUTE_TOKEN_RADIX - 1), 0)).astype(F32)
        is_expert = tlane < n_exp
        aff_rows = []
        for r in range(n_req):
            x = jnp.where(is_expert, lg_ref[pl.ds(r * n, n), :], -jnp.inf)
            e = jnp.exp(x - jnp.max(x, axis=-1, keepdims=True))
            aff = e / jnp.sum(e, axis=-1, keepdims=True)
            feat = tok_feat
            for p, part in enumerate(_split3(aff)):
                piece = part.astype(F32)
                feat = feat + (pltpu.roll(piece, p * n_exp, axis=1) if p else piece)
            feat_ref[r] = feat.astype(BF16)
            aff_rows.append(jnp.transpose(aff)[:n_exp])
        aff_t = jnp.concatenate(aff_rows, axis=0)
        bits = pltpu.bitcast(aff_t, jnp.int32)

        def bisect(_, carry):
            lo, hi = carry
            mid = lax.shift_right_arithmetic(lo + hi, 1)
            count = jnp.sum(jnp.where(bits >= mid, 1.0, 0.0), axis=-1, keepdims=True)
            enough = count >= cap
            return jnp.where(enough, mid, lo), jnp.where(enough, hi, mid)

        rows = n_req * n_exp
        lo0 = jnp.zeros((rows, 1), jnp.int32)
        hi0 = jnp.full((rows, 1), 0x3F800001, jnp.int32)
        kth, _ = lax.fori_loop(0, 31, bisect, (lo0, hi0))
        above = bits > kth
        tied = bits == kth
        blk_i = lax.broadcasted_iota(jnp.int32, (PREFIX_BLOCK, PREFIX_BLOCK), 0)
        blk_j = lax.broadcasted_iota(jnp.int32, (PREFIX_BLOCK, PREFIX_BLOCK), 1)
        strict_upper = (blk_i < blk_j).astype(BF16)
        need = cap - jnp.sum(jnp.where(above, 1.0, 0.0), axis=-1, keepdims=True)
        tie_rank = _prefix_count(jnp.where(tied, 1.0, 0.0).astype(BF16), strict_upper)
        chosen = jnp.logical_or(above, jnp.logical_and(tied, tie_rank < need))
        slot = _prefix_count(jnp.where(chosen, 1.0, 0.0).astype(BF16), strict_upper)
        slot_ref[...] = jnp.where(chosen, slot, -1.0)

    slot_id = lax.broadcasted_iota(jnp.int32, (cap, n), 0).astype(F32)
    for k in range(ROUTE_ROWS_PER_STEP):
        row = step * ROUTE_ROWS_PER_STEP + k
        onehot_t = jnp.where(slot_id == slot_ref[pl.ds(row, 1), :], 1.0, 0.0).astype(BF16)
        o_ref[k] = _dot(onehot_t, feat_ref[row // n_exp])


def _route(logits, n_req):
    m = logits.shape[0]
    n = m // n_req
    cap = EC_CAPACITY_FACTOR * n // N_EXPERTS
    rows = n_req * N_EXPERTS
    assert n % PREFIX_BLOCK == 0 and rows % ROUTE_ROWS_PER_STEP == 0
    rec, slot = pl.pallas_call(
        functools.partial(_router_kernel, n_req=n_req, n=n, cap=cap),
        grid=(rows // ROUTE_ROWS_PER_STEP,),
        in_specs=[pl.BlockSpec((m, ROUTE_LANES), lambda i: (0, 0))],
        out_specs=[pl.BlockSpec((ROUTE_ROWS_PER_STEP, cap, ROUTE_LANES), lambda i: (i, 0, 0)),
                   pl.BlockSpec((rows, n), lambda i: (0, 0))],
        out_shape=[jax.ShapeDtypeStruct((rows, cap, ROUTE_LANES), F32),
                   jax.ShapeDtypeStruct((rows, n), F32)],
        scratch_shapes=[pltpu.VMEM((n_req, n, ROUTE_LANES), BF16)],
        compiler_params=_cparams("arbitrary"),
        name="router",
    )(logits)
    rec = rec.reshape(n_req, N_EXPERTS, cap, ROUTE_LANES)
    idx = (ROUTE_TOKEN_RADIX * rec[..., ROUTE_TOKEN_LANE] + rec[..., ROUTE_TOKEN_LANE + 1]).astype(jnp.int32)
    pieces = rec[..., :3 * N_EXPERTS].reshape(n_req, N_EXPERTS, cap, 3, N_EXPERTS).sum(axis=3)
    own = jnp.eye(N_EXPERTS, dtype=F32)[None, :, None, :]
    gate = jnp.sum(pieces * own, axis=3)
    return idx, gate, slot


def _gather_kernel(slot_ref, h_ref, o_ref, *, n, ec, cap):
    slot_id = lax.broadcasted_iota(jnp.int32, (cap, n), 0).astype(F32)
    onehot = jnp.concatenate(
        [jnp.where(slot_id == slot_ref[k], 1.0, 0.0).astype(BF16) for k in range(ec)], axis=0)
    o_ref[...] = _dot(onehot, h_ref[...]).reshape(o_ref.shape).astype(o_ref.dtype)


def _gather_rows(h, slot, n_req):
    m, d = h.shape
    n = m // n_req
    cap = EC_CAPACITY_FACTOR * n // N_EXPERTS
    ec = max(1, min(N_EXPERTS, 2048 // cap))
    tn = d // 2
    slot4 = slot.reshape(n_req, N_EXPERTS, 1, n)
    return pl.pallas_call(
        functools.partial(_gather_kernel, n=n, ec=ec, cap=cap),
        grid=(n_req, d // tn, N_EXPERTS // ec),
        in_specs=[pl.BlockSpec((None, ec, 1, n), lambda b, j, s: (b, s, 0, 0)),
                  pl.BlockSpec((n, tn), lambda b, j, s: (b, j))],
        out_specs=pl.BlockSpec((ec, cap, tn), lambda b, j, s: (s, b, j)),
        out_shape=jax.ShapeDtypeStruct((N_EXPERTS, n_req * cap, d), h.dtype),
        compiler_params=_cparams("parallel", "parallel", "arbitrary"),
        name="gather_rows",
    )(slot4, h)


def _expert_choice(h, logits, x, n_req, w_gate, w_up, w_down, layer, mod4, gate_idx, first_row, shared_cond):
    idx, gate, slot = _route(logits, n_req)
    cap = idx.shape[2]
    gate = jnp.transpose(gate, (1, 0, 2)).reshape(N_EXPERTS, n_req * cap, 1)
    xe = _gather_rows(h, slot, n_req)
    ye = _expert_ffn(xe, gate, w_gate, w_up, w_down, layer)
    return _expert_combine(ye, idx, x, mod4, gate_idx, first_row, shared_cond)


class _Stream:
    def __init__(self, x3, first_row, shared_cond):
        self.batch, self.n, _ = x3.shape
        self.x = x3.reshape(self.batch * self.n, D_MODEL)
        self.first_row = first_row
        self.shared_cond = shared_cond
        self.rows_per_cond = self.batch * self.n if shared_cond else self.n

    def mod_args(self):
        return dict(rows_per_cond=self.rows_per_cond, first_row=self.first_row)


def kernel(x_prompt, x_sample, cache_na_k, cache_na_v, state_ssd, c, c_ctx, ada_w, ada_b, norm1_g, norm2_g,
           router_w, exp_w_gate, exp_w_up, exp_w_down, ab_w_in, pool_w, pool_scale, na_q_norm, na_k_norm,
           na_rpb, ab_w_out, ssd_w_in, ssd_conv_w, ssd_conv_b, ssd_a_log, ssd_dt_bias, ssd_d, ssd_norm_g,
           ssd_w_out):
    depth = ada_w.shape[0]
    dec_batch = x_sample.shape[0]
    latent = _Stream(x_sample, 0, shared_cond=False)
    prompt = _Stream(x_prompt, dec_batch, shared_cond=True)
    cond = jnp.concatenate(
        [c, c_ctx[None, :], jnp.zeros((MOD_ROWS - dec_batch - 1, D_MODEL), F32)], axis=0)

    router_pad = jnp.pad(router_w, ((0, 0), (0, 0), (0, ROUTE_LANES - N_EXPERTS)))

    new_k, new_v, new_s = [], [], []
    for layer in range(depth):
        i = layer // 2
        mod4 = _modulation(cond, ada_w, ada_b, layer).reshape(MOD_ROWS, N_MOD, 1, D_MODEL)
        for s in (prompt, latent):
            margs = s.mod_args()
            h = _norm_mod(s.x, norm1_g, mod4, layer, 0, 1, **margs)[0]
            tm = min(s.x.shape[0], 2048)
            if layer % 2 == 0:
                proj = _matmul([h], ab_w_in, i, tm=tm, tn=512, n_blocks=ab_w_in.shape[2] // 512)
                proj3 = proj.reshape(s.batch, s.n, -1)
                pooled = _pool_mixer(proj3, pool_w, pool_scale, i)
                if s is prompt:
                    att, k_new, v_new = _ctx_attention(proj3, na_q_norm, na_k_norm, i)
                    new_k.append(k_new)
                    new_v.append(v_new)
                else:
                    att = _neighbourhood_attention(proj3, cache_na_k, cache_na_v, na_q_norm, na_k_norm,
                                                   na_rpb, i)
                s.x = _matmul([pooled.reshape(-1, POOL_WIDTH), att.reshape(-1, NA_WIDTH)], ab_w_out, i,
                              tm=tm, tn=512, n_blocks=D_MODEL // 512, res=s.x, mod4=mod4, gate_idx=2,
                              **margs)
            else:
                proj = _matmul([h], ssd_w_in, i, tm=tm, tn=512, n_blocks=SSD_MAIN_DIM // 512)
                dt_raw = _matmul([h], ssd_w_in, i, tm=tm, tn=2 * SSD_HEADS, n_blocks=1,
                                 col_block0=SSD_MAIN_DIM // (2 * SSD_HEADS))
                proj3 = proj.reshape(s.batch, s.n, SSD_MAIN_DIM)
                dt3 = dt_raw.reshape(s.batch, s.n, 2 * SSD_HEADS)
                init = None if s is prompt else state_ssd
                y, s_fin = _ssd_scan(proj3, dt3, ssd_conv_w, ssd_conv_b, ssd_a_log, ssd_dt_bias, ssd_d, i,
                                     init=init)
                if s is prompt:
                    new_s.append(s_fin[:, None])
                yn = _gate_norm(y.reshape(-1, SSD_INNER), proj, ssd_norm_g, i)
                s.x = _matmul([yn], ssd_w_out, i, tm=min(tm, 1024), tn=512, n_blocks=D_MODEL // 512,
                              res=s.x, mod4=mod4, gate_idx=2, **margs)
            h2, logits = _norm_mod(s.x, norm2_g, mod4, layer, 3, 4, router_w=router_pad, **margs)
            s.x = _expert_choice(h2, logits, s.x, s.batch, exp_w_gate, exp_w_up, exp_w_down, layer,
                                 mod4, 5, s.first_row, s.shared_cond)

    y_prompt = prompt.x.reshape(x_prompt.shape)
    y_sample = latent.x.reshape(x_sample.shape)
    return (y_prompt, y_sample, jnp.concatenate(new_k, axis=1), jnp.concatenate(new_v, axis=1),
            jnp.concatenate(new_s, axis=1))
```

```python
import functools

import numpy as np
import jax
import jax.numpy as jnp
from jax import lax
from jax.experimental import pallas as pl
from jax.experimental.pallas import tpu as pltpu

F32 = jnp.float32
BF16 = jnp.bfloat16

D_MODEL = 2048
GRID_W = 64
POOL_WIDTH = 1024
POOL_GROUPS = 4
POOL_GROUP_DIM = 256
POOL_WINDOWS = (2, 4, 8, 16)
NA_WIDTH = 1024
NA_HEAD_DIM = 128
NA_HEADS = 8
WIN_H = 8
WIN_W = 16
SSD_INNER = 4096
SSD_HEAD_DIM = 64
SSD_HEADS = 64
SSD_GROUPS = 8
SSD_GROUP_HEADS = SSD_HEADS // SSD_GROUPS
SSD_GROUP_WIDTH = SSD_GROUP_HEADS * SSD_HEAD_DIM
SSD_STATE = 128
SSD_CONV = 4
SSD_CHUNK = 128
SSD_BC_WIDTH = SSD_GROUPS * SSD_STATE
SSD_MAIN_DIM = 2 * SSD_INNER + 2 * SSD_BC_WIDTH
N_EXPERTS = 16
EXPERT_FF = 1024
EC_CAPACITY_FACTOR = 2
NORM_EPS = 1e-6
N_MOD = 6
MOD_ROWS = 16
ROUTE_LANES = 128
ROUTE_TOKEN_LANE = 3 * N_EXPERTS
ROUTE_TOKEN_RADIX = 64
ROUTE_ROWS_PER_STEP = 8
PREFIX_BLOCK = 256

VMEM_LIMIT_BYTES = 56 * 1024 * 1024
ROW_TILE = 256


def _cparams(*sem):
    return pltpu.CompilerParams(dimension_semantics=sem, vmem_limit_bytes=VMEM_LIMIT_BYTES)


def _silu(x):
    return x * (1.0 / (1.0 + jnp.exp(-x)))


def _split2(x):
    hi = x.astype(BF16)
    lo = (x - hi.astype(F32)).astype(BF16)
    return hi, lo


def _split3(x):
    hi = x.astype(BF16)
    r = x - hi.astype(F32)
    mid = r.astype(BF16)
    lo = (r - mid.astype(F32)).astype(BF16)
    return hi, mid, lo


def _dot(a, b):
    return jnp.dot(a, b, preferred_element_type=F32)


def _dot_nt(a, b):
    return lax.dot_general(a, b, (((1,), (1,)), ((), ())), preferred_element_type=F32)


def _dot_f32_rhs(m, x):
    hi, mid, lo = _split3(x)
    return _dot(m, hi) + _dot(m, mid) + _dot(m, lo)


def _dot_split(a, b):
    a_hi, a_lo = _split2(a)
    b_hi, b_lo = _split2(b)
    return _dot(a_hi, b_hi) + _dot(a_lo, b_hi) + _dot(a_hi, b_lo)


def _mod_kernel(c_ref, w_ref, b_ref, o_ref):
    s = _silu(c_ref[...])
    o_ref[...] = _dot_split(s, w_ref[...]) + b_ref[...]


def _modulation(cond, ada_w, ada_b, layer):
    tn = 768
    n = N_MOD * D_MODEL
    ada_b3 = ada_b.reshape(ada_b.shape[0], 1, n)
    return pl.pallas_call(
        _mod_kernel,
        grid=(n // tn,),
        in_specs=[
            pl.BlockSpec((MOD_ROWS, D_MODEL), lambda j: (0, 0)),
            pl.BlockSpec((None, D_MODEL, tn), lambda j: (layer, 0, j)),
            pl.BlockSpec((None, 1, tn), lambda j: (layer, 0, j)),
        ],
        out_specs=pl.BlockSpec((MOD_ROWS, tn), lambda j: (0, j)),
        out_shape=jax.ShapeDtypeStruct((MOD_ROWS, n), F32),
        compiler_params=_cparams("parallel"),
        name="modulation",
    )(cond, ada_w, ada_b3)


def _mod_spec(which, rows_per_cond, first_row, tm):
    return pl.BlockSpec((None, None, 1, D_MODEL),
                        lambda i, *_: (first_row + (i * tm) // rows_per_cond, which, 0, 0))


def _norm_mod_kernel(x_ref, g_ref, sh_ref, sc_ref, *rest, with_router):
    x = x_ref[...]
    y = x * lax.rsqrt(jnp.mean(x * x, axis=-1, keepdims=True) + NORM_EPS) * g_ref[...]
    h = y * (1.0 + sc_ref[...]) + sh_ref[...]
    if with_router:
        rw_ref, h_ref, lg_ref = rest
        lg_ref[...] = _dot_split(h, rw_ref[...])
    else:
        (h_ref,) = rest
    h_ref[...] = h.astype(BF16)


def _norm_mod(x, g, mod4, layer, shift_idx, scale_idx, rows_per_cond, first_row, router_w=None):
    m = x.shape[0]
    tm = 2 * ROW_TILE
    g3 = g.reshape(g.shape[0], 1, D_MODEL)
    in_specs = [
        pl.BlockSpec((tm, D_MODEL), lambda i: (i, 0)),
        pl.BlockSpec((None, 1, D_MODEL), lambda i: (layer, 0, 0)),
        _mod_spec(shift_idx, rows_per_cond, first_row, tm),
        _mod_spec(scale_idx, rows_per_cond, first_row, tm),
    ]
    args = [x, g3, mod4, mod4]
    out_specs = [pl.BlockSpec((tm, D_MODEL), lambda i: (i, 0))]
    out_shape = [jax.ShapeDtypeStruct((m, D_MODEL), BF16)]
    if router_w is not None:
        in_specs.append(pl.BlockSpec((None, D_MODEL, ROUTE_LANES), lambda i: (layer, 0, 0)))
        args.append(router_w)
        out_specs.append(pl.BlockSpec((tm, ROUTE_LANES), lambda i: (i, 0)))
        out_shape.append(jax.ShapeDtypeStruct((m, ROUTE_LANES), F32))
    return pl.pallas_call(
        functools.partial(_norm_mod_kernel, with_router=router_w is not None),
        grid=(m // tm,),
        in_specs=in_specs,
        out_specs=out_specs,
        out_shape=out_shape,
        compiler_params=_cparams("parallel"),
        name="norm_mod",
    )(*args)


def _matmul_kernel(*refs, n_a, k_sizes, with_res):
    a_refs = refs[:n_a]
    w_ref = refs[n_a]
    if with_res:
        res_ref, gate_ref, o_ref = refs[n_a + 1:]
    else:
        (o_ref,) = refs[n_a + 1:]
    w = w_ref[...].astype(BF16)
    acc = None
    k0 = 0
    for a_ref, k in zip(a_refs, k_sizes):
        part = _dot(a_ref[...], w[k0:k0 + k])
        acc = part if acc is None else acc + part
        k0 += k
    if with_res:
        acc = res_ref[...] + gate_ref[...] * acc
    o_ref[...] = acc.astype(o_ref.dtype)


def _matmul(a_list, w, layer, *, tm, tn, n_blocks, col_block0=0, out_dtype=F32,
            res=None, mod4=None, gate_idx=None, rows_per_cond=None, first_row=None):
    m = a_list[0].shape[0]
    k_sizes = tuple(a.shape[1] for a in a_list)
    k_total = sum(k_sizes)
    in_specs = [pl.BlockSpec((tm, k), lambda i, j: (i, 0)) for k in k_sizes]
    in_specs.append(pl.BlockSpec((None, k_total, tn), lambda i, j: (layer, 0, col_block0 + j)))
    args = list(a_list) + [w]
    if res is not None:
        in_specs.append(pl.BlockSpec((tm, tn), lambda i, j: (i, j)))
        in_specs.append(pl.BlockSpec((None, None, 1, tn),
                                     lambda i, j: (first_row + (i * tm) // rows_per_cond, gate_idx, 0, j)))
        args += [res, mod4]
    return pl.pallas_call(
        functools.partial(_matmul_kernel, n_a=len(a_list), k_sizes=k_sizes, with_res=res is not None),
        grid=(m // tm, n_blocks),
        in_specs=in_specs,
        out_specs=pl.BlockSpec((tm, tn), lambda i, j: (i, j)),
        out_shape=jax.ShapeDtypeStruct((m, n_blocks * tn), out_dtype),
        compiler_params=_cparams("parallel", "arbitrary"),
        name="matmul",
    )(*args)


def _pool_kernel(u_ref, w_ref, s_ref, o_ref, *, n):
    g = pl.program_id(1)
    u = u_ref[...]
    t = lax.broadcasted_iota(jnp.int32, (n, 1), 0)
    for gi, win in enumerate(POOL_WINDOWS):
        @pl.when(g == gi)
        def _(win=win):
            half = win // 2
            total = u
            for d in range(-half, win - half):
                if d == 0:
                    continue
                shifted = pltpu.roll(u, (-d) % n, axis=0)
                valid = jnp.logical_and(t + d >= 0, t + d < n)
                total = total + jnp.where(valid, shifted, 0.0)
            cnt = (jnp.minimum(t - half + win, n) - jnp.maximum(t - half, 0)).astype(F32)
            pooled = total / cnt - u
            out = _dot(pooled.astype(BF16), w_ref[...].astype(BF16)) * s_ref[...]
            o_ref[...] = out.astype(o_ref.dtype)


def _pool_mixer(proj3, pool_w, pool_scale, i_even):
    b, n, _ = proj3.shape
    scale4 = pool_scale.reshape(pool_scale.shape[0], POOL_GROUPS, 1, POOL_GROUP_DIM)
    return pl.pallas_call(
        functools.partial(_pool_kernel, n=n),
        grid=(b, POOL_GROUPS),
        in_specs=[
            pl.BlockSpec((None, n, POOL_GROUP_DIM), lambda bi, g: (bi, 0, g)),
            pl.BlockSpec((None, None, POOL_GROUP_DIM, POOL_GROUP_DIM), lambda bi, g: (i_even, g, 0, 0)),
            pl.BlockSpec((None, None, 1, POOL_GROUP_DIM), lambda bi, g: (i_even, g, 0, 0)),
        ],
        out_specs=pl.BlockSpec((None, n, POOL_GROUP_DIM), lambda bi, g: (bi, 0, g)),
        out_shape=jax.ShapeDtypeStruct((b, n, POOL_WIDTH), BF16),
        compiler_params=_cparams("parallel", "parallel"),
        name="pool_mixer",
    )(proj3, pool_w, scale4)


def _head_norm(x, g):
    return x * lax.rsqrt(jnp.mean(x * x, axis=-1, keepdims=True) + NORM_EPS) * g


_Q_COL0 = POOL_WIDTH // NA_HEAD_DIM
_K_COL0 = (POOL_WIDTH + NA_WIDTH) // NA_HEAD_DIM
_V_COL0 = (POOL_WIDTH + 2 * NA_WIDTH) // NA_HEAD_DIM


def _ctx_attn_kernel(q_ref, k_ref, v_ref, qg_ref, kg_ref, o_ref, kn_ref, vn_ref):
    qn = _head_norm(q_ref[...], qg_ref[...])
    kn = _head_norm(k_ref[...], kg_ref[...])
    v = v_ref[...]
    kn_ref[...] = kn
    vn_ref[...] = v
    s = _dot_nt(qn.astype(BF16), kn.astype(BF16)) * (NA_HEAD_DIM ** -0.5)
    m = jnp.max(s, axis=-1, keepdims=True)
    p = jnp.exp(s - m)
    denom = jnp.sum(p, axis=-1, keepdims=True)
    o = _dot(p.astype(BF16), v.astype(BF16)) / denom
    o_ref[...] = o.astype(o_ref.dtype)


def _ctx_attention(proj3, q_norm, k_norm, i_even):
    b, n, _ = proj3.shape
    qg = q_norm.reshape(q_norm.shape[0], 1, NA_HEAD_DIM)
    kg = k_norm.reshape(k_norm.shape[0], 1, NA_HEAD_DIM)
    head_blk = lambda col0: pl.BlockSpec((None, n, NA_HEAD_DIM), lambda bi, h: (bi, 0, col0 + h))
    gain = pl.BlockSpec((None, 1, NA_HEAD_DIM), lambda bi, h: (i_even, 0, 0))
    cache = pl.BlockSpec((None, None, None, n, NA_HEAD_DIM), lambda bi, h: (bi, 0, h, 0, 0))
    return pl.pallas_call(
        _ctx_attn_kernel,
        grid=(b, NA_HEADS),
        in_specs=[head_blk(_Q_COL0), head_blk(_K_COL0), head_blk(_V_COL0), gain, gain],
        out_specs=[pl.BlockSpec((None, n, NA_HEAD_DIM), lambda bi, h: (bi, 0, h)), cache, cache],
        out_shape=[
            jax.ShapeDtypeStruct((b, n, NA_WIDTH), BF16),
            jax.ShapeDtypeStruct((b, 1, NA_HEADS, n, NA_HEAD_DIM), F32),
            jax.ShapeDtypeStruct((b, 1, NA_HEADS, n, NA_HEAD_DIM), F32),
        ],
        compiler_params=_cparams("parallel", "parallel"),
        name="ctx_attention",
    )(proj3, proj3, proj3, qg, kg)


NA_BAND = 4
NA_BAND_KEY_ROWS = WIN_H + NA_BAND - 1


def _na_band_key_start(r0, rows):
    return np.clip(r0 - WIN_H // 2, 0, rows - NA_BAND_KEY_ROWS)


def _na_bias_table(rpb, rows):
    col = np.arange(GRID_W)[:, None]
    kc = np.arange(GRID_W)[None, :]
    wstart = np.clip(col - WIN_W // 2, 0, GRID_W - WIN_W)
    inside = (kc >= wstart) & (kc < wstart + WIN_W)
    rel = np.clip(kc - col + WIN_W - 1, 0, 2 * WIN_W - 2)
    a = np.arange(NA_BAND)[:, None]
    kr = np.arange(NA_BAND_KEY_ROWS)[None, :]
    ridx, row_ok = [], []
    for r0 in (0, NA_BAND, rows - NA_BAND):
        r = r0 + a
        sr = np.clip(r - WIN_H // 2, 0, rows - WIN_H)
        krow = _na_band_key_start(r0, rows) + kr
        row_ok.append((krow >= sr) & (krow < sr + WIN_H))
        ridx.append(np.clip(krow - r + WIN_H - 1, 0, 2 * WIN_H - 2))
    ridx = np.stack(ridx)
    ok = np.stack(row_ok)[:, :, None, :, None] & inside[None, None, :, None, :]
    row_sel = np.eye(2 * WIN_H - 1, dtype=np.float32)[ridx]
    col_sel = np.eye(2 * WIN_W - 1, dtype=np.float32)[rel]
    tab = jnp.einsum('vaki,hij,cdj->hvackd', row_sel, rpb, col_sel, precision=lax.Precision.HIGHEST)
    tab = jnp.where(ok[None], tab, -jnp.inf)
    return tab.reshape(rpb.shape[0], 3, NA_BAND * GRID_W, NA_BAND_KEY_ROWS * GRID_W)


def _na_kernel(q_ref, k_ref, v_ref, kc_ref, vc_ref, qg_ref, kg_ref, bias_ref, o_ref,
               qs_ref, ks_ref, vs_ref, *, rows):
    qs_ref[...] = _head_norm(q_ref[...], qg_ref[...]).astype(BF16)
    ks_ref[...] = _head_norm(k_ref[...], kg_ref[...]).astype(BF16)
    vs_ref[...] = v_ref[...].astype(BF16)
    k_ctx = kc_ref[...].astype(BF16)
    v_ctx = vc_ref[...].astype(BF16)
    scale = NA_HEAD_DIM ** -0.5
    n_bands = rows // NA_BAND
    n_q = NA_BAND * GRID_W
    n_loc = NA_BAND_KEY_ROWS * GRID_W

    def band(i):
        r0 = i * NA_BAND
        key_row0 = jnp.clip(r0 - WIN_H // 2, 0, rows - NA_BAND_KEY_ROWS)
        variant = jnp.where(i == 0, 0, jnp.where(i == n_bands - 1, 2, 1))
        q0 = pl.multiple_of(r0 * GRID_W, n_q)
        q_b = qs_ref[pl.ds(q0, n_q), :]
        kstart = pl.multiple_of(key_row0 * GRID_W, GRID_W)
        k_blk = ks_ref[pl.ds(kstart, n_loc), :]
        v_blk = vs_ref[pl.ds(kstart, n_loc), :]
        s_loc = _dot_nt(q_b, k_blk) * scale + bias_ref[variant]
        s_ctx = _dot_nt(q_b, k_ctx) * scale
        m = jnp.maximum(jnp.max(s_loc, axis=-1, keepdims=True), jnp.max(s_ctx, axis=-1, keepdims=True))
        p_loc = jnp.exp(s_loc - m)
        p_ctx = jnp.exp(s_ctx - m)
        denom = jnp.sum(p_loc, axis=-1, keepdims=True) + jnp.sum(p_ctx, axis=-1, keepdims=True)
        o = (_dot(p_loc.astype(BF16), v_blk) + _dot(p_ctx.astype(BF16), v_ctx)) / denom
        o_ref[pl.ds(q0, n_q), :] = o.astype(o_ref.dtype)

    per_trip = 2 if n_bands % 2 == 0 else 1

    def trip(i, carry):
        for k in range(per_trip):
            band(i * per_trip + k)
        return carry

    lax.fori_loop(0, n_bands // per_trip, trip, 0)


def _neighbourhood_attention(proj3, cache_k, cache_v, q_norm, k_norm, rpb, i_even):
    b, t, _ = proj3.shape
    rows = t // GRID_W
    assert rows % NA_BAND == 0 and rows >= NA_BAND_KEY_ROWS + 1, rows
    n_ctx = cache_k.shape[3]
    bias = _na_bias_table(rpb[i_even], rows)
    qg = q_norm.reshape(q_norm.shape[0], 1, NA_HEAD_DIM)
    kg = k_norm.reshape(k_norm.shape[0], 1, NA_HEAD_DIM)
    head_blk = lambda col0: pl.BlockSpec((None, t, NA_HEAD_DIM), lambda bi, h: (bi, 0, col0 + h))
    gain = pl.BlockSpec((None, 1, NA_HEAD_DIM), lambda bi, h: (i_even, 0, 0))
    cache = pl.BlockSpec((None, None, None, n_ctx, NA_HEAD_DIM), lambda bi, h: (bi, i_even, h, 0, 0))
    return pl.pallas_call(
        functools.partial(_na_kernel, rows=rows),
        grid=(b, NA_HEADS),
        in_specs=[head_blk(_Q_COL0), head_blk(_K_COL0), head_blk(_V_COL0), cache, cache, gain, gain,
                  pl.BlockSpec((None,) + bias.shape[1:], lambda bi, h: (h, 0, 0, 0))],
        out_specs=pl.BlockSpec((None, t, NA_HEAD_DIM), lambda bi, h: (bi, 0, h)),
        out_shape=jax.ShapeDtypeStruct((b, t, NA_WIDTH), BF16),
        scratch_shapes=[pltpu.VMEM((t, NA_HEAD_DIM), BF16)] * 3,
        compiler_params=_cparams("parallel", "parallel"),
        name="neighbourhood_attention",
    )(proj3, proj3, proj3, cache_k, cache_v, qg, kg, bias)


def _softplus(x):
    return jnp.maximum(x, 0.0) + jnp.log1p(jnp.exp(-jnp.abs(x)))


def _ssd_kernel(*refs, l, with_init):
    (x_ref, b_ref, c_ref, dt_ref, cwx_ref, cwb_ref, cwc_ref, cbx_ref, cbb_ref, cbc_ref,
     alog_ref, dtbias_ref, dsk_ref) = refs[:13]
    if with_init:
        init_ref = refs[13]
        outs = refs[14:]
    else:
        outs = refs[13:]
    (y_ref, sfin_ref, xs_ref, bs_ref, cs_ref, stf_ref, stb_ref,
     cumf_ref, cumb_ref, srcf_ref, srcb_ref) = outs
    g = pl.program_id(1)
    q = SSD_CHUNK
    nc = l // q
    gw = SSD_GROUP_WIDTH
    gh = SSD_GROUP_HEADS
    pad_l = SSD_CONV // 2
    halo = 8

    win_rows = q + 2 * halo
    sel_col = lax.broadcasted_iota(jnp.int32, (q, SSD_CONV * win_rows), 1)
    sel_row = lax.broadcasted_iota(jnp.int32, (q, SSD_CONV * win_rows), 0)
    shift_mat = ((sel_col % win_rows) == sel_row + (halo - pad_l) + sel_col // win_rows).astype(BF16)

    tap_w = {id(w_ref): [jnp.broadcast_to(w_ref[j:j + 1, :], (win_rows, w_ref.shape[1])).astype(BF16)
                         for j in range(SSD_CONV)]
             for w_ref in (cwx_ref, cwb_ref, cwc_ref)}

    def conv_chunk(c, carry):
        t0 = pl.multiple_of(c * q, q)
        lo_start = pl.multiple_of(jnp.maximum(t0 - halo, 0), halo)
        hi_start = pl.multiple_of(jnp.minimum(t0 + q, l - halo), halo)
        for src, w_ref, bias_ref, dst in ((x_ref, cwx_ref, cbx_ref, xs_ref),
                                          (b_ref, cwb_ref, cbb_ref, bs_ref),
                                          (c_ref, cwc_ref, cbc_ref, cs_ref)):
            lo = jnp.where(c > 0, src[pl.ds(lo_start, halo), :], 0.0)
            hi = jnp.where(c < nc - 1, src[pl.ds(hi_start, halo), :], 0.0)
            win = jnp.concatenate([lo, src[pl.ds(t0, q), :], hi], axis=0).astype(BF16)
            taps = jnp.concatenate([win * wj for wj in tap_w[id(w_ref)]], axis=0)
            out = _silu(_dot(shift_mat, taps) + bias_ref[...])
            dst[pl.ds(t0, q), :] = out
            if dst is xs_ref:
                y_ref[pl.ds(t0, q), :] = out * dsk_ref[...]
        dt = _softplus(pltpu.roll(dt_ref[pl.ds(t0, q), :], to_lane0, axis=1) + bias_g)
        dta = dt * a_g
        cum_f = _dot_f32_rhs(tri_f, dta)
        cum_b = _dot_f32_rhs(tri_b, dta)
        cumf_ref[pl.ds(t0, q), :] = cum_f
        cumb_ref[pl.ds(t0, q), :] = cum_b
        log_dt = jnp.log(dt)
        srcf_ref[pl.ds(t0, q), :] = jnp.transpose(cum_f - log_dt)
        srcb_ref[pl.ds(t0, q), :] = jnp.transpose(cum_b - log_dt)
        return carry

    ii = lax.broadcasted_iota(jnp.int32, (q, q), 0)
    jj = lax.broadcasted_iota(jnp.int32, (q, q), 1)
    keep_f = jj <= ii
    keep_b = jj >= ii
    tri_f = keep_f.astype(BF16)
    tri_b = keep_b.astype(BF16)
    n_dt = 2 * SSD_HEADS
    assert q == n_dt
    to_lane0 = (n_dt - g * gh) % n_dt
    bias_g = pltpu.roll(jnp.broadcast_to(dtbias_ref[...], (8, n_dt)), to_lane0, axis=1)[0:1, :]
    a_g = pltpu.roll(jnp.broadcast_to(-jnp.exp(alog_ref[...]), (8, n_dt)), to_lane0, axis=1)[0:1, :]

    per_trip = 2 if nc % 2 == 0 else 1

    def conv_trip(i, carry):
        for k in range(per_trip):
            conv_chunk(i * per_trip + k, carry)
        return carry

    lax.fori_loop(0, nc // per_trip, conv_trip, 0)

    pair_w = 2 * SSD_HEAD_DIM
    lane = lax.broadcasted_iota(jnp.int32, (1, gw), 1)
    half_mask = [((lane % pair_w) // SSD_HEAD_DIM) == s for s in range(2)]
    first_head_lanes = lax.broadcasted_iota(jnp.int32, (1, pair_w), 1) < SSD_HEAD_DIM

    def scan_chunk(c, reverse):
        keep, edge, lane0 = (keep_b, 0, SSD_HEADS) if reverse else (keep_f, q - 1, 0)
        cum_ref, src_ref, st_ref = ((cumb_ref, srcb_ref, stb_ref) if reverse
                                    else (cumf_ref, srcf_ref, stf_ref))
        t0 = pl.multiple_of(c * q, q)
        xc = xs_ref[pl.ds(t0, q), :]
        bc = bs_ref[pl.ds(t0, q), :]
        cc = cs_ref[pl.ds(t0, q), :]
        cb = _dot_nt(cc.astype(BF16), bc.astype(BF16))
        bct = jnp.transpose(bc)
        x_bf = xc.astype(BF16)
        st = st_ref[...]
        st_bf = st.astype(BF16)
        zero = jnp.zeros((), BF16)
        x_half = [jnp.where(half_mask[s], x_bf, zero) for s in range(2)]
        st_half = [jnp.where(half_mask[s], st_bf, zero) for s in range(2)]
        pieces = []
        for pair in range(gh // 2):
            c0 = pair * pair_w
            lhs_y, lhs_s, carry_decay = [], [], []
            for e in (2 * pair, 2 * pair + 1):
                ln = lane0 + e
                col = jnp.broadcast_to(cum_ref[pl.ds(t0, q), ln:ln + 1], (q, q))
                src = src_ref[pl.ds(t0 + ln, 1), :]
                at_edge = cum_ref[pl.ds(t0 + edge, 1), ln:ln + 1]
                mix = jnp.exp(jnp.where(keep, col - src, -jnp.inf))
                lhs_y.append((cb * mix).astype(BF16))
                lhs_y.append((cc * jnp.exp(col)).astype(BF16))
                lhs_s.append((bct * jnp.exp(at_edge - src)).astype(BF16))
                carry_decay.append(jnp.exp(at_edge))
            rhs_y = jnp.concatenate([x_half[0][:, c0:c0 + pair_w], st_half[0][:, c0:c0 + pair_w],
                                     x_half[1][:, c0:c0 + pair_w], st_half[1][:, c0:c0 + pair_w]], axis=0)
            pieces.append(_dot(jnp.concatenate(lhs_y, axis=1), rhs_y))
            rhs_s = jnp.concatenate([x_half[0][:, c0:c0 + pair_w], x_half[1][:, c0:c0 + pair_w]], axis=0)
            keep_frac = jnp.where(first_head_lanes, carry_decay[0], carry_decay[1])
            st_ref[:, c0:c0 + pair_w] = (st[:, c0:c0 + pair_w] * keep_frac
                                         + _dot(jnp.concatenate(lhs_s, axis=1), rhs_s))
        y_ref[pl.ds(t0, q), :] = y_ref[pl.ds(t0, q), :] + jnp.concatenate(pieces, axis=1)

    for d, st_ref in enumerate((stf_ref, stb_ref)):
        if with_init:
            st_ref[...] = jnp.transpose(init_ref[d].reshape(gw, SSD_STATE))
        else:
            st_ref[...] = jnp.zeros_like(st_ref)

    def step(i, carry):
        for k in range(per_trip):
            s = i * per_trip + k
            scan_chunk(s, False)
            scan_chunk(nc - 1 - s, True)
        return carry

    lax.fori_loop(0, nc // per_trip, step, 0)
    for d, st_ref in enumerate((stf_ref, stb_ref)):
        sfin_ref[d] = jnp.transpose(st_ref[...]).reshape(gh, SSD_HEAD_DIM, SSD_STATE)


def _ssd_scan(proj3, dt_raw3, conv_w, conv_b, a_log, dt_bias, d_skip, i_odd, init=None):
    b, l, _ = proj3.shape
    gh, gw = SSD_GROUP_HEADS, SSD_GROUP_WIDTH
    n_dt = 2 * SSD_HEADS
    a_log3 = a_log.reshape(a_log.shape[0], 1, n_dt)
    dt_bias3 = dt_bias.reshape(dt_bias.shape[0], 1, n_dt)
    dsk = jnp.repeat(d_skip[i_odd].astype(F32), SSD_HEAD_DIM).reshape(1, SSD_INNER)
    conv_b3 = conv_b.reshape(conv_b.shape[0], 1, conv_b.shape[1])

    x_col0 = SSD_INNER // gw
    b_col0 = (2 * SSD_INNER) // SSD_STATE
    c_col0 = (2 * SSD_INNER + SSD_BC_WIDTH) // SSD_STATE
    cwx_col0 = 0
    cwb_col0 = SSD_INNER // SSD_STATE
    cwc_col0 = (SSD_INNER + SSD_BC_WIDTH) // SSD_STATE

    in_specs = [
        pl.BlockSpec((None, l, gw), lambda bi, g: (bi, 0, x_col0 + g)),
        pl.BlockSpec((None, l, SSD_STATE), lambda bi, g: (bi, 0, b_col0 + g)),
        pl.BlockSpec((None, l, SSD_STATE), lambda bi, g: (bi, 0, c_col0 + g)),
        pl.BlockSpec((None, l, n_dt), lambda bi, g: (bi, 0, 0)),
        pl.BlockSpec((None, SSD_CONV, gw), lambda bi, g: (i_odd, 0, cwx_col0 + g)),
        pl.BlockSpec((None, SSD_CONV, SSD_STATE), lambda bi, g: (i_odd, 0, cwb_col0 + g)),
        pl.BlockSpec((None, SSD_CONV, SSD_STATE), lambda bi, g: (i_odd, 0, cwc_col0 + g)),
        pl.BlockSpec((None, 1, gw), lambda bi, g: (i_odd, 0, cwx_col0 + g)),
        pl.BlockSpec((None, 1, SSD_STATE), lambda bi, g: (i_odd, 0, cwb_col0 + g)),
        pl.BlockSpec((None, 1, SSD_STATE), lambda bi, g: (i_odd, 0, cwc_col0 + g)),
        pl.BlockSpec((None, 1, n_dt), lambda bi, g: (i_odd, 0, 0)),
        pl.BlockSpec((None, 1, n_dt), lambda bi, g: (i_odd, 0, 0)),
        pl.BlockSpec((1, gw), lambda bi, g: (0, g)),
    ]
    args = [proj3, proj3, proj3, dt_raw3, conv_w, conv_w, conv_w,
            conv_b3, conv_b3, conv_b3, a_log3, dt_bias3, dsk]
    if init is not None:
        in_specs.append(pl.BlockSpec((None, None, 2, gh, SSD_HEAD_DIM, SSD_STATE),
                                     lambda bi, g: (bi, i_odd, 0, g, 0, 0)))
        args.append(init)
    return pl.pallas_call(
        functools.partial(_ssd_kernel, l=l, with_init=init is not None),
        grid=(b, SSD_GROUPS),
        in_specs=in_specs,
        out_specs=[
            pl.BlockSpec((None, l, gw), lambda bi, g: (bi, 0, g)),
            pl.BlockSpec((None, 2, gh, SSD_HEAD_DIM, SSD_STATE), lambda bi, g: (bi, 0, g, 0, 0)),
        ],
        out_shape=[
            jax.ShapeDtypeStruct((b, l, SSD_INNER), F32),
            jax.ShapeDtypeStruct((b, 2, SSD_HEADS, SSD_HEAD_DIM, SSD_STATE), F32),
        ],
        scratch_shapes=[
            pltpu.VMEM((l, gw), F32),
            pltpu.VMEM((l, SSD_STATE), F32),
            pltpu.VMEM((l, SSD_STATE), F32),
            pltpu.VMEM((SSD_STATE, gw), F32),
            pltpu.VMEM((SSD_STATE, gw), F32),
        ] + [pltpu.VMEM((l, n_dt), F32)] * 4,
        compiler_params=_cparams("parallel", "parallel"),
        name="ssd_scan",
    )(*args)


def _gate_norm_kernel(y_ref, z_ref, g_ref, o_ref):
    y = y_ref[...] * _silu(z_ref[...])
    o = y * lax.rsqrt(jnp.mean(y * y, axis=-1, keepdims=True) + NORM_EPS) * g_ref[...]
    o_ref[...] = o.astype(o_ref.dtype)


def _gate_norm(y, proj, norm_g, i_odd):
    m = y.shape[0]
    tm = ROW_TILE
    g3 = norm_g.reshape(norm_g.shape[0], 1, SSD_INNER)
    return pl.pallas_call(
        _gate_norm_kernel,
        grid=(m // tm,),
        in_specs=[
            pl.BlockSpec((tm, SSD_INNER), lambda i: (i, 0)),
            pl.BlockSpec((tm, SSD_INNER), lambda i: (i, 0)),
            pl.BlockSpec((None, 1, SSD_INNER), lambda i: (i_odd, 0, 0)),
        ],
        out_specs=pl.BlockSpec((tm, SSD_INNER), lambda i: (i, 0)),
        out_shape=jax.ShapeDtypeStruct((m, SSD_INNER), BF16),
        compiler_params=_cparams("parallel"),
        name="gate_norm",
    )(y, proj, g3)


def _expert_kernel(x_ref, wg_ref, wu_ref, wd_ref, gate_ref, o_ref, acc_ref):
    f = pl.program_id(2)
    x = x_ref[...]
    hg = _dot(x, wg_ref[...].astype(BF16))
    hu = _dot(x, wu_ref[...].astype(BF16))
    hdn = (_silu(hg) * hu).astype(BF16)

    @pl.when(f == 0)
    def _():
        acc_ref[...] = jnp.zeros_like(acc_ref)

    acc_ref[...] += _dot(hdn, wd_ref[...].astype(BF16))

    @pl.when(f == pl.num_programs(2) - 1)
    def _():
        o_ref[...] = (acc_ref[...] * gate_ref[...]).astype(o_ref.dtype)


def _expert_ffn(xe, gate, w_gate, w_up, w_down, layer):
    e, r, _ = xe.shape
    tr = min(r, 1024)
    tf = 256
    return pl.pallas_call(
        _expert_kernel,
        grid=(e, r // tr, EXPERT_FF // tf),
        in_specs=[
            pl.BlockSpec((None, tr, D_MODEL), lambda ei, ri, f: (ei, ri, 0)),
            pl.BlockSpec((None, None, D_MODEL, tf), lambda ei, ri, f: (layer, ei, 0, f)),
            pl.BlockSpec((None, None, D_MODEL, tf), lambda ei, ri, f: (layer, ei, 0, f)),
            pl.BlockSpec((None, None, tf, D_MODEL), lambda ei, ri, f: (layer, ei, f, 0)),
            pl.BlockSpec((None, tr, 1), lambda ei, ri, f: (ei, ri, 0)),
        ],
        out_specs=pl.BlockSpec((None, tr, D_MODEL), lambda ei, ri, f: (ei, ri, 0)),
        out_shape=jax.ShapeDtypeStruct((e, r, D_MODEL), BF16),
        scratch_shapes=[pltpu.VMEM((tr, D_MODEL), F32)],
        compiler_params=_cparams("parallel", "parallel", "arbitrary"),
        name="expert_ffn",
    )(xe, w_gate, w_up, w_down, gate)


def _combine_kernel(idx_ref, ye_ref, x_ref, gate_ref, o_ref, *, n, ec, cap):
    step = pl.program_id(2)
    token = lax.broadcasted_iota(jnp.int32, (n, cap), 0)
    onehot = jnp.concatenate(
        [jnp.where(token == idx_ref[k], 1.0, 0.0).astype(BF16) for k in range(ec)], axis=1)

    @pl.when(step == 0)
    def _():
        o_ref[...] = jnp.zeros_like(o_ref)

    o_ref[...] += _dot(onehot, ye_ref[...].reshape(ec * cap, ye_ref.shape[-1]))

    @pl.when(step == pl.num_programs(2) - 1)
    def _():
        o_ref[...] = x_ref[...] + gate_ref[...] * o_ref[...]


def _expert_combine(ye, idx, x, mod4, gate_idx, first_row, shared_cond):
    n_req, n_exp, cap = idx.shape
    m = x.shape[0]
    n = m // n_req
    ec = max(1, min(n_exp, 512 // cap))
    tn = D_MODEL // 2
    idx4 = idx.reshape(n_req, n_exp, 1, cap)
    cond_row = (lambda b: first_row) if shared_cond else (lambda b: first_row + b)
    return pl.pallas_call(
        functools.partial(_combine_kernel, n=n, ec=ec, cap=cap),
        grid=(n_req, D_MODEL // tn, n_exp // ec),
        in_specs=[
            pl.BlockSpec((None, ec, 1, cap), lambda b, j, s: (b, s, 0, 0)),
            pl.BlockSpec((ec, cap, tn), lambda b, j, s: (s, b, j)),
            pl.BlockSpec((n, tn), lambda b, j, s: (b, j)),
            pl.BlockSpec((None, None, 1, tn), lambda b, j, s: (cond_row(b), gate_idx, 0, j)),
        ],
        out_specs=pl.BlockSpec((n, tn), lambda b, j, s: (b, j)),
        out_shape=jax.ShapeDtypeStruct((m, D_MODEL), F32),
        compiler_params=_cparams("parallel", "parallel", "arbitrary"),
        name="expert_combine",
    )(idx4, ye, x, mod4)


def _prefix_count(mask_bf, strict_upper):
    n = mask_bf.shape[1]
    carry = jnp.zeros((mask_bf.shape[0], 1), F32)
    out = []
    for k in range(n // PREFIX_BLOCK):
        blk = mask_bf[:, k * PREFIX_BLOCK:(k + 1) * PREFIX_BLOCK]
        out.append(_dot(blk, strict_upper) + carry)
        carry = carry + jnp.sum(blk.astype(F32), axis=-1, keepdims=True)
    return jnp.concatenate(out, axis=1)


def _router_kernel(lg_ref, o_ref, slot_ref, feat_ref, *, n_req, n, cap):
    step = pl.program_id(0)
    n_exp = N_EXPERTS

    @pl.when(step == 0)
    def _():
        tok = lax.broadcasted_iota(jnp.int32, (n, ROUTE_LANES), 0)
        tlane = lax.broadcasted_iota(jnp.int32, (n, ROUTE_LANES), 1)
        radix_bits = ROUTE_TOKEN_RADIX.bit_length() - 1
        tok_feat = (jnp.where(tlane == ROUTE_TOKEN_LANE, tok >> radix_bits, 0)
                    + jnp.where(tlane == ROUTE_TOKEN_LANE + 1, tok & (ROUTE_TOKEN_RADIX - 1), 0)).astype(F32)
        is_expert = tlane < n_exp
        aff_rows = []
        for r in range(n_req):
            x = jnp.where(is_expert, lg_ref[pl.ds(r * n, n), :], -jnp.inf)
            e = jnp.exp(x - jnp.max(x, axis=-1, keepdims=True))
            aff = e / jnp.sum(e, axis=-1, keepdims=True)
            feat = tok_feat
            for p, part in enumerate(_split3(aff)):
                piece = part.astype(F32)
                feat = feat + (pltpu.roll(piece, p * n_exp, axis=1) if p else piece)
            feat_ref[r] = feat.astype(BF16)
            aff_rows.append(jnp.transpose(aff)[:n_exp])
        aff_t = jnp.concatenate(aff_rows, axis=0)
        bits = pltpu.bitcast(aff_t, jnp.int32)

        def bisect(_, carry):
            lo, hi = carry
            mid = lax.shift_right_arithmetic(lo + hi, 1)
            count = jnp.sum(jnp.where(bits >= mid, 1.0, 0.0), axis=-1, keepdims=True)
            enough = count >= cap
            return jnp.where(enough, mid, lo), jnp.where(enough, hi, mid)

        rows = n_req * n_exp
        lo0 = jnp.zeros((rows, 1), jnp.int32)
        hi0 = jnp.full((rows, 1), 0x3F800001, jnp.int32)
        kth, _ = lax.fori_loop(0, 31, bisect, (lo0, hi0))
        above = bits > kth
        tied = bits == kth
        blk_i = lax.broadcasted_iota(jnp.int32, (PREFIX_BLOCK, PREFIX_BLOCK), 0)
        blk_j = lax.broadcasted_iota(jnp.int32, (PREFIX_BLOCK, PREFIX_BLOCK), 1)
        strict_upper = (blk_i < blk_j).astype(BF16)
        need = cap - jnp.sum(jnp.where(above, 1.0, 0.0), axis=-1, keepdims=True)
        tie_rank = _prefix_count(jnp.where(tied, 1.0, 0.0).astype(BF16), strict_upper)
        chosen = jnp.logical_or(above, jnp.logical_and(tied, tie_rank < need))
        slot = _prefix_count(jnp.where(chosen, 1.0, 0.0).astype(BF16), strict_upper)
        slot_ref[...] = jnp.where(chosen, slot, -1.0)

    slot_id = lax.broadcasted_iota(jnp.int32, (cap, n), 0).astype(F32)
    for k in range(ROUTE_ROWS_PER_STEP):
        row = step * ROUTE_ROWS_PER_STEP + k
        onehot_t = jnp.where(slot_id == slot_ref[pl.ds(row, 1), :], 1.0, 0.0).astype(BF16)
        o_ref[k] = _dot(onehot_t, feat_ref[row // n_exp])


def _route(logits, n_req):
    m = logits.shape[0]
    n = m // n_req
    cap = EC_CAPACITY_FACTOR * n // N_EXPERTS
    rows = n_req * N_EXPERTS
    assert n % PREFIX_BLOCK == 0 and rows % ROUTE_ROWS_PER_STEP == 0
    rec, slot = pl.pallas_call(
        functools.partial(_router_kernel, n_req=n_req, n=n, cap=cap),
        grid=(rows // ROUTE_ROWS_PER_STEP,),
        in_specs=[pl.BlockSpec((m, ROUTE_LANES), lambda i: (0, 0))],
        out_specs=[pl.BlockSpec((ROUTE_ROWS_PER_STEP, cap, ROUTE_LANES), lambda i: (i, 0, 0)),
                   pl.BlockSpec((rows, n), lambda i: (0, 0))],
        out_shape=[jax.ShapeDtypeStruct((rows, cap, ROUTE_LANES), F32),
                   jax.ShapeDtypeStruct((rows, n), F32)],
        scratch_shapes=[pltpu.VMEM((n_req, n, ROUTE_LANES), BF16)],
        compiler_params=_cparams("arbitrary"),
        name="router",
    )(logits)
    rec = rec.reshape(n_req, N_EXPERTS, cap, ROUTE_LANES)
    idx = (ROUTE_TOKEN_RADIX * rec[..., ROUTE_TOKEN_LANE] + rec[..., ROUTE_TOKEN_LANE + 1]).astype(jnp.int32)
    pieces = rec[..., :3 * N_EXPERTS].reshape(n_req, N_EXPERTS, cap, 3, N_EXPERTS).sum(axis=3)
    own = jnp.eye(N_EXPERTS, dtype=F32)[None, :, None, :]
    gate = jnp.sum(pieces * own, axis=3)
    return idx, gate, slot


def _gather_kernel(slot_ref, h_ref, o_ref, *, n, ec, cap):
    slot_id = lax.broadcasted_iota(jnp.int32, (cap, n), 0).astype(F32)
    onehot = jnp.concatenate(
        [jnp.where(slot_id == slot_ref[k], 1.0, 0.0).astype(BF16) for k in range(ec)], axis=0)
    o_ref[...] = _dot(onehot, h_ref[...]).reshape(o_ref.shape).astype(o_ref.dtype)


def _gather_rows(h, slot, n_req):
    m, d = h.shape
    n = m // n_req
    cap = EC_CAPACITY_FACTOR * n // N_EXPERTS
    ec = max(1, min(N_EXPERTS, 2048 // cap))
    tn = d // 2
    slot4 = slot.reshape(n_req, N_EXPERTS, 1, n)
    return pl.pallas_call(
        functools.partial(_gather_kernel, n=n, ec=ec, cap=cap),
        grid=(n_req, d // tn, N_EXPERTS // ec),
        in_specs=[pl.BlockSpec((None, ec, 1, n), lambda b, j, s: (b, s, 0, 0)),
                  pl.BlockSpec((n, tn), lambda b, j, s: (b, j))],
        out_specs=pl.BlockSpec((ec, cap, tn), lambda b, j, s: (s, b, j)),
        out_shape=jax.ShapeDtypeStruct((N_EXPERTS, n_req * cap, d), h.dtype),
        compiler_params=_cparams("parallel", "parallel", "arbitrary"),
        name="gather_rows",
    )(slot4, h)


def _expert_choice(h, logits, x, n_req, w_gate, w_up, w_down, layer, mod4, gate_idx, first_row, shared_cond):
    idx, gate, slot = _route(logits, n_req)
    cap = idx.shape[2]
    gate = jnp.transpose(gate, (1, 0, 2)).reshape(N_EXPERTS, n_req * cap, 1)
    xe = _gather_rows(h, slot, n_req)
    ye = _expert_ffn(xe, gate, w_gate, w_up, w_down, layer)
    return _expert_combine(ye, idx, x, mod4, gate_idx, first_row, shared_cond)


class _Stream:
    def __init__(self, x3, first_row, shared_cond):
        self.batch, self.n, _ = x3.shape
        self.x = x3.reshape(self.batch * self.n, D_MODEL)
        self.first_row = first_row
        self.shared_cond = shared_cond
        self.rows_per_cond = self.batch * self.n if shared_cond else self.n

    def mod_args(self):
        return dict(rows_per_cond=self.rows_per_cond, first_row=self.first_row)


def kernel(x_prompt, x_sample, cache_na_k, cache_na_v, state_ssd, c, c_ctx, ada_w, ada_b, norm1_g, norm2_g,
           router_w, exp_w_gate, exp_w_up, exp_w_down, ab_w_in, pool_w, pool_scale, na_q_norm, na_k_norm,
           na_rpb, ab_w_out, ssd_w_in, ssd_conv_w, ssd_conv_b, ssd_a_log, ssd_dt_bias, ssd_d, ssd_norm_g,
           ssd_w_out):
    depth = ada_w.shape[0]
    dec_batch = x_sample.shape[0]
    latent = _Stream(x_sample, 0, shared_cond=False)
    prompt = _Stream(x_prompt, dec_batch, shared_cond=True)
    cond = jnp.concatenate(
        [c, c_ctx[None, :], jnp.zeros((MOD_ROWS - dec_batch - 1, D_MODEL), F32)], axis=0)

    router_pad = jnp.pad(router_w, ((0, 0), (0, 0), (0, ROUTE_LANES - N_EXPERTS)))

    new_k, new_v, new_s = [], [], []
    for layer in range(depth):
        i = layer // 2
        mod4 = _modulation(cond, ada_w, ada_b, layer).reshape(MOD_ROWS, N_MOD, 1, D_MODEL)
        for s in (prompt, latent):
            margs = s.mod_args()
            h = _norm_mod(s.x, norm1_g, mod4, layer, 0, 1, **margs)[0]
            tm = min(s.x.shape[0], 2048)
            if layer % 2 == 0:
                proj = _matmul([h], ab_w_in, i, tm=tm, tn=512, n_blocks=ab_w_in.shape[2] // 512)
                proj3 = proj.reshape(s.batch, s.n, -1)
                pooled = _pool_mixer(proj3, pool_w, pool_scale, i)
                if s is prompt:
                    att, k_new, v_new = _ctx_attention(proj3, na_q_norm, na_k_norm, i)
                    new_k.append(k_new)
                    new_v.append(v_new)
                else:
                    att = _neighbourhood_attention(proj3, cache_na_k, cache_na_v, na_q_norm, na_k_norm,
                                                   na_rpb, i)
                s.x = _matmul([pooled.reshape(-1, POOL_WIDTH), att.reshape(-1, NA_WIDTH)], ab_w_out, i,
                              tm=tm, tn=512, n_blocks=D_MODEL // 512, res=s.x, mod4=mod4, gate_idx=2,
                              **margs)
            else:
                proj = _matmul([h], ssd_w_in, i, tm=tm, tn=512, n_blocks=SSD_MAIN_DIM // 512)
                dt_raw = _matmul([h], ssd_w_in, i, tm=tm, tn=2 * SSD_HEADS, n_blocks=1,
                                 col_block0=SSD_MAIN_DIM // (2 * SSD_HEADS))
                proj3 = proj.reshape(s.batch, s.n, SSD_MAIN_DIM)
                dt3 = dt_raw.reshape(s.batch, s.n, 2 * SSD_HEADS)
                init = None if s is prompt else state_ssd
                y, s_fin = _ssd_scan(proj3, dt3, ssd_conv_w, ssd_conv_b, ssd_a_log, ssd_dt_bias, ssd_d, i,
                                     init=init)
                if s is prompt:
                    new_s.append(s_fin[:, None])
                yn = _gate_norm(y.reshape(-1, SSD_INNER), proj, ssd_norm_g, i)
                s.x = _matmul([yn], ssd_w_out, i, tm=min(tm, 1024), tn=512, n_blocks=D_MODEL // 512,
                              res=s.x, mod4=mod4, gate_idx=2, **margs)
            h2, logits = _norm_mod(s.x, norm2_g, mod4, layer, 3, 4, router_w=router_pad, **margs)
            s.x = _expert_choice(h2, logits, s.x, s.batch, exp_w_gate, exp_w_up, exp_w_down, layer,
                                 mod4, 5, s.first_row, s.shared_cond)

    y_prompt = prompt.x.reshape(x_prompt.shape)
    y_sample = latent.x.reshape(x_sample.shape)
    return (y_prompt, y_sample, jnp.concatenate(new_k, axis=1), jnp.concatenate(new_v, axis=1),
            jnp.concatenate(new_s, axis=1))
```

```python
import functools

import numpy as np
import jax
import jax.numpy as jnp
from jax import lax
from jax.experimental import pallas as pl
from jax.experimental.pallas import tpu as pltpu

F32 = jnp.float32
BF16 = jnp.bfloat16

D_MODEL = 2048
GRID_W = 64
POOL_WIDTH = 1024
POOL_GROUPS = 4
POOL_GROUP_DIM = 256
POOL_WINDOWS = (2, 4, 8, 16)
NA_WIDTH = 1024
NA_HEAD_DIM = 128
NA_HEADS = 8
WIN_H = 8
WIN_W = 16
SSD_INNER = 4096
SSD_HEAD_DIM = 64
SSD_HEADS = 64
SSD_GROUPS = 8
SSD_GROUP_HEADS = SSD_HEADS // SSD_GROUPS
SSD_GROUP_WIDTH = SSD_GROUP_HEADS * SSD_HEAD_DIM
SSD_STATE = 128
SSD_CONV = 4
SSD_CHUNK = 128
SSD_BC_WIDTH = SSD_GROUPS * SSD_STATE
SSD_MAIN_DIM = 2 * SSD_INNER + 2 * SSD_BC_WIDTH
N_EXPERTS = 16
EXPERT_FF = 1024
EC_CAPACITY_FACTOR = 2
NORM_EPS = 1e-6
N_MOD = 6
MOD_ROWS = 16
ROUTE_LANES = 128
ROUTE_TOKEN_LANE = 3 * N_EXPERTS
ROUTE_TOKEN_RADIX = 64
ROUTE_ROWS_PER_STEP = 8
PREFIX_BLOCK = 256

VMEM_LIMIT_BYTES = 56 * 1024 * 1024
ROW_TILE = 256


def _cparams(*sem):
    return pltpu.CompilerParams(dimension_semantics=sem, vmem_limit_bytes=VMEM_LIMIT_BYTES)


def _silu(x):
    return x * (1.0 / (1.0 + jnp.exp(-x)))


def _split2(x):
    hi = x.astype(BF16)
    lo = (x - hi.astype(F32)).astype(BF16)
    return hi, lo


def _split3(x):
    hi = x.astype(BF16)
    r = x - hi.astype(F32)
    mid = r.astype(BF16)
    lo = (r - mid.astype(F32)).astype(BF16)
    return hi, mid, lo


def _dot(a, b):
    return jnp.dot(a, b, preferred_element_type=F32)


def _dot_nt(a, b):
    return lax.dot_general(a, b, (((1,), (1,)), ((), ())), preferred_element_type=F32)


def _dot_f32_rhs(m, x):
    hi, mid, lo = _split3(x)
    return _dot(m, hi) + _dot(m, mid) + _dot(m, lo)


def _dot_split(a, b):
    a_hi, a_lo = _split2(a)
    b_hi, b_lo = _split2(b)
    return _dot(a_hi, b_hi) + _dot(a_lo, b_hi) + _dot(a_hi, b_lo)


def _mod_kernel(c_ref, w_ref, b_ref, o_ref):
    s = _silu(c_ref[...])
    o_ref[...] = _dot_split(s, w_ref[...]) + b_ref[...]


def _modulation(cond, ada_w, ada_b, layer):
    tn = 768
    n = N_MOD * D_MODEL
    ada_b3 = ada_b.reshape(ada_b.shape[0], 1, n)
    return pl.pallas_call(
        _mod_kernel,
        grid=(n // tn,),
        in_specs=[
            pl.BlockSpec((MOD_ROWS, D_MODEL), lambda j: (0, 0)),
            pl.BlockSpec((None, D_MODEL, tn), lambda j: (layer, 0, j)),
            pl.BlockSpec((None, 1, tn), lambda j: (layer, 0, j)),
        ],
        out_specs=pl.BlockSpec((MOD_ROWS, tn), lambda j: (0, j)),
        out_shape=jax.ShapeDtypeStruct((MOD_ROWS, n), F32),
        compiler_params=_cparams("parallel"),
        name="modulation",
    )(cond, ada_w, ada_b3)


def _mod_spec(which, rows_per_cond, first_row, tm):
    return pl.BlockSpec((None, None, 1, D_MODEL),
                        lambda i, *_: (first_row + (i * tm) // rows_per_cond, which, 0, 0))


def _norm_mod_kernel(x_ref, g_ref, sh_ref, sc_ref, *rest, with_router):
    x = x_ref[...]
    y = x * lax.rsqrt(jnp.mean(x * x, axis=-1, keepdims=True) + NORM_EPS) * g_ref[...]
    h = y * (1.0 + sc_ref[...]) + sh_ref[...]
    if with_router:
        rw_ref, h_ref, lg_ref = rest
        lg_ref[...] = _dot_split(h, rw_ref[...])
    else:
        (h_ref,) = rest
    h_ref[...] = h.astype(BF16)


def _norm_mod(x, g, mod4, layer, shift_idx, scale_idx, rows_per_cond, first_row, router_w=None):
    m = x.shape[0]
    tm = 2 * ROW_TILE
    g3 = g.reshape(g.shape[0], 1, D_MODEL)
    in_specs = [
        pl.BlockSpec((tm, D_MODEL), lambda i: (i, 0)),
        pl.BlockSpec((None, 1, D_MODEL), lambda i: (layer, 0, 0)),
        _mod_spec(shift_idx, rows_per_cond, first_row, tm),
        _mod_spec(scale_idx, rows_per_cond, first_row, tm),
    ]
    args = [x, g3, mod4, mod4]
    out_specs = [pl.BlockSpec((tm, D_MODEL), lambda i: (i, 0))]
    out_shape = [jax.ShapeDtypeStruct((m, D_MODEL), BF16)]
    if router_w is not None:
        in_specs.append(pl.BlockSpec((None, D_MODEL, ROUTE_LANES), lambda i: (layer, 0, 0)))
        args.append(router_w)
        out_specs.append(pl.BlockSpec((tm, ROUTE_LANES), lambda i: (i, 0)))
        out_shape.append(jax.ShapeDtypeStruct((m, ROUTE_LANES), F32))
    return pl.pallas_call(
        functools.partial(_norm_mod_kernel, with_router=router_w is not None),
        grid=(m // tm,),
        in_specs=in_specs,
        out_specs=out_specs,
        out_shape=out_shape,
        compiler_params=_cparams("parallel"),
        name="norm_mod",
    )(*args)


def _matmul_kernel(*refs, n_a, k_sizes, with_res):
    a_refs = refs[:n_a]
    w_ref = refs[n_a]
    if with_res:
        res_ref, gate_ref, o_ref = refs[n_a + 1:]
    else:
        (o_ref,) = refs[n_a + 1:]
    w = w_ref[...].astype(BF16)
    acc = None
    k0 = 0
    for a_ref, k in zip(a_refs, k_sizes):
        part = _dot(a_ref[...], w[k0:k0 + k])
        acc = part if acc is None else acc + part
        k0 += k
    if with_res:
        acc = res_ref[...] + gate_ref[...] * acc
    o_ref[...] = acc.astype(o_ref.dtype)


def _matmul(a_list, w, layer, *, tm, tn, n_blocks, col_block0=0, out_dtype=F32,
            res=None, mod4=None, gate_idx=None, rows_per_cond=None, first_row=None):
    m = a_list[0].shape[0]
    k_sizes = tuple(a.shape[1] for a in a_list)
    k_total = sum(k_sizes)
    in_specs = [pl.BlockSpec((tm, k), lambda i, j: (i, 0)) for k in k_sizes]
    in_specs.append(pl.BlockSpec((None, k_total, tn), lambda i, j: (layer, 0, col_block0 + j)))
    args = list(a_list) + [w]
    if res is not None:
        in_specs.append(pl.BlockSpec((tm, tn), lambda i, j: (i, j)))
        in_specs.append(pl.BlockSpec((None, None, 1, tn),
                                     lambda i, j: (first_row + (i * tm) // rows_per_cond, gate_idx, 0, j)))
        args += [res, mod4]
    return pl.pallas_call(
        functools.partial(_matmul_kernel, n_a=len(a_list), k_sizes=k_sizes, with_res=res is not None),
        grid=(m // tm, n_blocks),
        in_specs=in_specs,
        out_specs=pl.BlockSpec((tm, tn), lambda i, j: (i, j)),
        out_shape=jax.ShapeDtypeStruct((m, n_blocks * tn), out_dtype),
        compiler_params=_cparams("parallel", "arbitrary"),
        name="matmul",
    )(*args)


def _pool_kernel(u_ref, w_ref, s_ref, o_ref, *, n):
    g = pl.program_id(1)
    u = u_ref[...]
    t = lax.broadcasted_iota(jnp.int32, (n, 1), 0)
    for gi, win in enumerate(POOL_WINDOWS):
        @pl.when(g == gi)
        def _(win=win):
            half = win // 2
            total = u
            for d in range(-half, win - half):
                if d == 0:
                    continue
                shifted = pltpu.roll(u, (-d) % n, axis=0)
                valid = jnp.logical_and(t + d >= 0, t + d < n)
                total = total + jnp.where(valid, shifted, 0.0)
            cnt = (jnp.minimum(t - half + win, n) - jnp.maximum(t - half, 0)).astype(F32)
            pooled = total / cnt - u
            out = _dot(pooled.astype(BF16), w_ref[...].astype(BF16)) * s_ref[...]
            o_ref[...] = out.astype(o_ref.dtype)


def _pool_mixer(proj3, pool_w, pool_scale, i_even):
    b, n, _ = proj3.shape
    scale4 = pool_scale.reshape(pool_scale.shape[0], POOL_GROUPS, 1, POOL_GROUP_DIM)
    return pl.pallas_call(
        functools.partial(_pool_kernel, n=n),
        grid=(b, POOL_GROUPS),
        in_specs=[
            pl.BlockSpec((None, n, POOL_GROUP_DIM), lambda bi, g: (bi, 0, g)),
            pl.BlockSpec((None, None, POOL_GROUP_DIM, POOL_GROUP_DIM), lambda bi, g: (i_even, g, 0, 0)),
            pl.BlockSpec((None, None, 1, POOL_GROUP_DIM), lambda bi, g: (i_even, g, 0, 0)),
        ],
        out_specs=pl.BlockSpec((None, n, POOL_GROUP_DIM), lambda bi, g: (bi, 0, g)),
        out_shape=jax.ShapeDtypeStruct((b, n, POOL_WIDTH), BF16),
        compiler_params=_cparams("parallel", "parallel"),
        name="pool_mixer",
    )(proj3, pool_w, scale4)


def _head_norm(x, g):
    return x * lax.rsqrt(jnp.mean(x * x, axis=-1, keepdims=True) + NORM_EPS) * g


_Q_COL0 = POOL_WIDTH // NA_HEAD_DIM
_K_COL0 = (POOL_WIDTH + NA_WIDTH) // NA_HEAD_DIM
_V_COL0 = (POOL_WIDTH + 2 * NA_WIDTH) // NA_HEAD_DIM


def _ctx_attn_kernel(q_ref, k_ref, v_ref, qg_ref, kg_ref, o_ref, kn_ref, vn_ref):
    for h in range(NA_HEADS):
        lanes = slice(h * NA_HEAD_DIM, (h + 1) * NA_HEAD_DIM)
        qn = _head_norm(q_ref[:, lanes], qg_ref[...])
        kn = _head_norm(k_ref[:, lanes], kg_ref[...])
        v = v_ref[:, lanes]
        kn_ref[h] = kn
        vn_ref[h] = v
        s = _dot_nt(qn.astype(BF16), kn.astype(BF16)) * (NA_HEAD_DIM ** -0.5)
        m = jnp.max(s, axis=-1, keepdims=True)
        p = jnp.exp(s - m)
        denom = jnp.sum(p, axis=-1, keepdims=True)
        o = _dot(p.astype(BF16), v.astype(BF16)) / denom
        o_ref[:, lanes] = o.astype(o_ref.dtype)


def _ctx_attention(proj3, q_norm, k_norm, i_even):
    b, n, _ = proj3.shape
    qg = q_norm.reshape(q_norm.shape[0], 1, NA_HEAD_DIM)
    kg = k_norm.reshape(k_norm.shape[0], 1, NA_HEAD_DIM)
    width_blocks = NA_WIDTH // NA_HEAD_DIM
    head_blk = lambda col0: pl.BlockSpec((None, n, NA_WIDTH), lambda bi: (bi, 0, col0 // width_blocks))
    gain = pl.BlockSpec((None, 1, NA_HEAD_DIM), lambda bi: (i_even, 0, 0))
    cache = pl.BlockSpec((None, None, NA_HEADS, n, NA_HEAD_DIM), lambda bi: (bi, 0, 0, 0, 0))
    return pl.pallas_call(
        _ctx_attn_kernel,
        grid=(b,),
        in_specs=[head_blk(_Q_COL0), head_blk(_K_COL0), head_blk(_V_COL0), gain, gain],
        out_specs=[pl.BlockSpec((None, n, NA_WIDTH), lambda bi: (bi, 0, 0)), cache, cache],
        out_shape=[
            jax.ShapeDtypeStruct((b, n, NA_WIDTH), BF16),
            jax.ShapeDtypeStruct((b, 1, NA_HEADS, n, NA_HEAD_DIM), F32),
            jax.ShapeDtypeStruct((b, 1, NA_HEADS, n, NA_HEAD_DIM), F32),
        ],
        compiler_params=_cparams("parallel"),
        name="ctx_attention",
    )(proj3, proj3, proj3, qg, kg)


NA_BAND = 4
NA_BAND_KEY_ROWS = WIN_H + NA_BAND - 1


def _na_band_key_start(r0, rows):
    return np.clip(r0 - WIN_H // 2, 0, rows - NA_BAND_KEY_ROWS)


def _na_bias_table(rpb, rows):
    col = np.arange(GRID_W)[:, None]
    kc = np.arange(GRID_W)[None, :]
    wstart = np.clip(col - WIN_W // 2, 0, GRID_W - WIN_W)
    inside = (kc >= wstart) & (kc < wstart + WIN_W)
    rel = np.clip(kc - col + WIN_W - 1, 0, 2 * WIN_W - 2)
    a = np.arange(NA_BAND)[:, None]
    kr = np.arange(NA_BAND_KEY_ROWS)[None, :]
    ridx, row_ok = [], []
    for r0 in (0, NA_BAND, rows - NA_BAND):
        r = r0 + a
        sr = np.clip(r - WIN_H // 2, 0, rows - WIN_H)
        krow = _na_band_key_start(r0, rows) + kr
        row_ok.append((krow >= sr) & (krow < sr + WIN_H))
        ridx.append(np.clip(krow - r + WIN_H - 1, 0, 2 * WIN_H - 2))
    ridx = np.stack(ridx)
    ok = np.stack(row_ok)[:, :, None, :, None] & inside[None, None, :, None, :]
    row_sel = np.eye(2 * WIN_H - 1, dtype=np.float32)[ridx]
    col_sel = np.eye(2 * WIN_W - 1, dtype=np.float32)[rel]
    tab = jnp.einsum('vaki,hij,cdj->hvackd', row_sel, rpb, col_sel, precision=lax.Precision.HIGHEST)
    tab = jnp.where(ok[None], tab, -jnp.inf)
    return tab.reshape(rpb.shape[0], 3, NA_BAND * GRID_W, NA_BAND_KEY_ROWS * GRID_W)


def _na_kernel(q_ref, k_ref, v_ref, kc_ref, vc_ref, qg_ref, kg_ref, bias_ref, o_ref,
               qs_ref, ks_ref, vs_ref, *, rows):
    qs_ref[...] = _head_norm(q_ref[...], qg_ref[...]).astype(BF16)
    ks_ref[...] = _head_norm(k_ref[...], kg_ref[...]).astype(BF16)
    vs_ref[...] = v_ref[...].astype(BF16)
    k_ctx = kc_ref[...].astype(BF16)
    v_ctx = vc_ref[...].astype(BF16)
    scale = NA_HEAD_DIM ** -0.5
    n_bands = rows // NA_BAND
    n_q = NA_BAND * GRID_W
    n_loc = NA_BAND_KEY_ROWS * GRID_W

    def band(i):
        r0 = i * NA_BAND
        key_row0 = jnp.clip(r0 - WIN_H // 2, 0, rows - NA_BAND_KEY_ROWS)
        variant = jnp.where(i == 0, 0, jnp.where(i == n_bands - 1, 2, 1))
        q0 = pl.multiple_of(r0 * GRID_W, n_q)
        q_b = qs_ref[pl.ds(q0, n_q), :]
        kstart = pl.multiple_of(key_row0 * GRID_W, GRID_W)
        k_blk = ks_ref[pl.ds(kstart, n_loc), :]
        v_blk = vs_ref[pl.ds(kstart, n_loc), :]
        s_loc = _dot_nt(q_b, k_blk) * scale + bias_ref[variant]
        s_ctx = _dot_nt(q_b, k_ctx) * scale
        m = jnp.maximum(jnp.max(s_loc, axis=-1, keepdims=True), jnp.max(s_ctx, axis=-1, keepdims=True))
        p_loc = jnp.exp(s_loc - m)
        p_ctx = jnp.exp(s_ctx - m)
        denom = jnp.sum(p_loc, axis=-1, keepdims=True) + jnp.sum(p_ctx, axis=-1, keepdims=True)
        o = (_dot(p_loc.astype(BF16), v_blk) + _dot(p_ctx.astype(BF16), v_ctx)) / denom
        o_ref[pl.ds(q0, n_q), :] = o.astype(o_ref.dtype)

    per_trip = 2 if n_bands % 2 == 0 else 1

    def trip(i, carry):
        for k in range(per_trip):
            band(i * per_trip + k)
        return carry

    lax.fori_loop(0, n_bands // per_trip, trip, 0)


def _neighbourhood_attention(proj3, cache_k, cache_v, q_norm, k_norm, rpb, i_even):
    b, t, _ = proj3.shape
    rows = t // GRID_W
    assert rows % NA_BAND == 0 and rows >= NA_BAND_KEY_ROWS + 1, rows
    n_ctx = cache_k.shape[3]
    bias = _na_bias_table(rpb[i_even], rows)
    qg = q_norm.reshape(q_norm.shape[0], 1, NA_HEAD_DIM)
    kg = k_norm.reshape(k_norm.shape[0], 1, NA_HEAD_DIM)
    head_blk = lambda col0: pl.BlockSpec((None, t, NA_HEAD_DIM), lambda bi, h: (bi, 0, col0 + h))
    gain = pl.BlockSpec((None, 1, NA_HEAD_DIM), lambda bi, h: (i_even, 0, 0))
    cache = pl.BlockSpec((None, None, None, n_ctx, NA_HEAD_DIM), lambda bi, h: (bi, i_even, h, 0, 0))
    return pl.pallas_call(
        functools.partial(_na_kernel, rows=rows),
        grid=(b, NA_HEADS),
        in_specs=[head_blk(_Q_COL0), head_blk(_K_COL0), head_blk(_V_COL0), cache, cache, gain, gain,
                  pl.BlockSpec((None,) + bias.shape[1:], lambda bi, h: (h, 0, 0, 0))],
        out_specs=pl.BlockSpec((None, t, NA_HEAD_DIM), lambda bi, h: (bi, 0, h)),
        out_shape=jax.ShapeDtypeStruct((b, t, NA_WIDTH), BF16),
        scratch_shapes=[pltpu.VMEM((t, NA_HEAD_DIM), BF16)] * 3,
        compiler_params=_cparams("parallel", "parallel"),
        name="neighbourhood_attention",
    )(proj3, proj3, proj3, cache_k, cache_v, qg, kg, bias)


def _softplus(x):
    return jnp.maximum(x, 0.0) + jnp.log1p(jnp.exp(-jnp.abs(x)))


def _ssd_kernel(*refs, l, with_init):
    (x_ref, b_ref, c_ref, dt_ref, cwx_ref, cwb_ref, cwc_ref, cbx_ref, cbb_ref, cbc_ref,
     alog_ref, dtbias_ref, dsk_ref) = refs[:13]
    if with_init:
        init_ref = refs[13]
        outs = refs[14:]
    else:
        outs = refs[13:]
    (y_ref, sfin_ref, xs_ref, bs_ref, cs_ref, stf_ref, stb_ref,
     cumf_ref, cumb_ref, srcf_ref, srcb_ref) = outs
    g = pl.program_id(1)
    q = SSD_CHUNK
    nc = l // q
    gw = SSD_GROUP_WIDTH
    gh = SSD_GROUP_HEADS
    pad_l = SSD_CONV // 2
    halo = 8

    win_rows = q + 2 * halo
    sel_col = lax.broadcasted_iota(jnp.int32, (q, SSD_CONV * win_rows), 1)
    sel_row = lax.broadcasted_iota(jnp.int32, (q, SSD_CONV * win_rows), 0)
    shift_mat = ((sel_col % win_rows) == sel_row + (halo - pad_l) + sel_col // win_rows).astype(BF16)

    tap_w = {id(w_ref): [jnp.broadcast_to(w_ref[j:j + 1, :], (win_rows, w_ref.shape[1])).astype(BF16)
                         for j in range(SSD_CONV)]
             for w_ref in (cwx_ref, cwb_ref, cwc_ref)}

    def conv_chunk(c, carry):
        t0 = pl.multiple_of(c * q, q)
        lo_start = pl.multiple_of(jnp.maximum(t0 - halo, 0), halo)
        hi_start = pl.multiple_of(jnp.minimum(t0 + q, l - halo), halo)
        for src, w_ref, bias_ref, dst in ((x_ref, cwx_ref, cbx_ref, xs_ref),
                                          (b_ref, cwb_ref, cbb_ref, bs_ref),
                                          (c_ref, cwc_ref, cbc_ref, cs_ref)):
            lo = jnp.where(c > 0, src[pl.ds(lo_start, halo), :], 0.0)
            hi = jnp.where(c < nc - 1, src[pl.ds(hi_start, halo), :], 0.0)
            win = jnp.concatenate([lo, src[pl.ds(t0, q), :], hi], axis=0).astype(BF16)
            taps = jnp.concatenate([win * wj for wj in tap_w[id(w_ref)]], axis=0)
            out = _silu(_dot(shift_mat, taps) + bias_ref[...])
            dst[pl.ds(t0, q), :] = out
            if dst is xs_ref:
                y_ref[pl.ds(t0, q), :] = out * dsk_ref[...]
        dt = _softplus(pltpu.roll(dt_ref[pl.ds(t0, q), :], to_lane0, axis=1) + bias_g)
        dta = dt * a_g
        cum_f = _dot_f32_rhs(tri_f, dta)
        cum_b = _dot_f32_rhs(tri_b, dta)
        cumf_ref[pl.ds(t0, q), :] = cum_f
        cumb_ref[pl.ds(t0, q), :] = cum_b
        log_dt = jnp.log(dt)
        srcf_ref[pl.ds(t0, q), :] = jnp.transpose(cum_f - log_dt)
        srcb_ref[pl.ds(t0, q), :] = jnp.transpose(cum_b - log_dt)
        return carry

    ii = lax.broadcasted_iota(jnp.int32, (q, q), 0)
    jj = lax.broadcasted_iota(jnp.int32, (q, q), 1)
    keep_f = jj <= ii
    keep_b = jj >= ii
    tri_f = keep_f.astype(BF16)
    tri_b = keep_b.astype(BF16)
    n_dt = 2 * SSD_HEADS
    assert q == n_dt
    to_lane0 = (n_dt - g * gh) % n_dt
    bias_g = pltpu.roll(jnp.broadcast_to(dtbias_ref[...], (8, n_dt)), to_lane0, axis=1)[0:1, :]
    a_g = pltpu.roll(jnp.broadcast_to(-jnp.exp(alog_ref[...]), (8, n_dt)), to_lane0, axis=1)[0:1, :]

    per_trip = 4 if nc % 4 == 0 else (2 if nc % 2 == 0 else 1)

    def conv_trip(i, carry):
        for k in range(per_trip):
            conv_chunk(i * per_trip + k, carry)
        return carry

    lax.fori_loop(0, nc // per_trip, conv_trip, 0)

    pair_w = 2 * SSD_HEAD_DIM
    lane = lax.broadcasted_iota(jnp.int32, (1, gw), 1)
    half_mask = [((lane % pair_w) // SSD_HEAD_DIM) == s for s in range(2)]
    first_head_lanes = lax.broadcasted_iota(jnp.int32, (1, pair_w), 1) < SSD_HEAD_DIM

    def scan_chunk(c, reverse):
        keep, edge, lane0 = (keep_b, 0, SSD_HEADS) if reverse else (keep_f, q - 1, 0)
        cum_ref, src_ref, st_ref = ((cumb_ref, srcb_ref, stb_ref) if reverse
                                    else (cumf_ref, srcf_ref, stf_ref))
        t0 = pl.multiple_of(c * q, q)
        xc = xs_ref[pl.ds(t0, q), :]
        bc = bs_ref[pl.ds(t0, q), :]
        cc = cs_ref[pl.ds(t0, q), :]
        cb = _dot_nt(cc.astype(BF16), bc.astype(BF16))
        bct = jnp.transpose(bc)
        x_bf = xc.astype(BF16)
        st = st_ref[...]
        st_bf = st.astype(BF16)
        zero = jnp.zeros((), BF16)
        x_half = [jnp.where(half_mask[s], x_bf, zero) for s in range(2)]
        st_half = [jnp.where(half_mask[s], st_bf, zero) for s in range(2)]
        pieces = []
        for pair in range(gh // 2):
            c0 = pair * pair_w
            lhs_y, lhs_s, carry_decay = [], [], []
            for e in (2 * pair, 2 * pair + 1):
                ln = lane0 + e
                col = jnp.broadcast_to(cum_ref[pl.ds(t0, q), ln:ln + 1], (q, q))
                src = src_ref[pl.ds(t0 + ln, 1), :]
                at_edge = cum_ref[pl.ds(t0 + edge, 1), ln:ln + 1]
                mix = jnp.exp(jnp.where(keep, col - src, -jnp.inf))
                lhs_y.append((cb * mix).astype(BF16))
                lhs_y.append((cc * jnp.exp(col)).astype(BF16))
                lhs_s.append((bct * jnp.exp(at_edge - src)).astype(BF16))
                carry_decay.append(jnp.exp(at_edge))
            rhs_y = jnp.concatenate([x_half[0][:, c0:c0 + pair_w], st_half[0][:, c0:c0 + pair_w],
                                     x_half[1][:, c0:c0 + pair_w], st_half[1][:, c0:c0 + pair_w]], axis=0)
            pieces.append(_dot(jnp.concatenate(lhs_y, axis=1), rhs_y))
            rhs_s = jnp.concatenate([x_half[0][:, c0:c0 + pair_w], x_half[1][:, c0:c0 + pair_w]], axis=0)
            keep_frac = jnp.where(first_head_lanes, carry_decay[0], carry_decay[1])
            st_ref[:, c0:c0 + pair_w] = (st[:, c0:c0 + pair_w] * keep_frac
                                         + _dot(jnp.concatenate(lhs_s, axis=1), rhs_s))
        y_ref[pl.ds(t0, q), :] = y_ref[pl.ds(t0, q), :] + jnp.concatenate(pieces, axis=1)

    for d, st_ref in enumerate((stf_ref, stb_ref)):
        if with_init:
            st_ref[...] = jnp.transpose(init_ref[d].reshape(gw, SSD_STATE))
        else:
            st_ref[...] = jnp.zeros_like(st_ref)

    def step(i, carry):
        for k in range(per_trip):
            s = i * per_trip + k
            scan_chunk(s, False)
            scan_chunk(nc - 1 - s, True)
        return carry

    lax.fori_loop(0, nc // per_trip, step, 0)
    for d, st_ref in enumerate((stf_ref, stb_ref)):
        sfin_ref[d] = jnp.transpose(st_ref[...]).reshape(gh, SSD_HEAD_DIM, SSD_STATE)


def _ssd_scan(proj3, dt_raw3, conv_w, conv_b, a_log, dt_bias, d_skip, i_odd, init=None):
    b, l, _ = proj3.shape
    gh, gw = SSD_GROUP_HEADS, SSD_GROUP_WIDTH
    n_dt = 2 * SSD_HEADS
    a_log3 = a_log.reshape(a_log.shape[0], 1, n_dt)
    dt_bias3 = dt_bias.reshape(dt_bias.shape[0], 1, n_dt)
    dsk = jnp.repeat(d_skip[i_odd].astype(F32), SSD_HEAD_DIM).reshape(1, SSD_INNER)
    conv_b3 = conv_b.reshape(conv_b.shape[0], 1, conv_b.shape[1])

    x_col0 = SSD_INNER // gw
    b_col0 = (2 * SSD_INNER) // SSD_STATE
    c_col0 = (2 * SSD_INNER + SSD_BC_WIDTH) // SSD_STATE
    cwx_col0 = 0
    cwb_col0 = SSD_INNER // SSD_STATE
    cwc_col0 = (SSD_INNER + SSD_BC_WIDTH) // SSD_STATE

    in_specs = [
        pl.BlockSpec((None, l, gw), lambda bi, g: (bi, 0, x_col0 + g)),
        pl.BlockSpec((None, l, SSD_STATE), lambda bi, g: (bi, 0, b_col0 + g)),
        pl.BlockSpec((None, l, SSD_STATE), lambda bi, g: (bi, 0, c_col0 + g)),
        pl.BlockSpec((None, l, n_dt), lambda bi, g: (bi, 0, 0)),
        pl.BlockSpec((None, SSD_CONV, gw), lambda bi, g: (i_odd, 0, cwx_col0 + g)),
        pl.BlockSpec((None, SSD_CONV, SSD_STATE), lambda bi, g: (i_odd, 0, cwb_col0 + g)),
        pl.BlockSpec((None, SSD_CONV, SSD_STATE), lambda bi, g: (i_odd, 0, cwc_col0 + g)),
        pl.BlockSpec((None, 1, gw), lambda bi, g: (i_odd, 0, cwx_col0 + g)),
        pl.BlockSpec((None, 1, SSD_STATE), lambda bi, g: (i_odd, 0, cwb_col0 + g)),
        pl.BlockSpec((None, 1, SSD_STATE), lambda bi, g: (i_odd, 0, cwc_col0 + g)),
        pl.BlockSpec((None, 1, n_dt), lambda bi, g: (i_odd, 0, 0)),
        pl.BlockSpec((None, 1, n_dt), lambda bi, g: (i_odd, 0, 0)),
        pl.BlockSpec((1, gw), lambda bi, g: (0, g)),
    ]
    args = [proj3, proj3, proj3, dt_raw3, conv_w, conv_w, conv_w,
            conv_b3, conv_b3, conv_b3, a_log3, dt_bias3, dsk]
    if init is not None:
        in_specs.append(pl.BlockSpec((None, None, 2, gh, SSD_HEAD_DIM, SSD_STATE),
                                     lambda bi, g: (bi, i_odd, 0, g, 0, 0)))
        args.append(init)
    return pl.pallas_call(
        functools.partial(_ssd_kernel, l=l, with_init=init is not None),
        grid=(b, SSD_GROUPS),
        in_specs=in_specs,
        out_specs=[
            pl.BlockSpec((None, l, gw), lambda bi, g: (bi, 0, g)),
            pl.BlockSpec((None, 2, gh, SSD_HEAD_DIM, SSD_STATE), lambda bi, g: (bi, 0, g, 0, 0)),
        ],
        out_shape=[
            jax.ShapeDtypeStruct((b, l, SSD_INNER), F32),
            jax.ShapeDtypeStruct((b, 2, SSD_HEADS, SSD_HEAD_DIM, SSD_STATE), F32),
        ],
        scratch_shapes=[
            pltpu.VMEM((l, gw), F32),
            pltpu.VMEM((l, SSD_STATE), F32),
            pltpu.VMEM((l, SSD_STATE), F32),
            pltpu.VMEM((SSD_STATE, gw), F32),
            pltpu.VMEM((SSD_STATE, gw), F32),
        ] + [pltpu.VMEM((l, n_dt), F32)] * 4,
        compiler_params=_cparams("parallel", "parallel"),
        name="ssd_scan",
    )(*args)


def _gate_norm_kernel(y_ref, z_ref, g_ref, o_ref):
    y = y_ref[...] * _silu(z_ref[...])
    o = y * lax.rsqrt(jnp.mean(y * y, axis=-1, keepdims=True) + NORM_EPS) * g_ref[...]
    o_ref[...] = o.astype(o_ref.dtype)


def _gate_norm(y, proj, norm_g, i_odd):
    m = y.shape[0]
    tm = ROW_TILE
    g3 = norm_g.reshape(norm_g.shape[0], 1, SSD_INNER)
    return pl.pallas_call(
        _gate_norm_kernel,
        grid=(m // tm,),
        in_specs=[
            pl.BlockSpec((tm, SSD_INNER), lambda i: (i, 0)),
            pl.BlockSpec((tm, SSD_INNER), lambda i: (i, 0)),
            pl.BlockSpec((None, 1, SSD_INNER), lambda i: (i_odd, 0, 0)),
        ],
        out_specs=pl.BlockSpec((tm, SSD_INNER), lambda i: (i, 0)),
        out_shape=jax.ShapeDtypeStruct((m, SSD_INNER), BF16),
        compiler_params=_cparams("parallel"),
        name="gate_norm",
    )(y, proj, g3)


def _expert_kernel(x_ref, wg_ref, wu_ref, wd_ref, gate_ref, o_ref, acc_ref):
    f = pl.program_id(2)
    x = x_ref[...]
    hg = _dot(x, wg_ref[...].astype(BF16))
    hu = _dot(x, wu_ref[...].astype(BF16))
    hdn = (_silu(hg) * hu).astype(BF16)

    @pl.when(f == 0)
    def _():
        acc_ref[...] = jnp.zeros_like(acc_ref)

    acc_ref[...] += _dot(hdn, wd_ref[...].astype(BF16))

    @pl.when(f == pl.num_programs(2) - 1)
    def _():
        o_ref[...] = (acc_ref[...] * gate_ref[...]).astype(o_ref.dtype)


def _expert_ffn(xe, gate, w_gate, w_up, w_down, layer):
    e, r, _ = xe.shape
    tr = min(r, 1024)
    tf = 256
    return pl.pallas_call(
        _expert_kernel,
        grid=(e, r // tr, EXPERT_FF // tf),
        in_specs=[
            pl.BlockSpec((None, tr, D_MODEL), lambda ei, ri, f: (ei, ri, 0)),
            pl.BlockSpec((None, None, D_MODEL, tf), lambda ei, ri, f: (layer, ei, 0, f)),
            pl.BlockSpec((None, None, D_MODEL, tf), lambda ei, ri, f: (layer, ei, 0, f)),
            pl.BlockSpec((None, None, tf, D_MODEL), lambda ei, ri, f: (layer, ei, f, 0)),
            pl.BlockSpec((None, tr, 1), lambda ei, ri, f: (ei, ri, 0)),
        ],
        out_specs=pl.BlockSpec((None, tr, D_MODEL), lambda ei, ri, f: (ei, ri, 0)),
        out_shape=jax.ShapeDtypeStruct((e, r, D_MODEL), BF16),
        scratch_shapes=[pltpu.VMEM((tr, D_MODEL), F32)],
        compiler_params=_cparams("parallel", "parallel", "arbitrary"),
        name="expert_ffn",
    )(xe, w_gate, w_up, w_down, gate)


def _combine_kernel(idx_ref, ye_ref, x_ref, gate_ref, o_ref, *, n, ec, cap):
    step = pl.program_id(2)
    token = lax.broadcasted_iota(jnp.int32, (n, cap), 0)
    onehot = jnp.concatenate(
        [jnp.where(token == idx_ref[k], 1.0, 0.0).astype(BF16) for k in range(ec)], axis=1)

    @pl.when(step == 0)
    def _():
        o_ref[...] = jnp.zeros_like(o_ref)

    o_ref[...] += _dot(onehot, ye_ref[...].reshape(ec * cap, ye_ref.shape[-1]))

    @pl.when(step == pl.num_programs(2) - 1)
    def _():
        o_ref[...] = x_ref[...] + gate_ref[...] * o_ref[...]


def _expert_combine(ye, idx, x, mod4, gate_idx, first_row, shared_cond):
    n_req, n_exp, cap = idx.shape
    m = x.shape[0]
    n = m // n_req
    ec = max(1, min(n_exp, 512 // cap))
    tn = D_MODEL // 2
    idx4 = idx.reshape(n_req, n_exp, 1, cap)
    cond_row = (lambda b: first_row) if shared_cond else (lambda b: first_row + b)
    return pl.pallas_call(
        functools.partial(_combine_kernel, n=n, ec=ec, cap=cap),
        grid=(n_req, D_MODEL // tn, n_exp // ec),
        in_specs=[
            pl.BlockSpec((None, ec, 1, cap), lambda b, j, s: (b, s, 0, 0)),
            pl.BlockSpec((ec, cap, tn), lambda b, j, s: (s, b, j)),
            pl.BlockSpec((n, tn), lambda b, j, s: (b, j)),
            pl.BlockSpec((None, None, 1, tn), lambda b, j, s: (cond_row(b), gate_idx, 0, j)),
        ],
        out_specs=pl.BlockSpec((n, tn), lambda b, j, s: (b, j)),
        out_shape=jax.ShapeDtypeStruct((m, D_MODEL), F32),
        compiler_params=_cparams("parallel", "parallel", "arbitrary"),
        name="expert_combine",
    )(idx4, ye, x, mod4)


def _prefix_count(mask_bf, strict_upper):
    n = mask_bf.shape[1]
    carry = jnp.zeros((mask_bf.shape[0], 1), F32)
    out = []
    for k in range(n // PREFIX_BLOCK):
        blk = mask_bf[:, k * PREFIX_BLOCK:(k + 1) * PREFIX_BLOCK]
        out.append(_dot(blk, strict_upper) + carry)
        carry = carry + jnp.sum(blk.astype(F32), axis=-1, keepdims=True)
    return jnp.concatenate(out, axis=1)


def _router_kernel(lg_ref, o_ref, slot_ref, feat_ref, *, n_req, n, cap):
    step = pl.program_id(0)
    n_exp = N_EXPERTS

    @pl.when(step == 0)
    def _():
        tok = lax.broadcasted_iota(jnp.int32, (n, ROUTE_LANES), 0)
        tlane = lax.broadcasted_iota(jnp.int32, (n, ROUTE_LANES), 1)
        radix_bits = ROUTE_TOKEN_RADIX.bit_length() - 1
        tok_feat = (jnp.where(tlane == ROUTE_TOKEN_LANE, tok >> radix_bits, 0)
                    + jnp.where(tlane == ROUTE_TOKEN_LANE + 1, tok & (ROUTE_TOKEN_RADIX - 1), 0)).astype(F32)
        is_expert = tlane < n_exp
        aff_rows = []
        for r in range(n_req):
            x = jnp.where(is_expert, lg_ref[pl.ds(r * n, n), :], -jnp.inf)
            e = jnp.exp(x - jnp.max(x, axis=-1, keepdims=True))
            aff = e / jnp.sum(e, axis=-1, keepdims=True)
            feat = tok_feat
            for p, part in enumerate(_split3(aff)):
                piece = part.astype(F32)
                feat = feat + (pltpu.roll(piece, p * n_exp, axis=1) if p else piece)
            feat_ref[r] = feat.astype(BF16)
            aff_rows.append(jnp.transpose(aff)[:n_exp])
        aff_t = jnp.concatenate(aff_rows, axis=0)
        bits = pltpu.bitcast(aff_t, jnp.int32)

        def bisect(_, carry):
            lo, hi = carry
            mid = lax.shift_right_arithmetic(lo + hi, 1)
            count = jnp.sum(jnp.where(bits >= mid, 1.0, 0.0), axis=-1, keepdims=True)
            enough = count >= cap
            return jnp.where(enough, mid, lo), jnp.where(enough, hi, mid)

        rows = n_req * n_exp
        lo0 = jnp.zeros((rows, 1), jnp.int32)
        hi0 = jnp.full((rows, 1), 0x3F800001, jnp.int32)
        kth, _ = lax.fori_loop(0, 31, bisect, (lo0, hi0))
        above = bits > kth
        tied = bits == kth
        blk_i = lax.broadcasted_iota(jnp.int32, (PREFIX_BLOCK, PREFIX_BLOCK), 0)
        blk_j = lax.broadcasted_iota(jnp.int32, (PREFIX_BLOCK, PREFIX_BLOCK), 1)
        strict_upper = (blk_i < blk_j).astype(BF16)
        need = cap - jnp.sum(jnp.where(above, 1.0, 0.0), axis=-1, keepdims=True)
        tie_rank = _prefix_count(jnp.where(tied, 1.0, 0.0).astype(BF16), strict_upper)
        chosen = jnp.logical_or(above, jnp.logical_and(tied, tie_rank < need))
        slot = _prefix_count(jnp.where(chosen, 1.0, 0.0).astype(BF16), strict_upper)
        slot_ref[...] = jnp.where(chosen, slot, -1.0)

    slot_id = lax.broadcasted_iota(jnp.int32, (cap, n), 0).astype(F32)
    for k in range(ROUTE_ROWS_PER_STEP):
        row = step * ROUTE_ROWS_PER_STEP + k
        onehot_t = jnp.where(slot_id == slot_ref[pl.ds(row, 1), :], 1.0, 0.0).astype(BF16)
        o_ref[k] = _dot(onehot_t, feat_ref[row // n_exp])


def _route(logits, n_req):
    m = logits.shape[0]
    n = m // n_req
    cap = EC_CAPACITY_FACTOR * n // N_EXPERTS
    rows = n_req * N_EXPERTS
    assert n % PREFIX_BLOCK == 0 and rows % ROUTE_ROWS_PER_STEP == 0
    rec, slot = pl.pallas_call(
        functools.partial(_router_kernel, n_req=n_req, n=n, cap=cap),
        grid=(rows // ROUTE_ROWS_PER_STEP,),
        in_specs=[pl.BlockSpec((m, ROUTE_LANES), lambda i: (0, 0))],
        out_specs=[pl.BlockSpec((ROUTE_ROWS_PER_STEP, cap, ROUTE_LANES), lambda i: (i, 0, 0)),
                   pl.BlockSpec((rows, n), lambda i: (0, 0))],
        out_shape=[jax.ShapeDtypeStruct((rows, cap, ROUTE_LANES), F32),
                   jax.ShapeDtypeStruct((rows, n), F32)],
        scratch_shapes=[pltpu.VMEM((n_req, n, ROUTE_LANES), BF16)],
        compiler_params=_cparams("arbitrary"),
        name="router",
    )(logits)
    rec = rec.reshape(n_req, N_EXPERTS, cap, ROUTE_LANES)
    idx = (ROUTE_TOKEN_RADIX * rec[..., ROUTE_TOKEN_LANE] + rec[..., ROUTE_TOKEN_LANE + 1]).astype(jnp.int32)
    pieces = rec[..., :3 * N_EXPERTS].reshape(n_req, N_EXPERTS, cap, 3, N_EXPERTS).sum(axis=3)
    own = jnp.eye(N_EXPERTS, dtype=F32)[None, :, None, :]
    gate = jnp.sum(pieces * own, axis=3)
    return idx, gate, slot


def _gather_kernel(slot_ref, h_ref, o_ref, *, n, ec, cap):
    slot_id = lax.broadcasted_iota(jnp.int32, (cap, n), 0).astype(F32)
    onehot = jnp.concatenate(
        [jnp.where(slot_id == slot_ref[k], 1.0, 0.0).astype(BF16) for k in range(ec)], axis=0)
    o_ref[...] = _dot(onehot, h_ref[...]).reshape(o_ref.shape).astype(o_ref.dtype)


def _gather_rows(h, slot, n_req):
    m, d = h.shape
    n = m // n_req
    cap = EC_CAPACITY_FACTOR * n // N_EXPERTS
    ec = max(1, min(N_EXPERTS, 2048 // cap))
    tn = d // 2
    slot4 = slot.reshape(n_req, N_EXPERTS, 1, n)
    return pl.pallas_call(
        functools.partial(_gather_kernel, n=n, ec=ec, cap=cap),
        grid=(n_req, d // tn, N_EXPERTS // ec),
        in_specs=[pl.BlockSpec((None, ec, 1, n), lambda b, j, s: (b, s, 0, 0)),
                  pl.BlockSpec((n, tn), lambda b, j, s: (b, j))],
        out_specs=pl.BlockSpec((ec, cap, tn), lambda b, j, s: (s, b, j)),
        out_shape=jax.ShapeDtypeStruct((N_EXPERTS, n_req * cap, d), h.dtype),
        compiler_params=_cparams("parallel", "parallel", "arbitrary"),
        name="gather_rows",
    )(slot4, h)


def _expert_choice(h, logits, x, n_req, w_gate, w_up, w_down, layer, mod4, gate_idx, first_row, shared_cond):
    idx, gate, slot = _route(logits, n_req)
    cap = idx.shape[2]
    gate = jnp.transpose(gate, (1, 0, 2)).reshape(N_EXPERTS, n_req * cap, 1)
    xe = _gather_rows(h, slot, n_req)
    ye = _expert_ffn(xe, gate, w_gate, w_up, w_down, layer)
    return _expert_combine(ye, idx, x, mod4, gate_idx, first_row, shared_cond)


class _Stream:
    def __init__(self, x3, first_row, shared_cond):
        self.batch, self.n, _ = x3.shape
        self.x = x3.reshape(self.batch * self.n, D_MODEL)
        self.first_row = first_row
        self.shared_cond = shared_cond
        self.rows_per_cond = self.batch * self.n if shared_cond else self.n

    def mod_args(self):
        return dict(rows_per_cond=self.rows_per_cond, first_row=self.first_row)


def kernel(x_prompt, x_sample, cache_na_k, cache_na_v, state_ssd, c, c_ctx, ada_w, ada_b, norm1_g, norm2_g,
           router_w, exp_w_gate, exp_w_up, exp_w_down, ab_w_in, pool_w, pool_scale, na_q_norm, na_k_norm,
           na_rpb, ab_w_out, ssd_w_in, ssd_conv_w, ssd_conv_b, ssd_a_log, ssd_dt_bias, ssd_d, ssd_norm_g,
           ssd_w_out):
    depth = ada_w.shape[0]
    dec_batch = x_sample.shape[0]
    latent = _Stream(x_sample, 0, shared_cond=False)
    prompt = _Stream(x_prompt, dec_batch, shared_cond=True)
    cond = jnp.concatenate(
        [c, c_ctx[None, :], jnp.zeros((MOD_ROWS - dec_batch - 1, D_MODEL), F32)], axis=0)

    router_pad = jnp.pad(router_w, ((0, 0), (0, 0), (0, ROUTE_LANES - N_EXPERTS)))

    new_k, new_v, new_s = [], [], []
    for layer in range(depth):
        i = layer // 2
        mod4 = _modulation(cond, ada_w, ada_b, layer).reshape(MOD_ROWS, N_MOD, 1, D_MODEL)
        for s in (prompt, latent):
            margs = s.mod_args()
            h = _norm_mod(s.x, norm1_g, mod4, layer, 0, 1, **margs)[0]
            tm = min(s.x.shape[0], 2048)
            if layer % 2 == 0:
                proj = _matmul([h], ab_w_in, i, tm=tm, tn=512, n_blocks=ab_w_in.shape[2] // 512)
                proj3 = proj.reshape(s.batch, s.n, -1)
                pooled = _pool_mixer(proj3, pool_w, pool_scale, i)
                if s is prompt:
                    att, k_new, v_new = _ctx_attention(proj3, na_q_norm, na_k_norm, i)
                    new_k.append(k_new)
                    new_v.append(v_new)
                else:
                    att = _neighbourhood_attention(proj3, cache_na_k, cache_na_v, na_q_norm, na_k_norm,
                                                   na_rpb, i)
                s.x = _matmul([pooled.reshape(-1, POOL_WIDTH), att.reshape(-1, NA_WIDTH)], ab_w_out, i,
                              tm=tm, tn=512, n_blocks=D_MODEL // 512, res=s.x, mod4=mod4, gate_idx=2,
                              **margs)
            else:
                proj = _matmul([h], ssd_w_in, i, tm=tm, tn=512, n_blocks=SSD_MAIN_DIM // 512)
                dt_raw = _matmul([h], ssd_w_in, i, tm=tm, tn=2 * SSD_HEADS, n_blocks=1,
                                 col_block0=SSD_MAIN_DIM // (2 * SSD_HEADS))
                proj3 = proj.reshape(s.batch, s.n, SSD_MAIN_DIM)
                dt3 = dt_raw.reshape(s.batch, s.n, 2 * SSD_HEADS)
                init = None if s is prompt else state_ssd
                y, s_fin = _ssd_scan(proj3, dt3, ssd_conv_w, ssd_conv_b, ssd_a_log, ssd_dt_bias, ssd_d, i,
                                     init=init)
                if s is prompt:
                    new_s.append(s_fin[:, None])
                yn = _gate_norm(y.reshape(-1, SSD_INNER), proj, ssd_norm_g, i)
                s.x = _matmul([yn], ssd_w_out, i, tm=min(tm, 1024), tn=512, n_blocks=D_MODEL // 512,
                              res=s.x, mod4=mod4, gate_idx=2, **margs)
            h2, logits = _norm_mod(s.x, norm2_g, mod4, layer, 3, 4, router_w=router_pad, **margs)
            s.x = _expert_choice(h2, logits, s.x, s.batch, exp_w_gate, exp_w_up, exp_w_down, layer,
                                 mod4, 5, s.first_row, s.shared_cond)

    y_prompt = prompt.x.reshape(x_prompt.shape)
    y_sample = latent.x.reshape(x_sample.shape)
    return (y_prompt, y_sample, jnp.concatenate(new_k, axis=1), jnp.concatenate(new_v, axis=1),
            jnp.concatenate(new_s, axis=1))
```

```python
import functools

import numpy as np
import jax
import jax.numpy as jnp
from jax import lax
from jax.experimental import pallas as pl
from jax.experimental.pallas import tpu as pltpu

F32 = jnp.float32
BF16 = jnp.bfloat16

D_MODEL = 2048
GRID_W = 64
POOL_WIDTH = 1024
POOL_GROUPS = 4
POOL_GROUP_DIM = 256
POOL_WINDOWS = (2, 4, 8, 16)
NA_WIDTH = 1024
NA_HEAD_DIM = 128
NA_HEADS = 8
WIN_H = 8
WIN_W = 16
SSD_INNER = 4096
SSD_HEAD_DIM = 64
SSD_HEADS = 64
SSD_GROUPS = 8
SSD_GROUP_HEADS = SSD_HEADS // SSD_GROUPS
SSD_GROUP_WIDTH = SSD_GROUP_HEADS * SSD_HEAD_DIM
SSD_STATE = 128
SSD_CONV = 4
SSD_CHUNK = 128
SSD_BC_WIDTH = SSD_GROUPS * SSD_STATE
SSD_MAIN_DIM = 2 * SSD_INNER + 2 * SSD_BC_WIDTH
N_EXPERTS = 16
EXPERT_FF = 1024
EC_CAPACITY_FACTOR = 2
NORM_EPS = 1e-6
N_MOD = 6
MOD_ROWS = 16
ROUTE_LANES = 128
ROUTE_TOKEN_LANE = 3 * N_EXPERTS
ROUTE_TOKEN_RADIX = 64
ROUTE_ROWS_PER_STEP = 8
PREFIX_BLOCK = 256

VMEM_LIMIT_BYTES = 56 * 1024 * 1024
ROW_TILE = 256


def _cparams(*sem):
    return pltpu.CompilerParams(dimension_semantics=sem, vmem_limit_bytes=VMEM_LIMIT_BYTES)


def _silu(x):
    return x * (1.0 / (1.0 + jnp.exp(-x)))


def _split2(x):
    hi = x.astype(BF16)
    lo = (x - hi.astype(F32)).astype(BF16)
    return hi, lo


def _split3(x):
    hi = x.astype(BF16)
    r = x - hi.astype(F32)
    mid = r.astype(BF16)
    lo = (r - mid.astype(F32)).astype(BF16)
    return hi, mid, lo


def _dot(a, b):
    return jnp.dot(a, b, preferred_element_type=F32)


def _dot_nt(a, b):
    return lax.dot_general(a, b, (((1,), (1,)), ((), ())), preferred_element_type=F32)


def _dot_f32_rhs(m, x):
    hi, mid, lo = _split3(x)
    return _dot(m, hi) + _dot(m, mid) + _dot(m, lo)


def _dot_split(a, b):
    a_hi, a_lo = _split2(a)
    b_hi, b_lo = _split2(b)
    return _dot(a_hi, b_hi) + _dot(a_lo, b_hi) + _dot(a_hi, b_lo)


def _mod_kernel(c_ref, w_ref, b_ref, o_ref):
    s = _silu(c_ref[...])
    o_ref[...] = _dot_split(s, w_ref[...]) + b_ref[...]


def _modulation(cond, ada_w, ada_b, layer):
    tn = 768
    n = N_MOD * D_MODEL
    ada_b3 = ada_b.reshape(ada_b.shape[0], 1, n)
    return pl.pallas_call(
        _mod_kernel,
        grid=(n // tn,),
        in_specs=[
            pl.BlockSpec((MOD_ROWS, D_MODEL), lambda j: (0, 0)),
            pl.BlockSpec((None, D_MODEL, tn), lambda j: (layer, 0, j)),
            pl.BlockSpec((None, 1, tn), lambda j: (layer, 0, j)),
        ],
        out_specs=pl.BlockSpec((MOD_ROWS, tn), lambda j: (0, j)),
        out_shape=jax.ShapeDtypeStruct((MOD_ROWS, n), F32),
        compiler_params=_cparams("parallel"),
        name="modulation",
    )(cond, ada_w, ada_b3)


def _mod_spec(which, rows_per_cond, first_row, tm):
    return pl.BlockSpec((None, None, 1, D_MODEL),
                        lambda i, *_: (first_row + (i * tm) // rows_per_cond, which, 0, 0))


def _norm_mod_kernel(x_ref, g_ref, sh_ref, sc_ref, *rest, with_router):
    x = x_ref[...]
    y = x * lax.rsqrt(jnp.mean(x * x, axis=-1, keepdims=True) + NORM_EPS) * g_ref[...]
    h = y * (1.0 + sc_ref[...]) + sh_ref[...]
    if with_router:
        rw_ref, h_ref, lg_ref = rest
        lg_ref[...] = _dot_split(h, rw_ref[...])
    else:
        (h_ref,) = rest
    h_ref[...] = h.astype(BF16)


def _norm_mod(x, g, mod4, layer, shift_idx, scale_idx, rows_per_cond, first_row, router_w=None):
    m = x.shape[0]
    tm = 2 * ROW_TILE
    g3 = g.reshape(g.shape[0], 1, D_MODEL)
    in_specs = [
        pl.BlockSpec((tm, D_MODEL), lambda i: (i, 0)),
        pl.BlockSpec((None, 1, D_MODEL), lambda i: (layer, 0, 0)),
        _mod_spec(shift_idx, rows_per_cond, first_row, tm),
        _mod_spec(scale_idx, rows_per_cond, first_row, tm),
    ]
    args = [x, g3, mod4, mod4]
    out_specs = [pl.BlockSpec((tm, D_MODEL), lambda i: (i, 0))]
    out_shape = [jax.ShapeDtypeStruct((m, D_MODEL), BF16)]
    if router_w is not None:
        in_specs.append(pl.BlockSpec((None, D_MODEL, ROUTE_LANES), lambda i: (layer, 0, 0)))
        args.append(router_w)
        out_specs.append(pl.BlockSpec((tm, ROUTE_LANES), lambda i: (i, 0)))
        out_shape.append(jax.ShapeDtypeStruct((m, ROUTE_LANES), F32))
    return pl.pallas_call(
        functools.partial(_norm_mod_kernel, with_router=router_w is not None),
        grid=(m // tm,),
        in_specs=in_specs,
        out_specs=out_specs,
        out_shape=out_shape,
        compiler_params=_cparams("parallel"),
        name="norm_mod",
    )(*args)


def _matmul_kernel(*refs, n_a, k_sizes, with_res):
    a_refs = refs[:n_a]
    w_ref = refs[n_a]
    if with_res:
        res_ref, gate_ref, o_ref = refs[n_a + 1:]
    else:
        (o_ref,) = refs[n_a + 1:]
    w = w_ref[...].astype(BF16)
    acc = None
    k0 = 0
    for a_ref, k in zip(a_refs, k_sizes):
        part = _dot(a_ref[...], w[k0:k0 + k])
        acc = part if acc is None else acc + part
        k0 += k
    if with_res:
        acc = res_ref[...] + gate_ref[...] * acc
    o_ref[...] = acc.astype(o_ref.dtype)


def _matmul(a_list, w, layer, *, tm, tn, n_blocks, col_block0=0, out_dtype=F32,
            res=None, mod4=None, gate_idx=None, rows_per_cond=None, first_row=None):
    m = a_list[0].shape[0]
    k_sizes = tuple(a.shape[1] for a in a_list)
    k_total = sum(k_sizes)
    in_specs = [pl.BlockSpec((tm, k), lambda i, j: (i, 0)) for k in k_sizes]
    in_specs.append(pl.BlockSpec((None, k_total, tn), lambda i, j: (layer, 0, col_block0 + j)))
    args = list(a_list) + [w]
    if res is not None:
        in_specs.append(pl.BlockSpec((tm, tn), lambda i, j: (i, j)))
        in_specs.append(pl.BlockSpec((None, None, 1, tn),
                                     lambda i, j: (first_row + (i * tm) // rows_per_cond, gate_idx, 0, j)))
        args += [res, mod4]
    return pl.pallas_call(
        functools.partial(_matmul_kernel, n_a=len(a_list), k_sizes=k_sizes, with_res=res is not None),
        grid=(m // tm, n_blocks),
        in_specs=in_specs,
        out_specs=pl.BlockSpec((tm, tn), lambda i, j: (i, j)),
        out_shape=jax.ShapeDtypeStruct((m, n_blocks * tn), out_dtype),
        compiler_params=_cparams("parallel", "arbitrary"),
        name="matmul",
    )(*args)


def _pool_kernel(u_ref, w_ref, s_ref, o_ref, *, n):
    t = lax.broadcasted_iota(jnp.int32, (n, 1), 0)
    for gi, win in enumerate(POOL_WINDOWS):
        lanes = slice(gi * POOL_GROUP_DIM, (gi + 1) * POOL_GROUP_DIM)
        u = u_ref[:, lanes]
        half = win // 2
        total = u
        for d in range(-half, win - half):
            if d == 0:
                continue
            shifted = pltpu.roll(u, (-d) % n, axis=0)
            valid = jnp.logical_and(t + d >= 0, t + d < n)
            total = total + jnp.where(valid, shifted, 0.0)
        cnt = (jnp.minimum(t - half + win, n) - jnp.maximum(t - half, 0)).astype(F32)
        pooled = total / cnt - u
        out = _dot(pooled.astype(BF16), w_ref[gi].astype(BF16)) * s_ref[:, lanes]
        o_ref[:, lanes] = out.astype(o_ref.dtype)


def _pool_mixer(proj3, pool_w, pool_scale, i_even):
    b, n, _ = proj3.shape
    scale3 = pool_scale.reshape(pool_scale.shape[0], 1, POOL_WIDTH)
    return pl.pallas_call(
        functools.partial(_pool_kernel, n=n),
        grid=(b,),
        in_specs=[
            pl.BlockSpec((None, n, POOL_WIDTH), lambda bi: (bi, 0, 0)),
            pl.BlockSpec((None, POOL_GROUPS, POOL_GROUP_DIM, POOL_GROUP_DIM), lambda bi: (i_even, 0, 0, 0)),
            pl.BlockSpec((None, 1, POOL_WIDTH), lambda bi: (i_even, 0, 0)),
        ],
        out_specs=pl.BlockSpec((None, n, POOL_WIDTH), lambda bi: (bi, 0, 0)),
        out_shape=jax.ShapeDtypeStruct((b, n, POOL_WIDTH), BF16),
        compiler_params=_cparams("parallel"),
        name="pool_mixer",
    )(proj3, pool_w, scale3)


def _head_norm(x, g):
    return x * lax.rsqrt(jnp.mean(x * x, axis=-1, keepdims=True) + NORM_EPS) * g


_Q_COL0 = POOL_WIDTH // NA_HEAD_DIM
_K_COL0 = (POOL_WIDTH + NA_WIDTH) // NA_HEAD_DIM
_V_COL0 = (POOL_WIDTH + 2 * NA_WIDTH) // NA_HEAD_DIM


def _ctx_attn_kernel(q_ref, k_ref, v_ref, qg_ref, kg_ref, o_ref, kn_ref, vn_ref):
    for h in range(NA_HEADS):
        lanes = slice(h * NA_HEAD_DIM, (h + 1) * NA_HEAD_DIM)
        qn = _head_norm(q_ref[:, lanes], qg_ref[...])
        kn = _head_norm(k_ref[:, lanes], kg_ref[...])
        v = v_ref[:, lanes]
        kn_ref[h] = kn
        vn_ref[h] = v
        s = _dot_nt(qn.astype(BF16), kn.astype(BF16)) * (NA_HEAD_DIM ** -0.5)
        m = jnp.max(s, axis=-1, keepdims=True)
        p = jnp.exp(s - m)
        denom = jnp.sum(p, axis=-1, keepdims=True)
        o = _dot(p.astype(BF16), v.astype(BF16)) / denom
        o_ref[:, lanes] = o.astype(o_ref.dtype)


def _ctx_attention(proj3, q_norm, k_norm, i_even):
    b, n, _ = proj3.shape
    qg = q_norm.reshape(q_norm.shape[0], 1, NA_HEAD_DIM)
    kg = k_norm.reshape(k_norm.shape[0], 1, NA_HEAD_DIM)
    width_blocks = NA_WIDTH // NA_HEAD_DIM
    head_blk = lambda col0: pl.BlockSpec((None, n, NA_WIDTH), lambda bi: (bi, 0, col0 // width_blocks))
    gain = pl.BlockSpec((None, 1, NA_HEAD_DIM), lambda bi: (i_even, 0, 0))
    cache = pl.BlockSpec((None, None, NA_HEADS, n, NA_HEAD_DIM), lambda bi: (bi, 0, 0, 0, 0))
    return pl.pallas_call(
        _ctx_attn_kernel,
        grid=(b,),
        in_specs=[head_blk(_Q_COL0), head_blk(_K_COL0), head_blk(_V_COL0), gain, gain],
        out_specs=[pl.BlockSpec((None, n, NA_WIDTH), lambda bi: (bi, 0, 0)), cache, cache],
        out_shape=[
            jax.ShapeDtypeStruct((b, n, NA_WIDTH), BF16),
            jax.ShapeDtypeStruct((b, 1, NA_HEADS, n, NA_HEAD_DIM), F32),
            jax.ShapeDtypeStruct((b, 1, NA_HEADS, n, NA_HEAD_DIM), F32),
        ],
        compiler_params=_cparams("parallel"),
        name="ctx_attention",
    )(proj3, proj3, proj3, qg, kg)


NA_BAND = 4
NA_BAND_KEY_ROWS = WIN_H + NA_BAND - 1


def _na_band_key_start(r0, rows):
    return np.clip(r0 - WIN_H // 2, 0, rows - NA_BAND_KEY_ROWS)


def _na_bias_table(rpb, rows):
    col = np.arange(GRID_W)[:, None]
    kc = np.arange(GRID_W)[None, :]
    wstart = np.clip(col - WIN_W // 2, 0, GRID_W - WIN_W)
    inside = (kc >= wstart) & (kc < wstart + WIN_W)
    rel = np.clip(kc - col + WIN_W - 1, 0, 2 * WIN_W - 2)
    a = np.arange(NA_BAND)[:, None]
    kr = np.arange(NA_BAND_KEY_ROWS)[None, :]
    ridx, row_ok = [], []
    for r0 in (0, NA_BAND, rows - NA_BAND):
        r = r0 + a
        sr = np.clip(r - WIN_H // 2, 0, rows - WIN_H)
        krow = _na_band_key_start(r0, rows) + kr
        row_ok.append((krow >= sr) & (krow < sr + WIN_H))
        ridx.append(np.clip(krow - r + WIN_H - 1, 0, 2 * WIN_H - 2))
    ridx = np.stack(ridx)
    ok = np.stack(row_ok)[:, :, None, :, None] & inside[None, None, :, None, :]
    row_sel = np.eye(2 * WIN_H - 1, dtype=np.float32)[ridx]
    col_sel = np.eye(2 * WIN_W - 1, dtype=np.float32)[rel]
    tab = jnp.einsum('vaki,hij,cdj->hvackd', row_sel, rpb, col_sel, precision=lax.Precision.HIGHEST)
    tab = jnp.where(ok[None], tab, -jnp.inf)
    return tab.reshape(rpb.shape[0], 3, NA_BAND * GRID_W, NA_BAND_KEY_ROWS * GRID_W)


def _na_kernel(q_ref, k_ref, v_ref, kc_ref, vc_ref, qg_ref, kg_ref, bias_ref, o_ref,
               qs_ref, ks_ref, vs_ref, *, rows):
    qs_ref[...] = _head_norm(q_ref[...], qg_ref[...]).astype(BF16)
    ks_ref[...] = _head_norm(k_ref[...], kg_ref[...]).astype(BF16)
    vs_ref[...] = v_ref[...].astype(BF16)
    k_ctx = kc_ref[...].astype(BF16)
    v_ctx = vc_ref[...].astype(BF16)
    scale = NA_HEAD_DIM ** -0.5
    n_bands = rows // NA_BAND
    n_q = NA_BAND * GRID_W
    n_loc = NA_BAND_KEY_ROWS * GRID_W

    def band(i):
        r0 = i * NA_BAND
        key_row0 = jnp.clip(r0 - WIN_H // 2, 0, rows - NA_BAND_KEY_ROWS)
        variant = jnp.where(i == 0, 0, jnp.where(i == n_bands - 1, 2, 1))
        q0 = pl.multiple_of(r0 * GRID_W, n_q)
        q_b = qs_ref[pl.ds(q0, n_q), :]
        kstart = pl.multiple_of(key_row0 * GRID_W, GRID_W)
        k_blk = ks_ref[pl.ds(kstart, n_loc), :]
        v_blk = vs_ref[pl.ds(kstart, n_loc), :]
        s_loc = _dot_nt(q_b, k_blk) * scale + bias_ref[variant]
        s_ctx = _dot_nt(q_b, k_ctx) * scale
        m = jnp.maximum(jnp.max(s_loc, axis=-1, keepdims=True), jnp.max(s_ctx, axis=-1, keepdims=True))
        p_loc = jnp.exp(s_loc - m)
        p_ctx = jnp.exp(s_ctx - m)
        denom = jnp.sum(p_loc, axis=-1, keepdims=True) + jnp.sum(p_ctx, axis=-1, keepdims=True)
        o = (_dot(p_loc.astype(BF16), v_blk) + _dot(p_ctx.astype(BF16), v_ctx)) / denom
        o_ref[pl.ds(q0, n_q), :] = o.astype(o_ref.dtype)

    per_trip = 2 if n_bands % 2 == 0 else 1

    def trip(i, carry):
        for k in range(per_trip):
            band(i * per_trip + k)
        return carry

    lax.fori_loop(0, n_bands // per_trip, trip, 0)


def _neighbourhood_attention(proj3, cache_k, cache_v, q_norm, k_norm, rpb, i_even):
    b, t, _ = proj3.shape
    rows = t // GRID_W
    assert rows % NA_BAND == 0 and rows >= NA_BAND_KEY_ROWS + 1, rows
    n_ctx = cache_k.shape[3]
    bias = _na_bias_table(rpb[i_even], rows)
    qg = q_norm.reshape(q_norm.shape[0], 1, NA_HEAD_DIM)
    kg = k_norm.reshape(k_norm.shape[0], 1, NA_HEAD_DIM)
    head_blk = lambda col0: pl.BlockSpec((None, t, NA_HEAD_DIM), lambda bi, h: (bi, 0, col0 + h))
    gain = pl.BlockSpec((None, 1, NA_HEAD_DIM), lambda bi, h: (i_even, 0, 0))
    cache = pl.BlockSpec((None, None, None, n_ctx, NA_HEAD_DIM), lambda bi, h: (bi, i_even, h, 0, 0))
    return pl.pallas_call(
        functools.partial(_na_kernel, rows=rows),
        grid=(b, NA_HEADS),
        in_specs=[head_blk(_Q_COL0), head_blk(_K_COL0), head_blk(_V_COL0), cache, cache, gain, gain,
                  pl.BlockSpec((None,) + bias.shape[1:], lambda bi, h: (h, 0, 0, 0))],
        out_specs=pl.BlockSpec((None, t, NA_HEAD_DIM), lambda bi, h: (bi, 0, h)),
        out_shape=jax.ShapeDtypeStruct((b, t, NA_WIDTH), BF16),
        scratch_shapes=[pltpu.VMEM((t, NA_HEAD_DIM), BF16)] * 3,
        compiler_params=_cparams("parallel", "parallel"),
        name="neighbourhood_attention",
    )(proj3, proj3, proj3, cache_k, cache_v, qg, kg, bias)


def _softplus(x):
    return jnp.maximum(x, 0.0) + jnp.log1p(jnp.exp(-jnp.abs(x)))


def _ssd_kernel(*refs, l, with_init):
    (x_ref, b_ref, c_ref, dt_ref, cwx_ref, cwb_ref, cwc_ref, cbx_ref, cbb_ref, cbc_ref,
     alog_ref, dtbias_ref, dsk_ref) = refs[:13]
    if with_init:
        init_ref = refs[13]
        outs = refs[14:]
    else:
        outs = refs[13:]
    (y_ref, sfin_ref, xs_ref, bs_ref, cs_ref, stf_ref, stb_ref,
     cumf_ref, cumb_ref, srcf_ref, srcb_ref) = outs
    g = pl.program_id(1)
    q = SSD_CHUNK
    nc = l // q
    gw = SSD_GROUP_WIDTH
    gh = SSD_GROUP_HEADS
    pad_l = SSD_CONV // 2
    halo = 8

    win_rows = q + 2 * halo
    sel_col = lax.broadcasted_iota(jnp.int32, (q, SSD_CONV * win_rows), 1)
    sel_row = lax.broadcasted_iota(jnp.int32, (q, SSD_CONV * win_rows), 0)
    shift_mat = ((sel_col % win_rows) == sel_row + (halo - pad_l) + sel_col // win_rows).astype(BF16)

    tap_w = {id(w_ref): [jnp.broadcast_to(w_ref[j:j + 1, :], (win_rows, w_ref.shape[1])).astype(BF16)
                         for j in range(SSD_CONV)]
             for w_ref in (cwx_ref, cwb_ref, cwc_ref)}

    def conv_chunk(c, carry):
        t0 = pl.multiple_of(c * q, q)
        lo_start = pl.multiple_of(jnp.maximum(t0 - halo, 0), halo)
        hi_start = pl.multiple_of(jnp.minimum(t0 + q, l - halo), halo)
        for src, w_ref, bias_ref, dst in ((x_ref, cwx_ref, cbx_ref, xs_ref),
                                          (b_ref, cwb_ref, cbb_ref, bs_ref),
                                          (c_ref, cwc_ref, cbc_ref, cs_ref)):
            lo = jnp.where(c > 0, src[pl.ds(lo_start, halo), :], 0.0)
            hi = jnp.where(c < nc - 1, src[pl.ds(hi_start, halo), :], 0.0)
            win = jnp.concatenate([lo, src[pl.ds(t0, q), :], hi], axis=0).astype(BF16)
            taps = jnp.concatenate([win * wj for wj in tap_w[id(w_ref)]], axis=0)
            out = _silu(_dot(shift_mat, taps) + bias_ref[...])
            dst[pl.ds(t0, q), :] = out
            if dst is xs_ref:
                y_ref[pl.ds(t0, q), :] = out * dsk_ref[...]
        dt = _softplus(pltpu.roll(dt_ref[pl.ds(t0, q), :], to_lane0, axis=1) + bias_g)
        dta = dt * a_g
        cum_f = _dot_f32_rhs(tri_f, dta)
        cum_b = _dot_f32_rhs(tri_b, dta)
        cumf_ref[pl.ds(t0, q), :] = cum_f
        cumb_ref[pl.ds(t0, q), :] = cum_b
        log_dt = jnp.log(dt)
        srcf_ref[pl.ds(t0, q), :] = jnp.transpose(cum_f - log_dt)
        srcb_ref[pl.ds(t0, q), :] = jnp.transpose(cum_b - log_dt)
        return carry

    ii = lax.broadcasted_iota(jnp.int32, (q, q), 0)
    jj = lax.broadcasted_iota(jnp.int32, (q, q), 1)
    keep_f = jj <= ii
    keep_b = jj >= ii
    tri_f = keep_f.astype(BF16)
    tri_b = keep_b.astype(BF16)
    n_dt = 2 * SSD_HEADS
    assert q == n_dt
    to_lane0 = (n_dt - g * gh) % n_dt
    bias_g = pltpu.roll(jnp.broadcast_to(dtbias_ref[...], (8, n_dt)), to_lane0, axis=1)[0:1, :]
    a_g = pltpu.roll(jnp.broadcast_to(-jnp.exp(alog_ref[...]), (8, n_dt)), to_lane0, axis=1)[0:1, :]

    per_trip = 4 if nc % 4 == 0 else (2 if nc % 2 == 0 else 1)

    def conv_trip(i, carry):
        for k in range(per_trip):
            conv_chunk(i * per_trip + k, carry)
        return carry

    lax.fori_loop(0, nc // per_trip, conv_trip, 0)

    pair_w = 2 * SSD_HEAD_DIM
    lane = lax.broadcasted_iota(jnp.int32, (1, gw), 1)
    half_mask = [((lane % pair_w) // SSD_HEAD_DIM) == s for s in range(2)]
    first_head_lanes = lax.broadcasted_iota(jnp.int32, (1, pair_w), 1) < SSD_HEAD_DIM

    def scan_chunk(c, reverse):
        keep, edge, lane0 = (keep_b, 0, SSD_HEADS) if reverse else (keep_f, q - 1, 0)
        cum_ref, src_ref, st_ref = ((cumb_ref, srcb_ref, stb_ref) if reverse
                                    else (cumf_ref, srcf_ref, stf_ref))
        t0 = pl.multiple_of(c * q, q)
        xc = xs_ref[pl.ds(t0, q), :]
        bc = bs_ref[pl.ds(t0, q), :]
        cc = cs_ref[pl.ds(t0, q), :]
        cb = _dot_nt(cc.astype(BF16), bc.astype(BF16))
        bct = jnp.transpose(bc)
        x_bf = xc.astype(BF16)
        st = st_ref[...]
        st_bf = st.astype(BF16)
        zero = jnp.zeros((), BF16)
        x_half = [jnp.where(half_mask[s], x_bf, zero) for s in range(2)]
        st_half = [jnp.where(half_mask[s], st_bf, zero) for s in range(2)]
        pieces = []
        for pair in range(gh // 2):
            c0 = pair * pair_w
            lhs_y, lhs_s, carry_decay = [], [], []
            for e in (2 * pair, 2 * pair + 1):
                ln = lane0 + e
                col = jnp.broadcast_to(cum_ref[pl.ds(t0, q), ln:ln + 1], (q, q))
                src = src_ref[pl.ds(t0 + ln, 1), :]
                at_edge = cum_ref[pl.ds(t0 + edge, 1), ln:ln + 1]
                mix = jnp.exp(jnp.where(keep, col - src, -jnp.inf))
                lhs_y.append((cb * mix).astype(BF16))
                lhs_y.append((cc * jnp.exp(col)).astype(BF16))
                lhs_s.append((bct * jnp.exp(at_edge - src)).astype(BF16))
                carry_decay.append(jnp.exp(at_edge))
            rhs_y = jnp.concatenate([x_half[0][:, c0:c0 + pair_w], st_half[0][:, c0:c0 + pair_w],
                                     x_half[1][:, c0:c0 + pair_w], st_half[1][:, c0:c0 + pair_w]], axis=0)
            pieces.append(_dot(jnp.concatenate(lhs_y, axis=1), rhs_y))
            rhs_s = jnp.concatenate([x_half[0][:, c0:c0 + pair_w], x_half[1][:, c0:c0 + pair_w]], axis=0)
            keep_frac = jnp.where(first_head_lanes, carry_decay[0], carry_decay[1])
            st_ref[:, c0:c0 + pair_w] = (st[:, c0:c0 + pair_w] * keep_frac
                                         + _dot(jnp.concatenate(lhs_s, axis=1), rhs_s))
        y_ref[pl.ds(t0, q), :] = y_ref[pl.ds(t0, q), :] + jnp.concatenate(pieces, axis=1)

    for d, st_ref in enumerate((stf_ref, stb_ref)):
        if with_init:
            st_ref[...] = jnp.transpose(init_ref[d].reshape(gw, SSD_STATE))
        else:
            st_ref[...] = jnp.zeros_like(st_ref)

    def step(i, carry):
        for k in range(per_trip):
            s = i * per_trip + k
            scan_chunk(s, False)
            scan_chunk(nc - 1 - s, True)
        return carry

    lax.fori_loop(0, nc // per_trip, step, 0)
    for d, st_ref in enumerate((stf_ref, stb_ref)):
        sfin_ref[d] = jnp.transpose(st_ref[...]).reshape(gh, SSD_HEAD_DIM, SSD_STATE)


def _ssd_scan(proj3, dt_raw3, conv_w, conv_b, a_log, dt_bias, d_skip, i_odd, init=None):
    b, l, _ = proj3.shape
    gh, gw = SSD_GROUP_HEADS, SSD_GROUP_WIDTH
    n_dt = 2 * SSD_HEADS
    a_log3 = a_log.reshape(a_log.shape[0], 1, n_dt)
    dt_bias3 = dt_bias.reshape(dt_bias.shape[0], 1, n_dt)
    dsk = jnp.repeat(d_skip[i_odd].astype(F32), SSD_HEAD_DIM).reshape(1, SSD_INNER)
    conv_b3 = conv_b.reshape(conv_b.shape[0], 1, conv_b.shape[1])

    x_col0 = SSD_INNER // gw
    b_col0 = (2 * SSD_INNER) // SSD_STATE
    c_col0 = (2 * SSD_INNER + SSD_BC_WIDTH) // SSD_STATE
    cwx_col0 = 0
    cwb_col0 = SSD_INNER // SSD_STATE
    cwc_col0 = (SSD_INNER + SSD_BC_WIDTH) // SSD_STATE

    in_specs = [
        pl.BlockSpec((None, l, gw), lambda bi, g: (bi, 0, x_col0 + g)),
        pl.BlockSpec((None, l, SSD_STATE), lambda bi, g: (bi, 0, b_col0 + g)),
        pl.BlockSpec((None, l, SSD_STATE), lambda bi, g: (bi, 0, c_col0 + g)),
        pl.BlockSpec((None, l, n_dt), lambda bi, g: (bi, 0, 0)),
        pl.BlockSpec((None, SSD_CONV, gw), lambda bi, g: (i_odd, 0, cwx_col0 + g)),
        pl.BlockSpec((None, SSD_CONV, SSD_STATE), lambda bi, g: (i_odd, 0, cwb_col0 + g)),
        pl.BlockSpec((None, SSD_CONV, SSD_STATE), lambda bi, g: (i_odd, 0, cwc_col0 + g)),
        pl.BlockSpec((None, 1, gw), lambda bi, g: (i_odd, 0, cwx_col0 + g)),
        pl.BlockSpec((None, 1, SSD_STATE), lambda bi, g: (i_odd, 0, cwb_col0 + g)),
        pl.BlockSpec((None, 1, SSD_STATE), lambda bi, g: (i_odd, 0, cwc_col0 + g)),
        pl.BlockSpec((None, 1, n_dt), lambda bi, g: (i_odd, 0, 0)),
        pl.BlockSpec((None, 1, n_dt), lambda bi, g: (i_odd, 0, 0)),
        pl.BlockSpec((1, gw), lambda bi, g: (0, g)),
    ]
    args = [proj3, proj3, proj3, dt_raw3, conv_w, conv_w, conv_w,
            conv_b3, conv_b3, conv_b3, a_log3, dt_bias3, dsk]
    if init is not None:
        in_specs.append(pl.BlockSpec((None, None, 2, gh, SSD_HEAD_DIM, SSD_STATE),
                                     lambda bi, g: (bi, i_odd, 0, g, 0, 0)))
        args.append(init)
    return pl.pallas_call(
        functools.partial(_ssd_kernel, l=l, with_init=init is not None),
        grid=(b, SSD_GROUPS),
        in_specs=in_specs,
        out_specs=[
            pl.BlockSpec((None, l, gw), lambda bi, g: (bi, 0, g)),
            pl.BlockSpec((None, 2, gh, SSD_HEAD_DIM, SSD_STATE), lambda bi, g: (bi, 0, g, 0, 0)),
        ],
        out_shape=[
            jax.ShapeDtypeStruct((b, l, SSD_INNER), F32),
            jax.ShapeDtypeStruct((b, 2, SSD_HEADS, SSD_HEAD_DIM, SSD_STATE), F32),
        ],
        scratch_shapes=[
            pltpu.VMEM((l, gw), F32),
            pltpu.VMEM((l, SSD_STATE), F32),
            pltpu.VMEM((l, SSD_STATE), F32),
            pltpu.VMEM((SSD_STATE, gw), F32),
            pltpu.VMEM((SSD_STATE, gw), F32),
        ] + [pltpu.VMEM((l, n_dt), F32)] * 4,
        compiler_params=_cparams("parallel", "parallel"),
        name="ssd_scan",
    )(*args)


def _gate_norm_kernel(y_ref, z_ref, g_ref, o_ref):
    y = y_ref[...] * _silu(z_ref[...])
    o = y * lax.rsqrt(jnp.mean(y * y, axis=-1, keepdims=True) + NORM_EPS) * g_ref[...]
    o_ref[...] = o.astype(o_ref.dtype)


def _gate_norm(y, proj, norm_g, i_odd):
    m = y.shape[0]
    tm = ROW_TILE
    g3 = norm_g.reshape(norm_g.shape[0], 1, SSD_INNER)
    return pl.pallas_call(
        _gate_norm_kernel,
        grid=(m // tm,),
        in_specs=[
            pl.BlockSpec((tm, SSD_INNER), lambda i: (i, 0)),
            pl.BlockSpec((tm, SSD_INNER), lambda i: (i, 0)),
            pl.BlockSpec((None, 1, SSD_INNER), lambda i: (i_odd, 0, 0)),
        ],
        out_specs=pl.BlockSpec((tm, SSD_INNER), lambda i: (i, 0)),
        out_shape=jax.ShapeDtypeStruct((m, SSD_INNER), BF16),
        compiler_params=_cparams("parallel"),
        name="gate_norm",
    )(y, proj, g3)


def _expert_kernel(x_ref, wg_ref, wu_ref, wd_ref, gate_ref, o_ref, acc_ref):
    f = pl.program_id(2)
    x = x_ref[...]
    hg = _dot(x, wg_ref[...].astype(BF16))
    hu = _dot(x, wu_ref[...].astype(BF16))
    hdn = (_silu(hg) * hu).astype(BF16)

    @pl.when(f == 0)
    def _():
        acc_ref[...] = jnp.zeros_like(acc_ref)

    acc_ref[...] += _dot(hdn, wd_ref[...].astype(BF16))

    @pl.when(f == pl.num_programs(2) - 1)
    def _():
        o_ref[...] = (acc_ref[...] * gate_ref[...]).astype(o_ref.dtype)


def _expert_ffn(xe, gate, w_gate, w_up, w_down, layer):
    e, r, _ = xe.shape
    tr = min(r, 1024)
    tf = 256
    return pl.pallas_call(
        _expert_kernel,
        grid=(e, r // tr, EXPERT_FF // tf),
        in_specs=[
            pl.BlockSpec((None, tr, D_MODEL), lambda ei, ri, f: (ei, ri, 0)),
            pl.BlockSpec((None, None, D_MODEL, tf), lambda ei, ri, f: (layer, ei, 0, f)),
            pl.BlockSpec((None, None, D_MODEL, tf), lambda ei, ri, f: (layer, ei, 0, f)),
            pl.BlockSpec((None, None, tf, D_MODEL), lambda ei, ri, f: (layer, ei, f, 0)),
            pl.BlockSpec((None, tr, 1), lambda ei, ri, f: (ei, ri, 0)),
        ],
        out_specs=pl.BlockSpec((None, tr, D_MODEL), lambda ei, ri, f: (ei, ri, 0)),
        out_shape=jax.ShapeDtypeStruct((e, r, D_MODEL), BF16),
        scratch_shapes=[pltpu.VMEM((tr, D_MODEL), F32)],
        compiler_params=_cparams("parallel", "parallel", "arbitrary"),
        name="expert_ffn",
    )(xe, w_gate, w_up, w_down, gate)


def _combine_kernel(idx_ref, ye_ref, x_ref, gate_ref, o_ref, *, n, ec, cap):
    step = pl.program_id(2)
    token = lax.broadcasted_iota(jnp.int32, (n, cap), 0)
    onehot = jnp.concatenate(
        [jnp.where(token == idx_ref[k], 1.0, 0.0).astype(BF16) for k in range(ec)], axis=1)

    @pl.when(step == 0)
    def _():
        o_ref[...] = jnp.zeros_like(o_ref)

    o_ref[...] += _dot(onehot, ye_ref[...].reshape(ec * cap, ye_ref.shape[-1]))

    @pl.when(step == pl.num_programs(2) - 1)
    def _():
        o_ref[...] = x_ref[...] + gate_ref[...] * o_ref[...]


def _expert_combine(ye, idx, x, mod4, gate_idx, first_row, shared_cond):
    n_req, n_exp, cap = idx.shape
    m = x.shape[0]
    n = m // n_req
    ec = max(1, min(n_exp, 512 // cap))
    tn = D_MODEL // 2
    idx4 = idx.reshape(n_req, n_exp, 1, cap)
    cond_row = (lambda b: first_row) if shared_cond else (lambda b: first_row + b)
    return pl.pallas_call(
        functools.partial(_combine_kernel, n=n, ec=ec, cap=cap),
        grid=(n_req, D_MODEL // tn, n_exp // ec),
        in_specs=[
            pl.BlockSpec((None, ec, 1, cap), lambda b, j, s: (b, s, 0, 0)),
            pl.BlockSpec((ec, cap, tn), lambda b, j, s: (s, b, j)),
            pl.BlockSpec((n, tn), lambda b, j, s: (b, j)),
            pl.BlockSpec((None, None, 1, tn), lambda b, j, s: (cond_row(b), gate_idx, 0, j)),
        ],
        out_specs=pl.BlockSpec((n, tn), lambda b, j, s: (b, j)),
        out_shape=jax.ShapeDtypeStruct((m, D_MODEL), F32),
        compiler_params=_cparams("parallel", "parallel", "arbitrary"),
        name="expert_combine",
    )(idx4, ye, x, mod4)


def _prefix_count(mask_bf, strict_upper):
    n = mask_bf.shape[1]
    carry = jnp.zeros((mask_bf.shape[0], 1), F32)
    out = []
    for k in range(n // PREFIX_BLOCK):
        blk = mask_bf[:, k * PREFIX_BLOCK:(k + 1) * PREFIX_BLOCK]
        out.append(_dot(blk, strict_upper) + carry)
        carry = carry + jnp.sum(blk.astype(F32), axis=-1, keepdims=True)
    return jnp.concatenate(out, axis=1)


def _router_kernel(lg_ref, o_ref, slot_ref, feat_ref, *, n_req, n, cap):
    step = pl.program_id(0)
    n_exp = N_EXPERTS

    @pl.when(step == 0)
    def _():
        tok = lax.broadcasted_iota(jnp.int32, (n, ROUTE_LANES), 0)
        tlane = lax.broadcasted_iota(jnp.int32, (n, ROUTE_LANES), 1)
        radix_bits = ROUTE_TOKEN_RADIX.bit_length() - 1
        tok_feat = (jnp.where(tlane == ROUTE_TOKEN_LANE, tok >> radix_bits, 0)
                    + jnp.where(tlane == ROUTE_TOKEN_LANE + 1, tok & (ROUTE_TOKEN_RADIX - 1), 0)).astype(F32)
        is_expert = tlane < n_exp
        aff_rows = []
        for r in range(n_req):
            x = jnp.where(is_expert, lg_ref[pl.ds(r * n, n), :], -jnp.inf)
            e = jnp.exp(x - jnp.max(x, axis=-1, keepdims=True))
            aff = e / jnp.sum(e, axis=-1, keepdims=True)
            feat = tok_feat
            for p, part in enumerate(_split3(aff)):
                piece = part.astype(F32)
                feat = feat + (pltpu.roll(piece, p * n_exp, axis=1) if p else piece)
            feat_ref[r] = feat.astype(BF16)
            aff_rows.append(jnp.transpose(aff)[:n_exp])
        aff_t = jnp.concatenate(aff_rows, axis=0)
        bits = pltpu.bitcast(aff_t, jnp.int32)

        def bisect(_, carry):
            lo, hi = carry
            mid = lax.shift_right_arithmetic(lo + hi, 1)
            count = jnp.sum(jnp.where(bits >= mid, 1.0, 0.0), axis=-1, keepdims=True)
            enough = count >= cap
            return jnp.where(enough, mid, lo), jnp.where(enough, hi, mid)

        rows = n_req * n_exp
        lo0 = jnp.zeros((rows, 1), jnp.int32)
        hi0 = jnp.full((rows, 1), 0x3F800001, jnp.int32)
        kth, _ = lax.fori_loop(0, 31, bisect, (lo0, hi0))
        above = bits > kth
        tied = bits == kth
        blk_i = lax.broadcasted_iota(jnp.int32, (PREFIX_BLOCK, PREFIX_BLOCK), 0)
        blk_j = lax.broadcasted_iota(jnp.int32, (PREFIX_BLOCK, PREFIX_BLOCK), 1)
        strict_upper = (blk_i < blk_j).astype(BF16)
        need = cap - jnp.sum(jnp.where(above, 1.0, 0.0), axis=-1, keepdims=True)
        tie_rank = _prefix_count(jnp.where(tied, 1.0, 0.0).astype(BF16), strict_upper)
        chosen = jnp.logical_or(above, jnp.logical_and(tied, tie_rank < need))
        slot = _prefix_count(jnp.where(chosen, 1.0, 0.0).astype(BF16), strict_upper)
        slot_ref[...] = jnp.where(chosen, slot, -1.0)

    slot_id = lax.broadcasted_iota(jnp.int32, (cap, n), 0).astype(F32)
    for k in range(ROUTE_ROWS_PER_STEP):
        row = step * ROUTE_ROWS_PER_STEP + k
        onehot_t = jnp.where(slot_id == slot_ref[pl.ds(row, 1), :], 1.0, 0.0).astype(BF16)
        o_ref[k] = _dot(onehot_t, feat_ref[row // n_exp])


def _route(logits, n_req):
    m = logits.shape[0]
    n = m // n_req
    cap = EC_CAPACITY_FACTOR * n // N_EXPERTS
    rows = n_req * N_EXPERTS
    assert n % PREFIX_BLOCK == 0 and rows % ROUTE_ROWS_PER_STEP == 0
    rec, slot = pl.pallas_call(
        functools.partial(_router_kernel, n_req=n_req, n=n, cap=cap),
        grid=(rows // ROUTE_ROWS_PER_STEP,),
        in_specs=[pl.BlockSpec((m, ROUTE_LANES), lambda i: (0, 0))],
        out_specs=[pl.BlockSpec((ROUTE_ROWS_PER_STEP, cap, ROUTE_LANES), lambda i: (i, 0, 0)),
                   pl.BlockSpec((rows, n), lambda i: (0, 0))],
        out_shape=[jax.ShapeDtypeStruct((rows, cap, ROUTE_LANES), F32),
                   jax.ShapeDtypeStruct((rows, n), F32)],
        scratch_shapes=[pltpu.VMEM((n_req, n, ROUTE_LANES), BF16)],
        compiler_params=_cparams("arbitrary"),
        name="router",
    )(logits)
    rec = rec.reshape(n_req, N_EXPERTS, cap, ROUTE_LANES)
    idx = (ROUTE_TOKEN_RADIX * rec[..., ROUTE_TOKEN_LANE] + rec[..., ROUTE_TOKEN_LANE + 1]).astype(jnp.int32)
    pieces = rec[..., :3 * N_EXPERTS].reshape(n_req, N_EXPERTS, cap, 3, N_EXPERTS).sum(axis=3)
    own = jnp.eye(N_EXPERTS, dtype=F32)[None, :, None, :]
    gate = jnp.sum(pieces * own, axis=3)
    return idx, gate, slot


def _gather_kernel(slot_ref, h_ref, o_ref, *, n, ec, cap):
    slot_id = lax.broadcasted_iota(jnp.int32, (cap, n), 0).astype(F32)
    onehot = jnp.concatenate(
        [jnp.where(slot_id == slot_ref[k], 1.0, 0.0).astype(BF16) for k in range(ec)], axis=0)
    o_ref[...] = _dot(onehot, h_ref[...]).reshape(o_ref.shape).astype(o_ref.dtype)


def _gather_rows(h, slot, n_req):
    m, d = h.shape
    n = m // n_req
    cap = EC_CAPACITY_FACTOR * n // N_EXPERTS
    ec = max(1, min(N_EXPERTS, 2048 // cap))
    tn = d // 2
    slot4 = slot.reshape(n_req, N_EXPERTS, 1, n)
    return pl.pallas_call(
        functools.partial(_gather_kernel, n=n, ec=ec, cap=cap),
        grid=(n_req, d // tn, N_EXPERTS // ec),
        in_specs=[pl.BlockSpec((None, ec, 1, n), lambda b, j, s: (b, s, 0, 0)),
                  pl.BlockSpec((n, tn), lambda b, j, s: (b, j))],
        out_specs=pl.BlockSpec((ec, cap, tn), lambda b, j, s: (s, b, j)),
        out_shape=jax.ShapeDtypeStruct((N_EXPERTS, n_req * cap, d), h.dtype),
        compiler_params=_cparams("parallel", "parallel", "arbitrary"),
        name="gather_rows",
    )(slot4, h)


def _expert_choice(h, logits, x, n_req, w_gate, w_up, w_down, layer, mod4, gate_idx, first_row, shared_cond):
    idx, gate, slot = _route(logits, n_req)
    cap = idx.shape[2]
    gate = jnp.transpose(gate, (1, 0, 2)).reshape(N_EXPERTS, n_req * cap, 1)
    xe = _gather_rows(h, slot, n_req)
    ye = _expert_ffn(xe, gate, w_gate, w_up, w_down, layer)
    return _expert_combine(ye, idx, x, mod4, gate_idx, first_row, shared_cond)


class _Stream:
    def __init__(self, x3, first_row, shared_cond):
        self.batch, self.n, _ = x3.shape
        self.x = x3.reshape(self.batch * self.n, D_MODEL)
        self.first_row = first_row
        self.shared_cond = shared_cond
        self.rows_per_cond = self.batch * self.n if shared_cond else self.n

    def mod_args(self):
        return dict(rows_per_cond=self.rows_per_cond, first_row=self.first_row)


def kernel(x_prompt, x_sample, cache_na_k, cache_na_v, state_ssd, c, c_ctx, ada_w, ada_b, norm1_g, norm2_g,
           router_w, exp_w_gate, exp_w_up, exp_w_down, ab_w_in, pool_w, pool_scale, na_q_norm, na_k_norm,
           na_rpb, ab_w_out, ssd_w_in, ssd_conv_w, ssd_conv_b, ssd_a_log, ssd_dt_bias, ssd_d, ssd_norm_g,
           ssd_w_out):
    depth = ada_w.shape[0]
    dec_batch = x_sample.shape[0]
    latent = _Stream(x_sample, 0, shared_cond=False)
    prompt = _Stream(x_prompt, dec_batch, shared_cond=True)
    cond = jnp.concatenate(
        [c, c_ctx[None, :], jnp.zeros((MOD_ROWS - dec_batch - 1, D_MODEL), F32)], axis=0)

    router_pad = jnp.pad(router_w, ((0, 0), (0, 0), (0, ROUTE_LANES - N_EXPERTS)))

    new_k, new_v, new_s = [], [], []
    for layer in range(depth):
        i = layer // 2
        mod4 = _modulation(cond, ada_w, ada_b, layer).reshape(MOD_ROWS, N_MOD, 1, D_MODEL)
        for s in (prompt, latent):
            margs = s.mod_args()
            h = _norm_mod(s.x, norm1_g, mod4, layer, 0, 1, **margs)[0]
            tm = min(s.x.shape[0], 2048)
            if layer % 2 == 0:
                proj = _matmul([h], ab_w_in, i, tm=tm, tn=512, n_blocks=ab_w_in.shape[2] // 512)
                proj3 = proj.reshape(s.batch, s.n, -1)
                pooled = _pool_mixer(proj3, pool_w, pool_scale, i)
                if s is prompt:
                    att, k_new, v_new = _ctx_attention(proj3, na_q_norm, na_k_norm, i)
                    new_k.append(k_new)
                    new_v.append(v_new)
                else:
                    att = _neighbourhood_attention(proj3, cache_na_k, cache_na_v, na_q_norm, na_k_norm,
                                                   na_rpb, i)
                s.x = _matmul([pooled.reshape(-1, POOL_WIDTH), att.reshape(-1, NA_WIDTH)], ab_w_out, i,
                              tm=tm, tn=512, n_blocks=D_MODEL // 512, res=s.x, mod4=mod4, gate_idx=2,
                              **margs)
            else:
                proj = _matmul([h], ssd_w_in, i, tm=tm, tn=512, n_blocks=SSD_MAIN_DIM // 512)
                dt_raw = _matmul([h], ssd_w_in, i, tm=tm, tn=2 * SSD_HEADS, n_blocks=1,
                                 col_block0=SSD_MAIN_DIM // (2 * SSD_HEADS))
                proj3 = proj.reshape(s.batch, s.n, SSD_MAIN_DIM)
                dt3 = dt_raw.reshape(s.batch, s.n, 2 * SSD_HEADS)
                init = None if s is prompt else state_ssd
                y, s_fin = _ssd_scan(proj3, dt3, ssd_conv_w, ssd_conv_b, ssd_a_log, ssd_dt_bias, ssd_d, i,
                                     init=init)
                if s is prompt:
                    new_s.append(s_fin[:, None])
                yn = _gate_norm(y.reshape(-1, SSD_INNER), proj, ssd_norm_g, i)
                s.x = _matmul([yn], ssd_w_out, i, tm=min(tm, 1024), tn=512, n_blocks=D_MODEL // 512,
                              res=s.x, mod4=mod4, gate_idx=2, **margs)
            h2, logits = _norm_mod(s.x, norm2_g, mod4, layer, 3, 4, router_w=router_pad, **margs)
            s.x = _expert_choice(h2, logits, s.x, s.batch, exp_w_gate, exp_w_up, exp_w_down, layer,
                                 mod4, 5, s.first_row, s.shared_cond)

    y_prompt = prompt.x.reshape(x_prompt.shape)
    y_sample = latent.x.reshape(x_sample.shape)
    return (y_prompt, y_sample, jnp.concatenate(new_k, axis=1), jnp.concatenate(new_v, axis=1),
            jnp.concatenate(new_s, axis=1))
```

```python
import functools

import numpy as np
import jax
import jax.numpy as jnp
from jax import lax
from jax.experimental import pallas as pl
from jax.experimental.pallas import tpu as pltpu

F32 = jnp.float32
BF16 = jnp.bfloat16

D_MODEL = 2048
GRID_W = 64
POOL_WIDTH = 1024
POOL_GROUPS = 4
POOL_GROUP_DIM = 256
POOL_WINDOWS = (2, 4, 8, 16)
NA_WIDTH = 1024
NA_HEAD_DIM = 128
NA_HEADS = 8
WIN_H = 8
WIN_W = 16
SSD_INNER = 4096
SSD_HEAD_DIM = 64
SSD_HEADS = 64
SSD_GROUPS = 8
SSD_GROUP_HEADS = SSD_HEADS // SSD_GROUPS
SSD_GROUP_WIDTH = SSD_GROUP_HEADS * SSD_HEAD_DIM
SSD_STATE = 128
SSD_CONV = 4
SSD_CHUNK = 128
SSD_BC_WIDTH = SSD_GROUPS * SSD_STATE
SSD_MAIN_DIM = 2 * SSD_INNER + 2 * SSD_BC_WIDTH
N_EXPERTS = 16
EXPERT_FF = 1024
EC_CAPACITY_FACTOR = 2
NORM_EPS = 1e-6
N_MOD = 6
MOD_ROWS = 16
ROUTE_LANES = 128
ROUTE_TOKEN_LANE = 3 * N_EXPERTS
ROUTE_TOKEN_RADIX = 64
ROUTE_ROWS_PER_STEP = 8
PREFIX_BLOCK = 256

VMEM_LIMIT_BYTES = 56 * 1024 * 1024
ROW_TILE = 256


def _cparams(*sem):
    return pltpu.CompilerParams(dimension_semantics=sem, vmem_limit_bytes=VMEM_LIMIT_BYTES)


def _silu(x):
    return x * (1.0 / (1.0 + jnp.exp(-x)))


def _split2(x):
    hi = x.astype(BF16)
    lo = (x - hi.astype(F32)).astype(BF16)
    return hi, lo


def _split3(x):
    hi = x.astype(BF16)
    r = x - hi.astype(F32)
    mid = r.astype(BF16)
    lo = (r - mid.astype(F32)).astype(BF16)
    return hi, mid, lo


def _dot(a, b):
    return jnp.dot(a, b, preferred_element_type=F32)


def _dot_nt(a, b):
    return lax.dot_general(a, b, (((1,), (1,)), ((), ())), preferred_element_type=F32)


def _dot_f32_rhs(m, x):
    hi, mid, lo = _split3(x)
    return _dot(m, hi) + _dot(m, mid) + _dot(m, lo)


def _dot_split(a, b):
    a_hi, a_lo = _split2(a)
    b_hi, b_lo = _split2(b)
    return _dot(a_hi, b_hi) + _dot(a_lo, b_hi) + _dot(a_hi, b_lo)


def _mod_kernel(c_ref, w_ref, b_ref, o_ref):
    s = _silu(c_ref[...])
    o_ref[...] = _dot_split(s, w_ref[...]) + b_ref[...]


def _modulation(cond, ada_w, ada_b, layer):
    tn = 768
    n = N_MOD * D_MODEL
    ada_b3 = ada_b.reshape(ada_b.shape[0], 1, n)
    return pl.pallas_call(
        _mod_kernel,
        grid=(n // tn,),
        in_specs=[
            pl.BlockSpec((MOD_ROWS, D_MODEL), lambda j: (0, 0)),
            pl.BlockSpec((None, D_MODEL, tn), lambda j: (layer, 0, j)),
            pl.BlockSpec((None, 1, tn), lambda j: (layer, 0, j)),
        ],
        out_specs=pl.BlockSpec((MOD_ROWS, tn), lambda j: (0, j)),
        out_shape=jax.ShapeDtypeStruct((MOD_ROWS, n), F32),
        compiler_params=_cparams("parallel"),
        name="modulation",
    )(cond, ada_w, ada_b3)


def _mod_spec(which, rows_per_cond, first_row, tm):
    return pl.BlockSpec((None, None, 1, D_MODEL),
                        lambda i, *_: (first_row + (i * tm) // rows_per_cond, which, 0, 0))


def _norm_mod_kernel(x_ref, g_ref, sh_ref, sc_ref, *rest, with_router):
    x = x_ref[...]
    y = x * lax.rsqrt(jnp.mean(x * x, axis=-1, keepdims=True) + NORM_EPS) * g_ref[...]
    h = y * (1.0 + sc_ref[...]) + sh_ref[...]
    if with_router:
        rw_ref, h_ref, lg_ref = rest
        lg_ref[...] = _dot_split(h, rw_ref[...])
    else:
        (h_ref,) = rest
    h_ref[...] = h.astype(BF16)


def _norm_mod(x, g, mod4, layer, shift_idx, scale_idx, rows_per_cond, first_row, router_w=None):
    m = x.shape[0]
    tm = 2 * ROW_TILE
    g3 = g.reshape(g.shape[0], 1, D_MODEL)
    in_specs = [
        pl.BlockSpec((tm, D_MODEL), lambda i: (i, 0)),
        pl.BlockSpec((None, 1, D_MODEL), lambda i: (layer, 0, 0)),
        _mod_spec(shift_idx, rows_per_cond, first_row, tm),
        _mod_spec(scale_idx, rows_per_cond, first_row, tm),
    ]
    args = [x, g3, mod4, mod4]
    out_specs = [pl.BlockSpec((tm, D_MODEL), lambda i: (i, 0))]
    out_shape = [jax.ShapeDtypeStruct((m, D_MODEL), BF16)]
    if router_w is not None:
        in_specs.append(pl.BlockSpec((None, D_MODEL, ROUTE_LANES), lambda i: (layer, 0, 0)))
        args.append(router_w)
        out_specs.append(pl.BlockSpec((tm, ROUTE_LANES), lambda i: (i, 0)))
        out_shape.append(jax.ShapeDtypeStruct((m, ROUTE_LANES), F32))
    return pl.pallas_call(
        functools.partial(_norm_mod_kernel, with_router=router_w is not None),
        grid=(m // tm,),
        in_specs=in_specs,
        out_specs=out_specs,
        out_shape=out_shape,
        compiler_params=_cparams("parallel"),
        name="norm_mod",
    )(*args)


def _matmul_kernel(*refs, n_a, k_sizes, with_res):
    a_refs = refs[:n_a]
    w_ref = refs[n_a]
    if with_res:
        res_ref, gate_ref, o_ref = refs[n_a + 1:]
    else:
        (o_ref,) = refs[n_a + 1:]
    w = w_ref[...].astype(BF16)
    acc = None
    k0 = 0
    for a_ref, k in zip(a_refs, k_sizes):
        part = _dot(a_ref[...], w[k0:k0 + k])
        acc = part if acc is None else acc + part
        k0 += k
    if with_res:
        acc = res_ref[...] + gate_ref[...] * acc
    o_ref[...] = acc.astype(o_ref.dtype)


def _matmul(a_list, w, layer, *, tm, tn, n_blocks, col_block0=0, out_dtype=F32,
            res=None, mod4=None, gate_idx=None, rows_per_cond=None, first_row=None):
    m = a_list[0].shape[0]
    k_sizes = tuple(a.shape[1] for a in a_list)
    k_total = sum(k_sizes)
    in_specs = [pl.BlockSpec((tm, k), lambda i, j: (i, 0)) for k in k_sizes]
    in_specs.append(pl.BlockSpec((None, k_total, tn), lambda i, j: (layer, 0, col_block0 + j)))
    args = list(a_list) + [w]
    if res is not None:
        in_specs.append(pl.BlockSpec((tm, tn), lambda i, j: (i, j)))
        in_specs.append(pl.BlockSpec((None, None, 1, tn),
                                     lambda i, j: (first_row + (i * tm) // rows_per_cond, gate_idx, 0, j)))
        args += [res, mod4]
    return pl.pallas_call(
        functools.partial(_matmul_kernel, n_a=len(a_list), k_sizes=k_sizes, with_res=res is not None),
        grid=(m // tm, n_blocks),
        in_specs=in_specs,
        out_specs=pl.BlockSpec((tm, tn), lambda i, j: (i, j)),
        out_shape=jax.ShapeDtypeStruct((m, n_blocks * tn), out_dtype),
        compiler_params=_cparams("parallel", "arbitrary"),
        name="matmul",
    )(*args)


def _pool_kernel(u_ref, w_ref, s_ref, o_ref, *, n):
    t = lax.broadcasted_iota(jnp.int32, (n, 1), 0)
    for gi, win in enumerate(POOL_WINDOWS):
        lanes = slice(gi * POOL_GROUP_DIM, (gi + 1) * POOL_GROUP_DIM)
        u = u_ref[:, lanes]
        half = win // 2
        total = u
        for d in range(-half, win - half):
            if d == 0:
                continue
            shifted = pltpu.roll(u, (-d) % n, axis=0)
            valid = jnp.logical_and(t + d >= 0, t + d < n)
            total = total + jnp.where(valid, shifted, 0.0)
        cnt = (jnp.minimum(t - half + win, n) - jnp.maximum(t - half, 0)).astype(F32)
        pooled = total / cnt - u
        out = _dot(pooled.astype(BF16), w_ref[gi].astype(BF16)) * s_ref[:, lanes]
        o_ref[:, lanes] = out.astype(o_ref.dtype)


def _pool_mixer(proj3, pool_w, pool_scale, i_even):
    b, n, _ = proj3.shape
    scale3 = pool_scale.reshape(pool_scale.shape[0], 1, POOL_WIDTH)
    return pl.pallas_call(
        functools.partial(_pool_kernel, n=n),
        grid=(b,),
        in_specs=[
            pl.BlockSpec((None, n, POOL_WIDTH), lambda bi: (bi, 0, 0)),
            pl.BlockSpec((None, POOL_GROUPS, POOL_GROUP_DIM, POOL_GROUP_DIM), lambda bi: (i_even, 0, 0, 0)),
            pl.BlockSpec((None, 1, POOL_WIDTH), lambda bi: (i_even, 0, 0)),
        ],
        out_specs=pl.BlockSpec((None, n, POOL_WIDTH), lambda bi: (bi, 0, 0)),
        out_shape=jax.ShapeDtypeStruct((b, n, POOL_WIDTH), BF16),
        compiler_params=_cparams("parallel"),
        name="pool_mixer",
    )(proj3, pool_w, scale3)


def _head_norm(x, g):
    return x * lax.rsqrt(jnp.mean(x * x, axis=-1, keepdims=True) + NORM_EPS) * g


_Q_COL0 = POOL_WIDTH // NA_HEAD_DIM
_K_COL0 = (POOL_WIDTH + NA_WIDTH) // NA_HEAD_DIM
_V_COL0 = (POOL_WIDTH + 2 * NA_WIDTH) // NA_HEAD_DIM


def _ctx_attn_kernel(q_ref, k_ref, v_ref, qg_ref, kg_ref, o_ref, kn_ref, vn_ref):
    for h in range(NA_HEADS):
        lanes = slice(h * NA_HEAD_DIM, (h + 1) * NA_HEAD_DIM)
        qn = _head_norm(q_ref[:, lanes], qg_ref[...])
        kn = _head_norm(k_ref[:, lanes], kg_ref[...])
        v = v_ref[:, lanes]
        kn_ref[h] = kn
        vn_ref[h] = v
        s = _dot_nt(qn.astype(BF16), kn.astype(BF16)) * (NA_HEAD_DIM ** -0.5)
        m = jnp.max(s, axis=-1, keepdims=True)
        p = jnp.exp(s - m)
        denom = jnp.sum(p, axis=-1, keepdims=True)
        o = _dot(p.astype(BF16), v.astype(BF16)) / denom
        o_ref[:, lanes] = o.astype(o_ref.dtype)


def _ctx_attention(proj3, q_norm, k_norm, i_even):
    b, n, _ = proj3.shape
    qg = q_norm.reshape(q_norm.shape[0], 1, NA_HEAD_DIM)
    kg = k_norm.reshape(k_norm.shape[0], 1, NA_HEAD_DIM)
    width_blocks = NA_WIDTH // NA_HEAD_DIM
    head_blk = lambda col0: pl.BlockSpec((None, n, NA_WIDTH), lambda bi: (bi, 0, col0 // width_blocks))
    gain = pl.BlockSpec((None, 1, NA_HEAD_DIM), lambda bi: (i_even, 0, 0))
    cache = pl.BlockSpec((None, None, NA_HEADS, n, NA_HEAD_DIM), lambda bi: (bi, 0, 0, 0, 0))
    return pl.pallas_call(
        _ctx_attn_kernel,
        grid=(b,),
        in_specs=[head_blk(_Q_COL0), head_blk(_K_COL0), head_blk(_V_COL0), gain, gain],
        out_specs=[pl.BlockSpec((None, n, NA_WIDTH), lambda bi: (bi, 0, 0)), cache, cache],
        out_shape=[
            jax.ShapeDtypeStruct((b, n, NA_WIDTH), BF16),
            jax.ShapeDtypeStruct((b, 1, NA_HEADS, n, NA_HEAD_DIM), F32),
            jax.ShapeDtypeStruct((b, 1, NA_HEADS, n, NA_HEAD_DIM), F32),
        ],
        compiler_params=_cparams("parallel"),
        name="ctx_attention",
    )(proj3, proj3, proj3, qg, kg)


NA_BAND = 4
NA_BAND_KEY_ROWS = WIN_H + NA_BAND - 1


def _na_band_key_start(r0, rows):
    return np.clip(r0 - WIN_H // 2, 0, rows - NA_BAND_KEY_ROWS)


def _na_bias_table(rpb, rows):
    col = np.arange(GRID_W)[:, None]
    kc = np.arange(GRID_W)[None, :]
    wstart = np.clip(col - WIN_W // 2, 0, GRID_W - WIN_W)
    inside = (kc >= wstart) & (kc < wstart + WIN_W)
    rel = np.clip(kc - col + WIN_W - 1, 0, 2 * WIN_W - 2)
    a = np.arange(NA_BAND)[:, None]
    kr = np.arange(NA_BAND_KEY_ROWS)[None, :]
    ridx, row_ok = [], []
    for r0 in (0, NA_BAND, rows - NA_BAND):
        r = r0 + a
        sr = np.clip(r - WIN_H // 2, 0, rows - WIN_H)
        krow = _na_band_key_start(r0, rows) + kr
        row_ok.append((krow >= sr) & (krow < sr + WIN_H))
        ridx.append(np.clip(krow - r + WIN_H - 1, 0, 2 * WIN_H - 2))
    ridx = np.stack(ridx)
    ok = np.stack(row_ok)[:, :, None, :, None] & inside[None, None, :, None, :]
    row_sel = np.eye(2 * WIN_H - 1, dtype=np.float32)[ridx]
    col_sel = np.eye(2 * WIN_W - 1, dtype=np.float32)[rel]
    tab = jnp.einsum('vaki,hij,cdj->hvackd', row_sel, rpb, col_sel, precision=lax.Precision.HIGHEST)
    tab = jnp.where(ok[None], tab, -jnp.inf)
    return tab.reshape(rpb.shape[0], 3, NA_BAND * GRID_W, NA_BAND_KEY_ROWS * GRID_W)


def _na_kernel(q_ref, k_ref, v_ref, kc_ref, vc_ref, qg_ref, kg_ref, bias_ref, o_ref,
               qs_ref, ks_ref, vs_ref, *, rows):
    qs_ref[...] = _head_norm(q_ref[...], qg_ref[...]).astype(BF16)
    ks_ref[...] = _head_norm(k_ref[...], kg_ref[...]).astype(BF16)
    vs_ref[...] = v_ref[...].astype(BF16)
    k_ctx = kc_ref[...].astype(BF16)
    v_ctx = vc_ref[...].astype(BF16)
    scale = NA_HEAD_DIM ** -0.5
    n_bands = rows // NA_BAND
    n_q = NA_BAND * GRID_W
    n_loc = NA_BAND_KEY_ROWS * GRID_W

    def band(i):
        r0 = i * NA_BAND
        key_row0 = jnp.clip(r0 - WIN_H // 2, 0, rows - NA_BAND_KEY_ROWS)
        variant = jnp.where(i == 0, 0, jnp.where(i == n_bands - 1, 2, 1))
        q0 = pl.multiple_of(r0 * GRID_W, n_q)
        q_b = qs_ref[pl.ds(q0, n_q), :]
        kstart = pl.multiple_of(key_row0 * GRID_W, GRID_W)
        k_blk = ks_ref[pl.ds(kstart, n_loc), :]
        v_blk = vs_ref[pl.ds(kstart, n_loc), :]
        s_loc = _dot_nt(q_b, k_blk) * scale + bias_ref[variant]
        s_ctx = _dot_nt(q_b, k_ctx) * scale
        m = jnp.maximum(jnp.max(s_loc, axis=-1, keepdims=True), jnp.max(s_ctx, axis=-1, keepdims=True))
        p_loc = jnp.exp(s_loc - m)
        p_ctx = jnp.exp(s_ctx - m)
        denom = jnp.sum(p_loc, axis=-1, keepdims=True) + jnp.sum(p_ctx, axis=-1, keepdims=True)
        o = (_dot(p_loc.astype(BF16), v_blk) + _dot(p_ctx.astype(BF16), v_ctx)) / denom
        o_ref[pl.ds(q0, n_q), :] = o.astype(o_ref.dtype)

    per_trip = 2 if n_bands % 2 == 0 else 1

    def trip(i, carry):
        for k in range(per_trip):
            band(i * per_trip + k)
        return carry

    lax.fori_loop(0, n_bands // per_trip, trip, 0)


def _neighbourhood_attention(proj3, cache_k, cache_v, q_norm, k_norm, rpb, i_even):
    b, t, _ = proj3.shape
    rows = t // GRID_W
    assert rows % NA_BAND == 0 and rows >= NA_BAND_KEY_ROWS + 1, rows
    n_ctx = cache_k.shape[3]
    bias = _na_bias_table(rpb[i_even], rows)
    qg = q_norm.reshape(q_norm.shape[0], 1, NA_HEAD_DIM)
    kg = k_norm.reshape(k_norm.shape[0], 1, NA_HEAD_DIM)
    head_blk = lambda col0: pl.BlockSpec((None, t, NA_HEAD_DIM), lambda bi, h: (bi, 0, col0 + h))
    gain = pl.BlockSpec((None, 1, NA_HEAD_DIM), lambda bi, h: (i_even, 0, 0))
    cache = pl.BlockSpec((None, None, None, n_ctx, NA_HEAD_DIM), lambda bi, h: (bi, i_even, h, 0, 0))
    return pl.pallas_call(
        functools.partial(_na_kernel, rows=rows),
        grid=(b, NA_HEADS),
        in_specs=[head_blk(_Q_COL0), head_blk(_K_COL0), head_blk(_V_COL0), cache, cache, gain, gain,
                  pl.BlockSpec((None,) + bias.shape[1:], lambda bi, h: (h, 0, 0, 0))],
        out_specs=pl.BlockSpec((None, t, NA_HEAD_DIM), lambda bi, h: (bi, 0, h)),
        out_shape=jax.ShapeDtypeStruct((b, t, NA_WIDTH), BF16),
        scratch_shapes=[pltpu.VMEM((t, NA_HEAD_DIM), BF16)] * 3,
        compiler_params=_cparams("parallel", "parallel"),
        name="neighbourhood_attention",
    )(proj3, proj3, proj3, cache_k, cache_v, qg, kg, bias)


def _softplus(x):
    return jnp.maximum(x, 0.0) + jnp.log1p(jnp.exp(-jnp.abs(x)))


def _ssd_kernel(*refs, l, with_init):
    (x_ref, b_ref, c_ref, dt_ref, cwx_ref, cwb_ref, cwc_ref, cbx_ref, cbb_ref, cbc_ref,
     alog_ref, dtbias_ref, dsk_ref) = refs[:13]
    if with_init:
        init_ref = refs[13]
        outs = refs[14:]
    else:
        outs = refs[13:]
    (y_ref, sfin_ref, xs_ref, bs_ref, cs_ref, stf_ref, stb_ref,
     cumf_ref, cumb_ref, srcf_ref, srcb_ref) = outs
    g = pl.program_id(1)
    q = SSD_CHUNK
    nc = l // q
    gw = SSD_GROUP_WIDTH
    gh = SSD_GROUP_HEADS
    pad_l = SSD_CONV // 2
    halo = 8

    win_rows = q + 2 * halo
    sel_col = lax.broadcasted_iota(jnp.int32, (q, SSD_CONV * win_rows), 1)
    sel_row = lax.broadcasted_iota(jnp.int32, (q, SSD_CONV * win_rows), 0)
    shift_mat = ((sel_col % win_rows) == sel_row + (halo - pad_l) + sel_col // win_rows).astype(BF16)

    tap_w = {id(w_ref): [jnp.broadcast_to(w_ref[j:j + 1, :], (win_rows, w_ref.shape[1])).astype(BF16)
                         for j in range(SSD_CONV)]
             for w_ref in (cwx_ref, cwb_ref, cwc_ref)}

    def conv_chunk(c, carry):
        t0 = pl.multiple_of(c * q, q)
        lo_start = pl.multiple_of(jnp.maximum(t0 - halo, 0), halo)
        hi_start = pl.multiple_of(jnp.minimum(t0 + q, l - halo), halo)
        for src, w_ref, bias_ref, dst in ((x_ref, cwx_ref, cbx_ref, xs_ref),
                                          (b_ref, cwb_ref, cbb_ref, bs_ref),
                                          (c_ref, cwc_ref, cbc_ref, cs_ref)):
            lo = jnp.where(c > 0, src[pl.ds(lo_start, halo), :], 0.0)
            hi = jnp.where(c < nc - 1, src[pl.ds(hi_start, halo), :], 0.0)
            win = jnp.concatenate([lo, src[pl.ds(t0, q), :], hi], axis=0).astype(BF16)
            taps = jnp.concatenate([win * wj for wj in tap_w[id(w_ref)]], axis=0)
            out = _silu(_dot(shift_mat, taps) + bias_ref[...])
            dst[pl.ds(t0, q), :] = out
            if dst is xs_ref:
                y_ref[pl.ds(t0, q), :] = out * dsk_ref[...]
        dt = _softplus(pltpu.roll(dt_ref[pl.ds(t0, q), :], to_lane0, axis=1) + bias_g)
        dta = dt * a_g
        cum_f = _dot_f32_rhs(tri_f, dta)
        cum_b = _dot_f32_rhs(tri_b, dta)
        cumf_ref[pl.ds(t0, q), :] = cum_f
        cumb_ref[pl.ds(t0, q), :] = cum_b
        log_dt = jnp.log(dt)
        srcf_ref[pl.ds(t0, q), :] = jnp.transpose(cum_f - log_dt)
        srcb_ref[pl.ds(t0, q), :] = jnp.transpose(cum_b - log_dt)
        return carry

    ii = lax.broadcasted_iota(jnp.int32, (q, q), 0)
    jj = lax.broadcasted_iota(jnp.int32, (q, q), 1)
    keep_f = jj <= ii
    keep_b = jj >= ii
    tri_f = keep_f.astype(BF16)
    tri_b = keep_b.astype(BF16)
    n_dt = 2 * SSD_HEADS
    assert q == n_dt
    to_lane0 = (n_dt - g * gh) % n_dt
    bias_g = pltpu.roll(jnp.broadcast_to(dtbias_ref[...], (8, n_dt)), to_lane0, axis=1)[0:1, :]
    a_g = pltpu.roll(jnp.broadcast_to(-jnp.exp(alog_ref[...]), (8, n_dt)), to_lane0, axis=1)[0:1, :]

    per_trip = 4 if nc % 4 == 0 else (2 if nc % 2 == 0 else 1)

    def conv_trip(i, carry):
        for k in range(per_trip):
            conv_chunk(i * per_trip + k, carry)
        return carry

    lax.fori_loop(0, nc // per_trip, conv_trip, 0)

    pair_w = 2 * SSD_HEAD_DIM
    lane = lax.broadcasted_iota(jnp.int32, (1, gw), 1)
    half_mask = [((lane % pair_w) // SSD_HEAD_DIM) == s for s in range(2)]
    first_head_lanes = lax.broadcasted_iota(jnp.int32, (1, pair_w), 1) < SSD_HEAD_DIM

    def scan_chunk(c, reverse):
        keep, edge, lane0 = (keep_b, 0, SSD_HEADS) if reverse else (keep_f, q - 1, 0)
        cum_ref, src_ref, st_ref = ((cumb_ref, srcb_ref, stb_ref) if reverse
                                    else (cumf_ref, srcf_ref, stf_ref))
        t0 = pl.multiple_of(c * q, q)
        xc = xs_ref[pl.ds(t0, q), :]
        bc = bs_ref[pl.ds(t0, q), :]
        cc = cs_ref[pl.ds(t0, q), :]
        cb = _dot_nt(cc.astype(BF16), bc.astype(BF16))
        bct = jnp.transpose(bc)
        x_bf = xc.astype(BF16)
        st = st_ref[...]
        st_bf = st.astype(BF16)
        zero = jnp.zeros((), BF16)
        x_half = [jnp.where(half_mask[s], x_bf, zero) for s in range(2)]
        st_half = [jnp.where(half_mask[s], st_bf, zero) for s in range(2)]
        pieces = []
        for pair in range(gh // 2):
            c0 = pair * pair_w
            lhs_y, lhs_s, carry_decay = [], [], []
            for e in (2 * pair, 2 * pair + 1):
                ln = lane0 + e
                col = jnp.broadcast_to(cum_ref[pl.ds(t0, q), ln:ln + 1], (q, q))
                src = src_ref[pl.ds(t0 + ln, 1), :]
                at_edge = cum_ref[pl.ds(t0 + edge, 1), ln:ln + 1]
                mix = jnp.exp(jnp.where(keep, col - src, -jnp.inf))
                lhs_y.append((cb * mix).astype(BF16))
                lhs_y.append((cc * jnp.exp(col)).astype(BF16))
                lhs_s.append((bct * jnp.exp(at_edge - src)).astype(BF16))
                carry_decay.append(jnp.exp(at_edge))
            rhs_y = jnp.concatenate([x_half[0][:, c0:c0 + pair_w], st_half[0][:, c0:c0 + pair_w],
                                     x_half[1][:, c0:c0 + pair_w], st_half[1][:, c0:c0 + pair_w]], axis=0)
            pieces.append(_dot(jnp.concatenate(lhs_y, axis=1), rhs_y))
            rhs_s = jnp.concatenate([x_half[0][:, c0:c0 + pair_w], x_half[1][:, c0:c0 + pair_w]], axis=0)
            keep_frac = jnp.where(first_head_lanes, carry_decay[0], carry_decay[1])
            st_ref[:, c0:c0 + pair_w] = (st[:, c0:c0 + pair_w] * keep_frac
                                         + _dot(jnp.concatenate(lhs_s, axis=1), rhs_s))
        y_ref[pl.ds(t0, q), :] = y_ref[pl.ds(t0, q), :] + jnp.concatenate(pieces, axis=1)

    for d, st_ref in enumerate((stf_ref, stb_ref)):
        if with_init:
            st_ref[...] = jnp.transpose(init_ref[d].reshape(gw, SSD_STATE))
        else:
            st_ref[...] = jnp.zeros_like(st_ref)

    def step(i, carry):
        for k in range(per_trip):
            s = i * per_trip + k
            scan_chunk(s, False)
            scan_chunk(nc - 1 - s, True)
        return carry

    lax.fori_loop(0, nc // per_trip, step, 0)
    for d, st_ref in enumerate((stf_ref, stb_ref)):
        sfin_ref[d] = jnp.transpose(st_ref[...]).reshape(gh, SSD_HEAD_DIM, SSD_STATE)


def _ssd_scan(proj3, dt_raw3, conv_w, conv_b, a_log, dt_bias, d_skip, i_odd, init=None):
    b, l, _ = proj3.shape
    gh, gw = SSD_GROUP_HEADS, SSD_GROUP_WIDTH
    n_dt = 2 * SSD_HEADS
    a_log3 = a_log.reshape(a_log.shape[0], 1, n_dt)
    dt_bias3 = dt_bias.reshape(dt_bias.shape[0], 1, n_dt)
    dsk = jnp.repeat(d_skip[i_odd].astype(F32), SSD_HEAD_DIM).reshape(1, SSD_INNER)
    conv_b3 = conv_b.reshape(conv_b.shape[0], 1, conv_b.shape[1])

    x_col0 = SSD_INNER // gw
    b_col0 = (2 * SSD_INNER) // SSD_STATE
    c_col0 = (2 * SSD_INNER + SSD_BC_WIDTH) // SSD_STATE
    cwx_col0 = 0
    cwb_col0 = SSD_INNER // SSD_STATE
    cwc_col0 = (SSD_INNER + SSD_BC_WIDTH) // SSD_STATE

    in_specs = [
        pl.BlockSpec((None, l, gw), lambda bi, g: (bi, 0, x_col0 + g)),
        pl.BlockSpec((None, l, SSD_STATE), lambda bi, g: (bi, 0, b_col0 + g)),
        pl.BlockSpec((None, l, SSD_STATE), lambda bi, g: (bi, 0, c_col0 + g)),
        pl.BlockSpec((None, l, n_dt), lambda bi, g: (bi, 0, 0)),
        pl.BlockSpec((None, SSD_CONV, gw), lambda bi, g: (i_odd, 0, cwx_col0 + g)),
        pl.BlockSpec((None, SSD_CONV, SSD_STATE), lambda bi, g: (i_odd, 0, cwb_col0 + g)),
        pl.BlockSpec((None, SSD_CONV, SSD_STATE), lambda bi, g: (i_odd, 0, cwc_col0 + g)),
        pl.BlockSpec((None, 1, gw), lambda bi, g: (i_odd, 0, cwx_col0 + g)),
        pl.BlockSpec((None, 1, SSD_STATE), lambda bi, g: (i_odd, 0, cwb_col0 + g)),
        pl.BlockSpec((None, 1, SSD_STATE), lambda bi, g: (i_odd, 0, cwc_col0 + g)),
        pl.BlockSpec((None, 1, n_dt), lambda bi, g: (i_odd, 0, 0)),
        pl.BlockSpec((None, 1, n_dt), lambda bi, g: (i_odd, 0, 0)),
        pl.BlockSpec((1, gw), lambda bi, g: (0, g)),
    ]
    args = [proj3, proj3, proj3, dt_raw3, conv_w, conv_w, conv_w,
            conv_b3, conv_b3, conv_b3, a_log3, dt_bias3, dsk]
    if init is not None:
        in_specs.append(pl.BlockSpec((None, None, 2, gh, SSD_HEAD_DIM, SSD_STATE),
                                     lambda bi, g: (bi, i_odd, 0, g, 0, 0)))
        args.append(init)
    return pl.pallas_call(
        functools.partial(_ssd_kernel, l=l, with_init=init is not None),
        grid=(b, SSD_GROUPS),
        in_specs=in_specs,
        out_specs=[
            pl.BlockSpec((None, l, gw), lambda bi, g: (bi, 0, g)),
            pl.BlockSpec((None, 2, gh, SSD_HEAD_DIM, SSD_STATE), lambda bi, g: (bi, 0, g, 0, 0)),
        ],
        out_shape=[
            jax.ShapeDtypeStruct((b, l, SSD_INNER), F32),
            jax.ShapeDtypeStruct((b, 2, SSD_HEADS, SSD_HEAD_DIM, SSD_STATE), F32),
        ],
        scratch_shapes=[
            pltpu.VMEM((l, gw), F32),
            pltpu.VMEM((l, SSD_STATE), F32),
            pltpu.VMEM((l, SSD_STATE), F32),
            pltpu.VMEM((SSD_STATE, gw), F32),
            pltpu.VMEM((SSD_STATE, gw), F32),
        ] + [pltpu.VMEM((l, n_dt), F32)] * 4,
        compiler_params=_cparams("parallel", "parallel"),
        name="ssd_scan",
    )(*args)


def _gate_norm_kernel(y_ref, z_ref, g_ref, o_ref):
    y = y_ref[...] * _silu(z_ref[...])
    o = y * lax.rsqrt(jnp.mean(y * y, axis=-1, keepdims=True) + NORM_EPS) * g_ref[...]
    o_ref[...] = o.astype(o_ref.dtype)


def _gate_norm(y, proj, norm_g, i_odd):
    m = y.shape[0]
    tm = ROW_TILE
    g3 = norm_g.reshape(norm_g.shape[0], 1, SSD_INNER)
    return pl.pallas_call(
        _gate_norm_kernel,
        grid=(m // tm,),
        in_specs=[
            pl.BlockSpec((tm, SSD_INNER), lambda i: (i, 0)),
            pl.BlockSpec((tm, SSD_INNER), lambda i: (i, 0)),
            pl.BlockSpec((None, 1, SSD_INNER), lambda i: (i_odd, 0, 0)),
        ],
        out_specs=pl.BlockSpec((tm, SSD_INNER), lambda i: (i, 0)),
        out_shape=jax.ShapeDtypeStruct((m, SSD_INNER), BF16),
        compiler_params=_cparams("parallel"),
        name="gate_norm",
    )(y, proj, g3)


def _expert_kernel(x_ref, wg_ref, wu_ref, wd_ref, gate_ref, o_ref, acc_ref):
    f = pl.program_id(2)
    x = x_ref[...]
    hg = _dot(x, wg_ref[...].astype(BF16))
    hu = _dot(x, wu_ref[...].astype(BF16))
    hdn = (_silu(hg) * hu).astype(BF16)

    @pl.when(f == 0)
    def _():
        acc_ref[...] = jnp.zeros_like(acc_ref)

    acc_ref[...] += _dot(hdn, wd_ref[...].astype(BF16))

    @pl.when(f == pl.num_programs(2) - 1)
    def _():
        o_ref[...] = (acc_ref[...] * gate_ref[...]).astype(o_ref.dtype)


def _expert_ffn(xe, gate, w_gate, w_up, w_down, layer):
    e, r, _ = xe.shape
    tr = min(r, 1024)
    tf = 256
    return pl.pallas_call(
        _expert_kernel,
        grid=(e, r // tr, EXPERT_FF // tf),
        in_specs=[
            pl.BlockSpec((None, tr, D_MODEL), lambda ei, ri, f: (ei, ri, 0)),
            pl.BlockSpec((None, None, D_MODEL, tf), lambda ei, ri, f: (layer, ei, 0, f)),
            pl.BlockSpec((None, None, D_MODEL, tf), lambda ei, ri, f: (layer, ei, 0, f)),
            pl.BlockSpec((None, None, tf, D_MODEL), lambda ei, ri, f: (layer, ei, f, 0)),
            pl.BlockSpec((None, tr, 1), lambda ei, ri, f: (ei, ri, 0)),
        ],
        out_specs=pl.BlockSpec((None, tr, D_MODEL), lambda ei, ri, f: (ei, ri, 0)),
        out_shape=jax.ShapeDtypeStruct((e, r, D_MODEL), BF16),
        scratch_shapes=[pltpu.VMEM((tr, D_MODEL), F32)],
        compiler_params=_cparams("parallel", "parallel", "arbitrary"),
        name="expert_ffn",
    )(xe, w_gate, w_up, w_down, gate)


def _combine_kernel(idx_ref, ye_ref, x_ref, gate_ref, o_ref, *, n, ec, cap):
    step = pl.program_id(2)
    token = lax.broadcasted_iota(jnp.int32, (n, cap), 0)
    onehot = jnp.concatenate(
        [jnp.where(token == idx_ref[k], 1.0, 0.0).astype(BF16) for k in range(ec)], axis=1)

    @pl.when(step == 0)
    def _():
        o_ref[...] = jnp.zeros_like(o_ref)

    o_ref[...] += _dot(onehot, ye_ref[...].reshape(ec * cap, ye_ref.shape[-1]))

    @pl.when(step == pl.num_programs(2) - 1)
    def _():
        o_ref[...] = x_ref[...] + gate_ref[...] * o_ref[...]


def _expert_combine(ye, idx, x, mod4, gate_idx, first_row, shared_cond):
    n_req, n_exp, cap = idx.shape
    m = x.shape[0]
    n = m // n_req
    ec = max(1, min(n_exp, 1024 // cap))
    tn = D_MODEL // 2
    idx4 = idx.reshape(n_req, n_exp, 1, cap)
    cond_row = (lambda b: first_row) if shared_cond else (lambda b: first_row + b)
    return pl.pallas_call(
        functools.partial(_combine_kernel, n=n, ec=ec, cap=cap),
        grid=(n_req, D_MODEL // tn, n_exp // ec),
        in_specs=[
            pl.BlockSpec((None, ec, 1, cap), lambda b, j, s: (b, s, 0, 0)),
            pl.BlockSpec((ec, cap, tn), lambda b, j, s: (s, b, j)),
            pl.BlockSpec((n, tn), lambda b, j, s: (b, j)),
            pl.BlockSpec((None, None, 1, tn), lambda b, j, s: (cond_row(b), gate_idx, 0, j)),
        ],
        out_specs=pl.BlockSpec((n, tn), lambda b, j, s: (b, j)),
        out_shape=jax.ShapeDtypeStruct((m, D_MODEL), F32),
        compiler_params=_cparams("parallel", "parallel", "arbitrary"),
        name="expert_combine",
    )(idx4, ye, x, mod4)


def _prefix_count(mask_bf, strict_upper):
    n = mask_bf.shape[1]
    carry = jnp.zeros((mask_bf.shape[0], 1), F32)
    out = []
    for k in range(n // PREFIX_BLOCK):
        blk = mask_bf[:, k * PREFIX_BLOCK:(k + 1) * PREFIX_BLOCK]
        out.append(_dot(blk, strict_upper) + carry)
        carry = carry + jnp.sum(blk.astype(F32), axis=-1, keepdims=True)
    return jnp.concatenate(out, axis=1)


def _router_kernel(lg_ref, o_ref, slot_ref, feat_ref, *, n_req, n, cap):
    step = pl.program_id(0)
    n_exp = N_EXPERTS

    @pl.when(step == 0)
    def _():
        tok = lax.broadcasted_iota(jnp.int32, (n, ROUTE_LANES), 0)
        tlane = lax.broadcasted_iota(jnp.int32, (n, ROUTE_LANES), 1)
        radix_bits = ROUTE_TOKEN_RADIX.bit_length() - 1
        tok_feat = (jnp.where(tlane == ROUTE_TOKEN_LANE, tok >> radix_bits, 0)
                    + jnp.where(tlane == ROUTE_TOKEN_LANE + 1, tok & (ROUTE_TOKEN_RADIX - 1), 0)).astype(F32)
        is_expert = tlane < n_exp
        aff_rows = []
        for r in range(n_req):
            x = jnp.where(is_expert, lg_ref[pl.ds(r * n, n), :], -jnp.inf)
            e = jnp.exp(x - jnp.max(x, axis=-1, keepdims=True))
            aff = e / jnp.sum(e, axis=-1, keepdims=True)
            feat = tok_feat
            for p, part in enumerate(_split3(aff)):
                piece = part.astype(F32)
                feat = feat + (pltpu.roll(piece, p * n_exp, axis=1) if p else piece)
            feat_ref[r] = feat.astype(BF16)
            aff_rows.append(jnp.transpose(aff)[:n_exp])
        aff_t = jnp.concatenate(aff_rows, axis=0)
        bits = pltpu.bitcast(aff_t, jnp.int32)

        def bisect(_, carry):
            lo, hi = carry
            mid = lax.shift_right_arithmetic(lo + hi, 1)
            count = jnp.sum(jnp.where(bits >= mid, 1.0, 0.0), axis=-1, keepdims=True)
            enough = count >= cap
            return jnp.where(enough, mid, lo), jnp.where(enough, hi, mid)

        rows = n_req * n_exp
        lo0 = jnp.zeros((rows, 1), jnp.int32)
        hi0 = jnp.full((rows, 1), 0x3F800001, jnp.int32)
        kth, _ = lax.fori_loop(0, 31, bisect, (lo0, hi0))
        above = bits > kth
        tied = bits == kth
        blk_i = lax.broadcasted_iota(jnp.int32, (PREFIX_BLOCK, PREFIX_BLOCK), 0)
        blk_j = lax.broadcasted_iota(jnp.int32, (PREFIX_BLOCK, PREFIX_BLOCK), 1)
        strict_upper = (blk_i < blk_j).astype(BF16)
        need = cap - jnp.sum(jnp.where(above, 1.0, 0.0), axis=-1, keepdims=True)
        tie_rank = _prefix_count(jnp.where(tied, 1.0, 0.0).astype(BF16), strict_upper)
        chosen = jnp.logical_or(above, jnp.logical_and(tied, tie_rank < need))
        slot = _prefix_count(jnp.where(chosen, 1.0, 0.0).astype(BF16), strict_upper)
        slot_ref[...] = jnp.where(chosen, slot, -1.0)

    slot_id = lax.broadcasted_iota(jnp.int32, (cap, n), 0).astype(F32)
    for k in range(ROUTE_ROWS_PER_STEP):
        row = step * ROUTE_ROWS_PER_STEP + k
        onehot_t = jnp.where(slot_id == slot_ref[pl.ds(row, 1), :], 1.0, 0.0).astype(BF16)
        o_ref[k] = _dot(onehot_t, feat_ref[row // n_exp])


def _route(logits, n_req):
    m = logits.shape[0]
    n = m // n_req
    cap = EC_CAPACITY_FACTOR * n // N_EXPERTS
    rows = n_req * N_EXPERTS
    assert n % PREFIX_BLOCK == 0 and rows % ROUTE_ROWS_PER_STEP == 0
    rec, slot = pl.pallas_call(
        functools.partial(_router_kernel, n_req=n_req, n=n, cap=cap),
        grid=(rows // ROUTE_ROWS_PER_STEP,),
        in_specs=[pl.BlockSpec((m, ROUTE_LANES), lambda i: (0, 0))],
        out_specs=[pl.BlockSpec((ROUTE_ROWS_PER_STEP, cap, ROUTE_LANES), lambda i: (i, 0, 0)),
                   pl.BlockSpec((rows, n), lambda i: (0, 0))],
        out_shape=[jax.ShapeDtypeStruct((rows, cap, ROUTE_LANES), F32),
                   jax.ShapeDtypeStruct((rows, n), F32)],
        scratch_shapes=[pltpu.VMEM((n_req, n, ROUTE_LANES), BF16)],
        compiler_params=_cparams("arbitrary"),
        name="router",
    )(logits)
    rec = rec.reshape(n_req, N_EXPERTS, cap, ROUTE_LANES)
    idx = (ROUTE_TOKEN_RADIX * rec[..., ROUTE_TOKEN_LANE] + rec[..., ROUTE_TOKEN_LANE + 1]).astype(jnp.int32)
    pieces = rec[..., :3 * N_EXPERTS].reshape(n_req, N_EXPERTS, cap, 3, N_EXPERTS).sum(axis=3)
    own = jnp.eye(N_EXPERTS, dtype=F32)[None, :, None, :]
    gate = jnp.sum(pieces * own, axis=3)
    return idx, gate, slot


def _gather_kernel(slot_ref, h_ref, o_ref, *, n, ec, cap):
    slot_id = lax.broadcasted_iota(jnp.int32, (cap, n), 0).astype(F32)
    onehot = jnp.concatenate(
        [jnp.where(slot_id == slot_ref[k], 1.0, 0.0).astype(BF16) for k in range(ec)], axis=0)
    o_ref[...] = _dot(onehot, h_ref[...]).reshape(o_ref.shape).astype(o_ref.dtype)


def _gather_rows(h, slot, n_req):
    m, d = h.shape
    n = m // n_req
    cap = EC_CAPACITY_FACTOR * n // N_EXPERTS
    ec = max(1, min(N_EXPERTS, 2048 // cap))
    tn = d // 2
    slot4 = slot.reshape(n_req, N_EXPERTS, 1, n)
    return pl.pallas_call(
        functools.partial(_gather_kernel, n=n, ec=ec, cap=cap),
        grid=(n_req, d // tn, N_EXPERTS // ec),
        in_specs=[pl.BlockSpec((None, ec, 1, n), lambda b, j, s: (b, s, 0, 0)),
                  pl.BlockSpec((n, tn), lambda b, j, s: (b, j))],
        out_specs=pl.BlockSpec((ec, cap, tn), lambda b, j, s: (s, b, j)),
        out_shape=jax.ShapeDtypeStruct((N_EXPERTS, n_req * cap, d), h.dtype),
        compiler_params=_cparams("parallel", "parallel", "arbitrary"),
        name="gather_rows",
    )(slot4, h)


def _expert_choice(h, logits, x, n_req, w_gate, w_up, w_down, layer, mod4, gate_idx, first_row, shared_cond):
    idx, gate, slot = _route(logits, n_req)
    cap = idx.shape[2]
    gate = jnp.transpose(gate, (1, 0, 2)).reshape(N_EXPERTS, n_req * cap, 1)
    xe = _gather_rows(h, slot, n_req)
    ye = _expert_ffn(xe, gate, w_gate, w_up, w_down, layer)
    return _expert_combine(ye, idx, x, mod4, gate_idx, first_row, shared_cond)


class _Stream:
    def __init__(self, x3, first_row, shared_cond):
        self.batch, self.n, _ = x3.shape
        self.x = x3.reshape(self.batch * self.n, D_MODEL)
        self.first_row = first_row
        self.shared_cond = shared_cond
        self.rows_per_cond = self.batch * self.n if shared_cond else self.n

    def mod_args(self):
        return dict(rows_per_cond=self.rows_per_cond, first_row=self.first_row)


def kernel(x_prompt, x_sample, cache_na_k, cache_na_v, state_ssd, c, c_ctx, ada_w, ada_b, norm1_g, norm2_g,
           router_w, exp_w_gate, exp_w_up, exp_w_down, ab_w_in, pool_w, pool_scale, na_q_norm, na_k_norm,
           na_rpb, ab_w_out, ssd_w_in, ssd_conv_w, ssd_conv_b, ssd_a_log, ssd_dt_bias, ssd_d, ssd_norm_g,
           ssd_w_out):
    depth = ada_w.shape[0]
    dec_batch = x_sample.shape[0]
    latent = _Stream(x_sample, 0, shared_cond=False)
    prompt = _Stream(x_prompt, dec_batch, shared_cond=True)
    cond = jnp.concatenate(
        [c, c_ctx[None, :], jnp.zeros((MOD_ROWS - dec_batch - 1, D_MODEL), F32)], axis=0)

    router_pad = jnp.pad(router_w, ((0, 0), (0, 0), (0, ROUTE_LANES - N_EXPERTS)))

    new_k, new_v, new_s = [], [], []
    for layer in range(depth):
        i = layer // 2
        mod4 = _modulation(cond, ada_w, ada_b, layer).reshape(MOD_ROWS, N_MOD, 1, D_MODEL)
        for s in (prompt, latent):
            margs = s.mod_args()
            h = _norm_mod(s.x, norm1_g, mod4, layer, 0, 1, **margs)[0]
            tm = min(s.x.shape[0], 2048)
            if layer % 2 == 0:
                proj = _matmul([h], ab_w_in, i, tm=tm, tn=512, n_blocks=ab_w_in.shape[2] // 512)
                proj3 = proj.reshape(s.batch, s.n, -1)
                pooled = _pool_mixer(proj3, pool_w, pool_scale, i)
                if s is prompt:
                    att, k_new, v_new = _ctx_attention(proj3, na_q_norm, na_k_norm, i)
                    new_k.append(k_new)
                    new_v.append(v_new)
                else:
                    att = _neighbourhood_attention(proj3, cache_na_k, cache_na_v, na_q_norm, na_k_norm,
                                                   na_rpb, i)
                s.x = _matmul([pooled.reshape(-1, POOL_WIDTH), att.reshape(-1, NA_WIDTH)], ab_w_out, i,
                              tm=tm, tn=512, n_blocks=D_MODEL // 512, res=s.x, mod4=mod4, gate_idx=2,
                              **margs)
            else:
                proj = _matmul([h], ssd_w_in, i, tm=tm, tn=512, n_blocks=SSD_MAIN_DIM // 512)
                dt_raw = _matmul([h], ssd_w_in, i, tm=tm, tn=2 * SSD_HEADS, n_blocks=1,
                                 col_block0=SSD_MAIN_DIM // (2 * SSD_HEADS))
                proj3 = proj.reshape(s.batch, s.n, SSD_MAIN_DIM)
                dt3 = dt_raw.reshape(s.batch, s.n, 2 * SSD_HEADS)
                init = None if s is prompt else state_ssd
                y, s_fin = _ssd_scan(proj3, dt3, ssd_conv_w, ssd_conv_b, ssd_a_log, ssd_dt_bias, ssd_d, i,
                                     init=init)
                if s is prompt:
                    new_s.append(s_fin[:, None])
                yn = _gate_norm(y.reshape(-1, SSD_INNER), proj, ssd_norm_g, i)
                s.x = _matmul([yn], ssd_w_out, i, tm=min(tm, 1024), tn=512, n_blocks=D_MODEL // 512,
                              res=s.x, mod4=mod4, gate_idx=2, **margs)
            h2, logits = _norm_mod(s.x, norm2_g, mod4, layer, 3, 4, router_w=router_pad, **margs)
            s.x = _expert_choice(h2, logits, s.x, s.batch, exp_w_gate, exp_w_up, exp_w_down, layer,
                                 mod4, 5, s.first_row, s.shared_cond)

    y_prompt = prompt.x.reshape(x_prompt.shape)
    y_sample = latent.x.reshape(x_sample.shape)
    return (y_prompt, y_sample, jnp.concatenate(new_k, axis=1), jnp.concatenate(new_v, axis=1),
            jnp.concatenate(new_s, axis=1))
```

```python
import functools

import numpy as np
import jax
import jax.numpy as jnp
from jax import lax
from jax.experimental import pallas as pl
from jax.experimental.pallas import tpu as pltpu

F32 = jnp.float32
BF16 = jnp.bfloat16

D_MODEL = 2048
GRID_W = 64
POOL_WIDTH = 1024
POOL_GROUPS = 4
POOL_GROUP_DIM = 256
POOL_WINDOWS = (2, 4, 8, 16)
NA_WIDTH = 1024
NA_HEAD_DIM = 128
NA_HEADS = 8
WIN_H = 8
WIN_W = 16
SSD_INNER = 4096
SSD_HEAD_DIM = 64
SSD_HEADS = 64
SSD_GROUPS = 8
SSD_GROUP_HEADS = SSD_HEADS // SSD_GROUPS
SSD_GROUP_WIDTH = SSD_GROUP_HEADS * SSD_HEAD_DIM
SSD_STATE = 128
SSD_CONV = 4
SSD_CHUNK = 128
SSD_BC_WIDTH = SSD_GROUPS * SSD_STATE
SSD_MAIN_DIM = 2 * SSD_INNER + 2 * SSD_BC_WIDTH
N_EXPERTS = 16
EXPERT_FF = 1024
EC_CAPACITY_FACTOR = 2
NORM_EPS = 1e-6
N_MOD = 6
MOD_ROWS = 16
ROUTE_LANES = 128
ROUTE_TOKEN_LANE = 3 * N_EXPERTS
ROUTE_TOKEN_RADIX = 64
ROUTE_ROWS_PER_STEP = 16
PREFIX_BLOCK = 256

VMEM_LIMIT_BYTES = 56 * 1024 * 1024
ROW_TILE = 256


def _cparams(*sem):
    return pltpu.CompilerParams(dimension_semantics=sem, vmem_limit_bytes=VMEM_LIMIT_BYTES)


def _silu(x):
    return x * (1.0 / (1.0 + jnp.exp(-x)))


def _split2(x):
    hi = x.astype(BF16)
    lo = (x - hi.astype(F32)).astype(BF16)
    return hi, lo


def _split3(x):
    hi = x.astype(BF16)
    r = x - hi.astype(F32)
    mid = r.astype(BF16)
    lo = (r - mid.astype(F32)).astype(BF16)
    return hi, mid, lo


def _dot(a, b):
    return jnp.dot(a, b, preferred_element_type=F32)


def _dot_nt(a, b):
    return lax.dot_general(a, b, (((1,), (1,)), ((), ())), preferred_element_type=F32)


def _dot_f32_rhs(m, x):
    hi, mid, lo = _split3(x)
    return _dot(m, hi) + _dot(m, mid) + _dot(m, lo)


def _dot_split(a, b):
    a_hi, a_lo = _split2(a)
    b_hi, b_lo = _split2(b)
    return _dot(a_hi, b_hi) + _dot(a_lo, b_hi) + _dot(a_hi, b_lo)


def _mod_kernel(c_ref, w_ref, b_ref, o_ref):
    s = _silu(c_ref[...])
    o_ref[...] = _dot_split(s, w_ref[...]) + b_ref[...]


def _modulation(cond, ada_w, ada_b, layer):
    tn = 768
    n = N_MOD * D_MODEL
    ada_b3 = ada_b.reshape(ada_b.shape[0], 1, n)
    return pl.pallas_call(
        _mod_kernel,
        grid=(n // tn,),
        in_specs=[
            pl.BlockSpec((MOD_ROWS, D_MODEL), lambda j: (0, 0)),
            pl.BlockSpec((None, D_MODEL, tn), lambda j: (layer, 0, j)),
            pl.BlockSpec((None, 1, tn), lambda j: (layer, 0, j)),
        ],
        out_specs=pl.BlockSpec((MOD_ROWS, tn), lambda j: (0, j)),
        out_shape=jax.ShapeDtypeStruct((MOD_ROWS, n), F32),
        compiler_params=_cparams("parallel"),
        name="modulation",
    )(cond, ada_w, ada_b3)


def _mod_spec(which, rows_per_cond, first_row, tm):
    return pl.BlockSpec((None, None, 1, D_MODEL),
                        lambda i, *_: (first_row + (i * tm) // rows_per_cond, which, 0, 0))


def _norm_mod_kernel(x_ref, g_ref, sh_ref, sc_ref, *rest, with_router):
    x = x_ref[...]
    y = x * lax.rsqrt(jnp.mean(x * x, axis=-1, keepdims=True) + NORM_EPS) * g_ref[...]
    h = y * (1.0 + sc_ref[...]) + sh_ref[...]
    if with_router:
        rw_ref, h_ref, lg_ref = rest
        lg_ref[...] = _dot_split(h, rw_ref[...])
    else:
        (h_ref,) = rest
    h_ref[...] = h.astype(BF16)


def _norm_mod(x, g, mod4, layer, shift_idx, scale_idx, rows_per_cond, first_row, router_w=None):
    m = x.shape[0]
    tm = 2 * ROW_TILE
    g3 = g.reshape(g.shape[0], 1, D_MODEL)
    in_specs = [
        pl.BlockSpec((tm, D_MODEL), lambda i: (i, 0)),
        pl.BlockSpec((None, 1, D_MODEL), lambda i: (layer, 0, 0)),
        _mod_spec(shift_idx, rows_per_cond, first_row, tm),
        _mod_spec(scale_idx, rows_per_cond, first_row, tm),
    ]
    args = [x, g3, mod4, mod4]
    out_specs = [pl.BlockSpec((tm, D_MODEL), lambda i: (i, 0))]
    out_shape = [jax.ShapeDtypeStruct((m, D_MODEL), BF16)]
    if router_w is not None:
        in_specs.append(pl.BlockSpec((None, D_MODEL, ROUTE_LANES), lambda i: (layer, 0, 0)))
        args.append(router_w)
        out_specs.append(pl.BlockSpec((tm, ROUTE_LANES), lambda i: (i, 0)))
        out_shape.append(jax.ShapeDtypeStruct((m, ROUTE_LANES), F32))
    return pl.pallas_call(
        functools.partial(_norm_mod_kernel, with_router=router_w is not None),
        grid=(m // tm,),
        in_specs=in_specs,
        out_specs=out_specs,
        out_shape=out_shape,
        compiler_params=_cparams("parallel"),
        name="norm_mod",
    )(*args)


def _matmul_kernel(*refs, n_a, k_sizes, with_res):
    a_refs = refs[:n_a]
    w_ref = refs[n_a]
    if with_res:
        res_ref, gate_ref, o_ref = refs[n_a + 1:]
    else:
        (o_ref,) = refs[n_a + 1:]
    w = w_ref[...].astype(BF16)
    acc = None
    k0 = 0
    for a_ref, k in zip(a_refs, k_sizes):
        part = _dot(a_ref[...], w[k0:k0 + k])
        acc = part if acc is None else acc + part
        k0 += k
    if with_res:
        acc = res_ref[...] + gate_ref[...] * acc
    o_ref[...] = acc.astype(o_ref.dtype)


def _matmul(a_list, w, layer, *, tm, tn, n_blocks, col_block0=0, out_dtype=F32,
            res=None, mod4=None, gate_idx=None, rows_per_cond=None, first_row=None):
    m = a_list[0].shape[0]
    k_sizes = tuple(a.shape[1] for a in a_list)
    k_total = sum(k_sizes)
    in_specs = [pl.BlockSpec((tm, k), lambda i, j: (i, 0)) for k in k_sizes]
    in_specs.append(pl.BlockSpec((None, k_total, tn), lambda i, j: (layer, 0, col_block0 + j)))
    args = list(a_list) + [w]
    if res is not None:
        in_specs.append(pl.BlockSpec((tm, tn), lambda i, j: (i, j)))
        in_specs.append(pl.BlockSpec((None, None, 1, tn),
                                     lambda i, j: (first_row + (i * tm) // rows_per_cond, gate_idx, 0, j)))
        args += [res, mod4]
    return pl.pallas_call(
        functools.partial(_matmul_kernel, n_a=len(a_list), k_sizes=k_sizes, with_res=res is not None),
        grid=(m // tm, n_blocks),
        in_specs=in_specs,
        out_specs=pl.BlockSpec((tm, tn), lambda i, j: (i, j)),
        out_shape=jax.ShapeDtypeStruct((m, n_blocks * tn), out_dtype),
        compiler_params=_cparams("parallel", "arbitrary"),
        name="matmul",
    )(*args)


def _pool_kernel(u_ref, w_ref, s_ref, o_ref, *, n):
    t = lax.broadcasted_iota(jnp.int32, (n, 1), 0)
    for gi, win in enumerate(POOL_WINDOWS):
        lanes = slice(gi * POOL_GROUP_DIM, (gi + 1) * POOL_GROUP_DIM)
        u = u_ref[:, lanes]
        half = win // 2
        total = u
        for d in range(-half, win - half):
            if d == 0:
                continue
            shifted = pltpu.roll(u, (-d) % n, axis=0)
            valid = jnp.logical_and(t + d >= 0, t + d < n)
            total = total + jnp.where(valid, shifted, 0.0)
        cnt = (jnp.minimum(t - half + win, n) - jnp.maximum(t - half, 0)).astype(F32)
        pooled = total / cnt - u
        out = _dot(pooled.astype(BF16), w_ref[gi].astype(BF16)) * s_ref[:, lanes]
        o_ref[:, lanes] = out.astype(o_ref.dtype)


def _pool_mixer(proj3, pool_w, pool_scale, i_even):
    b, n, _ = proj3.shape
    scale3 = pool_scale.reshape(pool_scale.shape[0], 1, POOL_WIDTH)
    return pl.pallas_call(
        functools.partial(_pool_kernel, n=n),
        grid=(b,),
        in_specs=[
            pl.BlockSpec((None, n, POOL_WIDTH), lambda bi: (bi, 0, 0)),
            pl.BlockSpec((None, POOL_GROUPS, POOL_GROUP_DIM, POOL_GROUP_DIM), lambda bi: (i_even, 0, 0, 0)),
            pl.BlockSpec((None, 1, POOL_WIDTH), lambda bi: (i_even, 0, 0)),
        ],
        out_specs=pl.BlockSpec((None, n, POOL_WIDTH), lambda bi: (bi, 0, 0)),
        out_shape=jax.ShapeDtypeStruct((b, n, POOL_WIDTH), BF16),
        compiler_params=_cparams("parallel"),
        name="pool_mixer",
    )(proj3, pool_w, scale3)


def _head_norm(x, g):
    return x * lax.rsqrt(jnp.mean(x * x, axis=-1, keepdims=True) + NORM_EPS) * g


_Q_COL0 = POOL_WIDTH // NA_HEAD_DIM
_K_COL0 = (POOL_WIDTH + NA_WIDTH) // NA_HEAD_DIM
_V_COL0 = (POOL_WIDTH + 2 * NA_WIDTH) // NA_HEAD_DIM


def _ctx_attn_kernel(q_ref, k_ref, v_ref, qg_ref, kg_ref, o_ref, kn_ref, vn_ref):
    for h in range(NA_HEADS):
        lanes = slice(h * NA_HEAD_DIM, (h + 1) * NA_HEAD_DIM)
        qn = _head_norm(q_ref[:, lanes], qg_ref[...])
        kn = _head_norm(k_ref[:, lanes], kg_ref[...])
        v = v_ref[:, lanes]
        kn_ref[h] = kn
        vn_ref[h] = v
        s = _dot_nt(qn.astype(BF16), kn.astype(BF16)) * (NA_HEAD_DIM ** -0.5)
        m = jnp.max(s, axis=-1, keepdims=True)
        p = jnp.exp(s - m)
        denom = jnp.sum(p, axis=-1, keepdims=True)
        o = _dot(p.astype(BF16), v.astype(BF16)) / denom
        o_ref[:, lanes] = o.astype(o_ref.dtype)


def _ctx_attention(proj3, q_norm, k_norm, i_even):
    b, n, _ = proj3.shape
    qg = q_norm.reshape(q_norm.shape[0], 1, NA_HEAD_DIM)
    kg = k_norm.reshape(k_norm.shape[0], 1, NA_HEAD_DIM)
    width_blocks = NA_WIDTH // NA_HEAD_DIM
    head_blk = lambda col0: pl.BlockSpec((None, n, NA_WIDTH), lambda bi: (bi, 0, col0 // width_blocks))
    gain = pl.BlockSpec((None, 1, NA_HEAD_DIM), lambda bi: (i_even, 0, 0))
    cache = pl.BlockSpec((None, None, NA_HEADS, n, NA_HEAD_DIM), lambda bi: (bi, 0, 0, 0, 0))
    return pl.pallas_call(
        _ctx_attn_kernel,
        grid=(b,),
        in_specs=[head_blk(_Q_COL0), head_blk(_K_COL0), head_blk(_V_COL0), gain, gain],
        out_specs=[pl.BlockSpec((None, n, NA_WIDTH), lambda bi: (bi, 0, 0)), cache, cache],
        out_shape=[
            jax.ShapeDtypeStruct((b, n, NA_WIDTH), BF16),
            jax.ShapeDtypeStruct((b, 1, NA_HEADS, n, NA_HEAD_DIM), F32),
            jax.ShapeDtypeStruct((b, 1, NA_HEADS, n, NA_HEAD_DIM), F32),
        ],
        compiler_params=_cparams("parallel"),
        name="ctx_attention",
    )(proj3, proj3, proj3, qg, kg)


NA_BAND = 4
NA_BAND_KEY_ROWS = WIN_H + NA_BAND - 1


def _na_band_key_start(r0, rows):
    return np.clip(r0 - WIN_H // 2, 0, rows - NA_BAND_KEY_ROWS)


def _na_bias_table(rpb, rows):
    col = np.arange(GRID_W)[:, None]
    kc = np.arange(GRID_W)[None, :]
    wstart = np.clip(col - WIN_W // 2, 0, GRID_W - WIN_W)
    inside = (kc >= wstart) & (kc < wstart + WIN_W)
    rel = np.clip(kc - col + WIN_W - 1, 0, 2 * WIN_W - 2)
    a = np.arange(NA_BAND)[:, None]
    kr = np.arange(NA_BAND_KEY_ROWS)[None, :]
    ridx, row_ok = [], []
    for r0 in (0, NA_BAND, rows - NA_BAND):
        r = r0 + a
        sr = np.clip(r - WIN_H // 2, 0, rows - WIN_H)
        krow = _na_band_key_start(r0, rows) + kr
        row_ok.append((krow >= sr) & (krow < sr + WIN_H))
        ridx.append(np.clip(krow - r + WIN_H - 1, 0, 2 * WIN_H - 2))
    ridx = np.stack(ridx)
    ok = np.stack(row_ok)[:, :, None, :, None] & inside[None, None, :, None, :]
    row_sel = np.eye(2 * WIN_H - 1, dtype=np.float32)[ridx]
    col_sel = np.eye(2 * WIN_W - 1, dtype=np.float32)[rel]
    tab = jnp.einsum('vaki,hij,cdj->hvackd', row_sel, rpb, col_sel, precision=lax.Precision.HIGHEST)
    tab = tab + np.where(ok, 0.0, -np.inf).astype(np.float32)[None]
    return tab.reshape(rpb.shape[0], 3, NA_BAND * GRID_W, NA_BAND_KEY_ROWS * GRID_W)


def _na_kernel(q_ref, k_ref, v_ref, kc_ref, vc_ref, qg_ref, kg_ref, bias_ref, o_ref,
               qs_ref, ks_ref, vs_ref, *, rows):
    qs_ref[...] = _head_norm(q_ref[...], qg_ref[...]).astype(BF16)
    ks_ref[...] = _head_norm(k_ref[...], kg_ref[...]).astype(BF16)
    vs_ref[...] = v_ref[...].astype(BF16)
    k_ctx = kc_ref[...].astype(BF16)
    v_ctx = vc_ref[...].astype(BF16)
    scale = NA_HEAD_DIM ** -0.5
    n_bands = rows // NA_BAND
    n_q = NA_BAND * GRID_W
    n_loc = NA_BAND_KEY_ROWS * GRID_W

    def band(i):
        r0 = i * NA_BAND
        key_row0 = jnp.clip(r0 - WIN_H // 2, 0, rows - NA_BAND_KEY_ROWS)
        variant = jnp.where(i == 0, 0, jnp.where(i == n_bands - 1, 2, 1))
        q0 = pl.multiple_of(r0 * GRID_W, n_q)
        q_b = qs_ref[pl.ds(q0, n_q), :]
        kstart = pl.multiple_of(key_row0 * GRID_W, GRID_W)
        k_blk = ks_ref[pl.ds(kstart, n_loc), :]
        v_blk = vs_ref[pl.ds(kstart, n_loc), :]
        s_loc = _dot_nt(q_b, k_blk) * scale + bias_ref[variant]
        s_ctx = _dot_nt(q_b, k_ctx) * scale
        m = jnp.maximum(jnp.max(s_loc, axis=-1, keepdims=True), jnp.max(s_ctx, axis=-1, keepdims=True))
        p_loc = jnp.exp(s_loc - m)
        p_ctx = jnp.exp(s_ctx - m)
        denom = jnp.sum(p_loc, axis=-1, keepdims=True) + jnp.sum(p_ctx, axis=-1, keepdims=True)
        o = (_dot(p_loc.astype(BF16), v_blk) + _dot(p_ctx.astype(BF16), v_ctx)) / denom
        o_ref[pl.ds(q0, n_q), :] = o.astype(o_ref.dtype)

    per_trip = 4 if n_bands % 4 == 0 else (2 if n_bands % 2 == 0 else 1)

    def trip(i, carry):
        for k in range(per_trip):
            band(i * per_trip + k)
        return carry

    lax.fori_loop(0, n_bands // per_trip, trip, 0)


def _neighbourhood_attention(proj3, cache_k, cache_v, q_norm, k_norm, rpb, i_even):
    b, t, _ = proj3.shape
    rows = t // GRID_W
    assert rows % NA_BAND == 0 and rows >= NA_BAND_KEY_ROWS + 1, rows
    n_ctx = cache_k.shape[3]
    bias = _na_bias_table(rpb[i_even], rows)
    qg = q_norm.reshape(q_norm.shape[0], 1, NA_HEAD_DIM)
    kg = k_norm.reshape(k_norm.shape[0], 1, NA_HEAD_DIM)
    head_blk = lambda col0: pl.BlockSpec((None, t, NA_HEAD_DIM), lambda bi, h: (bi, 0, col0 + h))
    gain = pl.BlockSpec((None, 1, NA_HEAD_DIM), lambda bi, h: (i_even, 0, 0))
    cache = pl.BlockSpec((None, None, None, n_ctx, NA_HEAD_DIM), lambda bi, h: (bi, i_even, h, 0, 0))
    return pl.pallas_call(
        functools.partial(_na_kernel, rows=rows),
        grid=(b, NA_HEADS),
        in_specs=[head_blk(_Q_COL0), head_blk(_K_COL0), head_blk(_V_COL0), cache, cache, gain, gain,
                  pl.BlockSpec((None,) + bias.shape[1:], lambda bi, h: (h, 0, 0, 0))],
        out_specs=pl.BlockSpec((None, t, NA_HEAD_DIM), lambda bi, h: (bi, 0, h)),
        out_shape=jax.ShapeDtypeStruct((b, t, NA_WIDTH), BF16),
        scratch_shapes=[pltpu.VMEM((t, NA_HEAD_DIM), BF16)] * 3,
        compiler_params=_cparams("parallel", "parallel"),
        name="neighbourhood_attention",
    )(proj3, proj3, proj3, cache_k, cache_v, qg, kg, bias)


def _softplus(x):
    return jnp.maximum(x, 0.0) + jnp.log1p(jnp.exp(-jnp.abs(x)))


def _ssd_kernel(*refs, l, with_init):
    (x_ref, b_ref, c_ref, dt_ref, cwx_ref, cwb_ref, cwc_ref, cbx_ref, cbb_ref, cbc_ref,
     alog_ref, dtbias_ref, dsk_ref) = refs[:13]
    if with_init:
        init_ref = refs[13]
        outs = refs[14:]
    else:
        outs = refs[13:]
    (y_ref, sfin_ref, xs_ref, bs_ref, cs_ref, stf_ref, stb_ref,
     cumf_ref, cumb_ref, srcf_ref, srcb_ref) = outs
    g = pl.program_id(1)
    q = SSD_CHUNK
    nc = l // q
    gw = SSD_GROUP_WIDTH
    gh = SSD_GROUP_HEADS
    pad_l = SSD_CONV // 2
    halo = 8

    win_rows = q + 2 * halo
    sel_col = lax.broadcasted_iota(jnp.int32, (q, SSD_CONV * win_rows), 1)
    sel_row = lax.broadcasted_iota(jnp.int32, (q, SSD_CONV * win_rows), 0)
    shift_mat = ((sel_col % win_rows) == sel_row + (halo - pad_l) + sel_col // win_rows).astype(BF16)

    tap_w = {id(w_ref): [jnp.broadcast_to(w_ref[j:j + 1, :], (win_rows, w_ref.shape[1])).astype(BF16)
                         for j in range(SSD_CONV)]
             for w_ref in (cwx_ref, cwb_ref, cwc_ref)}

    def conv_chunk(c, carry):
        t0 = pl.multiple_of(c * q, q)
        lo_start = pl.multiple_of(jnp.maximum(t0 - halo, 0), halo)
        hi_start = pl.multiple_of(jnp.minimum(t0 + q, l - halo), halo)
        for src, w_ref, bias_ref, dst in ((x_ref, cwx_ref, cbx_ref, xs_ref),
                                          (b_ref, cwb_ref, cbb_ref, bs_ref),
                                          (c_ref, cwc_ref, cbc_ref, cs_ref)):
            lo = jnp.where(c > 0, src[pl.ds(lo_start, halo), :], 0.0)
            hi = jnp.where(c < nc - 1, src[pl.ds(hi_start, halo), :], 0.0)
            win = jnp.concatenate([lo, src[pl.ds(t0, q), :], hi], axis=0).astype(BF16)
            taps = jnp.concatenate([win * wj for wj in tap_w[id(w_ref)]], axis=0)
            out = _silu(_dot(shift_mat, taps) + bias_ref[...])
            dst[pl.ds(t0, q), :] = out
            if dst is xs_ref:
                y_ref[pl.ds(t0, q), :] = out * dsk_ref[...]
        dt = _softplus(pltpu.roll(dt_ref[pl.ds(t0, q), :], to_lane0, axis=1) + bias_g)
        dta = dt * a_g
        cum_f = _dot_f32_rhs(tri_f, dta)
        cum_b = _dot_f32_rhs(tri_b, dta)
        cumf_ref[pl.ds(t0, q), :] = cum_f
        cumb_ref[pl.ds(t0, q), :] = cum_b
        log_dt = jnp.log(dt)
        srcf_ref[pl.ds(t0, q), :] = jnp.transpose(cum_f - log_dt)
        srcb_ref[pl.ds(t0, q), :] = jnp.transpose(cum_b - log_dt)
        return carry

    ii = lax.broadcasted_iota(jnp.int32, (q, q), 0)
    jj = lax.broadcasted_iota(jnp.int32, (q, q), 1)
    keep_f = jj <= ii
    keep_b = jj >= ii
    tri_f = keep_f.astype(BF16)
    tri_b = keep_b.astype(BF16)
    n_dt = 2 * SSD_HEADS
    assert q == n_dt
    to_lane0 = (n_dt - g * gh) % n_dt
    bias_g = pltpu.roll(jnp.broadcast_to(dtbias_ref[...], (8, n_dt)), to_lane0, axis=1)[0:1, :]
    a_g = pltpu.roll(jnp.broadcast_to(-jnp.exp(alog_ref[...]), (8, n_dt)), to_lane0, axis=1)[0:1, :]

    per_trip = 4 if nc % 4 == 0 else (2 if nc % 2 == 0 else 1)

    def conv_trip(i, carry):
        for k in range(per_trip):
            conv_chunk(i * per_trip + k, carry)
        return carry

    lax.fori_loop(0, nc // per_trip, conv_trip, 0)

    pair_w = 2 * SSD_HEAD_DIM
    lane = lax.broadcasted_iota(jnp.int32, (1, gw), 1)
    half_mask = [((lane % pair_w) // SSD_HEAD_DIM) == s for s in range(2)]
    first_head_lanes = lax.broadcasted_iota(jnp.int32, (1, pair_w), 1) < SSD_HEAD_DIM

    def scan_chunk(c, reverse):
        keep, edge, lane0 = (keep_b, 0, SSD_HEADS) if reverse else (keep_f, q - 1, 0)
        cum_ref, src_ref, st_ref = ((cumb_ref, srcb_ref, stb_ref) if reverse
                                    else (cumf_ref, srcf_ref, stf_ref))
        t0 = pl.multiple_of(c * q, q)
        xc = xs_ref[pl.ds(t0, q), :]
        bc = bs_ref[pl.ds(t0, q), :]
        cc = cs_ref[pl.ds(t0, q), :]
        cb = _dot_nt(cc.astype(BF16), bc.astype(BF16))
        bct = jnp.transpose(bc)
        x_bf = xc.astype(BF16)
        st = st_ref[...]
        st_bf = st.astype(BF16)
        zero = jnp.zeros((), BF16)
        x_half = [jnp.where(half_mask[s], x_bf, zero) for s in range(2)]
        st_half = [jnp.where(half_mask[s], st_bf, zero) for s in range(2)]
        pieces = []
        for pair in range(gh // 2):
            c0 = pair * pair_w
            lhs_y, lhs_s, carry_decay = [], [], []
            for e in (2 * pair, 2 * pair + 1):
                ln = lane0 + e
                col = jnp.broadcast_to(cum_ref[pl.ds(t0, q), ln:ln + 1], (q, q))
                src = src_ref[pl.ds(t0 + ln, 1), :]
                at_edge = cum_ref[pl.ds(t0 + edge, 1), ln:ln + 1]
                mix = jnp.exp(jnp.where(keep, col - src, -jnp.inf))
                lhs_y.append((cb * mix).astype(BF16))
                lhs_y.append((cc * jnp.exp(col)).astype(BF16))
                lhs_s.append((bct * jnp.exp(at_edge - src)).astype(BF16))
                carry_decay.append(jnp.exp(at_edge))
            rhs_y = jnp.concatenate([x_half[0][:, c0:c0 + pair_w], st_half[0][:, c0:c0 + pair_w],
                                     x_half[1][:, c0:c0 + pair_w], st_half[1][:, c0:c0 + pair_w]], axis=0)
            pieces.append(_dot(jnp.concatenate(lhs_y, axis=1), rhs_y))
            rhs_s = jnp.concatenate([x_half[0][:, c0:c0 + pair_w], x_half[1][:, c0:c0 + pair_w]], axis=0)
            keep_frac = jnp.where(first_head_lanes, carry_decay[0], carry_decay[1])
            st_ref[:, c0:c0 + pair_w] = (st[:, c0:c0 + pair_w] * keep_frac
                                         + _dot(jnp.concatenate(lhs_s, axis=1), rhs_s))
        y_ref[pl.ds(t0, q), :] = y_ref[pl.ds(t0, q), :] + jnp.concatenate(pieces, axis=1)

    for d, st_ref in enumerate((stf_ref, stb_ref)):
        if with_init:
            st_ref[...] = jnp.transpose(init_ref[d].reshape(gw, SSD_STATE))
        else:
            st_ref[...] = jnp.zeros_like(st_ref)

    def step(i, carry):
        for k in range(per_trip):
            s = i * per_trip + k
            scan_chunk(s, False)
            scan_chunk(nc - 1 - s, True)
        return carry

    lax.fori_loop(0, nc // per_trip, step, 0)
    for d, st_ref in enumerate((stf_ref, stb_ref)):
        sfin_ref[d] = jnp.transpose(st_ref[...]).reshape(gh, SSD_HEAD_DIM, SSD_STATE)


def _ssd_scan(proj3, dt_raw3, conv_w, conv_b, a_log, dt_bias, d_skip, i_odd, init=None):
    b, l, _ = proj3.shape
    gh, gw = SSD_GROUP_HEADS, SSD_GROUP_WIDTH
    n_dt = 2 * SSD_HEADS
    a_log3 = a_log.reshape(a_log.shape[0], 1, n_dt)
    dt_bias3 = dt_bias.reshape(dt_bias.shape[0], 1, n_dt)
    dsk = jnp.repeat(d_skip[i_odd].astype(F32), SSD_HEAD_DIM).reshape(1, SSD_INNER)
    conv_b3 = conv_b.reshape(conv_b.shape[0], 1, conv_b.shape[1])

    x_col0 = SSD_INNER // gw
    b_col0 = (2 * SSD_INNER) // SSD_STATE
    c_col0 = (2 * SSD_INNER + SSD_BC_WIDTH) // SSD_STATE
    cwx_col0 = 0
    cwb_col0 = SSD_INNER // SSD_STATE
    cwc_col0 = (SSD_INNER + SSD_BC_WIDTH) // SSD_STATE

    in_specs = [
        pl.BlockSpec((None, l, gw), lambda bi, g: (bi, 0, x_col0 + g)),
        pl.BlockSpec((None, l, SSD_STATE), lambda bi, g: (bi, 0, b_col0 + g)),
        pl.BlockSpec((None, l, SSD_STATE), lambda bi, g: (bi, 0, c_col0 + g)),
        pl.BlockSpec((None, l, n_dt), lambda bi, g: (bi, 0, 0)),
        pl.BlockSpec((None, SSD_CONV, gw), lambda bi, g: (i_odd, 0, cwx_col0 + g)),
        pl.BlockSpec((None, SSD_CONV, SSD_STATE), lambda bi, g: (i_odd, 0, cwb_col0 + g)),
        pl.BlockSpec((None, SSD_CONV, SSD_STATE), lambda bi, g: (i_odd, 0, cwc_col0 + g)),
        pl.BlockSpec((None, 1, gw), lambda bi, g: (i_odd, 0, cwx_col0 + g)),
        pl.BlockSpec((None, 1, SSD_STATE), lambda bi, g: (i_odd, 0, cwb_col0 + g)),
        pl.BlockSpec((None, 1, SSD_STATE), lambda bi, g: (i_odd, 0, cwc_col0 + g)),
        pl.BlockSpec((None, 1, n_dt), lambda bi, g: (i_odd, 0, 0)),
        pl.BlockSpec((None, 1, n_dt), lambda bi, g: (i_odd, 0, 0)),
        pl.BlockSpec((1, gw), lambda bi, g: (0, g)),
    ]
    args = [proj3, proj3, proj3, dt_raw3, conv_w, conv_w, conv_w,
            conv_b3, conv_b3, conv_b3, a_log3, dt_bias3, dsk]
    if init is not None:
        in_specs.append(pl.BlockSpec((None, None, 2, gh, SSD_HEAD_DIM, SSD_STATE),
                                     lambda bi, g: (bi, i_odd, 0, g, 0, 0)))
        args.append(init)
    return pl.pallas_call(
        functools.partial(_ssd_kernel, l=l, with_init=init is not None),
        grid=(b, SSD_GROUPS),
        in_specs=in_specs,
        out_specs=[
            pl.BlockSpec((None, l, gw), lambda bi, g: (bi, 0, g)),
            pl.BlockSpec((None, 2, gh, SSD_HEAD_DIM, SSD_STATE), lambda bi, g: (bi, 0, g, 0, 0)),
        ],
        out_shape=[
            jax.ShapeDtypeStruct((b, l, SSD_INNER), F32),
            jax.ShapeDtypeStruct((b, 2, SSD_HEADS, SSD_HEAD_DIM, SSD_STATE), F32),
        ],
        scratch_shapes=[
            pltpu.VMEM((l, gw), F32),
            pltpu.VMEM((l, SSD_STATE), F32),
            pltpu.VMEM((l, SSD_STATE), F32),
            pltpu.VMEM((SSD_STATE, gw), F32),
            pltpu.VMEM((SSD_STATE, gw), F32),
        ] + [pltpu.VMEM((l, n_dt), F32)] * 4,
        compiler_params=_cparams("parallel", "parallel"),
        name="ssd_scan",
    )(*args)


def _gate_norm_kernel(y_ref, z_ref, g_ref, o_ref):
    y = y_ref[...] * _silu(z_ref[...])
    o = y * lax.rsqrt(jnp.mean(y * y, axis=-1, keepdims=True) + NORM_EPS) * g_ref[...]
    o_ref[...] = o.astype(o_ref.dtype)


def _gate_norm(y, proj, norm_g, i_odd):
    m = y.shape[0]
    tm = ROW_TILE
    g3 = norm_g.reshape(norm_g.shape[0], 1, SSD_INNER)
    return pl.pallas_call(
        _gate_norm_kernel,
        grid=(m // tm,),
        in_specs=[
            pl.BlockSpec((tm, SSD_INNER), lambda i: (i, 0)),
            pl.BlockSpec((tm, SSD_INNER), lambda i: (i, 0)),
            pl.BlockSpec((None, 1, SSD_INNER), lambda i: (i_odd, 0, 0)),
        ],
        out_specs=pl.BlockSpec((tm, SSD_INNER), lambda i: (i, 0)),
        out_shape=jax.ShapeDtypeStruct((m, SSD_INNER), BF16),
        compiler_params=_cparams("parallel"),
        name="gate_norm",
    )(y, proj, g3)


def _expert_kernel(x_ref, wg_ref, wu_ref, wd_ref, gate_ref, o_ref, acc_ref):
    f = pl.program_id(2)
    x = x_ref[...]
    hg = _dot(x, wg_ref[...].astype(BF16))
    hu = _dot(x, wu_ref[...].astype(BF16))
    hdn = (_silu(hg) * hu).astype(BF16)

    @pl.when(f == 0)
    def _():
        acc_ref[...] = jnp.zeros_like(acc_ref)

    acc_ref[...] += _dot(hdn, wd_ref[...].astype(BF16))

    @pl.when(f == pl.num_programs(2) - 1)
    def _():
        o_ref[...] = (acc_ref[...] * gate_ref[...]).astype(o_ref.dtype)


def _expert_ffn(xe, gate, w_gate, w_up, w_down, layer):
    e, r, _ = xe.shape
    tr = min(r, 1024)
    tf = 256
    return pl.pallas_call(
        _expert_kernel,
        grid=(e, r // tr, EXPERT_FF // tf),
        in_specs=[
            pl.BlockSpec((None, tr, D_MODEL), lambda ei, ri, f: (ei, ri, 0)),
            pl.BlockSpec((None, None, D_MODEL, tf), lambda ei, ri, f: (layer, ei, 0, f)),
            pl.BlockSpec((None, None, D_MODEL, tf), lambda ei, ri, f: (layer, ei, 0, f)),
            pl.BlockSpec((None, None, tf, D_MODEL), lambda ei, ri, f: (layer, ei, f, 0)),
            pl.BlockSpec((None, tr, 1), lambda ei, ri, f: (ei, ri, 0)),
        ],
        out_specs=pl.BlockSpec((None, tr, D_MODEL), lambda ei, ri, f: (ei, ri, 0)),
        out_shape=jax.ShapeDtypeStruct((e, r, D_MODEL), BF16),
        scratch_shapes=[pltpu.VMEM((tr, D_MODEL), F32)],
        compiler_params=_cparams("parallel", "parallel", "arbitrary"),
        name="expert_ffn",
    )(xe, w_gate, w_up, w_down, gate)


def _combine_kernel(idx_ref, ye_ref, x_ref, gate_ref, o_ref, *, n, ec, cap):
    step = pl.program_id(2)
    token = lax.broadcasted_iota(jnp.int32, (n, cap), 0)
    onehot = jnp.concatenate(
        [jnp.where(token == idx_ref[k], 1.0, 0.0).astype(BF16) for k in range(ec)], axis=1)

    @pl.when(step == 0)
    def _():
        o_ref[...] = jnp.zeros_like(o_ref)

    o_ref[...] += _dot(onehot, ye_ref[...].reshape(ec * cap, ye_ref.shape[-1]))

    @pl.when(step == pl.num_programs(2) - 1)
    def _():
        o_ref[...] = x_ref[...] + gate_ref[...] * o_ref[...]


def _expert_combine(ye, idx, x, mod4, gate_idx, first_row, shared_cond):
    n_req, n_exp, cap = idx.shape
    m = x.shape[0]
    n = m // n_req
    ec = max(1, min(n_exp, 1024 // cap))
    tn = D_MODEL // 2
    idx4 = idx.reshape(n_req, n_exp, 1, cap)
    cond_row = (lambda b: first_row) if shared_cond else (lambda b: first_row + b)
    return pl.pallas_call(
        functools.partial(_combine_kernel, n=n, ec=ec, cap=cap),
        grid=(n_req, D_MODEL // tn, n_exp // ec),
        in_specs=[
            pl.BlockSpec((None, ec, 1, cap), lambda b, j, s: (b, s, 0, 0)),
            pl.BlockSpec((ec, cap, tn), lambda b, j, s: (s, b, j)),
            pl.BlockSpec((n, tn), lambda b, j, s: (b, j)),
            pl.BlockSpec((None, None, 1, tn), lambda b, j, s: (cond_row(b), gate_idx, 0, j)),
        ],
        out_specs=pl.BlockSpec((n, tn), lambda b, j, s: (b, j)),
        out_shape=jax.ShapeDtypeStruct((m, D_MODEL), F32),
        compiler_params=_cparams("parallel", "parallel", "arbitrary"),
        name="expert_combine",
    )(idx4, ye, x, mod4)


def _prefix_count(mask_bf, strict_upper):
    n = mask_bf.shape[1]
    carry = jnp.zeros((mask_bf.shape[0], 1), F32)
    out = []
    for k in range(n // PREFIX_BLOCK):
        blk = mask_bf[:, k * PREFIX_BLOCK:(k + 1) * PREFIX_BLOCK]
        out.append(_dot(blk, strict_upper) + carry)
        carry = carry + jnp.sum(blk.astype(F32), axis=-1, keepdims=True)
    return jnp.concatenate(out, axis=1)


def _router_kernel(lg_ref, o_ref, slot_ref, feat_ref, *, n_req, n, cap):
    step = pl.program_id(0)
    n_exp = N_EXPERTS

    @pl.when(step == 0)
    def _():
        tok = lax.broadcasted_iota(jnp.int32, (n, ROUTE_LANES), 0)
        tlane = lax.broadcasted_iota(jnp.int32, (n, ROUTE_LANES), 1)
        radix_bits = ROUTE_TOKEN_RADIX.bit_length() - 1
        tok_feat = (jnp.where(tlane == ROUTE_TOKEN_LANE, tok >> radix_bits, 0)
                    + jnp.where(tlane == ROUTE_TOKEN_LANE + 1, tok & (ROUTE_TOKEN_RADIX - 1), 0)).astype(F32)
        is_expert = tlane < n_exp
        aff_rows = []
        for r in range(n_req):
            x = jnp.where(is_expert, lg_ref[pl.ds(r * n, n), :], -jnp.inf)
            e = jnp.exp(x - jnp.max(x, axis=-1, keepdims=True))
            aff = e / jnp.sum(e, axis=-1, keepdims=True)
            feat = tok_feat
            for p, part in enumerate(_split3(aff)):
                piece = part.astype(F32)
                feat = feat + (pltpu.roll(piece, p * n_exp, axis=1) if p else piece)
            feat_ref[r] = feat.astype(BF16)
            aff_rows.append(jnp.transpose(aff)[:n_exp])
        aff_t = jnp.concatenate(aff_rows, axis=0)
        bits = pltpu.bitcast(aff_t, jnp.int32)

        def bisect(_, carry):
            lo, hi = carry
            mid = lax.shift_right_arithmetic(lo + hi, 1)
            count = jnp.sum(jnp.where(bits >= mid, 1.0, 0.0), axis=-1, keepdims=True)
            enough = count >= cap
            return jnp.where(enough, mid, lo), jnp.where(enough, hi, mid)

        rows = n_req * n_exp
        lo0 = jnp.zeros((rows, 1), jnp.int32)
        hi0 = jnp.full((rows, 1), 0x3F800001, jnp.int32)
        kth, _ = lax.fori_loop(0, 31, bisect, (lo0, hi0))
        above = bits > kth
        tied = bits == kth
        blk_i = lax.broadcasted_iota(jnp.int32, (PREFIX_BLOCK, PREFIX_BLOCK), 0)
        blk_j = lax.broadcasted_iota(jnp.int32, (PREFIX_BLOCK, PREFIX_BLOCK), 1)
        strict_upper = (blk_i < blk_j).astype(BF16)
        need = cap - jnp.sum(jnp.where(above, 1.0, 0.0), axis=-1, keepdims=True)
        tie_rank = _prefix_count(jnp.where(tied, 1.0, 0.0).astype(BF16), strict_upper)
        chosen = jnp.logical_or(above, jnp.logical_and(tied, tie_rank < need))
        slot = _prefix_count(jnp.where(chosen, 1.0, 0.0).astype(BF16), strict_upper)
        slot_ref[...] = jnp.where(chosen, slot, -1.0)

    slot_id = lax.broadcasted_iota(jnp.int32, (cap, n), 0).astype(F32)
    for k in range(ROUTE_ROWS_PER_STEP):
        row = step * ROUTE_ROWS_PER_STEP + k
        onehot_t = jnp.where(slot_id == slot_ref[pl.ds(row, 1), :], 1.0, 0.0).astype(BF16)
        o_ref[k] = _dot(onehot_t, feat_ref[row // n_exp])


def _route(logits, n_req):
    m = logits.shape[0]
    n = m // n_req
    cap = EC_CAPACITY_FACTOR * n // N_EXPERTS
    rows = n_req * N_EXPERTS
    assert n % PREFIX_BLOCK == 0 and rows % ROUTE_ROWS_PER_STEP == 0
    rec, slot = pl.pallas_call(
        functools.partial(_router_kernel, n_req=n_req, n=n, cap=cap),
        grid=(rows // ROUTE_ROWS_PER_STEP,),
        in_specs=[pl.BlockSpec((m, ROUTE_LANES), lambda i: (0, 0))],
        out_specs=[pl.BlockSpec((ROUTE_ROWS_PER_STEP, cap, ROUTE_LANES), lambda i: (i, 0, 0)),
                   pl.BlockSpec((rows, n), lambda i: (0, 0))],
        out_shape=[jax.ShapeDtypeStruct((rows, cap, ROUTE_LANES), F32),
                   jax.ShapeDtypeStruct((rows, n), F32)],
        scratch_shapes=[pltpu.VMEM((n_req, n, ROUTE_LANES), BF16)],
        compiler_params=_cparams("arbitrary"),
        name="router",
    )(logits)
    rec = rec.reshape(n_req, N_EXPERTS, cap, ROUTE_LANES)
    idx = (ROUTE_TOKEN_RADIX * rec[..., ROUTE_TOKEN_LANE] + rec[..., ROUTE_TOKEN_LANE + 1]).astype(jnp.int32)
    pieces = rec[..., :3 * N_EXPERTS].reshape(n_req, N_EXPERTS, cap, 3, N_EXPERTS).sum(axis=3)
    own = jnp.eye(N_EXPERTS, dtype=F32)[None, :, None, :]
    gate = jnp.sum(pieces * own, axis=3)
    return idx, gate, slot


def _gather_kernel(slot_ref, h_ref, o_ref, *, n, ec, cap):
    slot_id = lax.broadcasted_iota(jnp.int32, (cap, n), 0).astype(F32)
    onehot = jnp.concatenate(
        [jnp.where(slot_id == slot_ref[k], 1.0, 0.0).astype(BF16) for k in range(ec)], axis=0)
    o_ref[...] = _dot(onehot, h_ref[...]).reshape(o_ref.shape).astype(o_ref.dtype)


def _gather_rows(h, slot, n_req):
    m, d = h.shape
    n = m // n_req
    cap = EC_CAPACITY_FACTOR * n // N_EXPERTS
    ec = max(1, min(N_EXPERTS, 2048 // cap))
    tn = d // 2
    slot4 = slot.reshape(n_req, N_EXPERTS, 1, n)
    return pl.pallas_call(
        functools.partial(_gather_kernel, n=n, ec=ec, cap=cap),
        grid=(n_req, d // tn, N_EXPERTS // ec),
        in_specs=[pl.BlockSpec((None, ec, 1, n), lambda b, j, s: (b, s, 0, 0)),
                  pl.BlockSpec((n, tn), lambda b, j, s: (b, j))],
        out_specs=pl.BlockSpec((ec, cap, tn), lambda b, j, s: (s, b, j)),
        out_shape=jax.ShapeDtypeStruct((N_EXPERTS, n_req * cap, d), h.dtype),
        compiler_params=_cparams("parallel", "parallel", "arbitrary"),
        name="gather_rows",
    )(slot4, h)


def _expert_choice(h, logits, x, n_req, w_gate, w_up, w_down, layer, mod4, gate_idx, first_row, shared_cond):
    idx, gate, slot = _route(logits, n_req)
    cap = idx.shape[2]
    gate = jnp.transpose(gate, (1, 0, 2)).reshape(N_EXPERTS, n_req * cap, 1)
    xe = _gather_rows(h, slot, n_req)
    ye = _expert_ffn(xe, gate, w_gate, w_up, w_down, layer)
    return _expert_combine(ye, idx, x, mod4, gate_idx, first_row, shared_cond)


class _Stream:
    def __init__(self, x3, first_row, shared_cond):
        self.batch, self.n, _ = x3.shape
        self.x = x3.reshape(self.batch * self.n, D_MODEL)
        self.first_row = first_row
        self.shared_cond = shared_cond
        self.rows_per_cond = self.batch * self.n if shared_cond else self.n

    def mod_args(self):
        return dict(rows_per_cond=self.rows_per_cond, first_row=self.first_row)


def kernel(x_prompt, x_sample, cache_na_k, cache_na_v, state_ssd, c, c_ctx, ada_w, ada_b, norm1_g, norm2_g,
           router_w, exp_w_gate, exp_w_up, exp_w_down, ab_w_in, pool_w, pool_scale, na_q_norm, na_k_norm,
           na_rpb, ab_w_out, ssd_w_in, ssd_conv_w, ssd_conv_b, ssd_a_log, ssd_dt_bias, ssd_d, ssd_norm_g,
           ssd_w_out):
    depth = ada_w.shape[0]
    dec_batch = x_sample.shape[0]
    latent = _Stream(x_sample, 0, shared_cond=False)
    prompt = _Stream(x_prompt, dec_batch, shared_cond=True)
    cond = jnp.concatenate(
        [c, c_ctx[None, :], jnp.zeros((MOD_ROWS - dec_batch - 1, D_MODEL), F32)], axis=0)

    router_pad = jnp.pad(router_w, ((0, 0), (0, 0), (0, ROUTE_LANES - N_EXPERTS)))

    new_k, new_v, new_s = [], [], []
    for layer in range(depth):
        i = layer // 2
        mod4 = _modulation(cond, ada_w, ada_b, layer).reshape(MOD_ROWS, N_MOD, 1, D_MODEL)
        for s in (prompt, latent):
            margs = s.mod_args()
            h = _norm_mod(s.x, norm1_g, mod4, layer, 0, 1, **margs)[0]
            tm = min(s.x.shape[0], 2048)
            if layer % 2 == 0:
                proj = _matmul([h], ab_w_in, i, tm=tm, tn=512, n_blocks=ab_w_in.shape[2] // 512)
                proj3 = proj.reshape(s.batch, s.n, -1)
                pooled = _pool_mixer(proj3, pool_w, pool_scale, i)
                if s is prompt:
                    att, k_new, v_new = _ctx_attention(proj3, na_q_norm, na_k_norm, i)
                    new_k.append(k_new)
                    new_v.append(v_new)
                else:
                    att = _neighbourhood_attention(proj3, cache_na_k, cache_na_v, na_q_norm, na_k_norm,
                                                   na_rpb, i)
                s.x = _matmul([pooled.reshape(-1, POOL_WIDTH), att.reshape(-1, NA_WIDTH)], ab_w_out, i,
                              tm=tm, tn=512, n_blocks=D_MODEL // 512, res=s.x, mod4=mod4, gate_idx=2,
                              **margs)
            else:
                proj = _matmul([h], ssd_w_in, i, tm=tm, tn=512, n_blocks=SSD_MAIN_DIM // 512)
                dt_raw = _matmul([h], ssd_w_in, i, tm=tm, tn=2 * SSD_HEADS, n_blocks=1,
                                 col_block0=SSD_MAIN_DIM // (2 * SSD_HEADS))
                proj3 = proj.reshape(s.batch, s.n, SSD_MAIN_DIM)
                dt3 = dt_raw.reshape(s.batch, s.n, 2 * SSD_HEADS)
                init = None if s is prompt else state_ssd
                y, s_fin = _ssd_scan(proj3, dt3, ssd_conv_w, ssd_conv_b, ssd_a_log, ssd_dt_bias, ssd_d, i,
                                     init=init)
                if s is prompt:
                    new_s.append(s_fin[:, None])
                yn = _gate_norm(y.reshape(-1, SSD_INNER), proj, ssd_norm_g, i)
                s.x = _matmul([yn], ssd_w_out, i, tm=min(tm, 1024), tn=512, n_blocks=D_MODEL // 512,
                              res=s.x, mod4=mod4, gate_idx=2, **margs)
            h2, logits = _norm_mod(s.x, norm2_g, mod4, layer, 3, 4, router_w=router_pad, **margs)
            s.x = _expert_choice(h2, logits, s.x, s.batch, exp_w_gate, exp_w_up, exp_w_down, layer,
                                 mod4, 5, s.first_row, s.shared_cond)

    y_prompt = prompt.x.reshape(x_prompt.shape)
    y_sample = latent.x.reshape(x_sample.shape)
    return (y_prompt, y_sample, jnp.concatenate(new_k, axis=1), jnp.concatenate(new_v, axis=1),
            jnp.concatenate(new_s, axis=1))
```

```python
import functools

import numpy as np
import jax
import jax.numpy as jnp
from jax import lax
from jax.experimental import pallas as pl
from jax.experimental.pallas import tpu as pltpu

F32 = jnp.float32
BF16 = jnp.bfloat16

D_MODEL = 2048
GRID_W = 64
POOL_WIDTH = 1024
POOL_GROUPS = 4
POOL_GROUP_DIM = 256
POOL_WINDOWS = (2, 4, 8, 16)
NA_WIDTH = 1024
NA_HEAD_DIM = 128
NA_HEADS = 8
WIN_H = 8
WIN_W = 16
SSD_INNER = 4096
SSD_HEAD_DIM = 64
SSD_HEADS = 64
SSD_GROUPS = 8
SSD_GROUP_HEADS = SSD_HEADS // SSD_GROUPS
SSD_GROUP_WIDTH = SSD_GROUP_HEADS * SSD_HEAD_DIM
SSD_STATE = 128
SSD_CONV = 4
SSD_CHUNK = 128
SSD_BC_WIDTH = SSD_GROUPS * SSD_STATE
SSD_MAIN_DIM = 2 * SSD_INNER + 2 * SSD_BC_WIDTH
N_EXPERTS = 16
EXPERT_FF = 1024
EC_CAPACITY_FACTOR = 2
NORM_EPS = 1e-6
N_MOD = 6
MOD_ROWS = 16
ROUTE_LANES = 128
ROUTE_TOKEN_LANE = 3 * N_EXPERTS
ROUTE_TOKEN_RADIX = 64
ROUTE_ROWS_PER_STEP = 16
PREFIX_BLOCK = 256

VMEM_LIMIT_BYTES = 56 * 1024 * 1024
ROW_TILE = 256


def _cparams(*sem):
    return pltpu.CompilerParams(dimension_semantics=sem, vmem_limit_bytes=VMEM_LIMIT_BYTES)


def _silu(x):
    return x * (1.0 / (1.0 + jnp.exp(-x)))


def _split2(x):
    hi = x.astype(BF16)
    lo = (x - hi.astype(F32)).astype(BF16)
    return hi, lo


def _split3(x):
    hi = x.astype(BF16)
    r = x - hi.astype(F32)
    mid = r.astype(BF16)
    lo = (r - mid.astype(F32)).astype(BF16)
    return hi, mid, lo


def _dot(a, b):
    return jnp.dot(a, b, preferred_element_type=F32)


def _dot_nt(a, b):
    return lax.dot_general(a, b, (((1,), (1,)), ((), ())), preferred_element_type=F32)


def _dot_f32_rhs(m, x):
    hi, mid, lo = _split3(x)
    return _dot(m, hi) + _dot(m, mid) + _dot(m, lo)


def _dot_split(a, b):
    a_hi, a_lo = _split2(a)
    b_hi, b_lo = _split2(b)
    return _dot(a_hi, b_hi) + _dot(a_lo, b_hi) + _dot(a_hi, b_lo)


def _mod_kernel(c_ref, w_ref, b_ref, o_ref):
    s = _silu(c_ref[...])
    o_ref[...] = _dot_split(s, w_ref[...]) + b_ref[...]


def _modulation(cond, ada_w, ada_b, layer):
    tn = 768
    n = N_MOD * D_MODEL
    ada_b3 = ada_b.reshape(ada_b.shape[0], 1, n)
    return pl.pallas_call(
        _mod_kernel,
        grid=(n // tn,),
        in_specs=[
            pl.BlockSpec((MOD_ROWS, D_MODEL), lambda j: (0, 0)),
            pl.BlockSpec((None, D_MODEL, tn), lambda j: (layer, 0, j)),
            pl.BlockSpec((None, 1, tn), lambda j: (layer, 0, j)),
        ],
        out_specs=pl.BlockSpec((MOD_ROWS, tn), lambda j: (0, j)),
        out_shape=jax.ShapeDtypeStruct((MOD_ROWS, n), F32),
        compiler_params=_cparams("parallel"),
        name="modulation",
    )(cond, ada_w, ada_b3)


def _mod_spec(which, rows_per_cond, first_row, tm):
    return pl.BlockSpec((None, None, 1, D_MODEL),
                        lambda i, *_: (first_row + (i * tm) // rows_per_cond, which, 0, 0))


def _norm_mod_kernel(x_ref, g_ref, sh_ref, sc_ref, *rest, with_router):
    x = x_ref[...]
    y = x * lax.rsqrt(jnp.mean(x * x, axis=-1, keepdims=True) + NORM_EPS) * g_ref[...]
    h = y * (1.0 + sc_ref[...]) + sh_ref[...]
    if with_router:
        rw_ref, h_ref, lg_ref = rest
        lg_ref[...] = _dot_split(h, rw_ref[...])
    else:
        (h_ref,) = rest
    h_ref[...] = h.astype(BF16)


def _norm_mod(x, g, mod4, layer, shift_idx, scale_idx, rows_per_cond, first_row, router_w=None):
    m = x.shape[0]
    tm = 2 * ROW_TILE
    g3 = g.reshape(g.shape[0], 1, D_MODEL)
    in_specs = [
        pl.BlockSpec((tm, D_MODEL), lambda i: (i, 0)),
        pl.BlockSpec((None, 1, D_MODEL), lambda i: (layer, 0, 0)),
        _mod_spec(shift_idx, rows_per_cond, first_row, tm),
        _mod_spec(scale_idx, rows_per_cond, first_row, tm),
    ]
    args = [x, g3, mod4, mod4]
    out_specs = [pl.BlockSpec((tm, D_MODEL), lambda i: (i, 0))]
    out_shape = [jax.ShapeDtypeStruct((m, D_MODEL), BF16)]
    if router_w is not None:
        in_specs.append(pl.BlockSpec((None, D_MODEL, ROUTE_LANES), lambda i: (layer, 0, 0)))
        args.append(router_w)
        out_specs.append(pl.BlockSpec((tm, ROUTE_LANES), lambda i: (i, 0)))
        out_shape.append(jax.ShapeDtypeStruct((m, ROUTE_LANES), F32))
    return pl.pallas_call(
        functools.partial(_norm_mod_kernel, with_router=router_w is not None),
        grid=(m // tm,),
        in_specs=in_specs,
        out_specs=out_specs,
        out_shape=out_shape,
        compiler_params=_cparams("parallel"),
        name="norm_mod",
    )(*args)


def _matmul_kernel(*refs, n_a, k_sizes, with_res, with_side):
    a_refs = refs[:n_a]
    w_ref = refs[n_a]
    if with_res:
        res_ref, gate_ref, o_ref = refs[n_a + 1:]
    elif with_side:
        side_w_ref, o_ref, side_o_ref = refs[n_a + 1:]

        @pl.when(pl.program_id(1) == 0)
        def _():
            side_o_ref[...] = _dot(a_refs[0][...], side_w_ref[...].astype(BF16))
    else:
        (o_ref,) = refs[n_a + 1:]
    w = w_ref[...].astype(BF16)
    acc = None
    k0 = 0
    for a_ref, k in zip(a_refs, k_sizes):
        part = _dot(a_ref[...], w[k0:k0 + k])
        acc = part if acc is None else acc + part
        k0 += k
    if with_res:
        acc = res_ref[...] + gate_ref[...] * acc
    o_ref[...] = acc.astype(o_ref.dtype)


def _matmul(a_list, w, layer, *, tm, tn, n_blocks, col_block0=0, out_dtype=F32,
            res=None, mod4=None, gate_idx=None, rows_per_cond=None, first_row=None, side=None):
    m = a_list[0].shape[0]
    k_sizes = tuple(a.shape[1] for a in a_list)
    k_total = sum(k_sizes)
    in_specs = [pl.BlockSpec((tm, k), lambda i, j: (i, 0)) for k in k_sizes]
    in_specs.append(pl.BlockSpec((None, k_total, tn), lambda i, j: (layer, 0, col_block0 + j)))
    args = list(a_list) + [w]
    out_specs = [pl.BlockSpec((tm, tn), lambda i, j: (i, j))]
    out_shape = [jax.ShapeDtypeStruct((m, n_blocks * tn), out_dtype)]
    if res is not None:
        in_specs.append(pl.BlockSpec((tm, tn), lambda i, j: (i, j)))
        in_specs.append(pl.BlockSpec((None, None, 1, tn),
                                     lambda i, j: (first_row + (i * tm) // rows_per_cond, gate_idx, 0, j)))
        args += [res, mod4]
    elif side is not None:
        width, block = side
        in_specs.append(pl.BlockSpec((None, k_total, width), lambda i, j: (layer, 0, block)))
        args.append(w)
        out_specs.append(pl.BlockSpec((tm, width), lambda i, j: (i, 0)))
        out_shape.append(jax.ShapeDtypeStruct((m, width), F32))
    outs = pl.pallas_call(
        functools.partial(_matmul_kernel, n_a=len(a_list), k_sizes=k_sizes, with_res=res is not None,
                          with_side=side is not None),
        grid=(m // tm, n_blocks),
        in_specs=in_specs,
        out_specs=out_specs,
        out_shape=out_shape,
        compiler_params=_cparams("parallel", "arbitrary"),
        name="matmul",
    )(*args)
    return outs if side is not None else outs[0]


def _pool_kernel(u_ref, w_ref, s_ref, o_ref, *, n):
    t = lax.broadcasted_iota(jnp.int32, (n, 1), 0)
    for gi, win in enumerate(POOL_WINDOWS):
        lanes = slice(gi * POOL_GROUP_DIM, (gi + 1) * POOL_GROUP_DIM)
        u = u_ref[:, lanes]
        half = win // 2
        total = u
        for d in range(-half, win - half):
            if d == 0:
                continue
            shifted = pltpu.roll(u, (-d) % n, axis=0)
            valid = jnp.logical_and(t + d >= 0, t + d < n)
            total = total + jnp.where(valid, shifted, 0.0)
        cnt = (jnp.minimum(t - half + win, n) - jnp.maximum(t - half, 0)).astype(F32)
        pooled = total / cnt - u
        out = _dot(pooled.astype(BF16), w_ref[gi].astype(BF16)) * s_ref[:, lanes]
        o_ref[:, lanes] = out.astype(o_ref.dtype)


def _pool_mixer(proj3, pool_w, pool_scale, i_even):
    b, n, _ = proj3.shape
    scale3 = pool_scale.reshape(pool_scale.shape[0], 1, POOL_WIDTH)
    return pl.pallas_call(
        functools.partial(_pool_kernel, n=n),
        grid=(b,),
        in_specs=[
            pl.BlockSpec((None, n, POOL_WIDTH), lambda bi: (bi, 0, 0)),
            pl.BlockSpec((None, POOL_GROUPS, POOL_GROUP_DIM, POOL_GROUP_DIM), lambda bi: (i_even, 0, 0, 0)),
            pl.BlockSpec((None, 1, POOL_WIDTH), lambda bi: (i_even, 0, 0)),
        ],
        out_specs=pl.BlockSpec((None, n, POOL_WIDTH), lambda bi: (bi, 0, 0)),
        out_shape=jax.ShapeDtypeStruct((b, n, POOL_WIDTH), BF16),
        compiler_params=_cparams("parallel"),
        name="pool_mixer",
    )(proj3, pool_w, scale3)


def _head_norm(x, g):
    return x * lax.rsqrt(jnp.mean(x * x, axis=-1, keepdims=True) + NORM_EPS) * g


_Q_COL0 = POOL_WIDTH // NA_HEAD_DIM
_K_COL0 = (POOL_WIDTH + NA_WIDTH) // NA_HEAD_DIM
_V_COL0 = (POOL_WIDTH + 2 * NA_WIDTH) // NA_HEAD_DIM


def _ctx_attn_kernel(q_ref, k_ref, v_ref, qg_ref, kg_ref, o_ref, kn_ref, vn_ref):
    for h in range(NA_HEADS):
        lanes = slice(h * NA_HEAD_DIM, (h + 1) * NA_HEAD_DIM)
        qn = _head_norm(q_ref[:, lanes], qg_ref[...])
        kn = _head_norm(k_ref[:, lanes], kg_ref[...])
        v = v_ref[:, lanes]
        kn_ref[h] = kn
        vn_ref[h] = v
        s = _dot_nt(qn.astype(BF16), kn.astype(BF16)) * (NA_HEAD_DIM ** -0.5)
        m = jnp.max(s, axis=-1, keepdims=True)
        p = jnp.exp(s - m)
        denom = jnp.sum(p, axis=-1, keepdims=True)
        o = _dot(p.astype(BF16), v.astype(BF16)) / denom
        o_ref[:, lanes] = o.astype(o_ref.dtype)


def _ctx_attention(proj3, q_norm, k_norm, i_even):
    b, n, _ = proj3.shape
    qg = q_norm.reshape(q_norm.shape[0], 1, NA_HEAD_DIM)
    kg = k_norm.reshape(k_norm.shape[0], 1, NA_HEAD_DIM)
    width_blocks = NA_WIDTH // NA_HEAD_DIM
    head_blk = lambda col0: pl.BlockSpec((None, n, NA_WIDTH), lambda bi: (bi, 0, col0 // width_blocks))
    gain = pl.BlockSpec((None, 1, NA_HEAD_DIM), lambda bi: (i_even, 0, 0))
    cache = pl.BlockSpec((None, None, NA_HEADS, n, NA_HEAD_DIM), lambda bi: (bi, 0, 0, 0, 0))
    return pl.pallas_call(
        _ctx_attn_kernel,
        grid=(b,),
        in_specs=[head_blk(_Q_COL0), head_blk(_K_COL0), head_blk(_V_COL0), gain, gain],
        out_specs=[pl.BlockSpec((None, n, NA_WIDTH), lambda bi: (bi, 0, 0)), cache, cache],
        out_shape=[
            jax.ShapeDtypeStruct((b, n, NA_WIDTH), BF16),
            jax.ShapeDtypeStruct((b, 1, NA_HEADS, n, NA_HEAD_DIM), F32),
            jax.ShapeDtypeStruct((b, 1, NA_HEADS, n, NA_HEAD_DIM), F32),
        ],
        compiler_params=_cparams("parallel"),
        name="ctx_attention",
    )(proj3, proj3, proj3, qg, kg)


NA_BAND = 4
NA_BAND_KEY_ROWS = WIN_H + NA_BAND - 1


def _na_band_key_start(r0, rows):
    return np.clip(r0 - WIN_H // 2, 0, rows - NA_BAND_KEY_ROWS)


def _na_bias_table(rpb, rows):
    col = np.arange(GRID_W)[:, None]
    kc = np.arange(GRID_W)[None, :]
    wstart = np.clip(col - WIN_W // 2, 0, GRID_W - WIN_W)
    inside = (kc >= wstart) & (kc < wstart + WIN_W)
    rel = np.clip(kc - col + WIN_W - 1, 0, 2 * WIN_W - 2)
    a = np.arange(NA_BAND)[:, None]
    kr = np.arange(NA_BAND_KEY_ROWS)[None, :]
    ridx, row_ok = [], []
    for r0 in (0, NA_BAND, rows - NA_BAND):
        r = r0 + a
        sr = np.clip(r - WIN_H // 2, 0, rows - WIN_H)
        krow = _na_band_key_start(r0, rows) + kr
        row_ok.append((krow >= sr) & (krow < sr + WIN_H))
        ridx.append(np.clip(krow - r + WIN_H - 1, 0, 2 * WIN_H - 2))
    ridx = np.stack(ridx)
    ok = np.stack(row_ok)[:, :, None, :, None] & inside[None, None, :, None, :]
    row_sel = np.eye(2 * WIN_H - 1, dtype=np.float32)[ridx]
    col_sel = np.eye(2 * WIN_W - 1, dtype=np.float32)[rel]
    tab = jnp.einsum('vaki,hij,cdj->hvackd', row_sel, rpb, col_sel, precision=lax.Precision.HIGHEST)
    tab = tab + np.where(ok, 0.0, -np.inf).astype(np.float32)[None]
    return tab.reshape(rpb.shape[0], 3, NA_BAND * GRID_W, NA_BAND_KEY_ROWS * GRID_W)


def _na_kernel(q_ref, k_ref, v_ref, kc_ref, vc_ref, qg_ref, kg_ref, bias_ref, o_ref,
               qs_ref, ks_ref, vs_ref, *, rows):
    qs_ref[...] = _head_norm(q_ref[...], qg_ref[...]).astype(BF16)
    ks_ref[...] = _head_norm(k_ref[...], kg_ref[...]).astype(BF16)
    vs_ref[...] = v_ref[...].astype(BF16)
    k_ctx = kc_ref[...].astype(BF16)
    v_ctx = vc_ref[...].astype(BF16)
    scale = NA_HEAD_DIM ** -0.5
    n_bands = rows // NA_BAND
    n_q = NA_BAND * GRID_W
    n_loc = NA_BAND_KEY_ROWS * GRID_W

    def band(i):
        r0 = i * NA_BAND
        key_row0 = jnp.clip(r0 - WIN_H // 2, 0, rows - NA_BAND_KEY_ROWS)
        variant = jnp.where(i == 0, 0, jnp.where(i == n_bands - 1, 2, 1))
        q0 = pl.multiple_of(r0 * GRID_W, n_q)
        q_b = qs_ref[pl.ds(q0, n_q), :]
        kstart = pl.multiple_of(key_row0 * GRID_W, GRID_W)
        k_blk = ks_ref[pl.ds(kstart, n_loc), :]
        v_blk = vs_ref[pl.ds(kstart, n_loc), :]
        s_loc = _dot_nt(q_b, k_blk) * scale + bias_ref[variant]
        s_ctx = _dot_nt(q_b, k_ctx) * scale
        m = jnp.maximum(jnp.max(s_loc, axis=-1, keepdims=True), jnp.max(s_ctx, axis=-1, keepdims=True))
        p_loc = jnp.exp(s_loc - m)
        p_ctx = jnp.exp(s_ctx - m)
        denom = jnp.sum(p_loc, axis=-1, keepdims=True) + jnp.sum(p_ctx, axis=-1, keepdims=True)
        o = (_dot(p_loc.astype(BF16), v_blk) + _dot(p_ctx.astype(BF16), v_ctx)) / denom
        o_ref[pl.ds(q0, n_q), :] = o.astype(o_ref.dtype)

    per_trip = 4 if n_bands % 4 == 0 else (2 if n_bands % 2 == 0 else 1)

    def trip(i, carry):
        for k in range(per_trip):
            band(i * per_trip + k)
        return carry

    lax.fori_loop(0, n_bands // per_trip, trip, 0)


def _neighbourhood_attention(proj3, cache_k, cache_v, q_norm, k_norm, rpb, i_even):
    b, t, _ = proj3.shape
    rows = t // GRID_W
    assert rows % NA_BAND == 0 and rows >= NA_BAND_KEY_ROWS + 1, rows
    n_ctx = cache_k.shape[3]
    bias = _na_bias_table(rpb[i_even], rows)
    qg = q_norm.reshape(q_norm.shape[0], 1, NA_HEAD_DIM)
    kg = k_norm.reshape(k_norm.shape[0], 1, NA_HEAD_DIM)
    head_blk = lambda col0: pl.BlockSpec((None, t, NA_HEAD_DIM), lambda bi, h: (bi, 0, col0 + h))
    gain = pl.BlockSpec((None, 1, NA_HEAD_DIM), lambda bi, h: (i_even, 0, 0))
    cache = pl.BlockSpec((None, None, None, n_ctx, NA_HEAD_DIM), lambda bi, h: (bi, i_even, h, 0, 0))
    return pl.pallas_call(
        functools.partial(_na_kernel, rows=rows),
        grid=(b, NA_HEADS),
        in_specs=[head_blk(_Q_COL0), head_blk(_K_COL0), head_blk(_V_COL0), cache, cache, gain, gain,
                  pl.BlockSpec((None,) + bias.shape[1:], lambda bi, h: (h, 0, 0, 0))],
        out_specs=pl.BlockSpec((None, t, NA_HEAD_DIM), lambda bi, h: (bi, 0, h)),
        out_shape=jax.ShapeDtypeStruct((b, t, NA_WIDTH), BF16),
        scratch_shapes=[pltpu.VMEM((t, NA_HEAD_DIM), BF16)] * 3,
        compiler_params=_cparams("parallel", "parallel"),
        name="neighbourhood_attention",
    )(proj3, proj3, proj3, cache_k, cache_v, qg, kg, bias)


def _softplus(x):
    return jnp.maximum(x, 0.0) + jnp.log1p(jnp.exp(-jnp.abs(x)))


def _ssd_kernel(*refs, l, with_init):
    (x_ref, b_ref, c_ref, dt_ref, cwx_ref, cwb_ref, cwc_ref, cbx_ref, cbb_ref, cbc_ref,
     alog_ref, dtbias_ref, dsk_ref) = refs[:13]
    if with_init:
        init_ref = refs[13]
        outs = refs[14:]
    else:
        outs = refs[13:]
    (y_ref, sfin_ref, xs_ref, bs_ref, cs_ref, stf_ref, stb_ref,
     cumf_ref, cumb_ref, srcf_ref, srcb_ref) = outs
    g = pl.program_id(1)
    q = SSD_CHUNK
    nc = l // q
    gw = SSD_GROUP_WIDTH
    gh = SSD_GROUP_HEADS
    pad_l = SSD_CONV // 2
    halo = 8

    win_rows = q + 2 * halo
    sel_col = lax.broadcasted_iota(jnp.int32, (q, SSD_CONV * win_rows), 1)
    sel_row = lax.broadcasted_iota(jnp.int32, (q, SSD_CONV * win_rows), 0)
    shift_mat = ((sel_col % win_rows) == sel_row + (halo - pad_l) + sel_col // win_rows).astype(BF16)

    tap_w = {id(w_ref): [jnp.broadcast_to(w_ref[j:j + 1, :], (win_rows, w_ref.shape[1])).astype(BF16)
                         for j in range(SSD_CONV)]
             for w_ref in (cwx_ref, cwb_ref, cwc_ref)}

    def conv_chunk(c, carry):
        t0 = pl.multiple_of(c * q, q)
        lo_start = pl.multiple_of(jnp.maximum(t0 - halo, 0), halo)
        hi_start = pl.multiple_of(jnp.minimum(t0 + q, l - halo), halo)
        for src, w_ref, bias_ref, dst in ((x_ref, cwx_ref, cbx_ref, xs_ref),
                                          (b_ref, cwb_ref, cbb_ref, bs_ref),
                                          (c_ref, cwc_ref, cbc_ref, cs_ref)):
            lo = jnp.where(c > 0, src[pl.ds(lo_start, halo), :], 0.0)
            hi = jnp.where(c < nc - 1, src[pl.ds(hi_start, halo), :], 0.0)
            win = jnp.concatenate([lo, src[pl.ds(t0, q), :], hi], axis=0).astype(BF16)
            taps = jnp.concatenate([win * wj for wj in tap_w[id(w_ref)]], axis=0)
            out = _silu(_dot(shift_mat, taps) + bias_ref[...])
            dst[pl.ds(t0, q), :] = out
            if dst is xs_ref:
                y_ref[pl.ds(t0, q), :] = out * dsk_ref[...]
        dt = _softplus(pltpu.roll(dt_ref[pl.ds(t0, q), :], to_lane0, axis=1) + bias_g)
        dta = dt * a_g
        cum_f = _dot_f32_rhs(tri_f, dta)
        cum_b = _dot_f32_rhs(tri_b, dta)
        cumf_ref[pl.ds(t0, q), :] = cum_f
        cumb_ref[pl.ds(t0, q), :] = cum_b
        log_dt = jnp.log(dt)
        srcf_ref[pl.ds(t0, q), :] = jnp.transpose(cum_f - log_dt)
        srcb_ref[pl.ds(t0, q), :] = jnp.transpose(cum_b - log_dt)
        return carry

    ii = lax.broadcasted_iota(jnp.int32, (q, q), 0)
    jj = lax.broadcasted_iota(jnp.int32, (q, q), 1)
    keep_f = jj <= ii
    keep_b = jj >= ii
    tri_f = keep_f.astype(BF16)
    tri_b = keep_b.astype(BF16)
    n_dt = 2 * SSD_HEADS
    assert q == n_dt
    to_lane0 = (n_dt - g * gh) % n_dt
    bias_g = pltpu.roll(jnp.broadcast_to(dtbias_ref[...], (8, n_dt)), to_lane0, axis=1)[0:1, :]
    a_g = pltpu.roll(jnp.broadcast_to(-jnp.exp(alog_ref[...]), (8, n_dt)), to_lane0, axis=1)[0:1, :]

    per_trip = 4 if nc % 4 == 0 else (2 if nc % 2 == 0 else 1)

    def conv_trip(i, carry):
        for k in range(per_trip):
            conv_chunk(i * per_trip + k, carry)
        return carry

    lax.fori_loop(0, nc // per_trip, conv_trip, 0)

    pair_w = 2 * SSD_HEAD_DIM
    lane = lax.broadcasted_iota(jnp.int32, (1, gw), 1)
    half_mask = [((lane % pair_w) // SSD_HEAD_DIM) == s for s in range(2)]
    first_head_lanes = lax.broadcasted_iota(jnp.int32, (1, pair_w), 1) < SSD_HEAD_DIM

    def scan_chunk(c, reverse):
        keep, edge, lane0 = (keep_b, 0, SSD_HEADS) if reverse else (keep_f, q - 1, 0)
        cum_ref, src_ref, st_ref = ((cumb_ref, srcb_ref, stb_ref) if reverse
                                    else (cumf_ref, srcf_ref, stf_ref))
        t0 = pl.multiple_of(c * q, q)
        xc = xs_ref[pl.ds(t0, q), :]
        bc = bs_ref[pl.ds(t0, q), :]
        cc = cs_ref[pl.ds(t0, q), :]
        cb = _dot_nt(cc.astype(BF16), bc.astype(BF16))
        bct = jnp.transpose(bc)
        x_bf = xc.astype(BF16)
        st = st_ref[...]
        st_bf = st.astype(BF16)
        zero = jnp.zeros((), BF16)
        x_half = [jnp.where(half_mask[s], x_bf, zero) for s in range(2)]
        st_half = [jnp.where(half_mask[s], st_bf, zero) for s in range(2)]
        pieces = []
        for pair in range(gh // 2):
            c0 = pair * pair_w
            lhs_y, lhs_s, carry_decay = [], [], []
            for e in (2 * pair, 2 * pair + 1):
                ln = lane0 + e
                col = jnp.broadcast_to(cum_ref[pl.ds(t0, q), ln:ln + 1], (q, q))
                src = src_ref[pl.ds(t0 + ln, 1), :]
                at_edge = cum_ref[pl.ds(t0 + edge, 1), ln:ln + 1]
                mix = jnp.exp(jnp.where(keep, col - src, -jnp.inf))
                lhs_y.append((cb * mix).astype(BF16))
                lhs_y.append((cc * jnp.exp(col)).astype(BF16))
                lhs_s.append((bct * jnp.exp(at_edge - src)).astype(BF16))
                carry_decay.append(jnp.exp(at_edge))
            rhs_y = jnp.concatenate([x_half[0][:, c0:c0 + pair_w], st_half[0][:, c0:c0 + pair_w],
                                     x_half[1][:, c0:c0 + pair_w], st_half[1][:, c0:c0 + pair_w]], axis=0)
            pieces.append(_dot(jnp.concatenate(lhs_y, axis=1), rhs_y))
            rhs_s = jnp.concatenate([x_half[0][:, c0:c0 + pair_w], x_half[1][:, c0:c0 + pair_w]], axis=0)
            keep_frac = jnp.where(first_head_lanes, carry_decay[0], carry_decay[1])
            st_ref[:, c0:c0 + pair_w] = (st[:, c0:c0 + pair_w] * keep_frac
                                         + _dot(jnp.concatenate(lhs_s, axis=1), rhs_s))
        y_ref[pl.ds(t0, q), :] = y_ref[pl.ds(t0, q), :] + jnp.concatenate(pieces, axis=1)

    for d, st_ref in enumerate((stf_ref, stb_ref)):
        if with_init:
            st_ref[...] = jnp.transpose(init_ref[d].reshape(gw, SSD_STATE))
        else:
            st_ref[...] = jnp.zeros_like(st_ref)

    def step(i, carry):
        for k in range(per_trip):
            s = i * per_trip + k
            scan_chunk(s, False)
            scan_chunk(nc - 1 - s, True)
        return carry

    lax.fori_loop(0, nc // per_trip, step, 0)
    for d, st_ref in enumerate((stf_ref, stb_ref)):
        sfin_ref[d] = jnp.transpose(st_ref[...]).reshape(gh, SSD_HEAD_DIM, SSD_STATE)


def _ssd_scan(proj3, dt_raw3, conv_w, conv_b, a_log, dt_bias, d_skip, i_odd, init=None):
    b, l, _ = proj3.shape
    gh, gw = SSD_GROUP_HEADS, SSD_GROUP_WIDTH
    n_dt = 2 * SSD_HEADS
    a_log3 = a_log.reshape(a_log.shape[0], 1, n_dt)
    dt_bias3 = dt_bias.reshape(dt_bias.shape[0], 1, n_dt)
    dsk = jnp.repeat(d_skip[i_odd].astype(F32), SSD_HEAD_DIM).reshape(1, SSD_INNER)
    conv_b3 = conv_b.reshape(conv_b.shape[0], 1, conv_b.shape[1])

    x_col0 = SSD_INNER // gw
    b_col0 = (2 * SSD_INNER) // SSD_STATE
    c_col0 = (2 * SSD_INNER + SSD_BC_WIDTH) // SSD_STATE
    cwx_col0 = 0
    cwb_col0 = SSD_INNER // SSD_STATE
    cwc_col0 = (SSD_INNER + SSD_BC_WIDTH) // SSD_STATE

    in_specs = [
        pl.BlockSpec((None, l, gw), lambda bi, g: (bi, 0, x_col0 + g)),
        pl.BlockSpec((None, l, SSD_STATE), lambda bi, g: (bi, 0, b_col0 + g)),
        pl.BlockSpec((None, l, SSD_STATE), lambda bi, g: (bi, 0, c_col0 + g)),
        pl.BlockSpec((None, l, n_dt), lambda bi, g: (bi, 0, 0)),
        pl.BlockSpec((None, SSD_CONV, gw), lambda bi, g: (i_odd, 0, cwx_col0 + g)),
        pl.BlockSpec((None, SSD_CONV, SSD_STATE), lambda bi, g: (i_odd, 0, cwb_col0 + g)),
        pl.BlockSpec((None, SSD_CONV, SSD_STATE), lambda bi, g: (i_odd, 0, cwc_col0 + g)),
        pl.BlockSpec((None, 1, gw), lambda bi, g: (i_odd, 0, cwx_col0 + g)),
        pl.BlockSpec((None, 1, SSD_STATE), lambda bi, g: (i_odd, 0, cwb_col0 + g)),
        pl.BlockSpec((None, 1, SSD_STATE), lambda bi, g: (i_odd, 0, cwc_col0 + g)),
        pl.BlockSpec((None, 1, n_dt), lambda bi, g: (i_odd, 0, 0)),
        pl.BlockSpec((None, 1, n_dt), lambda bi, g: (i_odd, 0, 0)),
        pl.BlockSpec((1, gw), lambda bi, g: (0, g)),
    ]
    args = [proj3, proj3, proj3, dt_raw3, conv_w, conv_w, conv_w,
            conv_b3, conv_b3, conv_b3, a_log3, dt_bias3, dsk]
    if init is not None:
        in_specs.append(pl.BlockSpec((None, None, 2, gh, SSD_HEAD_DIM, SSD_STATE),
                                     lambda bi, g: (bi, i_odd, 0, g, 0, 0)))
        args.append(init)
    return pl.pallas_call(
        functools.partial(_ssd_kernel, l=l, with_init=init is not None),
        grid=(b, SSD_GROUPS),
        in_specs=in_specs,
        out_specs=[
            pl.BlockSpec((None, l, gw), lambda bi, g: (bi, 0, g)),
            pl.BlockSpec((None, 2, gh, SSD_HEAD_DIM, SSD_STATE), lambda bi, g: (bi, 0, g, 0, 0)),
        ],
        out_shape=[
            jax.ShapeDtypeStruct((b, l, SSD_INNER), F32),
            jax.ShapeDtypeStruct((b, 2, SSD_HEADS, SSD_HEAD_DIM, SSD_STATE), F32),
        ],
        scratch_shapes=[
            pltpu.VMEM((l, gw), F32),
            pltpu.VMEM((l, SSD_STATE), F32),
            pltpu.VMEM((l, SSD_STATE), F32),
            pltpu.VMEM((SSD_STATE, gw), F32),
            pltpu.VMEM((SSD_STATE, gw), F32),
        ] + [pltpu.VMEM((l, n_dt), F32)] * 4,
        compiler_params=_cparams("parallel", "parallel"),
        name="ssd_scan",
    )(*args)


def _gate_norm_kernel(y_ref, z_ref, g_ref, o_ref):
    y = y_ref[...] * _silu(z_ref[...])
    o = y * lax.rsqrt(jnp.mean(y * y, axis=-1, keepdims=True) + NORM_EPS) * g_ref[...]
    o_ref[...] = o.astype(o_ref.dtype)


def _gate_norm(y, proj, norm_g, i_odd):
    m = y.shape[0]
    tm = ROW_TILE
    g3 = norm_g.reshape(norm_g.shape[0], 1, SSD_INNER)
    return pl.pallas_call(
        _gate_norm_kernel,
        grid=(m // tm,),
        in_specs=[
            pl.BlockSpec((tm, SSD_INNER), lambda i: (i, 0)),
            pl.BlockSpec((tm, SSD_INNER), lambda i: (i, 0)),
            pl.BlockSpec((None, 1, SSD_INNER), lambda i: (i_odd, 0, 0)),
        ],
        out_specs=pl.BlockSpec((tm, SSD_INNER), lambda i: (i, 0)),
        out_shape=jax.ShapeDtypeStruct((m, SSD_INNER), BF16),
        compiler_params=_cparams("parallel"),
        name="gate_norm",
    )(y, proj, g3)


def _expert_kernel(x_ref, wg_ref, wu_ref, wd_ref, gate_ref, o_ref, acc_ref):
    f = pl.program_id(2)
    x = x_ref[...]
    hg = _dot(x, wg_ref[...].astype(BF16))
    hu = _dot(x, wu_ref[...].astype(BF16))
    hdn = (_silu(hg) * hu).astype(BF16)

    @pl.when(f == 0)
    def _():
        acc_ref[...] = jnp.zeros_like(acc_ref)

    acc_ref[...] += _dot(hdn, wd_ref[...].astype(BF16))

    @pl.when(f == pl.num_programs(2) - 1)
    def _():
        o_ref[...] = (acc_ref[...] * gate_ref[...]).astype(o_ref.dtype)


def _expert_ffn(xe, gate, w_gate, w_up, w_down, layer):
    e, r, _ = xe.shape
    tr = min(r, 1024)
    tf = 256
    return pl.pallas_call(
        _expert_kernel,
        grid=(e, r // tr, EXPERT_FF // tf),
        in_specs=[
            pl.BlockSpec((None, tr, D_MODEL), lambda ei, ri, f: (ei, ri, 0)),
            pl.BlockSpec((None, None, D_MODEL, tf), lambda ei, ri, f: (layer, ei, 0, f)),
            pl.BlockSpec((None, None, D_MODEL, tf), lambda ei, ri, f: (layer, ei, 0, f)),
            pl.BlockSpec((None, None, tf, D_MODEL), lambda ei, ri, f: (layer, ei, f, 0)),
            pl.BlockSpec((None, tr, 1), lambda ei, ri, f: (ei, ri, 0)),
        ],
        out_specs=pl.BlockSpec((None, tr, D_MODEL), lambda ei, ri, f: (ei, ri, 0)),
        out_shape=jax.ShapeDtypeStruct((e, r, D_MODEL), BF16),
        scratch_shapes=[pltpu.VMEM((tr, D_MODEL), F32)],
        compiler_params=_cparams("parallel", "parallel", "arbitrary"),
        name="expert_ffn",
    )(xe, w_gate, w_up, w_down, gate)


def _combine_kernel(idx_ref, ye_ref, x_ref, gate_ref, o_ref, *, n, ec, cap):
    step = pl.program_id(2)
    token = lax.broadcasted_iota(jnp.int32, (n, cap), 0)
    onehot = jnp.concatenate(
        [jnp.where(token == idx_ref[k], 1.0, 0.0).astype(BF16) for k in range(ec)], axis=1)

    @pl.when(step == 0)
    def _():
        o_ref[...] = jnp.zeros_like(o_ref)

    o_ref[...] += _dot(onehot, ye_ref[...].reshape(ec * cap, ye_ref.shape[-1]))

    @pl.when(step == pl.num_programs(2) - 1)
    def _():
        o_ref[...] = x_ref[...] + gate_ref[...] * o_ref[...]


def _expert_combine(ye, idx, x, mod4, gate_idx, first_row, shared_cond):
    n_req, n_exp, cap = idx.shape
    m = x.shape[0]
    n = m // n_req
    ec = max(1, min(n_exp, 1024 // cap))
    tn = D_MODEL // 2
    idx4 = idx.reshape(n_req, n_exp, 1, cap)
    cond_row = (lambda b: first_row) if shared_cond else (lambda b: first_row + b)
    return pl.pallas_call(
        functools.partial(_combine_kernel, n=n, ec=ec, cap=cap),
        grid=(n_req, D_MODEL // tn, n_exp // ec),
        in_specs=[
            pl.BlockSpec((None, ec, 1, cap), lambda b, j, s: (b, s, 0, 0)),
            pl.BlockSpec((ec, cap, tn), lambda b, j, s: (s, b, j)),
            pl.BlockSpec((n, tn), lambda b, j, s: (b, j)),
            pl.BlockSpec((None, None, 1, tn), lambda b, j, s: (cond_row(b), gate_idx, 0, j)),
        ],
        out_specs=pl.BlockSpec((n, tn), lambda b, j, s: (b, j)),
        out_shape=jax.ShapeDtypeStruct((m, D_MODEL), F32),
        compiler_params=_cparams("parallel", "parallel", "arbitrary"),
        name="expert_combine",
    )(idx4, ye, x, mod4)


def _prefix_count(mask_bf, strict_upper):
    n = mask_bf.shape[1]
    carry = jnp.zeros((mask_bf.shape[0], 1), F32)
    out = []
    for k in range(n // PREFIX_BLOCK):
        blk = mask_bf[:, k * PREFIX_BLOCK:(k + 1) * PREFIX_BLOCK]
        out.append(_dot(blk, strict_upper) + carry)
        carry = carry + jnp.sum(blk.astype(F32), axis=-1, keepdims=True)
    return jnp.concatenate(out, axis=1)


def _router_kernel(lg_ref, o_ref, slot_ref, feat_ref, *, n_req, n, cap):
    step = pl.program_id(0)
    n_exp = N_EXPERTS

    @pl.when(step == 0)
    def _():
        tok = lax.broadcasted_iota(jnp.int32, (n, ROUTE_LANES), 0)
        tlane = lax.broadcasted_iota(jnp.int32, (n, ROUTE_LANES), 1)
        radix_bits = ROUTE_TOKEN_RADIX.bit_length() - 1
        tok_feat = (jnp.where(tlane == ROUTE_TOKEN_LANE, tok >> radix_bits, 0)
                    + jnp.where(tlane == ROUTE_TOKEN_LANE + 1, tok & (ROUTE_TOKEN_RADIX - 1), 0)).astype(F32)
        is_expert = tlane < n_exp
        aff_rows = []
        for r in range(n_req):
            x = jnp.where(is_expert, lg_ref[pl.ds(r * n, n), :], -jnp.inf)
            e = jnp.exp(x - jnp.max(x, axis=-1, keepdims=True))
            aff = e / jnp.sum(e, axis=-1, keepdims=True)
            feat = tok_feat
            for p, part in enumerate(_split3(aff)):
                piece = part.astype(F32)
                feat = feat + (pltpu.roll(piece, p * n_exp, axis=1) if p else piece)
            feat_ref[r] = feat.astype(BF16)
            aff_rows.append(jnp.transpose(aff)[:n_exp])
        aff_t = jnp.concatenate(aff_rows, axis=0)
        bits = pltpu.bitcast(aff_t, jnp.int32)

        def bisect(_, carry):
            lo, hi = carry
            mid = lax.shift_right_arithmetic(lo + hi, 1)
            count = jnp.sum(jnp.where(bits >= mid, 1.0, 0.0), axis=-1, keepdims=True)
            enough = count >= cap
            return jnp.where(enough, mid, lo), jnp.where(enough, hi, mid)

        rows = n_req * n_exp
        lo0 = jnp.zeros((rows, 1), jnp.int32)
        hi0 = jnp.full((rows, 1), 0x3F800001, jnp.int32)
        kth, _ = lax.fori_loop(0, 31, bisect, (lo0, hi0))
        above = bits > kth
        tied = bits == kth
        blk_i = lax.broadcasted_iota(jnp.int32, (PREFIX_BLOCK, PREFIX_BLOCK), 0)
        blk_j = lax.broadcasted_iota(jnp.int32, (PREFIX_BLOCK, PREFIX_BLOCK), 1)
        strict_upper = (blk_i < blk_j).astype(BF16)
        need = cap - jnp.sum(jnp.where(above, 1.0, 0.0), axis=-1, keepdims=True)
        tie_rank = _prefix_count(jnp.where(tied, 1.0, 0.0).astype(BF16), strict_upper)
        chosen = jnp.logical_or(above, jnp.logical_and(tied, tie_rank < need))
        slot = _prefix_count(jnp.where(chosen, 1.0, 0.0).astype(BF16), strict_upper)
        slot_ref[...] = jnp.where(chosen, slot, -1.0)

    slot_id = lax.broadcasted_iota(jnp.int32, (cap, n), 0).astype(F32)
    for k in range(ROUTE_ROWS_PER_STEP):
        row = step * ROUTE_ROWS_PER_STEP + k
        onehot_t = jnp.where(slot_id == slot_ref[pl.ds(row, 1), :], 1.0, 0.0).astype(BF16)
        o_ref[k] = _dot(onehot_t, feat_ref[row // n_exp])


def _route(logits, n_req):
    m = logits.shape[0]
    n = m // n_req
    cap = EC_CAPACITY_FACTOR * n // N_EXPERTS
    rows = n_req * N_EXPERTS
    assert n % PREFIX_BLOCK == 0 and rows % ROUTE_ROWS_PER_STEP == 0
    rec, slot = pl.pallas_call(
        functools.partial(_router_kernel, n_req=n_req, n=n, cap=cap),
        grid=(rows // ROUTE_ROWS_PER_STEP,),
        in_specs=[pl.BlockSpec((m, ROUTE_LANES), lambda i: (0, 0))],
        out_specs=[pl.BlockSpec((ROUTE_ROWS_PER_STEP, cap, ROUTE_LANES), lambda i: (i, 0, 0)),
                   pl.BlockSpec((rows, n), lambda i: (0, 0))],
        out_shape=[jax.ShapeDtypeStruct((rows, cap, ROUTE_LANES), F32),
                   jax.ShapeDtypeStruct((rows, n), F32)],
        scratch_shapes=[pltpu.VMEM((n_req, n, ROUTE_LANES), BF16)],
        compiler_params=_cparams("arbitrary"),
        name="router",
    )(logits)
    rec = rec.reshape(n_req, N_EXPERTS, cap, ROUTE_LANES)
    idx = (ROUTE_TOKEN_RADIX * rec[..., ROUTE_TOKEN_LANE] + rec[..., ROUTE_TOKEN_LANE + 1]).astype(jnp.int32)
    pieces = rec[..., :3 * N_EXPERTS].reshape(n_req, N_EXPERTS, cap, 3, N_EXPERTS).sum(axis=3)
    own = jnp.eye(N_EXPERTS, dtype=F32)[None, :, None, :]
    gate = jnp.sum(pieces * own, axis=3)
    return idx, gate, slot


def _gather_kernel(slot_ref, h_ref, o_ref, *, n, ec, cap):
    slot_id = lax.broadcasted_iota(jnp.int32, (cap, n), 0).astype(F32)
    onehot = jnp.concatenate(
        [jnp.where(slot_id == slot_ref[k], 1.0, 0.0).astype(BF16) for k in range(ec)], axis=0)
    o_ref[...] = _dot(onehot, h_ref[...]).reshape(o_ref.shape).astype(o_ref.dtype)


def _gather_rows(h, slot, n_req):
    m, d = h.shape
    n = m // n_req
    cap = EC_CAPACITY_FACTOR * n // N_EXPERTS
    ec = max(1, min(N_EXPERTS, 2048 // cap))
    tn = d // 2
    slot4 = slot.reshape(n_req, N_EXPERTS, 1, n)
    return pl.pallas_call(
        functools.partial(_gather_kernel, n=n, ec=ec, cap=cap),
        grid=(n_req, d // tn, N_EXPERTS // ec),
        in_specs=[pl.BlockSpec((None, ec, 1, n), lambda b, j, s: (b, s, 0, 0)),
                  pl.BlockSpec((n, tn), lambda b, j, s: (b, j))],
        out_specs=pl.BlockSpec((ec, cap, tn), lambda b, j, s: (s, b, j)),
        out_shape=jax.ShapeDtypeStruct((N_EXPERTS, n_req * cap, d), h.dtype),
        compiler_params=_cparams("parallel", "parallel", "arbitrary"),
        name="gather_rows",
    )(slot4, h)


def _expert_choice(h, logits, x, n_req, w_gate, w_up, w_down, layer, mod4, gate_idx, first_row, shared_cond):
    idx, gate, slot = _route(logits, n_req)
    cap = idx.shape[2]
    gate = jnp.transpose(gate, (1, 0, 2)).reshape(N_EXPERTS, n_req * cap, 1)
    xe = _gather_rows(h, slot, n_req)
    ye = _expert_ffn(xe, gate, w_gate, w_up, w_down, layer)
    return _expert_combine(ye, idx, x, mod4, gate_idx, first_row, shared_cond)


class _Stream:
    def __init__(self, x3, first_row, shared_cond):
        self.batch, self.n, _ = x3.shape
        self.x = x3.reshape(self.batch * self.n, D_MODEL)
        self.first_row = first_row
        self.shared_cond = shared_cond
        self.rows_per_cond = self.batch * self.n if shared_cond else self.n

    def mod_args(self):
        return dict(rows_per_cond=self.rows_per_cond, first_row=self.first_row)


def kernel(x_prompt, x_sample, cache_na_k, cache_na_v, state_ssd, c, c_ctx, ada_w, ada_b, norm1_g, norm2_g,
           router_w, exp_w_gate, exp_w_up, exp_w_down, ab_w_in, pool_w, pool_scale, na_q_norm, na_k_norm,
           na_rpb, ab_w_out, ssd_w_in, ssd_conv_w, ssd_conv_b, ssd_a_log, ssd_dt_bias, ssd_d, ssd_norm_g,
           ssd_w_out):
    depth = ada_w.shape[0]
    dec_batch = x_sample.shape[0]
    latent = _Stream(x_sample, 0, shared_cond=False)
    prompt = _Stream(x_prompt, dec_batch, shared_cond=True)
    cond = jnp.concatenate(
        [c, c_ctx[None, :], jnp.zeros((MOD_ROWS - dec_batch - 1, D_MODEL), F32)], axis=0)

    router_pad = jnp.pad(router_w, ((0, 0), (0, 0), (0, ROUTE_LANES - N_EXPERTS)))

    new_k, new_v, new_s = [], [], []
    for layer in range(depth):
        i = layer // 2
        mod4 = _modulation(cond, ada_w, ada_b, layer).reshape(MOD_ROWS, N_MOD, 1, D_MODEL)
        for s in (prompt, latent):
            margs = s.mod_args()
            h = _norm_mod(s.x, norm1_g, mod4, layer, 0, 1, **margs)[0]
            tm = min(s.x.shape[0], 2048)
            if layer % 2 == 0:
                proj = _matmul([h], ab_w_in, i, tm=tm, tn=512, n_blocks=ab_w_in.shape[2] // 512)
                proj3 = proj.reshape(s.batch, s.n, -1)
                pooled = _pool_mixer(proj3, pool_w, pool_scale, i)
                if s is prompt:
                    att, k_new, v_new = _ctx_attention(proj3, na_q_norm, na_k_norm, i)
                    new_k.append(k_new)
                    new_v.append(v_new)
                else:
                    att = _neighbourhood_attention(proj3, cache_na_k, cache_na_v, na_q_norm, na_k_norm,
                                                   na_rpb, i)
                s.x = _matmul([pooled.reshape(-1, POOL_WIDTH), att.reshape(-1, NA_WIDTH)], ab_w_out, i,
                              tm=tm, tn=512, n_blocks=D_MODEL // 512, res=s.x, mod4=mod4, gate_idx=2,
                              **margs)
            else:
                proj, dt_raw = _matmul([h], ssd_w_in, i, tm=tm, tn=512, n_blocks=SSD_MAIN_DIM // 512,
                                       side=(2 * SSD_HEADS, SSD_MAIN_DIM // (2 * SSD_HEADS)))
                proj3 = proj.reshape(s.batch, s.n, SSD_MAIN_DIM)
                dt3 = dt_raw.reshape(s.batch, s.n, 2 * SSD_HEADS)
                init = None if s is prompt else state_ssd
                y, s_fin = _ssd_scan(proj3, dt3, ssd_conv_w, ssd_conv_b, ssd_a_log, ssd_dt_bias, ssd_d, i,
                                     init=init)
                if s is prompt:
                    new_s.append(s_fin[:, None])
                yn = _gate_norm(y.reshape(-1, SSD_INNER), proj, ssd_norm_g, i)
                s.x = _matmul([yn], ssd_w_out, i, tm=min(tm, 1024), tn=512, n_blocks=D_MODEL // 512,
                              res=s.x, mod4=mod4, gate_idx=2, **margs)
            h2, logits = _norm_mod(s.x, norm2_g, mod4, layer, 3, 4, router_w=router_pad, **margs)
            s.x = _expert_choice(h2, logits, s.x, s.batch, exp_w_gate, exp_w_up, exp_w_down, layer,
                                 mod4, 5, s.first_row, s.shared_cond)

    y_prompt = prompt.x.reshape(x_prompt.shape)
    y_sample = latent.x.reshape(x_sample.shape)
    return (y_prompt, y_sample, jnp.concatenate(new_k, axis=1), jnp.concatenate(new_v, axis=1),
            jnp.concatenate(new_s, axis=1))
```
